```python
import jax
import jax.numpy as jnp
from jax import lax
import numpy as np

D_MODEL = 2048
BATCH = 8
SEQ = 8192
DEPTH = 2

GRID_W = 64
CTX_LEN = 256
N_EVEN = (DEPTH + 1) // 2
N_ODD = DEPTH // 2
EPS = 1e-6
D_FF = 5632
FFN_STEP = 0.5
LRU_WIDTH = D_MODEL // 2
LRU_BLOCKS = 8
LRU_BLOCK = LRU_WIDTH // LRU_BLOCKS
LRU_C = 8.0
CONV_W = 4
CONV_LEFT = 2
RET_HEADS = 4
RET_DK = 256
RET_DV = 256
RET_QK = RET_HEADS * RET_DK
RET_V = RET_HEADS * RET_DV
RET_CHUNK = 128
RET_THETA = 10000.0
EV_IN = 2 * LRU_WIDTH + 2 * RET_QK + 2 * RET_V
EV_MIX = LRU_WIDTH + RET_V
POOL_WINDOWS = (2, 4, 8, 16)
POOL_GROUP = 128
POOL_WIDTH = POOL_GROUP * len(POOL_WINDOWS)
ATT_HEADS = 12
KV_HEADS = 4
GROUP = ATT_HEADS // KV_HEADS
HEAD_DIM = 128
ATT_QW = ATT_HEADS * HEAD_DIM
ATT_KVW = KV_HEADS * HEAD_DIM
Q_BLOCK = 128
ROPE_THETA = 10000.0
OD_IN = POOL_WIDTH + ATT_QW + 2 * ATT_KVW
OD_MIX = POOL_WIDTH + ATT_QW

kernel_name = "hybrid_lru_retention_pool_gqa_diffusion_block"


def rmsnorm(x, g):
    xf = x.astype(jnp.float32)
    y = xf * lax.rsqrt(jnp.mean(xf * xf, axis=-1, keepdims=True) + EPS)
    return (y * g.astype(jnp.float32)).astype(x.dtype)


def modulate(x, g, shift, scale):
    return rmsnorm(x, g) * (1 + scale) + shift


def residual(x, y, g_post, gate, w):
    return x + w * gate * rmsnorm(y, g_post)


def swiglu(h, w_gate, w_up, w_down):
    return (jax.nn.silu(h @ w_gate) * (h @ w_up)) @ w_down


def split_heads(t, h, d):
    return t.reshape(t.shape[0], t.shape[1], h, d)


def apply_rotary(x, cos, sin):
    x1, x2 = jnp.split(x, 2, axis=-1)
    c = cos[:, None, :]
    s = sin[:, None, :]
    return jnp.concatenate([x1 * c - x2 * s, x1 * s + x2 * c], axis=-1).astype(x.dtype)


def dwconv(x, w, b):
    L = x.shape[1]
    xp = jnp.pad(x, ((0, 0), (CONV_LEFT, CONV_W - 1 - CONV_LEFT), (0, 0)))
    y = xp[:, 0:L] * w[0]
    for k in range(1, CONV_W):
        y = y + xp[:, k:k + L] * w[k]
    return y + b


def blockdiag(x, w, b):
    B_, L, _ = x.shape
    xb = x.reshape(B_, L, LRU_BLOCKS, LRU_BLOCK)
    return jnp.einsum('blnc,ncd->blnd', xb, w).reshape(B_, L, LRU_WIDTH) + b


def linear_scan(a, b, h0):
    def comb(l, r):
        return r[0] * l[0], r[0] * l[1] + r[1]
    a_cum, b_cum = lax.associative_scan(comb, (a, b), axis=1)
    if h0 is None:
        return b_cum
    return b_cum + a_cum * h0[:, None, :]


def rglru_coeffs(u, wa, ba, wx, bx, lam):
    r = jax.nn.sigmoid(blockdiag(u, wa, ba))
    i = jax.nn.sigmoid(blockdiag(u, wx, bx))
    log_a = -LRU_C * r * jax.nn.softplus(-lam.astype(jnp.float32))
    a = jnp.exp(log_a)
    bterm = jnp.sqrt(-jnp.expm1(2.0 * log_a)) * (i * u)
    return a, bterm


def rglru_bidir(ul, uc, wa, ba, wx, bx, lam):
    outs_l, outs_c = [], []
    for d in range(2):
        al, bl = rglru_coeffs(ul, wa[d], ba[d], wx[d], bx[d], lam[d])
        ac, bc = rglru_coeffs(uc, wa[d], ba[d], wx[d], bx[d], lam[d])
        if d == 1:
            al, bl, ac, bc = [jnp.flip(t, axis=1) for t in (al, bl, ac, bc)]
        h_c = linear_scan(ac, bc, None)
        h_l = linear_scan(al, bl, h_c[:, -1])
        if d == 1:
            h_c = jnp.flip(h_c, axis=1)
            h_l = jnp.flip(h_l, axis=1)
        outs_l.append(h_l)
        outs_c.append(h_c)
    return outs_l[0] + outs_l[1], outs_c[0] + outs_c[1]


def retention_scan(q, k, v, log_g, s0):
    B_, H, L, _ = q.shape
    C = RET_CHUNK
    n = L // C
    idx = jnp.arange(C, dtype=jnp.float32)
    diff = idx[:, None] - idx[None, :]
    lg = log_g[:, None, None]
    intra = jnp.where(diff >= 0, jnp.exp(lg * jnp.maximum(diff, 0.0)), 0.0)
    q_dec = jnp.exp(log_g[:, None] * (idx + 1.0))
    k_dec = jnp.exp(log_g[:, None] * (C - 1.0 - idx))
    s_dec = jnp.exp(log_g * C)

    def to_chunks(t):
        return t.reshape(B_, H, n, C, t.shape[-1]).transpose(2, 0, 1, 3, 4)

    def step(s, qkv):
        qc, kc, vc = qkv
        scores = jnp.einsum('bhid,bhjd->bhij', qc, kc) * intra
        o = (jnp.einsum('bhij,bhje->bhie', scores, vc)
             + jnp.einsum('bhid,bhde->bhie', qc * q_dec[..., None], s))
        s_new = s * s_dec[:, None, None] + jnp.einsum('bhjd,bhje->bhde', kc * k_dec[..., None], vc)
        return s_new, o

    s_fin, o = lax.scan(step, s0, (to_chunks(q), to_chunks(k), to_chunks(v)))
    o = o.transpose(1, 2, 0, 3, 4).reshape(B_, H, L, o.shape[-1])
    return o, s_fin


def retention_bidir(ql, kl, vl, qc, kc, vc, log_g):
    B_, H = ql.shape[0], ql.shape[1]
    s0 = jnp.zeros((B_, H, RET_DK, RET_DV), jnp.float32)
    outs_l, outs_c = [], []
    for d in range(2):
        seqs = (ql, kl, vl, qc, kc, vc)
        if d == 1:
            seqs = tuple(jnp.flip(t, axis=2) for t in seqs)
        o_c, s_c = retention_scan(seqs[3], seqs[4], seqs[5], log_g[d], s0)
        o_l, _ = retention_scan(seqs[0], seqs[1], seqs[2], log_g[d], s_c)
        if d == 1:
            o_c = jnp.flip(o_c, axis=2)
            o_l = jnp.flip(o_l, axis=2)
        outs_l.append(o_l)
        outs_c.append(o_c)
    return outs_l[0] + outs_l[1], outs_c[0] + outs_c[1]


def head_groupnorm(o, g):
    mu = jnp.mean(o, axis=-1, keepdims=True)
    var = jnp.mean(jnp.square(o - mu), axis=-1, keepdims=True)
    y = (o - mu) * lax.rsqrt(var + EPS)
    B_, H, L, dv = y.shape
    return y.transpose(0, 2, 1, 3).reshape(B_, L, H * dv) * g


def to_bhld(t):
    return t.astype(jnp.float32).transpose(0, 2, 1, 3)


def even_mixer(hl, hc, w_in, w_out, conv_w, conv_b, wa, ba, wx, bx, lam, decay_logit, gn_g, cos, sin):
    splits = [LRU_WIDTH, 2 * LRU_WIDTH, 2 * LRU_WIDTH + RET_QK, 2 * LRU_WIDTH + 2 * RET_QK,
              2 * LRU_WIDTH + 2 * RET_QK + RET_V]
    gl, rl, ql, kl, vl, ol = jnp.split(hl @ w_in, splits, axis=-1)
    gc, rc, qc, kc, vc, oc = jnp.split(hc @ w_in, splits, axis=-1)
    ul = dwconv(rl, conv_w, conv_b).astype(jnp.float32)
    uc = dwconv(rc, conv_w, conv_b).astype(jnp.float32)
    hl_lru, hc_lru = rglru_bidir(ul, uc, wa, ba, wx, bx, lam)
    lru_l = jax.nn.gelu(gl.astype(jnp.float32)) * hl_lru
    lru_c = jax.nn.gelu(gc.astype(jnp.float32)) * hc_lru
    k_scale = RET_DK ** -0.5
    ql = to_bhld(apply_rotary(split_heads(ql, RET_HEADS, RET_DK), cos, sin))
    kl = to_bhld(apply_rotary(split_heads(kl, RET_HEADS, RET_DK), cos, sin)) * k_scale
    vl = to_bhld(split_heads(vl, RET_HEADS, RET_DV))
    qc = to_bhld(split_heads(qc, RET_HEADS, RET_DK))
    kc = to_bhld(split_heads(kc, RET_HEADS, RET_DK)) * k_scale
    vc = to_bhld(split_heads(vc, RET_HEADS, RET_DV))
    log_g = -jax.nn.softplus(-decay_logit.astype(jnp.float32))
    rl_out, rc_out = retention_bidir(ql, kl, vl, qc, kc, vc, log_g)
    ret_l = head_groupnorm(rl_out, gn_g) * jax.nn.silu(ol.astype(jnp.float32))
    ret_c = head_groupnorm(rc_out, gn_g) * jax.nn.silu(oc.astype(jnp.float32))
    yl = jnp.concatenate([lru_l, ret_l], axis=-1).astype(hl.dtype) @ w_out
    yc = jnp.concatenate([lru_c, ret_c], axis=-1).astype(hc.dtype) @ w_out
    return yl, yc


def multiscale_pool(x, pool_w, pool_scale):
    B_, L, _ = x.shape
    xf = x.astype(jnp.float32)
    cs = jnp.concatenate([jnp.zeros((B_, 1, POOL_WIDTH), jnp.float32), jnp.cumsum(xf, axis=1)], axis=1)
    t = jnp.arange(L)
    outs = []
    for gi, w in enumerate(POOL_WINDOWS):
        lo = jnp.clip(t - w // 2, 0, L)
        hi = jnp.clip(t + w // 2, 0, L)
        sl = slice(gi * POOL_GROUP, (gi + 1) * POOL_GROUP)
        csg = cs[..., sl]
        cnt = (hi - lo).astype(jnp.float32)[None, :, None]
        mean = (csg[:, hi] - csg[:, lo]) / cnt
        outs.append(jnp.einsum('blc,cd->bld', mean - xf[..., sl], pool_w[gi].astype(jnp.float32)))
    return (jnp.concatenate(outs, axis=-1) * pool_scale).astype(x.dtype)


def attend(q, k, v):
    B_, Lq = q.shape[0], q.shape[1]
    nb = Lq // Q_BLOCK
    qb = q.reshape(B_, nb, Q_BLOCK, KV_HEADS, GROUP, HEAD_DIM).transpose(1, 0, 3, 4, 2, 5)
    kt = k.transpose(0, 2, 1, 3)
    vt = v.transpose(0, 2, 1, 3)
    scale = HEAD_DIM ** -0.5

    def blk(qi):
        s = jnp.einsum('bkgqd,bksd->bkgqs', qi, kt, preferred_element_type=jnp.float32) * scale
        p = jax.nn.softmax(s, axis=-1)
        return jnp.einsum('bkgqs,bksd->bkgqd', p.astype(vt.dtype), vt)

    o = lax.map(blk, qb)
    return o.transpose(1, 0, 4, 2, 3, 5).reshape(B_, Lq, ATT_QW)


def odd_mixer(hl, hc, w_in, w_out, pool_w, pool_scale, q_g, k_g, cos, sin, with_ctx):
    splits = [POOL_WIDTH, POOL_WIDTH + ATT_QW, POOL_WIDTH + ATT_QW + ATT_KVW]
    pool_l, ql, kl, vl = jnp.split(hl @ w_in, splits, axis=-1)
    ql = apply_rotary(rmsnorm(split_heads(ql, ATT_HEADS, HEAD_DIM), q_g), cos, sin)
    kl = apply_rotary(rmsnorm(split_heads(kl, KV_HEADS, HEAD_DIM), k_g), cos, sin)
    vl = split_heads(vl, KV_HEADS, HEAD_DIM)
    if with_ctx:
        pool_c, qc, kc, vc = jnp.split(hc @ w_in, splits, axis=-1)
    else:
        kc, vc = jnp.split(hc @ w_in[:, POOL_WIDTH + ATT_QW:], [ATT_KVW], axis=-1)
    kc = rmsnorm(split_heads(kc, KV_HEADS, HEAD_DIM), k_g)
    vc = split_heads(vc, KV_HEADS, HEAD_DIM)
    att_l = attend(ql, jnp.concatenate([kc, kl], axis=1), jnp.concatenate([vc, vl], axis=1))
    yl = jnp.concatenate([multiscale_pool(pool_l, pool_w, pool_scale), att_l.astype(hl.dtype)], axis=-1) @ w_out
    if not with_ctx:
        return yl, None
    qc = rmsnorm(split_heads(qc, ATT_HEADS, HEAD_DIM), q_g)
    att_c = attend(qc, kc, vc)
    yc = jnp.concatenate([multiscale_pool(pool_c, pool_w, pool_scale), att_c.astype(hc.dtype)], axis=-1) @ w_out
    return yl, yc


def _fwd_setup_inputs(seed: int = 0) -> dict:
    key = jax.random.key(seed)
    ks = list(jax.random.split(key, 28))

    def nrm(i, shape, s):
        return jax.random.normal(ks[i], shape, jnp.float32) * s

    D = D_MODEL
    u = jax.random.uniform(ks[19], (N_EVEN, 2, LRU_WIDTH), jnp.float32, 0.9, 0.999)
    a = u ** (1.0 / LRU_C)
    lam = jnp.log(a) - jnp.log1p(-a)
    gam = 1.0 - 2.0 ** (-5.0 - jnp.arange(RET_HEADS, dtype=jnp.float32))
    decay_base = jnp.log(gam) - jnp.log1p(-gam)
    return {
        "x": nrm(0, (BATCH, SEQ, D), 1.0),
        "c": nrm(1, (BATCH, D), 1.0),
        "ctx": nrm(2, (BATCH, CTX_LEN, D), 1.0),
        "c_ctx": nrm(3, (D,), 1.0),
        "mod_w": nrm(4, (DEPTH, D, 9 * D), 0.5 * D ** -0.5),
        "mod_b": nrm(5, (DEPTH, 9 * D), 0.02),
        "norm_pre": 1.0 + nrm(6, (DEPTH, 3, D), 0.02),
        "norm_post": 1.0 + nrm(7, (DEPTH, 3, D), 0.02),
        "ffn_gate": nrm(8, (DEPTH, 2, D, D_FF), D ** -0.5),
        "ffn_up": nrm(9, (DEPTH, 2, D, D_FF), D ** -0.5),
        "ffn_down": nrm(10, (DEPTH, 2, D_FF, D), D_FF ** -0.5),
        "ev_w_in": nrm(11, (N_EVEN, D, EV_IN), D ** -0.5),
        "ev_w_out": nrm(12, (N_EVEN, EV_MIX, D), EV_MIX ** -0.5),
        "lru_conv_w": nrm(13, (N_EVEN, CONV_W, LRU_WIDTH), CONV_W ** -0.5),
        "lru_conv_b": nrm(14, (N_EVEN, LRU_WIDTH), 0.01),
        "lru_wa": nrm(15, (N_EVEN, 2, LRU_BLOCKS, LRU_BLOCK, LRU_BLOCK), LRU_BLOCK ** -0.5),
        "lru_ba": nrm(16, (N_EVEN, 2, LRU_WIDTH), 0.01),
        "lru_wx": nrm(17, (N_EVEN, 2, LRU_BLOCKS, LRU_BLOCK, LRU_BLOCK), LRU_BLOCK ** -0.5),
        "lru_bx": nrm(18, (N_EVEN, 2, LRU_WIDTH), 0.01),
        "lru_lambda": lam,
        "ret_decay_logit": decay_base + nrm(20, (N_EVEN, 2, RET_HEADS), 0.05),
        "ret_gn": 1.0 + nrm(21, (N_EVEN, RET_V), 0.02),
        "od_w_in": nrm(22, (N_ODD, D, OD_IN), D ** -0.5),
        "od_w_out": nrm(23, (N_ODD, OD_MIX, D), OD_MIX ** -0.5),
        "pool_w": nrm(24, (N_ODD, len(POOL_WINDOWS), POOL_GROUP, POOL_GROUP), POOL_GROUP ** -0.5),
        "pool_scale": 1.0 + nrm(25, (N_ODD, POOL_WIDTH), 0.1),
        "q_norm": 1.0 + nrm(26, (N_ODD, HEAD_DIM), 0.02),
        "k_norm": 1.0 + nrm(27, (N_ODD, HEAD_DIM), 0.02),
    }


def _fwd_reference(x, c, ctx, c_ctx, mod_w, mod_b, norm_pre, norm_post, ffn_gate, ffn_up, ffn_down,
              ev_w_in, ev_w_out, lru_conv_w, lru_conv_b, lru_wa, lru_ba, lru_wx, lru_bx, lru_lambda,
              ret_decay_logit, ret_gn, od_w_in, od_w_out, pool_w, pool_scale, q_norm, k_norm):
    D = D_MODEL
    S = x.shape[1]
    rows = S // GRID_W
    row = jnp.repeat(jnp.arange(rows, dtype=jnp.float32), GRID_W)
    col = jnp.tile(jnp.arange(GRID_W, dtype=jnp.float32), rows)
    n_ax = HEAD_DIM // 4
    f_ax = ROPE_THETA ** (-jnp.arange(n_ax, dtype=jnp.float32) / n_ax)
    ang2 = jnp.concatenate([row[:, None] * f_ax, col[:, None] * f_ax], axis=-1)
    cos2, sin2 = jnp.cos(ang2), jnp.sin(ang2)
    n_r = RET_DK // 2
    f_r = RET_THETA ** (-jnp.arange(n_r, dtype=jnp.float32) / n_r)
    ang1 = jnp.arange(S, dtype=jnp.float32)[:, None] * f_r
    cos1, sin1 = jnp.cos(ang1), jnp.sin(ang1)

    sc = jax.nn.silu(c)
    scc = jax.nn.silu(c_ctx)
    xl, xc = x, ctx
    for li in range(DEPTH):
        last = li == DEPTH - 1
        n_ctx_sub = 2 if last else 3
        mod_l = (sc @ mod_w[li] + mod_b[li]).reshape(-1, 3, 3, 1, D)
        mod_c = (scc @ mod_w[li][:, :n_ctx_sub * 3 * D] + mod_b[li, :n_ctx_sub * 3 * D]).reshape(n_ctx_sub, 3, D)

        hl = modulate(xl, norm_pre[li, 0], mod_l[:, 0, 0], mod_l[:, 0, 1])
        hc = modulate(xc, norm_pre[li, 0], mod_c[0, 0], mod_c[0, 1])
        xl = residual(xl, swiglu(hl, ffn_gate[li, 0], ffn_up[li, 0], ffn_down[li, 0]), norm_post[li, 0], mod_l[:, 0, 2], FFN_STEP)
        xc = residual(xc, swiglu(hc, ffn_gate[li, 0], ffn_up[li, 0], ffn_down[li, 0]), norm_post[li, 0], mod_c[0, 2], FFN_STEP)

        hl = modulate(xl, norm_pre[li, 1], mod_l[:, 1, 0], mod_l[:, 1, 1])
        hc = modulate(xc, norm_pre[li, 1], mod_c[1, 0], mod_c[1, 1])
        if li % 2 == 0:
            e = li // 2
            yl, yc = even_mixer(hl, hc, ev_w_in[e], ev_w_out[e], lru_conv_w[e], lru_conv_b[e],
                                lru_wa[e], lru_ba[e], lru_wx[e], lru_bx[e], lru_lambda[e],
                                ret_decay_logit[e], ret_gn[e], cos1, sin1)
        else:
            o = li // 2
            yl, yc = odd_mixer(hl, hc, od_w_in[o], od_w_out[o], pool_w[o], pool_scale[o],
                               q_norm[o], k_norm[o], cos2, sin2, not last)
        xl = residual(xl, yl, norm_post[li, 1], mod_l[:, 1, 2], 1.0)
        if not last:
            xc = residual(xc, yc, norm_post[li, 1], mod_c[1, 2], 1.0)

        hl = modulate(xl, norm_pre[li, 2], mod_l[:, 2, 0], mod_l[:, 2, 1])
        xl = residual(xl, swiglu(hl, ffn_gate[li, 1], ffn_up[li, 1], ffn_down[li, 1]), norm_post[li, 2], mod_l[:, 2, 2], FFN_STEP)
        if not last:
            hc = modulate(xc, norm_pre[li, 2], mod_c[2, 0], mod_c[2, 1])
            xc = residual(xc, swiglu(hc, ffn_gate[li, 1], ffn_up[li, 1], ffn_down[li, 1]), norm_post[li, 2], mod_c[2, 2], FFN_STEP)
    return xl


import jax as _jax
import jax.numpy as _jnp

TWIN_FORMAT = 'train_step'
FWD_PARAMS = ['x', 'c', 'ctx', 'c_ctx', 'mod_w', 'mod_b', 'norm_pre', 'norm_post', 'ffn_gate', 'ffn_up', 'ffn_down', 'ev_w_in', 'ev_w_out', 'lru_conv_w', 'lru_conv_b', 'lru_wa', 'lru_ba', 'lru_wx', 'lru_bx', 'lru_lambda', 'ret_decay_logit', 'ret_gn', 'od_w_in', 'od_w_out', 'pool_w', 'pool_scale', 'q_norm', 'k_norm']
TWIN_WEIGHTS = ['c_ctx', 'mod_w', 'mod_b', 'norm_pre', 'norm_post', 'ffn_gate', 'ffn_up', 'ffn_down', 'ev_w_in', 'ev_w_out', 'lru_conv_w', 'lru_conv_b', 'lru_wa', 'lru_ba', 'lru_wx', 'lru_bx', 'lru_lambda', 'ret_decay_logit', 'ret_gn', 'od_w_in', 'od_w_out', 'pool_w', 'pool_scale', 'q_norm', 'k_norm']
TWIN_DIFF_INPUT = 'x'
TWIN_INPUTS = ['x', 'c', 'ctx', 'c_ctx', 'mod_w', 'mod_b', 'norm_pre', 'norm_post', 'ffn_gate', 'ffn_up', 'ffn_down', 'ev_w_in', 'ev_w_out', 'lru_conv_w', 'lru_conv_b', 'lru_wa', 'lru_ba', 'lru_wx', 'lru_bx', 'lru_lambda', 'ret_decay_logit', 'ret_gn', 'od_w_in', 'od_w_out', 'pool_w', 'pool_scale', 'q_norm', 'k_norm', 'loss_target', 'm_c_ctx', 'm_mod_w', 'm_mod_b', 'm_norm_pre', 'm_norm_post', 'm_ffn_gate', 'm_ffn_up', 'm_ffn_down', 'm_ev_w_in', 'm_ev_w_out', 'm_lru_conv_w', 'm_lru_conv_b', 'm_lru_wa', 'm_lru_ba', 'm_lru_wx', 'm_lru_bx', 'm_lru_lambda', 'm_ret_decay_logit', 'm_ret_gn', 'm_od_w_in', 'm_od_w_out', 'm_pool_w', 'm_pool_scale', 'm_q_norm', 'm_k_norm', 'v_c_ctx', 'v_mod_w', 'v_mod_b', 'v_norm_pre', 'v_norm_post', 'v_ffn_gate', 'v_ffn_up', 'v_ffn_down', 'v_ev_w_in', 'v_ev_w_out', 'v_lru_conv_w', 'v_lru_conv_b', 'v_lru_wa', 'v_lru_ba', 'v_lru_wx', 'v_lru_bx', 'v_lru_lambda', 'v_ret_decay_logit', 'v_ret_gn', 'v_od_w_in', 'v_od_w_out', 'v_pool_w', 'v_pool_scale', 'v_q_norm', 'v_k_norm']
TWIN_OUTPUTS = ['loss', 'grad_x', 'grad_c_ctx', 'grad_mod_w', 'grad_mod_b', 'grad_norm_pre', 'grad_norm_post', 'grad_ffn_gate', 'grad_ffn_up', 'grad_ffn_down', 'grad_ev_w_in', 'grad_ev_w_out', 'grad_lru_conv_w', 'grad_lru_conv_b', 'grad_lru_wa', 'grad_lru_ba', 'grad_lru_wx', 'grad_lru_bx', 'grad_lru_lambda', 'grad_ret_decay_logit', 'grad_ret_gn', 'grad_od_w_in', 'grad_od_w_out', 'grad_pool_w', 'grad_pool_scale', 'grad_q_norm', 'grad_k_norm', 'delta_c_ctx', 'delta_mod_w', 'delta_mod_b', 'delta_norm_pre', 'delta_norm_post', 'delta_ffn_gate', 'delta_ffn_up', 'delta_ffn_down', 'delta_ev_w_in', 'delta_ev_w_out', 'delta_lru_conv_w', 'delta_lru_conv_b', 'delta_lru_wa', 'delta_lru_ba', 'delta_lru_wx', 'delta_lru_bx', 'delta_lru_lambda', 'delta_ret_decay_logit', 'delta_ret_gn', 'delta_od_w_in', 'delta_od_w_out', 'delta_pool_w', 'delta_pool_scale', 'delta_q_norm', 'delta_k_norm', 'new_m_c_ctx', 'new_m_mod_w', 'new_m_mod_b', 'new_m_norm_pre', 'new_m_norm_post', 'new_m_ffn_gate', 'new_m_ffn_up', 'new_m_ffn_down', 'new_m_ev_w_in', 'new_m_ev_w_out', 'new_m_lru_conv_w', 'new_m_lru_conv_b', 'new_m_lru_wa', 'new_m_lru_ba', 'new_m_lru_wx', 'new_m_lru_bx', 'new_m_lru_lambda', 'new_m_ret_decay_logit', 'new_m_ret_gn', 'new_m_od_w_in', 'new_m_od_w_out', 'new_m_pool_w', 'new_m_pool_scale', 'new_m_q_norm', 'new_m_k_norm', 'new_v_c_ctx', 'new_v_mod_w', 'new_v_mod_b', 'new_v_norm_pre', 'new_v_norm_post', 'new_v_ffn_gate', 'new_v_ffn_up', 'new_v_ffn_down', 'new_v_ev_w_in', 'new_v_ev_w_out', 'new_v_lru_conv_w', 'new_v_lru_conv_b', 'new_v_lru_wa', 'new_v_lru_ba', 'new_v_lru_wx', 'new_v_lru_bx', 'new_v_lru_lambda', 'new_v_ret_decay_logit', 'new_v_ret_gn', 'new_v_od_w_in', 'new_v_od_w_out', 'new_v_pool_w', 'new_v_pool_scale', 'new_v_q_norm', 'new_v_k_norm']
TWIN_LEAF_KINDS = {'loss': 'loss', 'grad_x': 'grad_x', 'grad_c_ctx': 'grad_w', 'grad_mod_w': 'grad_w', 'grad_mod_b': 'grad_w', 'grad_norm_pre': 'grad_w', 'grad_norm_post': 'grad_w', 'grad_ffn_gate': 'grad_w', 'grad_ffn_up': 'grad_w', 'grad_ffn_down': 'grad_w', 'grad_ev_w_in': 'grad_w', 'grad_ev_w_out': 'grad_w', 'grad_lru_conv_w': 'grad_w', 'grad_lru_conv_b': 'grad_w', 'grad_lru_wa': 'grad_w', 'grad_lru_ba': 'grad_w', 'grad_lru_wx': 'grad_w', 'grad_lru_bx': 'grad_w', 'grad_lru_lambda': 'grad_w', 'grad_ret_decay_logit': 'grad_w', 'grad_ret_gn': 'grad_w', 'grad_od_w_in': 'grad_w', 'grad_od_w_out': 'grad_w', 'grad_pool_w': 'grad_w', 'grad_pool_scale': 'grad_w', 'grad_q_norm': 'grad_w', 'grad_k_norm': 'grad_w', 'delta_c_ctx': 'delta_w', 'delta_mod_w': 'delta_w', 'delta_mod_b': 'delta_w', 'delta_norm_pre': 'delta_w', 'delta_norm_post': 'delta_w', 'delta_ffn_gate': 'delta_w', 'delta_ffn_up': 'delta_w', 'delta_ffn_down': 'delta_w', 'delta_ev_w_in': 'delta_w', 'delta_ev_w_out': 'delta_w', 'delta_lru_conv_w': 'delta_w', 'delta_lru_conv_b': 'delta_w', 'delta_lru_wa': 'delta_w', 'delta_lru_ba': 'delta_w', 'delta_lru_wx': 'delta_w', 'delta_lru_bx': 'delta_w', 'delta_lru_lambda': 'delta_w', 'delta_ret_decay_logit': 'delta_w', 'delta_ret_gn': 'delta_w', 'delta_od_w_in': 'delta_w', 'delta_od_w_out': 'delta_w', 'delta_pool_w': 'delta_w', 'delta_pool_scale': 'delta_w', 'delta_q_norm': 'delta_w', 'delta_k_norm': 'delta_w', 'new_m_c_ctx': 'new_m', 'new_m_mod_w': 'new_m', 'new_m_mod_b': 'new_m', 'new_m_norm_pre': 'new_m', 'new_m_norm_post': 'new_m', 'new_m_ffn_gate': 'new_m', 'new_m_ffn_up': 'new_m', 'new_m_ffn_down': 'new_m', 'new_m_ev_w_in': 'new_m', 'new_m_ev_w_out': 'new_m', 'new_m_lru_conv_w': 'new_m', 'new_m_lru_conv_b': 'new_m', 'new_m_lru_wa': 'new_m', 'new_m_lru_ba': 'new_m', 'new_m_lru_wx': 'new_m', 'new_m_lru_bx': 'new_m', 'new_m_lru_lambda': 'new_m', 'new_m_ret_decay_logit': 'new_m', 'new_m_ret_gn': 'new_m', 'new_m_od_w_in': 'new_m', 'new_m_od_w_out': 'new_m', 'new_m_pool_w': 'new_m', 'new_m_pool_scale': 'new_m', 'new_m_q_norm': 'new_m', 'new_m_k_norm': 'new_m', 'new_v_c_ctx': 'new_v', 'new_v_mod_w': 'new_v', 'new_v_mod_b': 'new_v', 'new_v_norm_pre': 'new_v', 'new_v_norm_post': 'new_v', 'new_v_ffn_gate': 'new_v', 'new_v_ffn_up': 'new_v', 'new_v_ffn_down': 'new_v', 'new_v_ev_w_in': 'new_v', 'new_v_ev_w_out': 'new_v', 'new_v_lru_conv_w': 'new_v', 'new_v_lru_conv_b': 'new_v', 'new_v_lru_wa': 'new_v', 'new_v_lru_ba': 'new_v', 'new_v_lru_wx': 'new_v', 'new_v_lru_bx': 'new_v', 'new_v_lru_lambda': 'new_v', 'new_v_ret_decay_logit': 'new_v', 'new_v_ret_gn': 'new_v', 'new_v_od_w_in': 'new_v', 'new_v_od_w_out': 'new_v', 'new_v_pool_w': 'new_v', 'new_v_pool_scale': 'new_v', 'new_v_q_norm': 'new_v', 'new_v_k_norm': 'new_v'}


def _forward(args):
    return _fwd_reference(*[args[k] for k in FWD_PARAMS])


def _output_shape():
    def fwd():
        inp = _fwd_setup_inputs(0)
        return _fwd_reference(*[inp[k] for k in FWD_PARAMS])
    out = _jax.eval_shape(fwd)
    return out.shape, out.dtype

N_MICROBATCH = 1
ADAM_LR = 0.001
ADAM_B1 = 0.9
ADAM_B2 = 0.999
ADAM_EPS = 1e-08
ADAM_WD = 0.01
ADAM_STEP = 10
PER_EXAMPLE_BATCH_AXIS = {'x': 0, 'c': 0, 'ctx': 0, 'loss_target': 0}
SHARED_INPUTS = []
_WEIGHT_DTYPES = {'c_ctx': _jnp.float32, 'mod_w': _jnp.float32, 'mod_b': _jnp.float32, 'norm_pre': _jnp.float32, 'norm_post': _jnp.float32, 'ffn_gate': _jnp.float32, 'ffn_up': _jnp.float32, 'ffn_down': _jnp.float32, 'ev_w_in': _jnp.float32, 'ev_w_out': _jnp.float32, 'lru_conv_w': _jnp.float32, 'lru_conv_b': _jnp.float32, 'lru_wa': _jnp.float32, 'lru_ba': _jnp.float32, 'lru_wx': _jnp.float32, 'lru_bx': _jnp.float32, 'lru_lambda': _jnp.float32, 'ret_decay_logit': _jnp.float32, 'ret_gn': _jnp.float32, 'od_w_in': _jnp.float32, 'od_w_out': _jnp.float32, 'pool_w': _jnp.float32, 'pool_scale': _jnp.float32, 'q_norm': _jnp.float32, 'k_norm': _jnp.float32}
MOMENT_SCALE = {'c_ctx': 5.823146e-02, 'mod_w': 8.155558e-01, 'mod_b': 1.672927e+00, 'norm_pre': 6.829502e-02, 'norm_post': 2.003170e+00, 'ffn_gate': 2.218141e-02, 'ffn_up': 2.734773e-02, 'ffn_down': 4.532654e-02, 'ev_w_in': 1.935302e-01, 'ev_w_out': 3.545279e-01, 'lru_conv_w': 5.236172e-01, 'lru_conv_b': 1.639706e+00, 'lru_wa': 2.276786e-02, 'lru_ba': 3.458205e-02, 'lru_wx': 5.005542e-02, 'lru_bx': 9.600940e-02, 'lru_lambda': 9.115402e-02, 'ret_decay_logit': 2.167010e-01, 'ret_gn': 3.062200e-02, 'od_w_in': 3.141442e-01, 'od_w_out': 3.969568e-01, 'pool_w': 1.505293e-01, 'pool_scale': 1.703165e-01, 'q_norm': 3.995712e-02, 'k_norm': 3.942975e-02}


def _to_microbatches(a, axis):
    t = _jnp.moveaxis(a, axis, 0)
    t = t.reshape((N_MICROBATCH, t.shape[0] // N_MICROBATCH) + t.shape[1:])
    return _jnp.moveaxis(t, 1, axis + 1)


def setup_inputs(seed: int = 0) -> dict:
    inp = _fwd_setup_inputs(seed)
    key = _jax.random.fold_in(_jax.random.key(seed), 7919)
    shape, _ = _output_shape()
    out = dict(inp)
    out["loss_target"] = _jax.random.normal(_jax.random.fold_in(key, 0), shape, _jnp.float32)
    for i, name in enumerate(TWIN_WEIGHTS):
        w = inp[name].astype(_jnp.float32)
        if MOMENT_SCALE is None:
            s = _jnp.sqrt(_jnp.mean(_jnp.square(w)) + 1e-30)
        else:
            s = MOMENT_SCALE[name]
        km, kv = _jax.random.split(_jax.random.fold_in(key, i + 1))
        out[name] = w
        out["m_" + name] = s * _jax.random.normal(km, w.shape, _jnp.float32)
        out["v_" + name] = (s * s) * _jax.random.uniform(kv, w.shape, _jnp.float32, 0.5, 1.5)
    if N_MICROBATCH > 1:
        for name, axis in PER_EXAMPLE_BATCH_AXIS.items():
            out[name] = _to_microbatches(out[name], axis)
    return {'x': out['x'], 'c': out['c'], 'ctx': out['ctx'], 'c_ctx': out['c_ctx'], 'mod_w': out['mod_w'], 'mod_b': out['mod_b'], 'norm_pre': out['norm_pre'], 'norm_post': out['norm_post'], 'ffn_gate': out['ffn_gate'], 'ffn_up': out['ffn_up'], 'ffn_down': out['ffn_down'], 'ev_w_in': out['ev_w_in'], 'ev_w_out': out['ev_w_out'], 'lru_conv_w': out['lru_conv_w'], 'lru_conv_b': out['lru_conv_b'], 'lru_wa': out['lru_wa'], 'lru_ba': out['lru_ba'], 'lru_wx': out['lru_wx'], 'lru_bx': out['lru_bx'], 'lru_lambda': out['lru_lambda'], 'ret_decay_logit': out['ret_decay_logit'], 'ret_gn': out['ret_gn'], 'od_w_in': out['od_w_in'], 'od_w_out': out['od_w_out'], 'pool_w': out['pool_w'], 'pool_scale': out['pool_scale'], 'q_norm': out['q_norm'], 'k_norm': out['k_norm'], 'loss_target': out['loss_target'], 'm_c_ctx': out['m_c_ctx'], 'm_mod_w': out['m_mod_w'], 'm_mod_b': out['m_mod_b'], 'm_norm_pre': out['m_norm_pre'], 'm_norm_post': out['m_norm_post'], 'm_ffn_gate': out['m_ffn_gate'], 'm_ffn_up': out['m_ffn_up'], 'm_ffn_down': out['m_ffn_down'], 'm_ev_w_in': out['m_ev_w_in'], 'm_ev_w_out': out['m_ev_w_out'], 'm_lru_conv_w': out['m_lru_conv_w'], 'm_lru_conv_b': out['m_lru_conv_b'], 'm_lru_wa': out['m_lru_wa'], 'm_lru_ba': out['m_lru_ba'], 'm_lru_wx': out['m_lru_wx'], 'm_lru_bx': out['m_lru_bx'], 'm_lru_lambda': out['m_lru_lambda'], 'm_ret_decay_logit': out['m_ret_decay_logit'], 'm_ret_gn': out['m_ret_gn'], 'm_od_w_in': out['m_od_w_in'], 'm_od_w_out': out['m_od_w_out'], 'm_pool_w': out['m_pool_w'], 'm_pool_scale': out['m_pool_scale'], 'm_q_norm': out['m_q_norm'], 'm_k_norm': out['m_k_norm'], 'v_c_ctx': out['v_c_ctx'], 'v_mod_w': out['v_mod_w'], 'v_mod_b': out['v_mod_b'], 'v_norm_pre': out['v_norm_pre'], 'v_norm_post': out['v_norm_post'], 'v_ffn_gate': out['v_ffn_gate'], 'v_ffn_up': out['v_ffn_up'], 'v_ffn_down': out['v_ffn_down'], 'v_ev_w_in': out['v_ev_w_in'], 'v_ev_w_out': out['v_ev_w_out'], 'v_lru_conv_w': out['v_lru_conv_w'], 'v_lru_conv_b': out['v_lru_conv_b'], 'v_lru_wa': out['v_lru_wa'], 'v_lru_ba': out['v_lru_ba'], 'v_lru_wx': out['v_lru_wx'], 'v_lru_bx': out['v_lru_bx'], 'v_lru_lambda': out['v_lru_lambda'], 'v_ret_decay_logit': out['v_ret_decay_logit'], 'v_ret_gn': out['v_ret_gn'], 'v_od_w_in': out['v_od_w_in'], 'v_od_w_out': out['v_od_w_out'], 'v_pool_w': out['v_pool_w'], 'v_pool_scale': out['v_pool_scale'], 'v_q_norm': out['v_q_norm'], 'v_k_norm': out['v_k_norm']}


def _loss(weights, diff, rest, loss_target):
    with _jax.named_scope("forward"):
        args = {**rest, TWIN_DIFF_INPUT: diff, **{k: w.astype(_WEIGHT_DTYPES[k]) for k, w in weights.items()}}
        y = _forward(args)
    with _jax.named_scope("loss_head"):
        err = _jnp.square(y.astype(_jnp.float32) - loss_target)
        return 0.5 * _jnp.sum(_jnp.mean(err, axis=-1)) if err.ndim else 0.5 * err


def _adamw(w, g, m, v):
    m = ADAM_B1 * m + (1.0 - ADAM_B1) * g
    v = ADAM_B2 * v + (1.0 - ADAM_B2) * _jnp.square(g)
    m_hat = m / (1.0 - ADAM_B1 ** ADAM_STEP)
    v_hat = v / (1.0 - ADAM_B2 ** ADAM_STEP)
    delta = -ADAM_LR * (m_hat / (_jnp.sqrt(v_hat) + ADAM_EPS) + ADAM_WD * w)
    return delta, m, v


def reference(x, c, ctx, c_ctx, mod_w, mod_b, norm_pre, norm_post, ffn_gate, ffn_up, ffn_down, ev_w_in, ev_w_out, lru_conv_w, lru_conv_b, lru_wa, lru_ba, lru_wx, lru_bx, lru_lambda, ret_decay_logit, ret_gn, od_w_in, od_w_out, pool_w, pool_scale, q_norm, k_norm, loss_target, m_c_ctx, m_mod_w, m_mod_b, m_norm_pre, m_norm_post, m_ffn_gate, m_ffn_up, m_ffn_down, m_ev_w_in, m_ev_w_out, m_lru_conv_w, m_lru_conv_b, m_lru_wa, m_lru_ba, m_lru_wx, m_lru_bx, m_lru_lambda, m_ret_decay_logit, m_ret_gn, m_od_w_in, m_od_w_out, m_pool_w, m_pool_scale, m_q_norm, m_k_norm, v_c_ctx, v_mod_w, v_mod_b, v_norm_pre, v_norm_post, v_ffn_gate, v_ffn_up, v_ffn_down, v_ev_w_in, v_ev_w_out, v_lru_conv_w, v_lru_conv_b, v_lru_wa, v_lru_ba, v_lru_wx, v_lru_bx, v_lru_lambda, v_ret_decay_logit, v_ret_gn, v_od_w_in, v_od_w_out, v_pool_w, v_pool_scale, v_q_norm, v_k_norm):
    given = dict(x=x, c=c, ctx=ctx, c_ctx=c_ctx, mod_w=mod_w, mod_b=mod_b, norm_pre=norm_pre, norm_post=norm_post, ffn_gate=ffn_gate, ffn_up=ffn_up, ffn_down=ffn_down, ev_w_in=ev_w_in, ev_w_out=ev_w_out, lru_conv_w=lru_conv_w, lru_conv_b=lru_conv_b, lru_wa=lru_wa, lru_ba=lru_ba, lru_wx=lru_wx, lru_bx=lru_bx, lru_lambda=lru_lambda, ret_decay_logit=ret_decay_logit, ret_gn=ret_gn, od_w_in=od_w_in, od_w_out=od_w_out, pool_w=pool_w, pool_scale=pool_scale, q_norm=q_norm, k_norm=k_norm, loss_target=loss_target, m_c_ctx=m_c_ctx, m_mod_w=m_mod_w, m_mod_b=m_mod_b, m_norm_pre=m_norm_pre, m_norm_post=m_norm_post, m_ffn_gate=m_ffn_gate, m_ffn_up=m_ffn_up, m_ffn_down=m_ffn_down, m_ev_w_in=m_ev_w_in, m_ev_w_out=m_ev_w_out, m_lru_conv_w=m_lru_conv_w, m_lru_conv_b=m_lru_conv_b, m_lru_wa=m_lru_wa, m_lru_ba=m_lru_ba, m_lru_wx=m_lru_wx, m_lru_bx=m_lru_bx, m_lru_lambda=m_lru_lambda, m_ret_decay_logit=m_ret_decay_logit, m_ret_gn=m_ret_gn, m_od_w_in=m_od_w_in, m_od_w_out=m_od_w_out, m_pool_w=m_pool_w, m_pool_scale=m_pool_scale, m_q_norm=m_q_norm, m_k_norm=m_k_norm, v_c_ctx=v_c_ctx, v_mod_w=v_mod_w, v_mod_b=v_mod_b, v_norm_pre=v_norm_pre, v_norm_post=v_norm_post, v_ffn_gate=v_ffn_gate, v_ffn_up=v_ffn_up, v_ffn_down=v_ffn_down, v_ev_w_in=v_ev_w_in, v_ev_w_out=v_ev_w_out, v_lru_conv_w=v_lru_conv_w, v_lru_conv_b=v_lru_conv_b, v_lru_wa=v_lru_wa, v_lru_ba=v_lru_ba, v_lru_wx=v_lru_wx, v_lru_bx=v_lru_bx, v_lru_lambda=v_lru_lambda, v_ret_decay_logit=v_ret_decay_logit, v_ret_gn=v_ret_gn, v_od_w_in=v_od_w_in, v_od_w_out=v_od_w_out, v_pool_w=v_pool_w, v_pool_scale=v_pool_scale, v_q_norm=v_q_norm, v_k_norm=v_k_norm)
    weights = {n: given[n] for n in TWIN_WEIGHTS}
    shared = {n: given[n] for n in SHARED_INPUTS}
    per_example = {n: given[n] for n in ['x', 'c', 'ctx']}
    grad_fn = _jax.value_and_grad(_loss, argnums=(0, 1))

    def one_microbatch(ex, loss_target):
        ex = dict(ex)
        diff = ex.pop(TWIN_DIFF_INPUT)
        return grad_fn(weights, diff, {**shared, **ex}, loss_target)

    if N_MICROBATCH == 1:
        loss, (grad_w, grad_x) = one_microbatch(per_example, given["loss_target"])
    else:
        def body(carry, xs):
            loss_sum, grad_sum = carry
            l_k, (gw_k, gx_k) = one_microbatch(xs[0], xs[1])
            with _jax.named_scope("update"):
                return (loss_sum + l_k, _jax.tree.map(_jnp.add, grad_sum, gw_k)), gx_k

        init = (_jnp.zeros((), _jnp.float32), _jax.tree.map(_jnp.zeros_like, weights))
        (loss, grad_w), grad_x = _jax.lax.scan(body, init, (per_example, given["loss_target"]))
    with _jax.named_scope("update"):
        delta_w, new_m, new_v = {}, {}, {}
        for n in TWIN_WEIGHTS:
            delta_w[n], new_m[n], new_v[n] = _adamw(weights[n], grad_w[n], given["m_" + n], given["v_" + n])
    return (loss, grad_x, *[grad_w[n] for n in TWIN_WEIGHTS], *[delta_w[n] for n in TWIN_WEIGHTS],
            *[new_m[n] for n in TWIN_WEIGHTS], *[new_v[n] for n in TWIN_WEIGHTS])
```

```python
import functools

import jax
import jax.numpy as jnp
from jax import lax
from jax.experimental import pallas as pl
from jax.experimental.pallas import tpu as pltpu

F32 = jnp.float32
BF = jnp.bfloat16
S_ = jax.ShapeDtypeStruct
MESH = pl.DeviceIdType.MESH

NDEV = 8
EPS = 1e-6
FFN_STEP = 0.5
LRU_C = 8.0
RET_CHUNK = 128
RET_THETA = 10000.0
ROPE_THETA = 10000.0
GRID_W = 64
POOL_WINDOWS = (2, 4, 8, 16)
ROW_TILE = 256
HALO = 8
VMEM_LIMIT = 58 * 1024 * 1024

ADAM_LR = 0.001
ADAM_B1 = 0.9
ADAM_B2 = 0.999
ADAM_EPS = 1e-08
ADAM_WD = 0.01
ADAM_STEP = 10

NN = ((1,), (0,))
NT = ((1,), (1,))
TN = ((0,), (0,))


def _dot(a, b, dn):
    return lax.dot_general(a.astype(BF), b.astype(BF), (dn, ((), ())), preferred_element_type=F32)


@functools.partial(jax.custom_vjp, nondiff_argnums=(2,))
def _bdot(a, b, mode):
    return _dot(a, b, {"nn": NN, "nt": NT, "tn": TN}[mode])


def _bdot_fwd(a, b, mode):
    return _bdot(a, b, mode), (a, b)


def _bdot_bwd(mode, res, g):
    a, b = res
    if mode == "nn":
        return _dot(g, b, NT), _dot(a, g, TN)
    if mode == "nt":
        return _dot(g, b, NN), _dot(g, a, TN)
    return _dot(b, g, NT), _dot(a, g, NN)


_bdot.defvjp(_bdot_fwd, _bdot_bwd)


@functools.partial(jax.custom_vjp, nondiff_argnums=(1, 2))
def _rollv(x, shift, axis):
    return pltpu.roll(x, shift, axis)


def _rollv_fwd(x, shift, axis):
    return pltpu.roll(x, shift, axis), None


def _rollv_bwd(shift, axis, _, g):
    n = g.shape[axis]
    return (pltpu.roll(g, (n - shift) % n, axis),)


_rollv.defvjp(_rollv_fwd, _rollv_bwd)


def _cp(vmem=VMEM_LIMIT):
    return pltpu.CompilerParams(vmem_limit_bytes=vmem)


def _tile(n, pref, mult=8):
    if n <= pref:
        return n
    for t in range(pref, 0, -1):
        if n % t == 0 and t % mult == 0:
            return t
    return n


def _full(shape):
    nd = len(shape)
    return pl.BlockSpec(tuple(shape), lambda *_: (0,) * nd)


def _me():
    return lax.axis_index("x"), lax.axis_index("y"), lax.axis_index("c")


def _all_gather(arrs, name):
    n = len(arrs)

    def body(*refs):
        xs, outs = refs[:n], refs[n:2 * n]
        send_sems, recv_sems, local_sems = refs[2 * n:]
        x, y, c = _me()
        me, sibling = (x, y, c), (x, y, 1 - c)
        chips = [(1 - x, y), (x, 1 - y), (1 - x, 1 - y)]

        def blk(out, p):
            return out.at[4 * p[0] + 2 * p[1] + p[2]]

        def copy(a, k, block, to, src=None):
            return pltpu.make_async_remote_copy(
                src_ref=blk(outs[a], block) if src is None else src, dst_ref=blk(outs[a], block),
                send_sem=send_sems.at[a, k], recv_sem=recv_sems.at[a, k], device_id=to, device_id_type=MESH)

        mine = [pltpu.make_async_copy(xs[a], blk(outs[a], me), local_sems.at[a]) for a in range(n)]
        for cp in mine:
            cp.start()
        first = []
        for a in range(n):
            first.append(copy(a, 0, me, sibling, src=xs[a]))
            first += [copy(a, 1 + j, me, (*chip, c), src=xs[a]) for j, chip in enumerate(chips)]
        for cp in first:
            cp.start()
        passed = []
        for j, chip in enumerate(chips):
            for a in range(n):
                copy(a, 1 + j, (*chip, c), me).wait_recv()
                fw = copy(a, 4 + j, (*chip, c), sibling)
                fw.start()
                passed.append(fw)
        for a in range(n):
            copy(a, 0, sibling, me).wait_recv()
            for j, chip in enumerate(chips):
                copy(a, 4 + j, (*chip, 1 - c), me).wait_recv()
        for cp in first + passed:
            cp.wait_send()
        for cp in mine:
            cp.wait()

    anyspec = pl.BlockSpec(memory_space=pl.ANY)
    return pl.pallas_call(
        body, name=name,
        out_shape=[S_((NDEV,) + a.shape, a.dtype) for a in arrs],
        in_specs=[anyspec] * n, out_specs=[anyspec] * n,
        scratch_shapes=[pltpu.SemaphoreType.DMA((n, 7)), pltpu.SemaphoreType.DMA((n, 7)), pltpu.SemaphoreType.DMA((n,))],
    )(*arrs)


def _all_to_all(arrs, name):
    n = len(arrs)

    def body(*refs):
        xs, outs = refs[:n], refs[n:2 * n]
        send_sems, recv_sems, local_sems = refs[2 * n:]
        x, y, c = _me()
        me_idx = 4 * x + 2 * y + c
        mine = [pltpu.make_async_copy(xs[a].at[me_idx], outs[a].at[me_idx], local_sems.at[a]) for a in range(n)]
        for cp in mine:
            cp.start()
        copies = []
        for k in range(1, NDEV):
            kx, ky, kc = (k >> 2) & 1, (k >> 1) & 1, k & 1
            px = 1 - x if kx else x
            py = 1 - y if ky else y
            pc = 1 - c if kc else c
            p_idx = 4 * px + 2 * py + pc
            for a in range(n):
                copies.append(pltpu.make_async_remote_copy(
                    src_ref=xs[a].at[p_idx], dst_ref=outs[a].at[me_idx],
                    send_sem=send_sems.at[a, k - 1], recv_sem=recv_sems.at[a, k - 1],
                    device_id=(px, py, pc), device_id_type=MESH))
        for cp in copies:
            cp.start()
        for cp in copies:
            cp.wait_recv()
        for cp in copies:
            cp.wait_send()
        for cp in mine:
            cp.wait()

    anyspec = pl.BlockSpec(memory_space=pl.ANY)
    return pl.pallas_call(
        body, name=name,
        out_shape=[S_(a.shape, a.dtype) for a in arrs],
        in_specs=[anyspec] * n, out_specs=[anyspec] * n,
        scratch_shapes=[pltpu.SemaphoreType.DMA((n, 7)), pltpu.SemaphoreType.DMA((n, 7)), pltpu.SemaphoreType.DMA((n,))],
    )(*arrs)


def _cast_bf16(a, name):
    R, C = a.shape
    tr = _tile(R, 512, 16)

    def body(a_ref, o_ref):
        o_ref[...] = a_ref[...].astype(BF)

    return pl.pallas_call(body, name=name, grid=(R // tr,), in_specs=[pl.BlockSpec((tr, C), lambda i: (i, 0))],
                          out_specs=pl.BlockSpec((tr, C), lambda i: (i, 0)), out_shape=S_((R, C), BF), compiler_params=_cp())(a)


def _sum_n(a, name):
    n, R, C = a.shape
    tr = _tile(R, 256, 8)

    def body(a_ref, o_ref):
        acc = a_ref[0].astype(F32)
        for i in range(1, n):
            acc = acc + a_ref[i].astype(F32)
        o_ref[...] = acc

    return pl.pallas_call(body, name=name, grid=(R // tr,), in_specs=[pl.BlockSpec((n, tr, C), lambda i: (0, i, 0))],
                          out_specs=pl.BlockSpec((tr, C), lambda i: (i, 0)), out_shape=S_((R, C), F32), compiler_params=_cp())(a)


def _adam_math(w, g, m, v):
    m = ADAM_B1 * m + (1.0 - ADAM_B1) * g
    v = ADAM_B2 * v + (1.0 - ADAM_B2) * jnp.square(g)
    m_hat = m / (1.0 - ADAM_B1 ** ADAM_STEP)
    v_hat = v / (1.0 - ADAM_B2 ** ADAM_STEP)
    delta = -ADAM_LR * (m_hat / (jnp.sqrt(v_hat) + ADAM_EPS) + ADAM_WD * w)
    return delta, m, v


def _reduce_adam(pieces, w, m, v, name):
    n, R, C = pieces.shape
    tr = _tile(R, 256, 8)

    def body(p_ref, w_ref, m_ref, v_ref, g_ref, d_ref, mo_ref, vo_ref):
        g = p_ref[0].astype(F32)
        for i in range(1, n):
            g = g + p_ref[i].astype(F32)
        d, mn, vn = _adam_math(w_ref[...], g, m_ref[...], v_ref[...])
        g_ref[...] = g
        d_ref[...] = d
        mo_ref[...] = mn
        vo_ref[...] = vn

    row = pl.BlockSpec((tr, C), lambda i: (i, 0))
    return pl.pallas_call(body, name=name, grid=(R // tr,),
                          in_specs=[pl.BlockSpec((n, tr, C), lambda i: (0, i, 0)), row, row, row],
                          out_specs=[row] * 4, out_shape=[S_((R, C), F32)] * 4, compiler_params=_cp())(pieces, w, m, v)


def _mm_cols(a, wb, name, out_dtype=F32):
    M, K = a.shape
    NB, _, nb = wb.shape
    tm = _tile(M, 768, 16)

    def body(a_ref, w_ref, o_ref):
        o_ref[...] = _dot(a_ref[...], w_ref[...], NN).astype(out_dtype)

    return pl.pallas_call(body, name=name, grid=(M // tm, NB),
                          in_specs=[pl.BlockSpec((tm, K), lambda i, j: (i, 0)), pl.BlockSpec((None, K, nb), lambda i, j: (j, 0, 0))],
                          out_specs=pl.BlockSpec((tm, nb), lambda i, j: (i, j)), out_shape=S_((M, NB * nb), out_dtype),
                          compiler_params=_cp())(a, wb)


def _mm_nt_cols(g, wb, name):
    M = g.shape[0]
    NB, K, nb = wb.shape
    tm = _tile(M, 768, 16)

    def body(g_ref, w_ref, o_ref):
        j = pl.program_id(1)
        part = _dot(g_ref[...], w_ref[...], NT)

        @pl.when(j == 0)
        def _():
            o_ref[...] = part

        @pl.when(j > 0)
        def _():
            o_ref[...] += part

    return pl.pallas_call(body, name=name, grid=(M // tm, NB),
                          in_specs=[pl.BlockSpec((tm, nb), lambda i, j: (i, j)), pl.BlockSpec((None, K, nb), lambda i, j: (j, 0, 0))],
                          out_specs=pl.BlockSpec((tm, K), lambda i, j: (i, 0)), out_shape=S_((M, K), F32),
                          compiler_params=_cp())(g, wb)


def _mm_full(a, w, dn, name, out_dtype=F32):
    M, K = a.shape
    N = w.shape[1] if dn == NN else w.shape[0]
    tm = _tile(M, 768, 16)

    def body(a_ref, w_ref, o_ref):
        o_ref[...] = _dot(a_ref[...], w_ref[...], dn).astype(out_dtype)

    return pl.pallas_call(body, name=name, grid=(M // tm,),
                          in_specs=[pl.BlockSpec((tm, K), lambda i: (i, 0)), _full(w.shape)],
                          out_specs=pl.BlockSpec((tm, N), lambda i: (i, 0)), out_shape=S_((M, N), out_dtype),
                          compiler_params=_cp())(a, w)


def _mm_tn_cols(a, g, NB, name):
    M, K = a.shape
    nb = g.shape[1] // NB
    tk = _tile(M, 768, 16)
    nk = M // tk

    def body(a_ref, g_ref, o_ref, acc):
        k = pl.program_id(1)
        part = _dot(a_ref[...], g_ref[...], TN)

        @pl.when(k == 0)
        def _():
            acc[...] = part

        @pl.when(k > 0)
        def _():
            acc[...] += part

        @pl.when(k == nk - 1)
        def _():
            o_ref[...] = acc[...].astype(BF)

    return pl.pallas_call(body, name=name, grid=(NB, nk),
                          in_specs=[pl.BlockSpec((tk, K), lambda b, k: (k, 0)), pl.BlockSpec((tk, nb), lambda b, k: (k, b))],
                          out_specs=pl.BlockSpec((None, K, nb), lambda b, k: (b, 0, 0)), out_shape=S_((NB, K, nb), BF),
                          scratch_shapes=[pltpu.VMEM((K, nb), F32)], compiler_params=_cp())(a, g)


def _mm_tn_rows(a, g, NB, name):
    M = a.shape[0]
    kb = a.shape[1] // NB
    N = g.shape[1]
    tk = _tile(M, 768, 16)
    nk = M // tk

    def body(a_ref, g_ref, o_ref, acc):
        k = pl.program_id(1)
        part = _dot(a_ref[...], g_ref[...], TN)

        @pl.when(k == 0)
        def _():
            acc[...] = part

        @pl.when(k > 0)
        def _():
            acc[...] += part

        @pl.when(k == nk - 1)
        def _():
            o_ref[...] = acc[...].astype(BF)

    return pl.pallas_call(body, name=name, grid=(NB, nk),
                          in_specs=[pl.BlockSpec((tk, kb), lambda b, k: (k, b)), pl.BlockSpec((tk, N), lambda b, k: (k, 0))],
                          out_specs=pl.BlockSpec((None, kb, N), lambda b, k: (b, 0, 0)), out_shape=S_((NB, kb, N), BF),
                          scratch_shapes=[pltpu.VMEM((kb, N), F32)], compiler_params=_cp())(a, g)


def _ffn_fwd(h, wg, wu, wd, name):
    T, D = h.shape
    NB, _, nb = wg.shape
    tm = _tile(T, 384, 16)

    def body(h_ref, wg_ref, wu_ref, wd_ref, y_ref, g_ref, u_ref):
        b = pl.program_id(1)
        hh = h_ref[...]
        g = _dot(hh, wg_ref[...], NN)
        u = _dot(hh, wu_ref[...], NN)
        g_ref[...] = g
        u_ref[...] = u
        part = _dot(g * jax.nn.sigmoid(g) * u, wd_ref[...], NN)

        @pl.when(b == 0)
        def _():
            y_ref[...] = part

        @pl.when(b > 0)
        def _():
            y_ref[...] += part

    wcol = pl.BlockSpec((None, D, nb), lambda i, b: (b, 0, 0))
    act = pl.BlockSpec((None, tm, nb), lambda i, b: (b, i, 0))
    return pl.pallas_call(
        body, name=name, grid=(T // tm, NB),
        in_specs=[pl.BlockSpec((tm, D), lambda i, b: (i, 0)), wcol, wcol, pl.BlockSpec((None, nb, D), lambda i, b: (b, 0, 0))],
        out_specs=[pl.BlockSpec((tm, D), lambda i, b: (i, 0)), act, act],
        out_shape=[S_((T, D), F32), S_((NB, T, nb), F32), S_((NB, T, nb), F32)], compiler_params=_cp())(h, wg, wu, wd)


def _ffn_bwd_act(dy, G, U, wg, wu, wd, name):
    T, D = dy.shape
    NB, _, nb = wg.shape
    tm = _tile(T, 384, 16)

    def body(dy_ref, g_ref, u_ref, wg_ref, wu_ref, wd_ref, dh_ref, dg_ref, du_ref, a_ref):
        b = pl.program_id(1)
        g, u = g_ref[...], u_ref[...]
        da = _dot(dy_ref[...], wd_ref[...], NT)
        s = jax.nn.sigmoid(g)
        silu = g * s
        du = da * silu
        dg = da * u * (s * (1.0 + g * (1.0 - s)))
        dg_ref[...] = dg.astype(BF)
        du_ref[...] = du.astype(BF)
        a_ref[...] = (silu * u).astype(BF)
        part = _dot(dg, wg_ref[...], NT) + _dot(du, wu_ref[...], NT)

        @pl.when(b == 0)
        def _():
            dh_ref[...] = part

        @pl.when(b > 0)
        def _():
            dh_ref[...] += part

    wcol = pl.BlockSpec((None, D, nb), lambda i, b: (b, 0, 0))
    act = pl.BlockSpec((None, tm, nb), lambda i, b: (b, i, 0))
    row = pl.BlockSpec((tm, D), lambda i, b: (i, 0))
    return pl.pallas_call(
        body, name=name, grid=(T // tm, NB),
        in_specs=[row, act, act, wcol, wcol, pl.BlockSpec((None, nb, D), lambda i, b: (b, 0, 0))],
        out_specs=[row, act, act, act],
        out_shape=[S_((T, D), F32)] + [S_((NB, T, nb), BF)] * 3, compiler_params=_cp())(dy, G, U, wg, wu, wd)


def _ffn_wgrad_in(h, dact, name):
    T, D = h.shape
    NB, _, nb = dact.shape
    tk = _tile(T, 768, 16)
    nk = T // tk

    def body(h_ref, d_ref, o_ref, acc):
        k = pl.program_id(1)
        part = _dot(h_ref[...], d_ref[...], TN)

        @pl.when(k == 0)
        def _():
            acc[...] = part

        @pl.when(k > 0)
        def _():
            acc[...] += part

        @pl.when(k == nk - 1)
        def _():
            o_ref[...] = acc[...].astype(BF)

    return pl.pallas_call(body, name=name, grid=(NB, nk),
                          in_specs=[pl.BlockSpec((tk, D), lambda b, k: (k, 0)), pl.BlockSpec((None, tk, nb), lambda b, k: (b, k, 0))],
                          out_specs=pl.BlockSpec((None, D, nb), lambda b, k: (b, 0, 0)), out_shape=S_((NB, D, nb), BF),
                          scratch_shapes=[pltpu.VMEM((D, nb), F32)], compiler_params=_cp())(h, dact)


def _ffn_wgrad_out(act, dy, name):
    NB, T, nb = act.shape
    D = dy.shape[1]
    tk = _tile(T, 768, 16)
    nk = T // tk

    def body(a_ref, d_ref, o_ref, acc):
        k = pl.program_id(1)
        part = _dot(a_ref[...], d_ref[...], TN)

        @pl.when(k == 0)
        def _():
            acc[...] = part

        @pl.when(k > 0)
        def _():
            acc[...] += part

        @pl.when(k == nk - 1)
        def _():
            o_ref[...] = acc[...].astype(BF)

    return pl.pallas_call(body, name=name, grid=(NB, nk),
                          in_specs=[pl.BlockSpec((None, tk, nb), lambda b, k: (b, k, 0)), pl.BlockSpec((tk, D), lambda b, k: (k, 0))],
                          out_specs=pl.BlockSpec((None, nb, D), lambda b, k: (b, 0, 0)), out_shape=S_((NB, nb, D), BF),
                          scratch_shapes=[pltpu.VMEM((nb, D), F32)], compiler_params=_cp())(act, dy)


class _Geo:
    def __init__(self, S, LC, D):
        self.S, self.LC, self.D, self.T = S, LC, D, S + LC
        self.RT = _tile(LC, ROW_TILE, 8)
        assert S % self.RT == 0 and self.RT >= 2 * HALO
        self.nlat, self.nctx = S // self.RT, LC // self.RT
        self.nt = self.nlat + self.nctx

    def row(self, C, cb=0):
        return pl.BlockSpec((self.RT, C), lambda i: (i, cb))

    def prev(self, C, cb=0):
        return pl.BlockSpec((self.RT, C), lambda i: (jnp.maximum(i - 1, 0), cb))

    def next(self, C, cb=0):
        nt = self.nt
        return pl.BlockSpec((self.RT, C), lambda i: (jnp.minimum(i + 1, nt - 1), cb))

    def seg(self, r, C):
        nlat = self.nlat
        return pl.BlockSpec((None, r, C), lambda i: (jnp.minimum(i // nlat, 1), 0, 0))

    def first_of_seg(self, i):
        return jnp.logical_or(i == 0, i == self.nlat)

    def prev_ok(self, i):
        return jnp.logical_and(i != 0, i != self.nlat)

    def next_ok(self, i):
        return jnp.logical_and(i != self.nlat - 1, i != self.nt - 1)

    def pos(self, i):
        r = lax.broadcasted_iota(jnp.int32, (self.RT, 1), 0)
        is_ctx = i >= self.nlat
        base = jnp.where(is_ctx, (i - self.nlat) * self.RT, i * self.RT)
        return base + r, jnp.where(is_ctx, self.LC, self.S)


def _rms(x):
    return x * lax.rsqrt(jnp.mean(x * x, axis=-1, keepdims=True) + EPS)


def _modulate(x, g, shift, scale):
    return (_rms(x) * g) * (1 + scale) + shift


def _post(x, y, g, gate, w):
    return x + w * gate * (_rms(y) * g)


def _norm_fwd(geo, x, post, pre, name):
    D = geo.D
    ins, specs = [x], [geo.row(D)]
    if post is not None:
        ins += [post[0], post[1], post[2]]
        specs += [geo.row(D), _full((1, D)), geo.seg(9, D)]
    if pre is not None:
        ins += [pre[0], pre[1]]
        specs += [_full((1, D)), geo.seg(9, D)]

    def body(*refs):
        it = iter(refs)
        xv = next(it)[...]
        if post is not None:
            y_ref, gp_ref, tab_ref = next(it), next(it), next(it)
        if pre is not None:
            gq_ref, tabn_ref = next(it), next(it)
        if post is not None:
            r = 3 * post[3] + 2
            xv = _post(xv, y_ref[...], gp_ref[...], tab_ref[r:r + 1, :], post[4])
            next(it)[...] = xv
        if pre is not None:
            r = 3 * pre[2]
            next(it)[...] = _modulate(xv, gq_ref[...], tabn_ref[r:r + 1, :], tabn_ref[r + 1:r + 2, :]).astype(BF)

    outs, ospecs = [], []
    if post is not None:
        outs.append(S_((geo.T, D), F32))
        ospecs.append(geo.row(D))
    if pre is not None:
        outs.append(S_((geo.T, D), BF))
        ospecs.append(geo.row(D))
    return pl.pallas_call(body, name=name, grid=(geo.nt,), in_specs=specs, out_specs=ospecs, out_shape=outs,
                          compiler_params=_cp())(*ins)


def _acc_rows(ref, first, rows):
    for k, v in enumerate(rows):
        @pl.when(first)
        def _(k=k, v=v):
            ref[k:k + 1, :] = v

        @pl.when(jnp.logical_not(first))
        def _(k=k, v=v):
            ref[k:k + 1, :] += v


def _norm_bwd(geo, dxo, dh, x, pre, post, name):
    D = geo.D
    ins = [dxo, dh, x, pre[0], pre[1]]
    specs = [geo.row(D)] * 3 + [_full((1, D)), geo.seg(9, D)]
    if post is not None:
        ins += [post[0], post[1], post[2]]
        specs += [geo.row(D), _full((1, D)), geo.seg(9, D)]

    def body(*refs):
        i = pl.program_id(0)
        first = geo.first_of_seg(i)
        dxo_ref, dh_ref, x_ref, gq_ref, tab_ref = refs[:5]
        k = 5
        if post is not None:
            y_ref, gp_ref, tabp_ref = refs[5:8]
            k = 8
        outs = refs[k:]
        r = 3 * pre[2]
        _, vjp = jax.vjp(_modulate, x_ref[...], gq_ref[...], tab_ref[r:r + 1, :], tab_ref[r + 1:r + 2, :])
        dx, dg, dsh, dsc = vjp(dh_ref[...].astype(F32))
        dx = dx + dxo_ref[...]
        outs[0][...] = dx
        if post is None:
            _acc_rows(outs[1], first, [dg, dsh, dsc])
            return
        _acc_rows(outs[2], first, [dg, dsh, dsc])
        rp = 3 * post[3] + 2
        w = post[4]
        _, vjp2 = jax.vjp(lambda yy, gg, ga: w * ga * (_rms(yy) * gg), y_ref[...], gp_ref[...], tabp_ref[rp:rp + 1, :])
        dy, dgp, dga = vjp2(dx)
        outs[1][...] = dy
        _acc_rows(outs[3], first, [dgp, dga])

    if post is None:
        outs, ospecs = [S_((geo.T, D), F32), S_((2, 3, D), F32)], [geo.row(D), geo.seg(3, D)]
    else:
        outs = [S_((geo.T, D), F32), S_((geo.T, D), F32), S_((2, 3, D), F32), S_((2, 2, D), F32)]
        ospecs = [geo.row(D), geo.row(D), geo.seg(3, D), geo.seg(2, D)]
    return pl.pallas_call(body, name=name, grid=(geo.nt,), in_specs=specs, out_specs=ospecs, out_shape=outs,
                          compiler_params=_cp())(*ins)


def _loss_bwd(geo, xf, tgt, post, name):
    D = geo.D
    nlat = geo.nlat

    def body(x_ref, t_ref, y_ref, gp_ref, tabp_ref, loss_ref, dx_ref, dy_ref, dpost_ref):
        i = pl.program_id(0)
        first = geo.first_of_seg(i)
        lat = i < nlat
        diff = x_ref[...] - t_ref[...]
        part = jnp.where(lat, 0.5 * jnp.sum(jnp.mean(diff * diff, axis=-1, keepdims=True), axis=0, keepdims=True), 0.0)

        @pl.when(i == 0)
        def _():
            loss_ref[...] = part

        @pl.when(i > 0)
        def _():
            loss_ref[...] += part

        dx = jnp.where(lat, diff * (1.0 / D), 0.0)
        dx_ref[...] = dx
        rp = 3 * post[3] + 2
        w = post[4]
        _, vjp2 = jax.vjp(lambda yy, gg, ga: w * ga * (_rms(yy) * gg), y_ref[...], gp_ref[...], tabp_ref[rp:rp + 1, :])
        dy, dgp, dga = vjp2(dx)
        dy_ref[...] = dy
        _acc_rows(dpost_ref, first, [dgp, dga])

    tspec = pl.BlockSpec((geo.RT, D), lambda i: (jnp.minimum(i, nlat - 1), 0))
    return pl.pallas_call(
        body, name=name, grid=(geo.nt,),
        in_specs=[geo.row(D), tspec, geo.row(D), _full((1, D)), geo.seg(9, D)],
        out_specs=[_full((1, 1)), geo.row(D), geo.row(D), geo.seg(2, D)],
        out_shape=[S_((1, 1), F32), S_((geo.T, D), F32), S_((geo.T, D), F32), S_((2, 2, D), F32)],
        compiler_params=_cp())(xf, tgt, post[0], post[1], post[2])


def _fill_ext(geo, ext, prev_ref, cur_ref, next_ref, i):
    RT = geo.RT
    ext[0:HALO, :] = jnp.where(geo.prev_ok(i), prev_ref[RT - HALO:RT, :], 0.0).astype(ext.dtype)
    ext[HALO:HALO + RT, :] = cur_ref[...].astype(ext.dtype)
    ext[HALO + RT:2 * HALO + RT, :] = jnp.where(geo.next_ok(i), next_ref[0:HALO, :], 0.0).astype(ext.dtype)


CONV_W = 4
CONV_LEFT = 2


def _lru_gates(u, za, zx, lam):
    r = jax.nn.sigmoid(za)
    i = jax.nn.sigmoid(zx)
    log_a = -LRU_C * r * jax.nn.softplus(-lam)
    a = jnp.exp(log_a)
    return a, jnp.sqrt(1.0 - jnp.exp(2.0 * log_a)) * (i * u)


def _lru_coef_fwd(geo, P, W, cw, cb, wa, ba, wx, bx, lam, name):
    RT = geo.RT
    nblk, LB = wa.shape[1], wa.shape[2]

    def body(pp, pc, pn, cw_ref, cb_ref, wa_ref, ba_ref, wx_ref, bx_ref, lam_ref, u_ref, a0, b0, a1, b1, ext):
        i = pl.program_id(0)
        _fill_ext(geo, ext, pp, pc, pn, i)
        u = cb_ref[...] + ext[pl.ds(HALO - CONV_LEFT, RT), :] * cw_ref[0:1, :]
        for k in range(1, CONV_W):
            u = u + ext[pl.ds(HALO - CONV_LEFT + k, RT), :] * cw_ref[k:k + 1, :]
        u_ref[...] = u
        for d, (a_ref, b_ref) in enumerate(((a0, b0), (a1, b1))):
            for n in range(nblk):
                sl = slice(n * LB, (n + 1) * LB)
                un = u[:, sl]
                za = _dot(un, wa_ref[d, n], NN) + ba_ref[d:d + 1, sl]
                zx = _dot(un, wx_ref[d, n], NN) + bx_ref[d:d + 1, sl]
                a, b = _lru_gates(un, za, zx, lam_ref[d:d + 1, sl])
                a_ref[:, sl] = a
                b_ref[:, sl] = b

    return pl.pallas_call(
        body, name=name, grid=(geo.nt,),
        in_specs=[geo.prev(W, 1), geo.row(W, 1), geo.next(W, 1), _full(cw.shape), _full(cb.shape), _full(wa.shape),
                  _full(ba.shape), _full(wx.shape), _full(bx.shape), _full(lam.shape)],
        out_specs=[geo.row(W)] * 5, out_shape=[S_((geo.T, W), F32)] * 5,
        scratch_shapes=[pltpu.VMEM((RT + 2 * HALO, W), F32)], compiler_params=_cp())(P, P, P, cw, cb, wa, ba, wx, bx, lam)


def _lru_coef_bwd(geo, u, da, db, W, wa, ba, wx, bx, lam, name):
    nblk, LB = wa.shape[1], wa.shape[2]

    def body(u_ref, da0, db0, da1, db1, wa_ref, ba_ref, wx_ref, bx_ref, lam_ref, du_ref, dwa, dba, dwx, dbx, dlam):
        i = pl.program_id(0)

        @pl.when(i == 0)
        def _():
            for r in (dwa, dba, dwx, dbx, dlam):
                r[...] = jnp.zeros(r.shape, F32)

        u = u_ref[...]
        for n in range(nblk):
            sl = slice(n * LB, (n + 1) * LB)
            un = u[:, sl]
            dun = jnp.zeros_like(un)
            for d, (da_ref, db_ref) in enumerate(((da0, db0), (da1, db1))):
                za = _dot(un, wa_ref[d, n], NN) + ba_ref[d:d + 1, sl]
                zx = _dot(un, wx_ref[d, n], NN) + bx_ref[d:d + 1, sl]
                _, vjp = jax.vjp(_lru_gates, un, za, zx, lam_ref[d:d + 1, sl])
                du_e, dza, dzx, dl = vjp((da_ref[:, sl], db_ref[:, sl]))
                dun = dun + du_e + _dot(dza, wa_ref[d, n], NT) + _dot(dzx, wx_ref[d, n], NT)
                dwa[d, n] += _dot(un, dza, TN)
                dwx[d, n] += _dot(un, dzx, TN)
                dba[d:d + 1, sl] += jnp.sum(dza, axis=0, keepdims=True)
                dbx[d:d + 1, sl] += jnp.sum(dzx, axis=0, keepdims=True)
                dlam[d:d + 1, sl] += dl
            du_ref[:, sl] = dun

    row = geo.row(W)
    return pl.pallas_call(
        body, name=name, grid=(geo.nt,),
        in_specs=[row] * 5 + [_full(wa.shape), _full(ba.shape), _full(wx.shape), _full(bx.shape), _full(lam.shape)],
        out_specs=[row, _full(wa.shape), _full(ba.shape), _full(wx.shape), _full(bx.shape), _full(lam.shape)],
        out_shape=[S_((geo.T, W), F32), S_(wa.shape, F32), S_(ba.shape, F32), S_(wx.shape, F32), S_(bx.shape, F32), S_(lam.shape, F32)],
        compiler_params=_cp())(u, da[0], db[0], da[1], db[1], wa, ba, wx, bx, lam)


def _scan_order(d, k, nlat, nctx):
    if d == 0:
        return jnp.where(k < nctx, nlat + k, k - nctx)
    return jnp.where(k < nctx, nlat + nctx - 1 - k, nlat - 1 - (k - nctx))


def _chunk_scan(a, b, reverse):
    n = a.shape[0]
    row = lax.broadcasted_iota(jnp.int32, a.shape, 0)
    s = 1
    while s < n:
        if reverse:
            ok = row < n - s
            a_s, b_s = pltpu.roll(a, n - s, 0), pltpu.roll(b, n - s, 0)
        else:
            ok = row >= s
            a_s, b_s = pltpu.roll(a, s, 0), pltpu.roll(b, s, 0)
        b = a * jnp.where(ok, b_s, 0.0) + b
        a = a * jnp.where(ok, a_s, 1.0)
        s *= 2
    return a, b


def _shift1(x, reverse, fill):
    n = x.shape[0]
    row = lax.broadcasted_iota(jnp.int32, x.shape, 0)
    if reverse:
        return jnp.where(row == n - 1, fill, pltpu.roll(x, n - 1, 0))
    return jnp.where(row == 0, fill, pltpu.roll(x, 1, 0))


def _lru_scan_fwd(geo, a, b, d, name):
    W = a.shape[1]
    tw = _tile(W, 256, 128)
    RT, nlat, nctx = geo.RT, geo.nlat, geo.nctx
    rev = d == 1

    def body(a_ref, b_ref, h_ref, hp_ref, carry):
        k = pl.program_id(1)

        @pl.when(k == 0)
        def _():
            carry[...] = jnp.zeros(carry.shape, F32)

        ac, bc = _chunk_scan(a_ref[...], b_ref[...], rev)
        h = bc + ac * carry[...]
        h_ref[...] = h
        hp_ref[...] = _shift1(h, rev, carry[...])
        carry[...] = h[0:1, :] if rev else h[RT - 1:RT, :]

    spec = pl.BlockSpec((RT, tw), lambda j, k: (_scan_order(d, k, nlat, nctx), j))
    return pl.pallas_call(body, name=name, grid=(W // tw, geo.nt), in_specs=[spec, spec], out_specs=[spec, spec],
                          out_shape=[S_((geo.T, W), F32)] * 2, scratch_shapes=[pltpu.VMEM((1, tw), F32)],
                          compiler_params=_cp())(a, b)


def _lru_scan_bwd(geo, dh, a, hprev, d, name):
    W = a.shape[1]
    tw = _tile(W, 256, 128)
    RT, nlat, nctx, nt = geo.RT, geo.nlat, geo.nctx, geo.nt
    rev = d == 1

    def body(dh_ref, a_ref, hp_ref, da_ref, db_ref, carry):
        k = pl.program_id(1)

        @pl.when(k == 0)
        def _():
            carry[...] = jnp.zeros(carry.shape, F32)

        av = a_ref[...]
        a_next = _shift1(av, not rev, jnp.ones((1, tw), F32))
        ac, bc = _chunk_scan(a_next, dh_ref[...], not rev)
        lam = bc + ac * carry[...]
        db_ref[...] = lam
        da_ref[...] = lam * hp_ref[...]
        first = (av * lam)[RT - 1:RT, :] if rev else (av * lam)[0:1, :]
        carry[...] = first

    spec = pl.BlockSpec((RT, tw), lambda j, k: (_scan_order(d, nt - 1 - k, nlat, nctx), j))
    return pl.pallas_call(body, name=name, grid=(W // tw, nt), in_specs=[spec] * 3, out_specs=[spec] * 2,
                          out_shape=[S_((geo.T, W), F32)] * 2, scratch_shapes=[pltpu.VMEM((1, tw), F32)],
                          compiler_params=_cp())(dh, a, hprev)


def _conv_bwd_assemble(geo, P, du, pieces, W, cw, name):
    RT = geo.RT

    def body(pp, pc, pn, dup, duc, dun, dg, dq, dk, dv, dog, cw_ref, dP_ref, dcw, dcb, ext_r, ext_d):
        i = pl.program_id(0)
        _fill_ext(geo, ext_r, pp, pc, pn, i)
        _fill_ext(geo, ext_d, dup, duc, dun, i)
        du_c = duc[...]
        rows = []
        dr = None
        for k in range(CONV_W):
            rows.append(jnp.sum(du_c * ext_r[pl.ds(HALO - CONV_LEFT + k, RT), :], axis=0, keepdims=True))
            t = ext_d[pl.ds(HALO + CONV_LEFT - k, RT), :] * cw_ref[k:k + 1, :]
            dr = t if dr is None else dr + t
        first = i == 0
        _acc_rows(dcw, first, rows)
        _acc_rows(dcb, first, [jnp.sum(du_c, axis=0, keepdims=True)])
        for j, v in enumerate((dg[...], dr, dq[...], dk[...], dv[...], dog[...])):
            dP_ref[:, j * W:(j + 1) * W] = v

    row = geo.row(W)
    return pl.pallas_call(
        body, name=name, grid=(geo.nt,),
        in_specs=[geo.prev(W, 1), geo.row(W, 1), geo.next(W, 1), geo.prev(W), row, geo.next(W)] + [row] * 5 + [_full(cw.shape)],
        out_specs=[geo.row(6 * W), _full((CONV_W, W)), _full((1, W))],
        out_shape=[S_((geo.T, 6 * W), F32), S_((CONV_W, W), F32), S_((1, W), F32)],
        scratch_shapes=[pltpu.VMEM((RT + 2 * HALO, W), F32)] * 2, compiler_params=_cp())(P, P, P, du, du, du, *pieces, cw)


def _rot_half(x, cos, sin):
    hw = x.shape[1] // 2
    x1, x2 = x[:, :hw], x[:, hw:]
    return jnp.concatenate([x1 * cos - x2 * sin, x1 * sin + x2 * cos], axis=-1)


def _ret_chunk(q, k, v, s, logit, cos, sin, rev):
    C, dk = q.shape
    lg = -jax.nn.softplus(-logit)
    q = _rot_half(q, cos, sin)
    k = _rot_half(k, cos, sin) * (dk ** -0.5)
    i = lax.broadcasted_iota(jnp.int32, (C, 1), 0).astype(F32)
    j = lax.broadcasted_iota(jnp.int32, (1, C), 1).astype(F32)
    if rev:
        diff, qe, ke = j - i, C - i, i
    else:
        diff, qe, ke = i - j, i + 1.0, C - 1.0 - i
    intra = jnp.where(diff >= 0, jnp.exp(lg * jnp.maximum(diff, 0.0)), 0.0)
    scores = _bdot(q, k, "nt") * intra
    o = _bdot(scores, v, "nn") + _bdot(q * jnp.exp(lg * qe), s, "nn")
    s_new = s * jnp.exp(lg * C) + _bdot(k * jnp.exp(lg * ke), v, "tn")
    return o, s_new


def _ret_specs(P, W, H, d, nlc, ncc, order_of_step):
    C = RET_CHUNK
    dk = W // H

    def cidx(k):
        return _scan_order(d, order_of_step(k), nlc, ncc)

    per_w = W // dk
    q = pl.BlockSpec((C, dk), lambda h, k: (cidx(k), 2 * per_w + h))
    kk = pl.BlockSpec((C, dk), lambda h, k: (cidx(k), 3 * per_w + h))
    v = pl.BlockSpec((C, dk), lambda h, k: (cidx(k), 4 * per_w + h))
    tab = pl.BlockSpec((C, dk // 2), lambda h, k: (cidx(k), 0))
    logit = pl.BlockSpec((None, 1, 1), lambda h, k: (h, 0, 0))
    o = pl.BlockSpec((C, dk), lambda h, k: (cidx(k), h))
    return q, kk, v, tab, logit, o


def _ret_fwd(geo, P, W, H, logit_d, cos, sin, d, name):
    C = RET_CHUNK
    dk = W // H
    nlc, ncc = geo.S // C, geo.LC // C
    nch = nlc + ncc
    rev = d == 1

    def body(q_ref, k_ref, v_ref, cos_ref, sin_ref, lg_ref, o_ref, st_ref, s_scr):
        k = pl.program_id(1)

        @pl.when(k == 0)
        def _():
            s_scr[...] = jnp.zeros(s_scr.shape, F32)

        st_ref[...] = s_scr[...]
        o, s_new = _ret_chunk(q_ref[...], k_ref[...], v_ref[...], s_scr[...], lg_ref[...], cos_ref[...], sin_ref[...], rev)
        o_ref[...] = o
        s_scr[...] = s_new

    q, kk, v, tab, logit, o = _ret_specs(P, W, H, d, nlc, ncc, lambda k: k)
    st = pl.BlockSpec((None, None, dk, dk), lambda h, k: (h, k, 0, 0))
    return pl.pallas_call(body, name=name, grid=(H, nch), in_specs=[q, kk, v, tab, tab, logit], out_specs=[o, st],
                          out_shape=[S_((geo.T, W), F32), S_((H, nch, dk, dk), F32)],
                          scratch_shapes=[pltpu.VMEM((dk, dk), F32)], compiler_params=_cp())(P, P, P, cos, sin, logit_d)


def _ret_bwd(geo, P, W, H, logit_d, cos, sin, states, do, prev, d, name):
    C = RET_CHUNK
    dk = W // H
    nlc, ncc = geo.S // C, geo.LC // C
    nch = nlc + ncc
    rev = d == 1
    np_ = 0 if prev is None else 3

    def body(*refs):
        q_ref, k_ref, v_ref, cos_ref, sin_ref, lg_ref, st_ref, do_ref = refs[:8]
        prevs = refs[8:8 + np_]
        dq_ref, dk_ref, dv_ref, dlg_ref, ds_scr = refs[8 + np_:]
        k = pl.program_id(1)

        @pl.when(k == 0)
        def _():
            ds_scr[...] = jnp.zeros(ds_scr.shape, F32)
            dlg_ref[...] = jnp.zeros(dlg_ref.shape, F32)

        cos, sin = cos_ref[...], sin_ref[...]
        _, vjp = jax.vjp(lambda a, b, c, s, lg: _ret_chunk(a, b, c, s, lg, cos, sin, rev),
                         q_ref[...], k_ref[...], v_ref[...], st_ref[...], lg_ref[...])
        dq, dkk, dv, ds, dlg = vjp((do_ref[...], ds_scr[...]))
        if prev is not None:
            dq, dkk, dv = dq + prevs[0][...], dkk + prevs[1][...], dv + prevs[2][...]
        dq_ref[...] = dq
        dk_ref[...] = dkk
        dv_ref[...] = dv
        ds_scr[...] = ds
        dlg_ref[...] += dlg

    q, kk, v, tab, logit, o = _ret_specs(P, W, H, d, nlc, ncc, lambda k: nch - 1 - k)
    st = pl.BlockSpec((None, None, dk, dk), lambda h, k: (h, nch - 1 - k, 0, 0))
    ins = [P, P, P, cos, sin, logit_d, states, do] + ([] if prev is None else list(prev))
    return pl.pallas_call(body, name=name, grid=(H, nch), in_specs=[q, kk, v, tab, tab, logit, st, o] + [o] * np_,
                          out_specs=[o, o, o, logit], out_shape=[S_((geo.T, W), F32)] * 3 + [S_((H, 1, 1), F32)],
                          scratch_shapes=[pltpu.VMEM((dk, dk), F32)], compiler_params=_cp())(*ins)


def _mix_even_math(g, h0, h1, o0, o1, og, gn, H):
    lru = jax.nn.gelu(g) * (h0 + h1)
    o = o0 + o1
    dv = o.shape[1] // H
    parts = []
    for h in range(H):
        oh = o[:, h * dv:(h + 1) * dv]
        mu = jnp.mean(oh, axis=-1, keepdims=True)
        var = jnp.mean(jnp.square(oh - mu), axis=-1, keepdims=True)
        parts.append((oh - mu) * lax.rsqrt(var + EPS))
    y = jnp.concatenate(parts, axis=-1) * gn
    return lru, y * jax.nn.silu(og)


def _mix_even_fwd(geo, P, W, H, h0, h1, o0, o1, gn, name):
    def body(g_ref, h0r, h1r, o0r, o1r, og_ref, gn_ref, m_ref):
        lru, ret = _mix_even_math(g_ref[...], h0r[...], h1r[...], o0r[...], o1r[...], og_ref[...], gn_ref[...], H)
        m_ref[:, :W] = lru.astype(BF)
        m_ref[:, W:] = ret.astype(BF)

    row = geo.row(W)
    return pl.pallas_call(body, name=name, grid=(geo.nt,), in_specs=[geo.row(W, 0), row, row, row, row, geo.row(W, 5), _full((1, W))],
                          out_specs=geo.row(2 * W), out_shape=S_((geo.T, 2 * W), BF), compiler_params=_cp())(P, h0, h1, o0, o1, P, gn)


def _mix_even_bwd(geo, P, W, H, h0, h1, o0, o1, gn, dmix, name):
    def body(g_ref, h0r, h1r, o0r, o1r, og_ref, gn_ref, dl_ref, dr_ref, dg_ref, dh_ref, do_ref, dog_ref, dgn_ref):
        i = pl.program_id(0)
        _, vjp = jax.vjp(lambda g, hs, os_, og, gn_: _mix_even_math(g, hs, 0.0, os_, 0.0, og, gn_, H),
                         g_ref[...], h0r[...] + h1r[...], o0r[...] + o1r[...], og_ref[...], gn_ref[...])
        dg, dh, do, dog, dgn = vjp((dl_ref[...], dr_ref[...]))
        dg_ref[...] = dg
        dh_ref[...] = dh
        do_ref[...] = do
        dog_ref[...] = dog
        _acc_rows(dgn_ref, i == 0, [dgn])

    row = geo.row(W)
    return pl.pallas_call(
        body, name=name, grid=(geo.nt,),
        in_specs=[geo.row(W, 0), row, row, row, row, geo.row(W, 5), _full((1, W)), geo.row(W, 0), geo.row(W, 1)],
        out_specs=[row] * 4 + [_full((1, W))], out_shape=[S_((geo.T, W), F32)] * 4 + [S_((1, W), F32)],
        compiler_params=_cp())(P, h0, h1, o0, o1, P, gn, dmix, dmix)


def _head_norm_rot(x, g, cosf, sinf):
    xn = _rms(x) * g
    return xn * cosf + _rollv(xn, x.shape[1] // 2, 1) * sinf


def _qk_prep_fwd(geo, P, dims, qg, kg, cosf, sinf, name):
    PW, QW, KW, hd = dims

    def body(p_ref, qg_ref, kg_ref, c_ref, s_ref, q_ref, k_ref, v_ref):
        c, s = c_ref[...], s_ref[...]
        for h in range(QW // hd):
            q_ref[:, h * hd:(h + 1) * hd] = _head_norm_rot(p_ref[:, PW + h * hd:PW + (h + 1) * hd], qg_ref[...], c, s).astype(BF)
        for h in range(KW // hd):
            o = PW + QW + h * hd
            k_ref[:, h * hd:(h + 1) * hd] = _head_norm_rot(p_ref[:, o:o + hd], kg_ref[...], c, s).astype(BF)
        v_ref[...] = p_ref[:, PW + QW + KW:].astype(BF)

    tot = PW + QW + 2 * KW
    return pl.pallas_call(
        body, name=name, grid=(geo.nt,),
        in_specs=[geo.row(tot), _full((1, hd)), _full((1, hd)), geo.row(hd), geo.row(hd)],
        out_specs=[geo.row(QW), geo.row(KW), geo.row(KW)],
        out_shape=[S_((geo.T, QW), BF), S_((geo.T, KW), BF), S_((geo.T, KW), BF)], compiler_params=_cp())(P, qg, kg, cosf, sinf)


def _qk_prep_bwd(geo, P, dims, qg, kg, cosf, sinf, dpool, dq, dk, dv, name):
    PW, QW, KW, hd = dims

    def body(p_ref, qg_ref, kg_ref, c_ref, s_ref, dpool_ref, dq_ref, dk_ref, dv_ref, dP_ref, dqg_ref, dkg_ref):
        i = pl.program_id(0)
        c, s = c_ref[...], s_ref[...]
        dP_ref[:, :PW] = dpool_ref[...]
        f = lambda x, g: _head_norm_rot(x, g, c, s)
        dqg = jnp.zeros((1, hd), F32)
        for h in range(QW // hd):
            o = PW + h * hd
            _, vjp = jax.vjp(f, p_ref[:, o:o + hd], qg_ref[...])
            dx, dg = vjp(dq_ref[:, h * hd:(h + 1) * hd])
            dP_ref[:, o:o + hd] = dx
            dqg = dqg + dg
        dkg = jnp.zeros((1, hd), F32)
        for h in range(KW // hd):
            o = PW + QW + h * hd
            _, vjp = jax.vjp(f, p_ref[:, o:o + hd], kg_ref[...])
            dx, dg = vjp(dk_ref[:, h * hd:(h + 1) * hd])
            dP_ref[:, o:o + hd] = dx
            dkg = dkg + dg
        dP_ref[:, PW + QW + KW:] = dv_ref[...]
        _acc_rows(dqg_ref, i == 0, [dqg])
        _acc_rows(dkg_ref, i == 0, [dkg])

    tot = PW + QW + 2 * KW
    return pl.pallas_call(
        body, name=name, grid=(geo.nt,),
        in_specs=[geo.row(tot), _full((1, hd)), _full((1, hd)), geo.row(hd), geo.row(hd), geo.row(PW), geo.row(QW), geo.row(KW), geo.row(KW)],
        out_specs=[geo.row(tot), _full((1, hd)), _full((1, hd))],
        out_shape=[S_((geo.T, tot), F32), S_((1, hd), F32), S_((1, hd), F32)], compiler_params=_cp())(
            P, qg, kg, cosf, sinf, dpool, dq, dk, dv)


def _att_tiles(T):
    return _tile(T, 256, 16), _tile(T, 768, 16)


def _stack_heads(ref, G, hd):
    return jnp.concatenate([ref[:, g * hd:(g + 1) * hd] for g in range(G)], axis=0)


def _att_fwd(q, k, v, hd, name):
    T, QW = q.shape
    KV = k.shape[1] // hd
    G = QW // hd // KV
    tq, tk = _att_tiles(T)
    nk = T // tk
    scale = hd ** -0.5

    def body(q_ref, k_ref, v_ref, o_ref, lse_ref, m_scr, l_scr, acc):
        ki = pl.program_id(2)

        @pl.when(ki == 0)
        def _():
            m_scr[...] = jnp.full(m_scr.shape, -jnp.inf, F32)
            l_scr[...] = jnp.zeros(l_scr.shape, F32)
            acc[...] = jnp.zeros(acc.shape, F32)

        s = _dot(_stack_heads(q_ref, G, hd), k_ref[...], NT) * scale
        m_prev = m_scr[...]
        m_new = jnp.maximum(m_prev, jnp.max(s, axis=-1, keepdims=True))
        alpha = jnp.exp(m_prev - m_new)
        p = jnp.exp(s - m_new)
        l_scr[...] = alpha * l_scr[...] + jnp.sum(p, axis=-1, keepdims=True)
        acc[...] = alpha * acc[...] + _dot(p, v_ref[...], NN)
        m_scr[...] = m_new

        @pl.when(ki == nk - 1)
        def _():
            o = acc[...] / l_scr[...]
            lse = m_scr[...] + jnp.log(l_scr[...])
            for g in range(G):
                o_ref[:, g * hd:(g + 1) * hd] = o[g * tq:(g + 1) * tq].astype(BF)
                lse_ref[g] = lse[g * tq:(g + 1) * tq]

    return pl.pallas_call(
        body, name=name, grid=(KV, T // tq, nk),
        in_specs=[pl.BlockSpec((tq, G * hd), lambda a, i, j: (i, a)), pl.BlockSpec((tk, hd), lambda a, i, j: (j, a)),
                  pl.BlockSpec((tk, hd), lambda a, i, j: (j, a))],
        out_specs=[pl.BlockSpec((tq, G * hd), lambda a, i, j: (i, a)), pl.BlockSpec((G, tq, 1), lambda a, i, j: (a, i, 0))],
        out_shape=[S_((T, QW), BF), S_((QW // hd, T, 1), F32)],
        scratch_shapes=[pltpu.VMEM((G * tq, 1), F32), pltpu.VMEM((G * tq, 1), F32), pltpu.VMEM((G * tq, hd), F32)],
        compiler_params=_cp())(q, k, v)


def _att_delta(geo, o, dmix, PW, hd, name):
    QW = o.shape[1]
    nh = QW // hd

    def body(o_ref, do_ref, d_ref):
        for h in range(nh):
            sl = slice(h * hd, (h + 1) * hd)
            d_ref[h] = jnp.sum(o_ref[:, sl].astype(F32) * do_ref[:, PW + h * hd:PW + (h + 1) * hd], axis=-1, keepdims=True)

    return pl.pallas_call(body, name=name, grid=(geo.nt,), in_specs=[geo.row(QW), geo.row(PW + QW)],
                          out_specs=pl.BlockSpec((nh, geo.RT, 1), lambda i: (0, i, 0)), out_shape=S_((nh, geo.T, 1), F32),
                          compiler_params=_cp())(o, dmix)


def _att_bwd_common(q_ref, k_ref, v_ref, do_refs, lse_ref, dl_ref, G, hd, scale):
    q3 = _stack_heads(q_ref, G, hd)
    do3 = jnp.concatenate([r[...] for r in do_refs], axis=0)
    lse = jnp.concatenate([lse_ref[g] for g in range(G)], axis=0)
    dl = jnp.concatenate([dl_ref[g] for g in range(G)], axis=0)
    p = jnp.exp(_dot(q3, k_ref[...], NT) * scale - lse)
    dp = _dot(do3, v_ref[...], NT)
    ds = p * (dp - dl) * scale
    return q3, do3, p, ds


def _att_bwd_q(q, k, v, dmix, PW, lse, delta, hd, name):
    T, QW = q.shape
    KV = k.shape[1] // hd
    G = QW // hd // KV
    tq, tk = _att_tiles(T)
    nk = T // tk
    scale = hd ** -0.5
    pb = PW // hd

    def body(q_ref, k_ref, v_ref, *rest):
        do_refs, (lse_ref, dl_ref, dq_ref, acc) = rest[:G], rest[G:]
        ki = pl.program_id(2)
        _, _, _, ds = _att_bwd_common(q_ref, k_ref, v_ref, do_refs, lse_ref, dl_ref, G, hd, scale)
        part = _dot(ds, k_ref[...], NN)

        @pl.when(ki == 0)
        def _():
            acc[...] = part

        @pl.when(ki > 0)
        def _():
            acc[...] += part

        @pl.when(ki == nk - 1)
        def _():
            for g in range(G):
                dq_ref[:, g * hd:(g + 1) * hd] = acc[g * tq:(g + 1) * tq]

    qs = pl.BlockSpec((tq, G * hd), lambda a, i, j: (i, a))
    ks = pl.BlockSpec((tk, hd), lambda a, i, j: (j, a))
    dos = [pl.BlockSpec((tq, hd), lambda a, i, j, g=g: (i, pb + a * G + g)) for g in range(G)]
    st = pl.BlockSpec((G, tq, 1), lambda a, i, j: (a, i, 0))
    return pl.pallas_call(body, name=name, grid=(KV, T // tq, nk), in_specs=[qs, ks, ks] + dos + [st, st], out_specs=qs,
                          out_shape=S_((T, QW), F32), scratch_shapes=[pltpu.VMEM((G * tq, hd), F32)],
                          compiler_params=_cp())(q, k, v, *([dmix] * G), lse, delta)


def _att_bwd_kv(q, k, v, dmix, PW, lse, delta, hd, name):
    T, QW = q.shape
    KW = k.shape[1]
    KV = KW // hd
    G = QW // hd // KV
    tq, tk = _att_tiles(T)
    nq = T // tq
    scale = hd ** -0.5
    pb = PW // hd

    def body(q_ref, k_ref, v_ref, *rest):
        do_refs, (lse_ref, dl_ref, dk_ref, dv_ref) = rest[:G], rest[G:]
        qi = pl.program_id(2)
        q3, do3, p, ds = _att_bwd_common(q_ref, k_ref, v_ref, do_refs, lse_ref, dl_ref, G, hd, scale)
        pk = _dot(ds, q3, TN)
        pv = _dot(p, do3, TN)

        @pl.when(qi == 0)
        def _():
            dk_ref[...] = pk
            dv_ref[...] = pv

        @pl.when(qi > 0)
        def _():
            dk_ref[...] += pk
            dv_ref[...] += pv

    qs = pl.BlockSpec((tq, G * hd), lambda a, j, i: (i, a))
    ks = pl.BlockSpec((tk, hd), lambda a, j, i: (j, a))
    dos = [pl.BlockSpec((tq, hd), lambda a, j, i, g=g: (i, pb + a * G + g)) for g in range(G)]
    st = pl.BlockSpec((G, tq, 1), lambda a, j, i: (a, i, 0))
    return pl.pallas_call(body, name=name, grid=(KV, T // tk, nq), in_specs=[qs, ks, ks] + dos + [st, st], out_specs=[ks, ks],
                          out_shape=[S_((T, KW), F32)] * 2, compiler_params=_cp())(q, k, v, *([dmix] * G), lse, delta)


def _pool_cnt(pos, L, w):
    return (jnp.minimum(pos + w // 2, L) - jnp.maximum(pos - w // 2, 0)).astype(F32)


def _pool_mean(ext, gi, w, G, RT, cnt):
    acc = ext[pl.ds(HALO - w // 2, RT), gi * G:(gi + 1) * G]
    for off in range(-w // 2 + 1, w // 2):
        acc = acc + ext[pl.ds(HALO + off, RT), gi * G:(gi + 1) * G]
    return acc / cnt


def _pool_fwd(geo, P, PW, att, pw, ps, name):
    RT = geo.RT
    G = pw.shape[1]
    QW = att.shape[1]

    def body(pp, pc, pn, att_ref, pw_ref, ps_ref, m_ref, ext):
        i = pl.program_id(0)
        _fill_ext(geo, ext, pp, pc, pn, i)
        pos, L = geo.pos(i)
        for gi, w in enumerate(POOL_WINDOWS):
            sl = slice(gi * G, (gi + 1) * G)
            xm = _pool_mean(ext, gi, w, G, RT, _pool_cnt(pos, L, w)) - pc[:, sl]
            m_ref[:, sl] = (_dot(xm, pw_ref[gi], NN) * ps_ref[:, sl]).astype(BF)
        m_ref[:, PW:] = att_ref[...]

    return pl.pallas_call(
        body, name=name, grid=(geo.nt,),
        in_specs=[geo.prev(PW), geo.row(PW), geo.next(PW), geo.row(QW), _full(pw.shape), _full(ps.shape)],
        out_specs=geo.row(PW + QW), out_shape=S_((geo.T, PW + QW), BF),
        scratch_shapes=[pltpu.VMEM((RT + 2 * HALO, PW), F32)], compiler_params=_cp())(P, P, P, att, pw, ps)


def _pool_bwd(geo, P, PW, dmix, pw, ps, name):
    RT = geo.RT
    G = pw.shape[1]
    RE = RT + 2 * HALO

    def body(pp, pc, pn, dp_, dc, dn, pw_ref, ps_ref, dx_ref, dpw, dps, ext, extd, dmc):
        i = pl.program_id(0)
        _fill_ext(geo, ext, pp, pc, pn, i)
        _fill_ext(geo, extd, dp_, dc, dn, i)
        pos, L = geo.pos(i)
        r = lax.broadcasted_iota(jnp.int32, (RE, 1), 0)
        pos_e = pos[0:1, :] - HALO + r
        rows_s = []
        for gi, w in enumerate(POOL_WINDOWS):
            sl = slice(gi * G, (gi + 1) * G)
            xm = _pool_mean(ext, gi, w, G, RT, _pool_cnt(pos, L, w)) - pc[:, sl]
            pre = _dot(xm, pw_ref[gi], NN)
            dout = dc[:, sl]
            rows_s.append(jnp.sum(dout * pre, axis=0, keepdims=True))
            gw = _dot(xm, dout * ps_ref[:, sl], TN)

            @pl.when(i == 0)
            def _(gi=gi, gw=gw):
                dpw[gi] = gw

            @pl.when(i > 0)
            def _(gi=gi, gw=gw):
                dpw[gi] += gw

            dm_e = _dot(extd[:, sl] * ps_ref[:, sl], pw_ref[gi], NT)
            dmc[...] = dm_e / jnp.maximum(_pool_cnt(pos_e, L, w), 1.0)
            acc = -dm_e[HALO:HALO + RT]
            for off in range(-w // 2 + 1, w // 2 + 1):
                acc = acc + dmc[pl.ds(HALO + off, RT), :]
            dx_ref[:, sl] = acc
        _acc_rows(dps, i == 0, [jnp.concatenate(rows_s, axis=-1)])

    return pl.pallas_call(
        body, name=name, grid=(geo.nt,),
        in_specs=[geo.prev(PW), geo.row(PW), geo.next(PW), geo.prev(PW), geo.row(PW), geo.next(PW), _full(pw.shape), _full(ps.shape)],
        out_specs=[geo.row(PW), _full(pw.shape), _full(ps.shape)],
        out_shape=[S_((geo.T, PW), F32), S_(pw.shape, F32), S_(ps.shape, F32)],
        scratch_shapes=[pltpu.VMEM((RE, PW), F32), pltpu.VMEM((RE, PW), F32), pltpu.VMEM((RE, G), F32)],
        compiler_params=_cp())(P, P, P, dmix, dmix, dmix, pw, ps)


def _mod_fwd(A, mod_w, name):
    L, D, MC = mod_w.shape
    tn = _tile(MC, 768, 128)

    def body(a_ref, w_ref, o_ref):
        o_ref[...] = _dot(jax.nn.silu(a_ref[...]), w_ref[...], NN)

    return pl.pallas_call(body, name=name, grid=(L, MC // tn),
                          in_specs=[_full(A.shape), pl.BlockSpec((None, D, tn), lambda l, j: (l, 0, j))],
                          out_specs=pl.BlockSpec((None, 16, tn), lambda l, j: (l, 0, j)), out_shape=S_((L, 16, MC), F32),
                          compiler_params=_cp())(A, mod_w)


def _mod_bwd(A, DM, mod_w, name):
    L, D, MC = mod_w.shape
    tn = _tile(MC, 768, 128)
    nj = MC // tn

    def body(a_ref, dm_ref, w_ref, gw_ref, da_ref, acc):
        l, j = pl.program_id(0), pl.program_id(1)
        sa, vjp = jax.vjp(jax.nn.silu, a_ref[...])
        gw_ref[...] = _dot(sa, dm_ref[...], TN)
        part = _dot(dm_ref[...], w_ref[...], NT)
        first = jnp.logical_and(l == 0, j == 0)

        @pl.when(first)
        def _():
            acc[...] = part

        @pl.when(jnp.logical_not(first))
        def _():
            acc[...] += part

        @pl.when(jnp.logical_and(l == L - 1, j == nj - 1))
        def _():
            da_ref[...] = vjp(acc[...])[0]

    wspec = pl.BlockSpec((None, D, tn), lambda l, j: (l, 0, j))
    return pl.pallas_call(body, name=name, grid=(L, nj),
                          in_specs=[_full(A.shape), pl.BlockSpec((None, 16, tn), lambda l, j: (l, 0, j)), wspec],
                          out_specs=[wspec, _full(A.shape)], out_shape=[S_((L, D, MC), F32), S_(A.shape, F32)],
                          scratch_shapes=[pltpu.VMEM(A.shape, F32)], compiler_params=_cp())(A, DM, mod_w)


PACK_COLS = 1024


def _pack(arrs):
    flat = jnp.concatenate([a.reshape(-1).astype(F32) for a in arrs])
    n = flat.shape[0]
    npad = -(-n // (8 * PACK_COLS)) * (8 * PACK_COLS)
    return jnp.pad(flat, (0, npad - n)).reshape(npad // PACK_COLS, PACK_COLS)


def _unpack(packed, shapes, lead=()):
    flat = packed.reshape(lead + (-1,))
    out, off = [], 0
    for s in shapes:
        n = 1
        for d in s:
            n *= d
        out.append(flat[..., off:off + n].reshape(lead + tuple(s)))
        off += n
    return out


def _unshard_last(g):
    g = jnp.moveaxis(g, 0, -2)
    return g.reshape(g.shape[:-2] + (g.shape[-2] * g.shape[-1],))


def _my_shard(a, me):
    n = a.shape[-1] // NDEV
    return lax.dynamic_slice_in_dim(a, me * n, n, axis=a.ndim - 1)


def _rot_tables(S, LC, dk, hd):
    t = jnp.arange(S, dtype=F32)
    n_r = dk // 2
    ang1 = t[:, None] * (RET_THETA ** (-jnp.arange(n_r, dtype=F32) / n_r))
    cos1 = jnp.concatenate([jnp.cos(ang1), jnp.ones((LC, n_r), F32)])
    sin1 = jnp.concatenate([jnp.sin(ang1), jnp.zeros((LC, n_r), F32)])
    n_ax = hd // 4
    f_ax = ROPE_THETA ** (-jnp.arange(n_ax, dtype=F32) / n_ax)
    row = jnp.floor(t / GRID_W)
    col = t - row * GRID_W
    ang2 = jnp.concatenate([row[:, None] * f_ax, col[:, None] * f_ax], axis=-1)
    c2, s2 = jnp.cos(ang2), jnp.sin(ang2)
    cosf = jnp.concatenate([jnp.concatenate([c2, c2], axis=-1), jnp.ones((LC, hd), F32)])
    sinf = jnp.concatenate([jnp.concatenate([-s2, s2], axis=-1), jnp.zeros((LC, hd), F32)])
    return cos1, sin1, cosf, sinf


SMALL = ("c_ctx", "mod_b", "norm_pre", "norm_post", "lru_conv_w", "lru_conv_b", "lru_wa", "lru_ba", "lru_wx", "lru_bx",
         "lru_lambda", "ret_decay_logit", "ret_gn", "pool_w", "pool_scale", "q_norm", "k_norm")
WEIGHTS = ("c_ctx", "mod_w", "mod_b", "norm_pre", "norm_post", "ffn_gate", "ffn_up", "ffn_down", "ev_w_in", "ev_w_out",
           "lru_conv_w", "lru_conv_b", "lru_wa", "lru_ba", "lru_wx", "lru_bx", "lru_lambda", "ret_decay_logit", "ret_gn",
           "od_w_in", "od_w_out", "pool_w", "pool_scale", "q_norm", "k_norm")
INPUTS = ("x", "c", "ctx") + WEIGHTS + ("loss_target",) + tuple("m_" + w for w in WEIGHTS) + tuple("v_" + w for w in WEIGHTS)


def _step(p):
    x, c, ctx = p["x"], p["c"], p["ctx"]
    _, S, D = x.shape
    LC = ctx.shape[1]
    geo = _Geo(S, LC, D)
    T = geo.T
    xi, yi, ci = _me()
    me = 4 * xi + 2 * yi + ci
    L = p["mod_w"].shape[0]
    assert L == 2
    W = p["lru_conv_b"].shape[-1]
    H = p["ret_decay_logit"].shape[-1]
    hd = p["q_norm"].shape[-1]
    G = p["pool_w"].shape[-1]
    PW = G * len(POOL_WINDOWS)
    od_mix = p["od_w_out"].shape[1] * NDEV
    od_in = p["od_w_in"].shape[2] * NDEV
    QW = od_mix - PW
    KW = (od_in - od_mix) // 2
    assert p["ev_w_in"].shape[2] * NDEV == 6 * W and p["ret_gn"].shape[-1] == W and p["ev_w_out"].shape[1] * NDEV == 2 * W
    odims = (PW, QW, KW, hd)
    cos1, sin1, cosf, sinf = _rot_tables(S, LC, W // H, hd)

    sh0 = [(D,), p["norm_pre"].shape, p["norm_post"].shape, p["lru_conv_w"].shape[1:], p["lru_ba"].shape[1:],
           p["lru_bx"].shape[1:], p["lru_lambda"].shape[1:], p["pool_scale"].shape[1:]]
    pack0 = _pack([c, p["norm_pre"], p["norm_post"], p["lru_conv_w"], p["lru_ba"], p["lru_bx"], p["lru_lambda"], p["pool_scale"]])
    (g0,) = _all_gather([pack0], "gather_small")
    c_all, npre, npost, conv_w, ba, bx, lam, pscale = _unpack(g0, sh0, (NDEV,))
    npre, npost, conv_w, ba, bx, lam, pscale = [_unshard_last(a) for a in (npre, npost, conv_w, ba, bx, lam, pscale)]
    pscale = pscale[None]
    conv_b = p["lru_conv_b"]
    wa, wx = p["lru_wa"][0], p["lru_wx"][0]
    gn = p["ret_gn"]
    logits = p["ret_decay_logit"][0].reshape(2, H, 1, 1)
    pool_w = p["pool_w"][0]
    qg, kg = p["q_norm"], p["k_norm"]

    A = jnp.concatenate([c_all, p["c_ctx"][None], jnp.zeros((7, D), F32)])
    M = _mod_fwd(A, p["mod_w"], "mod_fwd")
    (Mg,) = _all_gather([M], "gather_mod")
    MC = M.shape[2]
    tabs = []
    for l in range(L):
        ml = lax.dynamic_index_in_dim(Mg[:, l], me, axis=1, keepdims=False).reshape(NDEV * MC) + p["mod_b"][l]
        mc = Mg[:, l, 8].reshape(NDEV * MC) + p["mod_b"][l]
        tabs.append(jnp.stack([ml.reshape(9, D), mc.reshape(9, D)]))

    def cast2(a, name):
        return _cast_bf16(a.reshape(-1, a.shape[-1]), name).reshape(a.shape)

    ffn_w = {}
    for l in range(L):
        for j in range(2):
            loc = [cast2(p[n][l, j], f"cast_{n}_{l}{j}") for n in ("ffn_gate", "ffn_up", "ffn_down")]
            ffn_w[l, j] = _all_gather(loc, f"gather_ffn_{l}{j}")
    ev_in, ev_out = _all_gather([cast2(p["ev_w_in"][0], "cast_ev_in"), cast2(p["ev_w_out"][0], "cast_ev_out")], "gather_ev")
    od_inw, od_out = _all_gather([cast2(p["od_w_in"][0], "cast_od_in"), cast2(p["od_w_out"][0], "cast_od_out")], "gather_od")
    ev_out_f = ev_out.reshape(2 * W, D)
    od_out_f = od_out.reshape(od_mix, D)

    def gp(a, l, s):
        return a[l, s][None]

    x0 = jnp.concatenate([x[0], ctx[0]])
    (h0,) = _norm_fwd(geo, x0, None, (gp(npre, 0, 0), tabs[0], 0), "pre_00")
    y0, G0, U0 = _ffn_fwd(h0, *ffn_w[0, 0], name="ffn_fwd_00")
    x1, h1 = _norm_fwd(geo, x0, (y0, gp(npost, 0, 0), tabs[0], 0, FFN_STEP), (gp(npre, 0, 1), tabs[0], 1), "post_00")

    Pe = _mm_cols(h1, ev_in, "ev_in")
    u, a0, b0, a1, b1 = _lru_coef_fwd(geo, Pe, W, conv_w, conv_b, wa, ba, wx, bx, lam, "lru_coef")
    hs0, hp0 = _lru_scan_fwd(geo, a0, b0, 0, "lru_scan_f0")
    hs1, hp1 = _lru_scan_fwd(geo, a1, b1, 1, "lru_scan_f1")
    o0, st0 = _ret_fwd(geo, Pe, W, H, logits[0], cos1, sin1, 0, "ret_f0")
    o1, st1 = _ret_fwd(geo, Pe, W, H, logits[1], cos1, sin1, 1, "ret_f1")
    mixe = _mix_even_fwd(geo, Pe, W, H, hs0, hs1, o0, o1, gn, "mix_even")
    y1 = _mm_full(mixe, ev_out_f, NN, "ev_out")
    x2, h2 = _norm_fwd(geo, x1, (y1, gp(npost, 0, 1), tabs[0], 1, 1.0), (gp(npre, 0, 2), tabs[0], 2), "post_01")

    y2, G2, U2 = _ffn_fwd(h2, *ffn_w[0, 1], name="ffn_fwd_01")
    x3, h3 = _norm_fwd(geo, x2, (y2, gp(npost, 0, 2), tabs[0], 2, FFN_STEP), (gp(npre, 1, 0), tabs[1], 0), "post_02")

    y3, G3, U3 = _ffn_fwd(h3, *ffn_w[1, 0], name="ffn_fwd_10")
    x4, h4 = _norm_fwd(geo, x3, (y3, gp(npost, 1, 0), tabs[1], 0, FFN_STEP), (gp(npre, 1, 1), tabs[1], 1), "post_10")

    Po = _mm_cols(h4, od_inw, "od_in")
    qr, kr, vr = _qk_prep_fwd(geo, Po, odims, qg, kg, cosf, sinf, "qk_prep")
    att, lse = _att_fwd(qr, kr, vr, hd, "att_fwd")
    mixo = _pool_fwd(geo, Po, PW, att, pool_w, pscale, "pool_fwd")
    y4 = _mm_full(mixo, od_out_f, NN, "od_out")
    x5, h5 = _norm_fwd(geo, x4, (y4, gp(npost, 1, 1), tabs[1], 1, 1.0), (gp(npre, 1, 2), tabs[1], 2), "post_11")

    y5, G5, U5 = _ffn_fwd(h5, *ffn_w[1, 1], name="ffn_fwd_11")
    (x6,) = _norm_fwd(geo, x5, (y5, gp(npost, 1, 2), tabs[1], 2, FFN_STEP), None, "post_12")

    big_g = {}

    def ffn_bwd(dy, h, Gs, Us, key):
        tag = f"{key[0]}{key[1]}"
        dh, dG, dU, Aact = _ffn_bwd_act(dy, Gs, Us, *ffn_w[key], name=f"ffn_bwd_{tag}")
        big_g[key] = _all_to_all([_ffn_wgrad_in(h, dG, f"ffn_wg_{tag}"), _ffn_wgrad_in(h, dU, f"ffn_wu_{tag}"),
                                  _ffn_wgrad_out(Aact, dy, f"ffn_wd_{tag}")], f"a2a_ffn_{tag}")
        return dh

    loss_p, dx6, dy5, dpost5 = _loss_bwd(geo, x6, p["loss_target"][0], (y5, gp(npost, 1, 2), tabs[1], 2, FFN_STEP), "loss")
    dh5 = ffn_bwd(dy5, h5, G5, U5, (1, 1))
    dx5, dy4, dpre5, dpost4 = _norm_bwd(geo, dx6, dh5, x5, (gp(npre, 1, 2), tabs[1], 2),
                                        (y4, gp(npost, 1, 1), tabs[1], 1, 1.0), "nb_5")

    dmixo = _mm_full(dy4, od_out_f, NT, "od_out_d")
    g_od_out = _mm_tn_rows(mixo, dy4, NDEV, "od_out_w")
    dpool, g_pool_w, g_pscale = _pool_bwd(geo, Po, PW, dmixo, pool_w, pscale, "pool_bwd")
    delta = _att_delta(geo, att, dmixo, PW, hd, "att_delta")
    dq = _att_bwd_q(qr, kr, vr, dmixo, PW, lse, delta, hd, "att_bwd_q")
    dk, dv = _att_bwd_kv(qr, kr, vr, dmixo, PW, lse, delta, hd, "att_bwd_kv")
    dPo, g_qn, g_kn = _qk_prep_bwd(geo, Po, odims, qg, kg, cosf, sinf, dpool, dq, dk, dv, "qk_prep_bwd")
    dh4 = _mm_nt_cols(dPo, od_inw, "od_in_d")
    g_od_in = _mm_tn_cols(h4, dPo, NDEV, "od_in_w")
    big_g["od"] = _all_to_all([g_od_in, g_od_out], "a2a_od")
    dx4, dy3, dpre4, dpost3 = _norm_bwd(geo, dx5, dh4, x4, (gp(npre, 1, 1), tabs[1], 1),
                                        (y3, gp(npost, 1, 0), tabs[1], 0, FFN_STEP), "nb_4")

    dh3 = ffn_bwd(dy3, h3, G3, U3, (1, 0))
    dx3, dy2, dpre3, dpost2 = _norm_bwd(geo, dx4, dh3, x3, (gp(npre, 1, 0), tabs[1], 0),
                                        (y2, gp(npost, 0, 2), tabs[0], 2, FFN_STEP), "nb_3")

    dh2 = ffn_bwd(dy2, h2, G2, U2, (0, 1))
    dx2, dy1, dpre2, dpost1 = _norm_bwd(geo, dx3, dh2, x2, (gp(npre, 0, 2), tabs[0], 2),
                                        (y1, gp(npost, 0, 1), tabs[0], 1, 1.0), "nb_2")

    dmixe = _mm_full(dy1, ev_out_f, NT, "ev_out_d")
    g_ev_out = _mm_tn_rows(mixe, dy1, NDEV, "ev_out_w")
    dg, dhs, dos, dog, g_gn = _mix_even_bwd(geo, Pe, W, H, hs0, hs1, o0, o1, gn, dmixe, "mix_even_bwd")
    da0, db0 = _lru_scan_bwd(geo, dhs, a0, hp0, 0, "lru_scan_b0")
    da1, db1 = _lru_scan_bwd(geo, dhs, a1, hp1, 1, "lru_scan_b1")
    du, g_wa, g_ba, g_wx, g_bx, g_lam = _lru_coef_bwd(geo, u, (da0, da1), (db0, db1), W, wa, ba, wx, bx, lam, "lru_coef_bwd")
    dq0, dk0, dv0, glg0 = _ret_bwd(geo, Pe, W, H, logits[0], cos1, sin1, st0, dos, None, 0, "ret_b0")
    dqe, dke, dve, glg1 = _ret_bwd(geo, Pe, W, H, logits[1], cos1, sin1, st1, dos, (dq0, dk0, dv0), 1, "ret_b1")
    dPe, g_cw, g_cb = _conv_bwd_assemble(geo, Pe, du, (dg, dqe, dke, dve, dog), W, conv_w, "conv_bwd")
    dh1 = _mm_nt_cols(dPe, ev_in, "ev_in_d")
    g_ev_in = _mm_tn_cols(h1, dPe, NDEV, "ev_in_w")
    big_g["ev"] = _all_to_all([g_ev_in, g_ev_out], "a2a_ev")
    dx1, dy0, dpre1, dpost0 = _norm_bwd(geo, dx2, dh1, x1, (gp(npre, 0, 1), tabs[0], 1),
                                        (y0, gp(npost, 0, 0), tabs[0], 0, FFN_STEP), "nb_1")

    dh0 = ffn_bwd(dy0, h0, G0, U0, (0, 0))
    dx0, dpre0 = _norm_bwd(geo, dx1, dh0, x0, (gp(npre, 0, 0), tabs[0], 0), None, "nb_0")

    dpre = [[dpre0, dpre1, dpre2], [dpre3, dpre4, dpre5]]
    dpost = [[dpost0, dpost1, dpost2], [dpost3, dpost4, dpost5]]
    dtab = jnp.stack([jnp.stack([jnp.stack([dpre[l][s][:, 1], dpre[l][s][:, 2], dpost[l][s][:, 1]], axis=1) for s in range(3)], axis=1)
                      for l in range(L)])
    dtab_p = _pack([jnp.moveaxis(dtab.reshape(L, 2, 9 * D), 1, 0)])
    (dtab_g,) = _all_gather([dtab_p], "gather_dtab")
    dtab_sum = _sum_n(dtab_g, "sum_dtab")
    (dm_all,) = _unpack(dtab_g, [(2, L, 9 * D)], (NDEV,))
    (dm_sum,) = _unpack(dtab_sum, [(2, L, 9 * D)])
    (g_mod_b,) = _unpack(_sum_n(jnp.stack([_pack([dm_sum[0]]), _pack([dm_sum[1]])]), "sum_mod_b"), [(L, 9 * D)])
    dml = lax.dynamic_slice_in_dim(dm_all[:, 0], me * MC, MC, axis=2)
    dmc = lax.dynamic_slice_in_dim(dm_sum[1], me * MC, MC, axis=1)
    DM = jnp.concatenate([jnp.moveaxis(dml, 0, 1), dmc[:, None], jnp.zeros((L, 7, MC), F32)], axis=1)
    g_mod_w, dA = _mod_bwd(A, DM, p["mod_w"], "mod_bwd")

    g_npre = jnp.stack([jnp.stack([dpre[l][s][0, 0] + dpre[l][s][1, 0] for s in range(3)]) for l in range(L)])
    g_npost = jnp.stack([jnp.stack([dpost[l][s][0, 0] + dpost[l][s][1, 0] for s in range(3)]) for l in range(L)])
    g_logit = jnp.stack([glg0.reshape(H), glg1.reshape(H)])
    small_parts = [dA[8], g_npre, g_npost, g_cw, g_cb, g_wa, g_ba, g_wx, g_bx, g_lam, g_logit, g_gn, g_pool_w, g_pscale, g_qn, g_kn]
    (sg,) = _all_gather([_pack(small_parts)], "gather_small_g")
    ssum = _unpack(_sum_n(sg, "sum_small_g"), [a.shape for a in small_parts])
    (g_cctx, g_npre, g_npost, g_cw, g_cb, g_wa, g_ba, g_wx, g_bx, g_lam, g_logit, g_gn, g_pool_w, g_pscale, g_qn, g_kn) = ssum
    small_g = {
        "c_ctx": g_cctx, "mod_b": g_mod_b, "norm_pre": _my_shard(g_npre, me), "norm_post": _my_shard(g_npost, me),
        "lru_conv_w": _my_shard(g_cw, me)[None], "lru_conv_b": g_cb, "lru_wa": g_wa[None], "lru_ba": _my_shard(g_ba, me)[None],
        "lru_wx": g_wx[None], "lru_bx": _my_shard(g_bx, me)[None], "lru_lambda": _my_shard(g_lam, me)[None],
        "ret_decay_logit": g_logit[None], "ret_gn": g_gn, "pool_w": g_pool_w[None], "pool_scale": _my_shard(g_pscale, me),
        "q_norm": g_qn, "k_norm": g_kn,
    }
    shapes = [p[n].shape for n in SMALL]
    s_out = _reduce_adam(_pack([small_g[n] for n in SMALL])[None], _pack([p[n] for n in SMALL]),
                         _pack([p["m_" + n] for n in SMALL]), _pack([p["v_" + n] for n in SMALL]), "adam_small")
    res = {}
    for kind, packed in zip(("g", "d", "m", "v"), s_out):
        for n, a in zip(SMALL, _unpack(packed, shapes)):
            res[kind, n] = a

    def big(name, pieces, idx=None):
        w, m, v = p[name], p["m_" + name], p["v_" + name]
        if idx is not None:
            w, m, v = w[idx], m[idx], v[idx]
        shp = w.shape
        tag = name + ("" if idx is None else "_" + "".join(str(i) for i in idx))
        outs = _reduce_adam(pieces.reshape((pieces.shape[0], -1, shp[-1])), w.reshape(-1, shp[-1]), m.reshape(-1, shp[-1]),
                            v.reshape(-1, shp[-1]), "adam_" + tag)
        return [o.reshape(shp) for o in outs]

    for name, pieces in (("mod_w", g_mod_w[None]), ("ev_w_in", big_g["ev"][0]), ("ev_w_out", big_g["ev"][1]),
                         ("od_w_in", big_g["od"][0]), ("od_w_out", big_g["od"][1])):
        outs = big(name, pieces, None if name == "mod_w" else (0,))
        for kind, o in zip(("g", "d", "m", "v"), outs):
            res[kind, name] = o if name == "mod_w" else o[None]
    for wi, name in enumerate(("ffn_gate", "ffn_up", "ffn_down")):
        per = {(l, j): big(name, big_g[l, j][wi], (l, j)) for l in range(L) for j in range(2)}
        for ki, kind in enumerate(("g", "d", "m", "v")):
            res[kind, name] = jnp.stack([jnp.stack([per[l, j][ki] for j in range(2)]) for l in range(L)])

    loss = lax.psum(loss_p[0, 0], ("x", "y", "c"))
    grad_x = dx0[:S][None]
    return (loss, grad_x) + tuple(res[kind, n] for kind in ("g", "d", "m", "v") for n in WEIGHTS)


def kernel(
        x, c, ctx, c_ctx, mod_w, mod_b, norm_pre, norm_post, ffn_gate, ffn_up, ffn_down, ev_w_in, ev_w_out, lru_conv_w,
        lru_conv_b, lru_wa, lru_ba, lru_wx, lru_bx, lru_lambda, ret_decay_logit, ret_gn, od_w_in, od_w_out, pool_w,
        pool_scale, q_norm, k_norm, loss_target, m_c_ctx, m_mod_w, m_mod_b, m_norm_pre, m_norm_post, m_ffn_gate, m_ffn_up,
        m_ffn_down, m_ev_w_in, m_ev_w_out, m_lru_conv_w, m_lru_conv_b, m_lru_wa, m_lru_ba, m_lru_wx, m_lru_bx, m_lru_lambda,
        m_ret_decay_logit, m_ret_gn, m_od_w_in, m_od_w_out, m_pool_w, m_pool_scale, m_q_norm, m_k_norm, v_c_ctx, v_mod_w,
        v_mod_b, v_norm_pre, v_norm_post, v_ffn_gate, v_ffn_up, v_ffn_down, v_ev_w_in, v_ev_w_out, v_lru_conv_w,
        v_lru_conv_b, v_lru_wa, v_lru_ba, v_lru_wx, v_lru_bx, v_lru_lambda, v_ret_decay_logit, v_ret_gn, v_od_w_in,
        v_od_w_out, v_pool_w, v_pool_scale, v_q_norm, v_k_norm):
    args = locals()
    return _step({n: args[n] for n in INPUTS})
```

```python
import functools

import jax
import jax.numpy as jnp
from jax import lax
from jax.experimental import pallas as pl
from jax.experimental.pallas import tpu as pltpu

F32 = jnp.float32
BF = jnp.bfloat16
S_ = jax.ShapeDtypeStruct
MESH = pl.DeviceIdType.MESH

NDEV = 8
EPS = 1e-6
FFN_STEP = 0.5
LRU_C = 8.0
RET_CHUNK = 128
RET_THETA = 10000.0
ROPE_THETA = 10000.0
GRID_W = 64
POOL_WINDOWS = (2, 4, 8, 16)
ROW_TILE = 256
HALO = 8
VMEM_LIMIT = 58 * 1024 * 1024

ADAM_LR = 0.001
ADAM_B1 = 0.9
ADAM_B2 = 0.999
ADAM_EPS = 1e-08
ADAM_WD = 0.01
ADAM_STEP = 10

LOG2E = 1.4426950408889634
LN2 = 0.6931471805599453

NN = ((1,), (0,))
NT = ((1,), (1,))
TN = ((0,), (0,))


def _dot(a, b, dn):
    return lax.dot_general(a.astype(BF), b.astype(BF), (dn, ((), ())), preferred_element_type=F32)


@functools.partial(jax.custom_vjp, nondiff_argnums=(2,))
def _bdot(a, b, mode):
    return _dot(a, b, {"nn": NN, "nt": NT, "tn": TN}[mode])


def _bdot_fwd(a, b, mode):
    return _bdot(a, b, mode), (a, b)


def _bdot_bwd(mode, res, g):
    a, b = res
    if mode == "nn":
        return _dot(g, b, NT), _dot(a, g, TN)
    if mode == "nt":
        return _dot(g, b, NN), _dot(g, a, TN)
    return _dot(b, g, NT), _dot(a, g, NN)


_bdot.defvjp(_bdot_fwd, _bdot_bwd)


@functools.partial(jax.custom_vjp, nondiff_argnums=(1, 2))
def _rollv(x, shift, axis):
    return pltpu.roll(x, shift, axis)


def _rollv_fwd(x, shift, axis):
    return pltpu.roll(x, shift, axis), None


def _rollv_bwd(shift, axis, _, g):
    n = g.shape[axis]
    return (pltpu.roll(g, (n - shift) % n, axis),)


_rollv.defvjp(_rollv_fwd, _rollv_bwd)


def _cp(vmem=VMEM_LIMIT):
    return pltpu.CompilerParams(vmem_limit_bytes=vmem)


def _tile(n, pref, mult=8):
    if n <= pref:
        return n
    for t in range(pref, 0, -1):
        if n % t == 0 and t % mult == 0:
            return t
    return n


def _full(shape):
    nd = len(shape)
    return pl.BlockSpec(tuple(shape), lambda *_: (0,) * nd)


def _me():
    return lax.axis_index("x"), lax.axis_index("y"), lax.axis_index("c")


def _all_gather(arrs, name):
    n = len(arrs)

    def body(*refs):
        xs, outs = refs[:n], refs[n:2 * n]
        send_sems, recv_sems, local_sems = refs[2 * n:]
        x, y, c = _me()
        me, sibling = (x, y, c), (x, y, 1 - c)
        chips = [(1 - x, y), (x, 1 - y), (1 - x, 1 - y)]

        def blk(out, p):
            return out.at[4 * p[0] + 2 * p[1] + p[2]]

        def copy(a, k, block, to, src=None):
            return pltpu.make_async_remote_copy(
                src_ref=blk(outs[a], block) if src is None else src, dst_ref=blk(outs[a], block),
                send_sem=send_sems.at[a, k], recv_sem=recv_sems.at[a, k], device_id=to, device_id_type=MESH)

        mine = [pltpu.make_async_copy(xs[a], blk(outs[a], me), local_sems.at[a]) for a in range(n)]
        for cp in mine:
            cp.start()
        first = []
        for a in range(n):
            first.append(copy(a, 0, me, sibling, src=xs[a]))
            first += [copy(a, 1 + j, me, (*chip, c), src=xs[a]) for j, chip in enumerate(chips)]
        for cp in first:
            cp.start()
        passed = []
        for j, chip in enumerate(chips):
            for a in range(n):
                copy(a, 1 + j, (*chip, c), me).wait_recv()
                fw = copy(a, 4 + j, (*chip, c), sibling)
                fw.start()
                passed.append(fw)
        for a in range(n):
            copy(a, 0, sibling, me).wait_recv()
            for j, chip in enumerate(chips):
                copy(a, 4 + j, (*chip, 1 - c), me).wait_recv()
        for cp in first + passed:
            cp.wait_send()
        for cp in mine:
            cp.wait()

    anyspec = pl.BlockSpec(memory_space=pl.ANY)
    return pl.pallas_call(
        body, name=name,
        out_shape=[S_((NDEV,) + a.shape, a.dtype) for a in arrs],
        in_specs=[anyspec] * n, out_specs=[anyspec] * n,
        scratch_shapes=[pltpu.SemaphoreType.DMA((n, 7)), pltpu.SemaphoreType.DMA((n, 7)), pltpu.SemaphoreType.DMA((n,))],
    )(*arrs)


def _all_to_all(arrs, name):
    n = len(arrs)

    def body(*refs):
        xs, outs = refs[:n], refs[n:2 * n]
        send_sems, recv_sems, local_sems = refs[2 * n:]
        x, y, c = _me()
        me_idx = 4 * x + 2 * y + c
        mine = [pltpu.make_async_copy(xs[a].at[me_idx], outs[a].at[me_idx], local_sems.at[a]) for a in range(n)]
        for cp in mine:
            cp.start()
        copies = []
        for k in range(1, NDEV):
            kx, ky, kc = (k >> 2) & 1, (k >> 1) & 1, k & 1
            px = 1 - x if kx else x
            py = 1 - y if ky else y
            pc = 1 - c if kc else c
            p_idx = 4 * px + 2 * py + pc
            for a in range(n):
                copies.append(pltpu.make_async_remote_copy(
                    src_ref=xs[a].at[p_idx], dst_ref=outs[a].at[me_idx],
                    send_sem=send_sems.at[a, k - 1], recv_sem=recv_sems.at[a, k - 1],
                    device_id=(px, py, pc), device_id_type=MESH))
        for cp in copies:
            cp.start()
        for cp in copies:
            cp.wait_recv()
        for cp in copies:
            cp.wait_send()
        for cp in mine:
            cp.wait()

    anyspec = pl.BlockSpec(memory_space=pl.ANY)
    return pl.pallas_call(
        body, name=name,
        out_shape=[S_(a.shape, a.dtype) for a in arrs],
        in_specs=[anyspec] * n, out_specs=[anyspec] * n,
        scratch_shapes=[pltpu.SemaphoreType.DMA((n, 7)), pltpu.SemaphoreType.DMA((n, 7)), pltpu.SemaphoreType.DMA((n,))],
    )(*arrs)


HBM_SPEC = pl.BlockSpec(memory_space=pltpu.HBM)
SEM_SPEC = pl.BlockSpec(memory_space=pltpu.SEMAPHORE)
EFFECT = pltpu.SideEffectType.DATAFLOW_SIDE_EFFECTING


def _peers():
    x, y, c = _me()
    out = []
    for k in range(1, NDEV):
        px = 1 - x if (k >> 2) & 1 else x
        py = 1 - y if (k >> 1) & 1 else y
        pc = 1 - c if k & 1 else c
        out.append(((px, py, pc), 4 * px + 2 * py + pc))
    return out, 4 * x + 2 * y + c


def _xchg_copies(gather, xs, lands, send, recv):
    peers, me_idx = _peers()
    out = []
    for k, (dev, p_idx) in enumerate(peers):
        for a in range(len(xs)):
            out.append(pltpu.make_async_remote_copy(
                src_ref=xs[a] if gather else xs[a].at[p_idx], dst_ref=lands[a].at[me_idx],
                send_sem=send[a].at[k], recv_sem=recv[a].at[k], device_id=dev, device_id_type=MESH))
    return out


def _xchg_start(gather, xs, lands, name):
    n = len(xs)

    def body(*refs):
        xr, lr = refs[:n], refs[n:2 * n]
        send, recv = refs[2 * n:3 * n], refs[3 * n:4 * n]
        token = refs[6 * n]
        for cp in _xchg_copies(gather, xr, lr, send, recv):
            cp.start()
        token[...] = jnp.zeros(token.shape, F32)

    ops = [pltpu.with_memory_space_constraint(a, pltpu.HBM) for a in list(xs) + list(lands)]
    outs = pl.pallas_call(
        body, name=name,
        out_shape=[pltpu.SemaphoreType.DMA((NDEV - 1,))] * (2 * n) + [pltpu.HBM(a.shape, a.dtype) for a in ops]
        + [S_((8, 128), F32)],
        in_specs=[HBM_SPEC] * (2 * n), out_specs=[SEM_SPEC] * (2 * n) + [HBM_SPEC] * (2 * n) + [pl.BlockSpec(memory_space=pltpu.VMEM)],
        input_output_aliases={i: 2 * n + i for i in range(2 * n)},
        compiler_params=pltpu.CompilerParams(has_side_effects=EFFECT),
    )(*ops)
    return (gather, n, outs[:4 * n]), outs[4 * n]


def _xchg_wait(state, after, name):
    gather, n, st = state
    send, recv, xs, lands = st[:n], st[n:2 * n], st[2 * n:3 * n], st[3 * n:4 * n]

    def body(*refs):
        xr, lr = refs[:n], refs[n:2 * n]
        sr, rr = refs[2 * n:3 * n], refs[3 * n:4 * n]
        for cp in _xchg_copies(gather, xr, lr, sr, rr):
            cp.wait_send()
            cp.wait_recv()

    outs = pl.pallas_call(
        body, name=name,
        out_shape=[pltpu.HBM(a.shape, a.dtype) for a in list(xs) + list(lands)],
        in_specs=[HBM_SPEC] * (2 * n) + [SEM_SPEC] * (2 * n) + [pl.BlockSpec(memory_space=pl.ANY)],
        out_specs=[HBM_SPEC] * (2 * n), input_output_aliases={i: i for i in range(2 * n)},
        compiler_params=pltpu.CompilerParams(has_side_effects=EFFECT),
    )(*xs, *lands, *send, *recv, after)
    return outs[n:]


def _tie(a, token):
    return lax.optimization_barrier((a, token))[0]


def _cast_bf16(a, name):
    R, C = a.shape
    tr = _tile(R, 512, 16)

    def body(a_ref, o_ref):
        o_ref[...] = a_ref[...].astype(BF)

    return pl.pallas_call(body, name=name, grid=(R // tr,), in_specs=[pl.BlockSpec((tr, C), lambda i: (i, 0))],
                          out_specs=pl.BlockSpec((tr, C), lambda i: (i, 0)), out_shape=S_((R, C), BF), compiler_params=_cp())(a)


def _sum_n(a, name):
    n, R, C = a.shape
    tr = _tile(R, 256, 8)

    def body(a_ref, o_ref):
        acc = a_ref[0].astype(F32)
        for i in range(1, n):
            acc = acc + a_ref[i].astype(F32)
        o_ref[...] = acc

    return pl.pallas_call(body, name=name, grid=(R // tr,), in_specs=[pl.BlockSpec((n, tr, C), lambda i: (0, i, 0))],
                          out_specs=pl.BlockSpec((tr, C), lambda i: (i, 0)), out_shape=S_((R, C), F32), compiler_params=_cp())(a)


def _adam_math(w, g, m, v):
    m = ADAM_B1 * m + (1.0 - ADAM_B1) * g
    v = ADAM_B2 * v + (1.0 - ADAM_B2) * jnp.square(g)
    m_hat = m / (1.0 - ADAM_B1 ** ADAM_STEP)
    v_hat = v / (1.0 - ADAM_B2 ** ADAM_STEP)
    delta = -ADAM_LR * (m_hat / (jnp.sqrt(v_hat) + ADAM_EPS) + ADAM_WD * w)
    return delta, m, v


def _reduce_adam(pieces, w, m, v, name, own=None):
    n, R, C = pieces.shape
    tr = _tile(R, 256, 8)

    def body(*refs):
        p_ref, w_ref, m_ref, v_ref = refs[:4]
        g_ref, d_ref, mo_ref, vo_ref = refs[-4:]
        g = p_ref[0].astype(F32)
        for i in range(1, n):
            g = g + p_ref[i].astype(F32)
        if own is not None:
            g = g + refs[4][...].astype(F32)
        d, mn, vn = _adam_math(w_ref[...], g, m_ref[...], v_ref[...])
        g_ref[...] = g
        d_ref[...] = d
        mo_ref[...] = mn
        vo_ref[...] = vn

    row = pl.BlockSpec((tr, C), lambda i: (i, 0))
    ins = [pieces, w, m, v] + ([] if own is None else [own])
    return pl.pallas_call(body, name=name, grid=(R // tr,),
                          in_specs=[pl.BlockSpec((n, tr, C), lambda i: (0, i, 0))] + [row] * (len(ins) - 1),
                          out_specs=[row] * 4, out_shape=[S_((R, C), F32)] * 4, compiler_params=_cp())(*ins)


def _mm_cols(a, wb, name, out_dtype=F32):
    M, K = a.shape
    NB, _, nb = wb.shape
    tm = _tile(M, 768, 16)

    def body(a_ref, w_ref, o_ref):
        o_ref[...] = _dot(a_ref[...], w_ref[...], NN).astype(out_dtype)

    return pl.pallas_call(body, name=name, grid=(M // tm, NB),
                          in_specs=[pl.BlockSpec((tm, K), lambda i, j: (i, 0)), pl.BlockSpec((None, K, nb), lambda i, j: (j, 0, 0))],
                          out_specs=pl.BlockSpec((tm, nb), lambda i, j: (i, j)), out_shape=S_((M, NB * nb), out_dtype),
                          compiler_params=_cp())(a, wb)


def _mm_nt_cols(g, wb, name):
    M = g.shape[0]
    NB, K, nb = wb.shape
    tm = _tile(M, 768, 16)

    def body(g_ref, w_ref, o_ref):
        j = pl.program_id(1)
        part = _dot(g_ref[...], w_ref[...], NT)

        @pl.when(j == 0)
        def _():
            o_ref[...] = part

        @pl.when(j > 0)
        def _():
            o_ref[...] += part

    return pl.pallas_call(body, name=name, grid=(M // tm, NB),
                          in_specs=[pl.BlockSpec((tm, nb), lambda i, j: (i, j)), pl.BlockSpec((None, K, nb), lambda i, j: (j, 0, 0))],
                          out_specs=pl.BlockSpec((tm, K), lambda i, j: (i, 0)), out_shape=S_((M, K), F32),
                          compiler_params=_cp())(g, wb)


def _mm_full(a, w, dn, name, out_dtype=F32):
    M, K = a.shape
    N = w.shape[1] if dn == NN else w.shape[0]
    tm = _tile(M, 768, 16)

    def body(a_ref, w_ref, o_ref):
        o_ref[...] = _dot(a_ref[...], w_ref[...], dn).astype(out_dtype)

    return pl.pallas_call(body, name=name, grid=(M // tm,),
                          in_specs=[pl.BlockSpec((tm, K), lambda i: (i, 0)), _full(w.shape)],
                          out_specs=pl.BlockSpec((tm, N), lambda i: (i, 0)), out_shape=S_((M, N), out_dtype),
                          compiler_params=_cp())(a, w)


def _mm_tn_cols(a, g, NB, name):
    M, K = a.shape
    nb = g.shape[1] // NB
    tk = _tile(M, 768, 16)
    nk = M // tk

    def body(a_ref, g_ref, o_ref, acc):
        k = pl.program_id(1)
        part = _dot(a_ref[...], g_ref[...], TN)

        @pl.when(k == 0)
        def _():
            acc[...] = part

        @pl.when(k > 0)
        def _():
            acc[...] += part

        @pl.when(k == nk - 1)
        def _():
            o_ref[...] = acc[...].astype(BF)

    return pl.pallas_call(body, name=name, grid=(NB, nk),
                          in_specs=[pl.BlockSpec((tk, K), lambda b, k: (k, 0)), pl.BlockSpec((tk, nb), lambda b, k: (k, b))],
                          out_specs=pl.BlockSpec((None, K, nb), lambda b, k: (b, 0, 0)), out_shape=S_((NB, K, nb), BF),
                          scratch_shapes=[pltpu.VMEM((K, nb), F32)], compiler_params=_cp())(a, g)


def _mm_tn_rows(a, g, NB, name):
    M = a.shape[0]
    kb = a.shape[1] // NB
    N = g.shape[1]
    tk = _tile(M, 768, 16)
    nk = M // tk

    def body(a_ref, g_ref, o_ref, acc):
        k = pl.program_id(1)
        part = _dot(a_ref[...], g_ref[...], TN)

        @pl.when(k == 0)
        def _():
            acc[...] = part

        @pl.when(k > 0)
        def _():
            acc[...] += part

        @pl.when(k == nk - 1)
        def _():
            o_ref[...] = acc[...].astype(BF)

    return pl.pallas_call(body, name=name, grid=(NB, nk),
                          in_specs=[pl.BlockSpec((tk, kb), lambda b, k: (k, b)), pl.BlockSpec((tk, N), lambda b, k: (k, 0))],
                          out_specs=pl.BlockSpec((None, kb, N), lambda b, k: (b, 0, 0)), out_shape=S_((NB, kb, N), BF),
                          scratch_shapes=[pltpu.VMEM((kb, N), F32)], compiler_params=_cp())(a, g)


def _ffn_fwd(h, wg, wu, wd, name):
    T, D = h.shape
    NB, _, nb = wg.shape
    tm = _tile(T, 384, 16)

    def body(h_ref, wg_ref, wu_ref, wd_ref, y_ref, g_ref, u_ref):
        b = pl.program_id(1)
        hh = h_ref[...]
        g = _dot(hh, wg_ref[...], NN)
        u = _dot(hh, wu_ref[...], NN)
        g_ref[...] = g
        u_ref[...] = u
        part = _dot(g * jax.nn.sigmoid(g) * u, wd_ref[...], NN)

        @pl.when(b == 0)
        def _():
            y_ref[...] = part

        @pl.when(b > 0)
        def _():
            y_ref[...] += part

    wcol = pl.BlockSpec((None, D, nb), lambda i, b: (b, 0, 0))
    act = pl.BlockSpec((None, tm, nb), lambda i, b: (b, i, 0))
    return pl.pallas_call(
        body, name=name, grid=(T // tm, NB),
        in_specs=[pl.BlockSpec((tm, D), lambda i, b: (i, 0)), wcol, wcol, pl.BlockSpec((None, nb, D), lambda i, b: (b, 0, 0))],
        out_specs=[pl.BlockSpec((tm, D), lambda i, b: (i, 0)), act, act],
        out_shape=[S_((T, D), F32), S_((NB, T, nb), F32), S_((NB, T, nb), F32)], compiler_params=_cp())(h, wg, wu, wd)


def _ffn_bwd_act(dy, G, U, wg, wu, wd, name):
    T, D = dy.shape
    NB, _, nb = wg.shape
    tm = _tile(T, 384, 16)

    def body(dy_ref, g_ref, u_ref, wg_ref, wu_ref, wd_ref, dh_ref, dg_ref, du_ref, a_ref):
        b = pl.program_id(1)
        g, u = g_ref[...], u_ref[...]
        da = _dot(dy_ref[...], wd_ref[...], NT)
        s = jax.nn.sigmoid(g)
        silu = g * s
        du = da * silu
        dg = da * u * (s * (1.0 + g * (1.0 - s)))
        dg_ref[...] = dg.astype(BF)
        du_ref[...] = du.astype(BF)
        a_ref[...] = (silu * u).astype(BF)
        part = _dot(dg, wg_ref[...], NT) + _dot(du, wu_ref[...], NT)

        @pl.when(b == 0)
        def _():
            dh_ref[...] = part

        @pl.when(b > 0)
        def _():
            dh_ref[...] += part

    wcol = pl.BlockSpec((None, D, nb), lambda i, b: (b, 0, 0))
    act = pl.BlockSpec((None, tm, nb), lambda i, b: (b, i, 0))
    row = pl.BlockSpec((tm, D), lambda i, b: (i, 0))
    return pl.pallas_call(
        body, name=name, grid=(T // tm, NB),
        in_specs=[row, act, act, wcol, wcol, pl.BlockSpec((None, nb, D), lambda i, b: (b, 0, 0))],
        out_specs=[row, act, act, act],
        out_shape=[S_((T, D), F32)] + [S_((NB, T, nb), BF)] * 3, compiler_params=_cp())(dy, G, U, wg, wu, wd)


def _ffn_wgrad_in(h, dact, name):
    T, D = h.shape
    NB, _, nb = dact.shape
    tk = _tile(T, 768, 16)
    nk = T // tk

    def body(h_ref, d_ref, o_ref, acc):
        k = pl.program_id(1)
        part = _dot(h_ref[...], d_ref[...], TN)

        @pl.when(k == 0)
        def _():
            acc[...] = part

        @pl.when(k > 0)
        def _():
            acc[...] += part

        @pl.when(k == nk - 1)
        def _():
            o_ref[...] = acc[...].astype(BF)

    return pl.pallas_call(body, name=name, grid=(NB, nk),
                          in_specs=[pl.BlockSpec((tk, D), lambda b, k: (k, 0)), pl.BlockSpec((None, tk, nb), lambda b, k: (b, k, 0))],
                          out_specs=pl.BlockSpec((None, D, nb), lambda b, k: (b, 0, 0)), out_shape=S_((NB, D, nb), BF),
                          scratch_shapes=[pltpu.VMEM((D, nb), F32)], compiler_params=_cp())(h, dact)


def _ffn_wgrad_out(act, dy, name):
    NB, T, nb = act.shape
    D = dy.shape[1]
    tk = _tile(T, 768, 16)
    nk = T // tk

    def body(a_ref, d_ref, o_ref, acc):
        k = pl.program_id(1)
        part = _dot(a_ref[...], d_ref[...], TN)

        @pl.when(k == 0)
        def _():
            acc[...] = part

        @pl.when(k > 0)
        def _():
            acc[...] += part

        @pl.when(k == nk - 1)
        def _():
            o_ref[...] = acc[...].astype(BF)

    return pl.pallas_call(body, name=name, grid=(NB, nk),
                          in_specs=[pl.BlockSpec((None, tk, nb), lambda b, k: (b, k, 0)), pl.BlockSpec((tk, D), lambda b, k: (k, 0))],
                          out_specs=pl.BlockSpec((None, nb, D), lambda b, k: (b, 0, 0)), out_shape=S_((NB, nb, D), BF),
                          scratch_shapes=[pltpu.VMEM((nb, D), F32)], compiler_params=_cp())(act, dy)


class _Geo:
    def __init__(self, S, LC, D):
        self.S, self.LC, self.D, self.T = S, LC, D, S + LC
        self.RT = _tile(LC, ROW_TILE, 8)
        assert S % self.RT == 0 and self.RT >= 2 * HALO
        self.nlat, self.nctx = S // self.RT, LC // self.RT
        self.nt = self.nlat + self.nctx

    def row(self, C, cb=0):
        return pl.BlockSpec((self.RT, C), lambda i: (i, cb))

    def prev(self, C, cb=0):
        return pl.BlockSpec((self.RT, C), lambda i: (jnp.maximum(i - 1, 0), cb))

    def next(self, C, cb=0):
        nt = self.nt
        return pl.BlockSpec((self.RT, C), lambda i: (jnp.minimum(i + 1, nt - 1), cb))

    def seg(self, r, C):
        nlat = self.nlat
        return pl.BlockSpec((None, r, C), lambda i: (jnp.minimum(i // nlat, 1), 0, 0))

    def first_of_seg(self, i):
        return jnp.logical_or(i == 0, i == self.nlat)

    def prev_ok(self, i):
        return jnp.logical_and(i != 0, i != self.nlat)

    def next_ok(self, i):
        return jnp.logical_and(i != self.nlat - 1, i != self.nt - 1)

    def pos(self, i):
        r = lax.broadcasted_iota(jnp.int32, (self.RT, 1), 0)
        is_ctx = i >= self.nlat
        base = jnp.where(is_ctx, (i - self.nlat) * self.RT, i * self.RT)
        return base + r, jnp.where(is_ctx, self.LC, self.S)


def _rms(x):
    return x * lax.rsqrt(jnp.mean(x * x, axis=-1, keepdims=True) + EPS)


def _modulate(x, g, shift, scale):
    return (_rms(x) * g) * (1 + scale) + shift


def _post(x, y, g, gate, w):
    return x + w * gate * (_rms(y) * g)


def _norm_fwd(geo, x, post, pre, name):
    D = geo.D
    ins, specs = [x], [geo.row(D)]
    if post is not None:
        ins += [post[0], post[1], post[2]]
        specs += [geo.row(D), _full((1, D)), geo.seg(9, D)]
    if pre is not None:
        ins += [pre[0], pre[1]]
        specs += [_full((1, D)), geo.seg(9, D)]

    def body(*refs):
        it = iter(refs)
        xv = next(it)[...]
        if post is not None:
            y_ref, gp_ref, tab_ref = next(it), next(it), next(it)
        if pre is not None:
            gq_ref, tabn_ref = next(it), next(it)
        if post is not None:
            r = 3 * post[3] + 2
            xv = _post(xv, y_ref[...], gp_ref[...], tab_ref[r:r + 1, :], post[4])
            next(it)[...] = xv
        if pre is not None:
            r = 3 * pre[2]
            next(it)[...] = _modulate(xv, gq_ref[...], tabn_ref[r:r + 1, :], tabn_ref[r + 1:r + 2, :]).astype(BF)

    outs, ospecs = [], []
    if post is not None:
        outs.append(S_((geo.T, D), F32))
        ospecs.append(geo.row(D))
    if pre is not None:
        outs.append(S_((geo.T, D), BF))
        ospecs.append(geo.row(D))
    return pl.pallas_call(body, name=name, grid=(geo.nt,), in_specs=specs, out_specs=ospecs, out_shape=outs,
                          compiler_params=_cp())(*ins)


def _acc_rows(ref, first, rows):
    for k, v in enumerate(rows):
        @pl.when(first)
        def _(k=k, v=v):
            ref[k:k + 1, :] = v

        @pl.when(jnp.logical_not(first))
        def _(k=k, v=v):
            ref[k:k + 1, :] += v


def _norm_bwd(geo, dxo, dh, x, pre, post, name):
    D = geo.D
    ins = [dxo, dh, x, pre[0], pre[1]]
    specs = [geo.row(D)] * 3 + [_full((1, D)), geo.seg(9, D)]
    if post is not None:
        ins += [post[0], post[1], post[2]]
        specs += [geo.row(D), _full((1, D)), geo.seg(9, D)]

    def body(*refs):
        i = pl.program_id(0)
        first = geo.first_of_seg(i)
        dxo_ref, dh_ref, x_ref, gq_ref, tab_ref = refs[:5]
        k = 5
        if post is not None:
            y_ref, gp_ref, tabp_ref = refs[5:8]
            k = 8
        outs = refs[k:]
        r = 3 * pre[2]
        _, vjp = jax.vjp(_modulate, x_ref[...], gq_ref[...], tab_ref[r:r + 1, :], tab_ref[r + 1:r + 2, :])
        dx, dg, dsh, dsc = vjp(dh_ref[...].astype(F32))
        dx = dx + dxo_ref[...]
        outs[0][...] = dx
        if post is None:
            _acc_rows(outs[1], first, [dg, dsh, dsc])
            return
        _acc_rows(outs[2], first, [dg, dsh, dsc])
        rp = 3 * post[3] + 2
        w = post[4]
        _, vjp2 = jax.vjp(lambda yy, gg, ga: w * ga * (_rms(yy) * gg), y_ref[...], gp_ref[...], tabp_ref[rp:rp + 1, :])
        dy, dgp, dga = vjp2(dx)
        outs[1][...] = dy
        _acc_rows(outs[3], first, [dgp, dga])

    if post is None:
        outs, ospecs = [S_((geo.T, D), F32), S_((2, 3, D), F32)], [geo.row(D), geo.seg(3, D)]
    else:
        outs = [S_((geo.T, D), F32), S_((geo.T, D), F32), S_((2, 3, D), F32), S_((2, 2, D), F32)]
        ospecs = [geo.row(D), geo.row(D), geo.seg(3, D), geo.seg(2, D)]
    return pl.pallas_call(body, name=name, grid=(geo.nt,), in_specs=specs, out_specs=ospecs, out_shape=outs,
                          compiler_params=_cp())(*ins)


def _loss_bwd(geo, xf, tgt, post, name):
    D = geo.D
    nlat = geo.nlat

    def body(x_ref, t_ref, y_ref, gp_ref, tabp_ref, loss_ref, dx_ref, dy_ref, dpost_ref):
        i = pl.program_id(0)
        first = geo.first_of_seg(i)
        lat = i < nlat
        diff = x_ref[...] - t_ref[...]
        part = jnp.where(lat, 0.5 * jnp.sum(jnp.mean(diff * diff, axis=-1, keepdims=True), axis=0, keepdims=True), 0.0)

        @pl.when(i == 0)
        def _():
            loss_ref[...] = part

        @pl.when(i > 0)
        def _():
            loss_ref[...] += part

        dx = jnp.where(lat, diff * (1.0 / D), 0.0)
        dx_ref[...] = dx
        rp = 3 * post[3] + 2
        w = post[4]
        _, vjp2 = jax.vjp(lambda yy, gg, ga: w * ga * (_rms(yy) * gg), y_ref[...], gp_ref[...], tabp_ref[rp:rp + 1, :])
        dy, dgp, dga = vjp2(dx)
        dy_ref[...] = dy
        _acc_rows(dpost_ref, first, [dgp, dga])

    tspec = pl.BlockSpec((geo.RT, D), lambda i: (jnp.minimum(i, nlat - 1), 0))
    return pl.pallas_call(
        body, name=name, grid=(geo.nt,),
        in_specs=[geo.row(D), tspec, geo.row(D), _full((1, D)), geo.seg(9, D)],
        out_specs=[_full((1, 1)), geo.row(D), geo.row(D), geo.seg(2, D)],
        out_shape=[S_((1, 1), F32), S_((geo.T, D), F32), S_((geo.T, D), F32), S_((2, 2, D), F32)],
        compiler_params=_cp())(xf, tgt, post[0], post[1], post[2])


def _fill_ext(geo, ext, prev_ref, cur_ref, next_ref, i):
    RT = geo.RT
    ext[0:HALO, :] = jnp.where(geo.prev_ok(i), prev_ref[RT - HALO:RT, :], 0.0).astype(ext.dtype)
    ext[HALO:HALO + RT, :] = cur_ref[...].astype(ext.dtype)
    ext[HALO + RT:2 * HALO + RT, :] = jnp.where(geo.next_ok(i), next_ref[0:HALO, :], 0.0).astype(ext.dtype)


CONV_W = 4
CONV_LEFT = 2


def _lru_gates(u, za, zx, lam):
    r = jax.nn.sigmoid(za)
    i = jax.nn.sigmoid(zx)
    log_a = -LRU_C * r * jax.nn.softplus(-lam)
    a = jnp.exp(log_a)
    return a, jnp.sqrt(1.0 - jnp.exp(2.0 * log_a)) * (i * u)


def _lru_coef_fwd(geo, P, W, cw, cb, wa, ba, wx, bx, lam, name):
    RT = geo.RT
    nblk, LB = wa.shape[1], wa.shape[2]

    def body(pp, pc, pn, cw_ref, cb_ref, wa_ref, ba_ref, wx_ref, bx_ref, lam_ref, u_ref, a0, b0, a1, b1, ext):
        i = pl.program_id(0)
        _fill_ext(geo, ext, pp, pc, pn, i)
        u = cb_ref[...] + ext[pl.ds(HALO - CONV_LEFT, RT), :] * cw_ref[0:1, :]
        for k in range(1, CONV_W):
            u = u + ext[pl.ds(HALO - CONV_LEFT + k, RT), :] * cw_ref[k:k + 1, :]
        u_ref[...] = u
        for d, (a_ref, b_ref) in enumerate(((a0, b0), (a1, b1))):
            for n in range(nblk):
                sl = slice(n * LB, (n + 1) * LB)
                un = u[:, sl]
                za = _dot(un, wa_ref[d, n], NN) + ba_ref[d:d + 1, sl]
                zx = _dot(un, wx_ref[d, n], NN) + bx_ref[d:d + 1, sl]
                a, b = _lru_gates(un, za, zx, lam_ref[d:d + 1, sl])
                a_ref[:, sl] = a
                b_ref[:, sl] = b

    return pl.pallas_call(
        body, name=name, grid=(geo.nt,),
        in_specs=[geo.prev(W, 1), geo.row(W, 1), geo.next(W, 1), _full(cw.shape), _full(cb.shape), _full(wa.shape),
                  _full(ba.shape), _full(wx.shape), _full(bx.shape), _full(lam.shape)],
        out_specs=[geo.row(W)] * 5, out_shape=[S_((geo.T, W), F32)] * 5,
        scratch_shapes=[pltpu.VMEM((RT + 2 * HALO, W), F32)], compiler_params=_cp())(P, P, P, cw, cb, wa, ba, wx, bx, lam)


def _lru_coef_bwd(geo, u, da, db, W, wa, ba, wx, bx, lam, name):
    nblk, LB = wa.shape[1], wa.shape[2]

    def body(u_ref, da0, db0, da1, db1, wa_ref, ba_ref, wx_ref, bx_ref, lam_ref, du_ref, dwa, dba, dwx, dbx, dlam):
        i = pl.program_id(0)

        @pl.when(i == 0)
        def _():
            for r in (dwa, dba, dwx, dbx, dlam):
                r[...] = jnp.zeros(r.shape, F32)

        u = u_ref[...]
        for n in range(nblk):
            sl = slice(n * LB, (n + 1) * LB)
            un = u[:, sl]
            dun = jnp.zeros_like(un)
            for d, (da_ref, db_ref) in enumerate(((da0, db0), (da1, db1))):
                za = _dot(un, wa_ref[d, n], NN) + ba_ref[d:d + 1, sl]
                zx = _dot(un, wx_ref[d, n], NN) + bx_ref[d:d + 1, sl]
                _, vjp = jax.vjp(_lru_gates, un, za, zx, lam_ref[d:d + 1, sl])
                du_e, dza, dzx, dl = vjp((da_ref[:, sl], db_ref[:, sl]))
                dun = dun + du_e + _dot(dza, wa_ref[d, n], NT) + _dot(dzx, wx_ref[d, n], NT)
                dwa[d, n] += _dot(un, dza, TN)
                dwx[d, n] += _dot(un, dzx, TN)
                dba[d:d + 1, sl] += jnp.sum(dza, axis=0, keepdims=True)
                dbx[d:d + 1, sl] += jnp.sum(dzx, axis=0, keepdims=True)
                dlam[d:d + 1, sl] += dl
            du_ref[:, sl] = dun

    row = geo.row(W)
    return pl.pallas_call(
        body, name=name, grid=(geo.nt,),
        in_specs=[row] * 5 + [_full(wa.shape), _full(ba.shape), _full(wx.shape), _full(bx.shape), _full(lam.shape)],
        out_specs=[row, _full(wa.shape), _full(ba.shape), _full(wx.shape), _full(bx.shape), _full(lam.shape)],
        out_shape=[S_((geo.T, W), F32), S_(wa.shape, F32), S_(ba.shape, F32), S_(wx.shape, F32), S_(bx.shape, F32), S_(lam.shape, F32)],
        compiler_params=_cp())(u, da[0], db[0], da[1], db[1], wa, ba, wx, bx, lam)


def _scan_order(d, k, nlat, nctx):
    if d == 0:
        return jnp.where(k < nctx, nlat + k, k - nctx)
    return jnp.where(k < nctx, nlat + nctx - 1 - k, nlat - 1 - (k - nctx))


def _chunk_scan(a, b, reverse):
    n = a.shape[0]
    row = lax.broadcasted_iota(jnp.int32, a.shape, 0)
    s = 1
    while s < n:
        if reverse:
            ok = row < n - s
            a_s, b_s = pltpu.roll(a, n - s, 0), pltpu.roll(b, n - s, 0)
        else:
            ok = row >= s
            a_s, b_s = pltpu.roll(a, s, 0), pltpu.roll(b, s, 0)
        b = a * jnp.where(ok, b_s, 0.0) + b
        a = a * jnp.where(ok, a_s, 1.0)
        s *= 2
    return a, b


def _shift1(x, reverse, fill):
    n = x.shape[0]
    row = lax.broadcasted_iota(jnp.int32, x.shape, 0)
    if reverse:
        return jnp.where(row == n - 1, fill, pltpu.roll(x, n - 1, 0))
    return jnp.where(row == 0, fill, pltpu.roll(x, 1, 0))


def _lru_scan_fwd(geo, a, b, d, name):
    W = a.shape[1]
    tw = _tile(W, 256, 128)
    RT, nlat, nctx = geo.RT, geo.nlat, geo.nctx
    rev = d == 1

    def body(a_ref, b_ref, h_ref, hp_ref, carry):
        k = pl.program_id(1)

        @pl.when(k == 0)
        def _():
            carry[...] = jnp.zeros(carry.shape, F32)

        ac, bc = _chunk_scan(a_ref[...], b_ref[...], rev)
        h = bc + ac * carry[...]
        h_ref[...] = h
        hp_ref[...] = _shift1(h, rev, carry[...])
        carry[...] = h[0:1, :] if rev else h[RT - 1:RT, :]

    spec = pl.BlockSpec((RT, tw), lambda j, k: (_scan_order(d, k, nlat, nctx), j))
    return pl.pallas_call(body, name=name, grid=(W // tw, geo.nt), in_specs=[spec, spec], out_specs=[spec, spec],
                          out_shape=[S_((geo.T, W), F32)] * 2, scratch_shapes=[pltpu.VMEM((1, tw), F32)],
                          compiler_params=_cp())(a, b)


def _lru_scan_bwd(geo, dh, a, hprev, d, name):
    W = a.shape[1]
    tw = _tile(W, 256, 128)
    RT, nlat, nctx, nt = geo.RT, geo.nlat, geo.nctx, geo.nt
    rev = d == 1

    def body(dh_ref, a_ref, hp_ref, da_ref, db_ref, carry):
        k = pl.program_id(1)

        @pl.when(k == 0)
        def _():
            carry[...] = jnp.zeros(carry.shape, F32)

        av = a_ref[...]
        a_next = _shift1(av, not rev, jnp.ones((1, tw), F32))
        ac, bc = _chunk_scan(a_next, dh_ref[...], not rev)
        lam = bc + ac * carry[...]
        db_ref[...] = lam
        da_ref[...] = lam * hp_ref[...]
        first = (av * lam)[RT - 1:RT, :] if rev else (av * lam)[0:1, :]
        carry[...] = first

    spec = pl.BlockSpec((RT, tw), lambda j, k: (_scan_order(d, nt - 1 - k, nlat, nctx), j))
    return pl.pallas_call(body, name=name, grid=(W // tw, nt), in_specs=[spec] * 3, out_specs=[spec] * 2,
                          out_shape=[S_((geo.T, W), F32)] * 2, scratch_shapes=[pltpu.VMEM((1, tw), F32)],
                          compiler_params=_cp())(dh, a, hprev)


def _conv_bwd_assemble(geo, P, du, pieces, W, cw, name):
    RT = geo.RT

    def body(pp, pc, pn, dup, duc, dun, dg, dq, dk, dv, dog, cw_ref, dP_ref, dcw, dcb, ext_r, ext_d):
        i = pl.program_id(0)
        _fill_ext(geo, ext_r, pp, pc, pn, i)
        _fill_ext(geo, ext_d, dup, duc, dun, i)
        du_c = duc[...]
        rows = []
        dr = None
        for k in range(CONV_W):
            rows.append(jnp.sum(du_c * ext_r[pl.ds(HALO - CONV_LEFT + k, RT), :], axis=0, keepdims=True))
            t = ext_d[pl.ds(HALO + CONV_LEFT - k, RT), :] * cw_ref[k:k + 1, :]
            dr = t if dr is None else dr + t
        first = i == 0
        _acc_rows(dcw, first, rows)
        _acc_rows(dcb, first, [jnp.sum(du_c, axis=0, keepdims=True)])
        for j, v in enumerate((dg[...], dr, dq[...], dk[...], dv[...], dog[...])):
            dP_ref[:, j * W:(j + 1) * W] = v

    row = geo.row(W)
    return pl.pallas_call(
        body, name=name, grid=(geo.nt,),
        in_specs=[geo.prev(W, 1), geo.row(W, 1), geo.next(W, 1), geo.prev(W), row, geo.next(W)] + [row] * 5 + [_full(cw.shape)],
        out_specs=[geo.row(6 * W), _full((CONV_W, W)), _full((1, W))],
        out_shape=[S_((geo.T, 6 * W), F32), S_((CONV_W, W), F32), S_((1, W), F32)],
        scratch_shapes=[pltpu.VMEM((RT + 2 * HALO, W), F32)] * 2, compiler_params=_cp())(P, P, P, du, du, du, *pieces, cw)


def _rot_half(x, cos, sin):
    hw = x.shape[1] // 2
    x1, x2 = x[:, :hw], x[:, hw:]
    return jnp.concatenate([x1 * cos - x2 * sin, x1 * sin + x2 * cos], axis=-1)


def _ret_chunk(q, k, v, s, logit, cos, sin, rev):
    C, dk = q.shape
    lg = -jax.nn.softplus(-logit)
    q = _rot_half(q, cos, sin)
    k = _rot_half(k, cos, sin) * (dk ** -0.5)
    i = lax.broadcasted_iota(jnp.int32, (C, 1), 0).astype(F32)
    j = lax.broadcasted_iota(jnp.int32, (1, C), 1).astype(F32)
    if rev:
        diff, qe, ke = j - i, C - i, i
    else:
        diff, qe, ke = i - j, i + 1.0, C - 1.0 - i
    intra = jnp.where(diff >= 0, jnp.exp(lg * jnp.maximum(diff, 0.0)), 0.0)
    scores = _bdot(q, k, "nt") * intra
    o = _bdot(scores, v, "nn") + _bdot(q * jnp.exp(lg * qe), s, "nn")
    s_new = s * jnp.exp(lg * C) + _bdot(k * jnp.exp(lg * ke), v, "tn")
    return o, s_new


def _ret_specs(P, W, H, d, nlc, ncc, order_of_step):
    C = RET_CHUNK
    dk = W // H

    def cidx(k):
        return _scan_order(d, order_of_step(k), nlc, ncc)

    per_w = W // dk
    q = pl.BlockSpec((C, dk), lambda h, k: (cidx(k), 2 * per_w + h))
    kk = pl.BlockSpec((C, dk), lambda h, k: (cidx(k), 3 * per_w + h))
    v = pl.BlockSpec((C, dk), lambda h, k: (cidx(k), 4 * per_w + h))
    tab = pl.BlockSpec((C, dk // 2), lambda h, k: (cidx(k), 0))
    logit = pl.BlockSpec((None, 1, 1), lambda h, k: (h, 0, 0))
    o = pl.BlockSpec((C, dk), lambda h, k: (cidx(k), h))
    return q, kk, v, tab, logit, o


def _ret_fwd(geo, P, W, H, logit_d, cos, sin, d, name):
    C = RET_CHUNK
    dk = W // H
    nlc, ncc = geo.S // C, geo.LC // C
    nch = nlc + ncc
    rev = d == 1

    def body(q_ref, k_ref, v_ref, cos_ref, sin_ref, lg_ref, o_ref, st_ref, s_scr):
        k = pl.program_id(1)

        @pl.when(k == 0)
        def _():
            s_scr[...] = jnp.zeros(s_scr.shape, F32)

        st_ref[...] = s_scr[...]
        o, s_new = _ret_chunk(q_ref[...], k_ref[...], v_ref[...], s_scr[...], lg_ref[...], cos_ref[...], sin_ref[...], rev)
        o_ref[...] = o
        s_scr[...] = s_new

    q, kk, v, tab, logit, o = _ret_specs(P, W, H, d, nlc, ncc, lambda k: k)
    st = pl.BlockSpec((None, None, dk, dk), lambda h, k: (h, k, 0, 0))
    return pl.pallas_call(body, name=name, grid=(H, nch), in_specs=[q, kk, v, tab, tab, logit], out_specs=[o, st],
                          out_shape=[S_((geo.T, W), F32), S_((H, nch, dk, dk), F32)],
                          scratch_shapes=[pltpu.VMEM((dk, dk), F32)], compiler_params=_cp())(P, P, P, cos, sin, logit_d)


def _ret_bwd(geo, P, W, H, logit_d, cos, sin, states, do, prev, d, name):
    C = RET_CHUNK
    dk = W // H
    nlc, ncc = geo.S // C, geo.LC // C
    nch = nlc + ncc
    rev = d == 1
    np_ = 0 if prev is None else 3

    def body(*refs):
        q_ref, k_ref, v_ref, cos_ref, sin_ref, lg_ref, st_ref, do_ref = refs[:8]
        prevs = refs[8:8 + np_]
        dq_ref, dk_ref, dv_ref, dlg_ref, ds_scr = refs[8 + np_:]
        k = pl.program_id(1)

        @pl.when(k == 0)
        def _():
            ds_scr[...] = jnp.zeros(ds_scr.shape, F32)
            dlg_ref[...] = jnp.zeros(dlg_ref.shape, F32)

        cos, sin = cos_ref[...], sin_ref[...]
        _, vjp = jax.vjp(lambda a, b, c, s, lg: _ret_chunk(a, b, c, s, lg, cos, sin, rev),
                         q_ref[...], k_ref[...], v_ref[...], st_ref[...], lg_ref[...])
        dq, dkk, dv, ds, dlg = vjp((do_ref[...], ds_scr[...]))
        if prev is not None:
            dq, dkk, dv = dq + prevs[0][...], dkk + prevs[1][...], dv + prevs[2][...]
        dq_ref[...] = dq
        dk_ref[...] = dkk
        dv_ref[...] = dv
        ds_scr[...] = ds
        dlg_ref[...] += dlg

    q, kk, v, tab, logit, o = _ret_specs(P, W, H, d, nlc, ncc, lambda k: nch - 1 - k)
    st = pl.BlockSpec((None, None, dk, dk), lambda h, k: (h, nch - 1 - k, 0, 0))
    ins = [P, P, P, cos, sin, logit_d, states, do] + ([] if prev is None else list(prev))
    return pl.pallas_call(body, name=name, grid=(H, nch), in_specs=[q, kk, v, tab, tab, logit, st, o] + [o] * np_,
                          out_specs=[o, o, o, logit], out_shape=[S_((geo.T, W), F32)] * 3 + [S_((H, 1, 1), F32)],
                          scratch_shapes=[pltpu.VMEM((dk, dk), F32)], compiler_params=_cp())(*ins)


def _mix_even_math(g, h0, h1, o0, o1, og, gn, H):
    lru = jax.nn.gelu(g) * (h0 + h1)
    o = o0 + o1
    dv = o.shape[1] // H
    parts = []
    for h in range(H):
        oh = o[:, h * dv:(h + 1) * dv]
        mu = jnp.mean(oh, axis=-1, keepdims=True)
        var = jnp.mean(jnp.square(oh - mu), axis=-1, keepdims=True)
        parts.append((oh - mu) * lax.rsqrt(var + EPS))
    y = jnp.concatenate(parts, axis=-1) * gn
    return lru, y * jax.nn.silu(og)


def _mix_even_fwd(geo, P, W, H, h0, h1, o0, o1, gn, name):
    def body(g_ref, h0r, h1r, o0r, o1r, og_ref, gn_ref, m_ref):
        lru, ret = _mix_even_math(g_ref[...], h0r[...], h1r[...], o0r[...], o1r[...], og_ref[...], gn_ref[...], H)
        m_ref[:, :W] = lru.astype(BF)
        m_ref[:, W:] = ret.astype(BF)

    row = geo.row(W)
    return pl.pallas_call(body, name=name, grid=(geo.nt,), in_specs=[geo.row(W, 0), row, row, row, row, geo.row(W, 5), _full((1, W))],
                          out_specs=geo.row(2 * W), out_shape=S_((geo.T, 2 * W), BF), compiler_params=_cp())(P, h0, h1, o0, o1, P, gn)


def _mix_even_bwd(geo, P, W, H, h0, h1, o0, o1, gn, dmix, name):
    def body(g_ref, h0r, h1r, o0r, o1r, og_ref, gn_ref, dl_ref, dr_ref, dg_ref, dh_ref, do_ref, dog_ref, dgn_ref):
        i = pl.program_id(0)
        _, vjp = jax.vjp(lambda g, hs, os_, og, gn_: _mix_even_math(g, hs, 0.0, os_, 0.0, og, gn_, H),
                         g_ref[...], h0r[...] + h1r[...], o0r[...] + o1r[...], og_ref[...], gn_ref[...])
        dg, dh, do, dog, dgn = vjp((dl_ref[...], dr_ref[...]))
        dg_ref[...] = dg
        dh_ref[...] = dh
        do_ref[...] = do
        dog_ref[...] = dog
        _acc_rows(dgn_ref, i == 0, [dgn])

    row = geo.row(W)
    return pl.pallas_call(
        body, name=name, grid=(geo.nt,),
        in_specs=[geo.row(W, 0), row, row, row, row, geo.row(W, 5), _full((1, W)), geo.row(W, 0), geo.row(W, 1)],
        out_specs=[row] * 4 + [_full((1, W))], out_shape=[S_((geo.T, W), F32)] * 4 + [S_((1, W), F32)],
        compiler_params=_cp())(P, h0, h1, o0, o1, P, gn, dmix, dmix)


def _head_norm_rot(x, g, cosf, sinf):
    xn = _rms(x) * g
    return xn * cosf + _rollv(xn, x.shape[1] // 2, 1) * sinf


def _qk_prep_fwd(geo, P, dims, qg, kg, cosf, sinf, name):
    PW, QW, KW, hd = dims
    qscale = hd ** -0.5 * LOG2E

    def body(p_ref, qg_ref, kg_ref, c_ref, s_ref, q_ref, k_ref, v_ref):
        c, s = c_ref[...], s_ref[...]
        for h in range(QW // hd):
            qh = _head_norm_rot(p_ref[:, PW + h * hd:PW + (h + 1) * hd], qg_ref[...], c, s)
            q_ref[:, h * hd:(h + 1) * hd] = (qh * qscale).astype(BF)
        for h in range(KW // hd):
            o = PW + QW + h * hd
            k_ref[:, h * hd:(h + 1) * hd] = _head_norm_rot(p_ref[:, o:o + hd], kg_ref[...], c, s).astype(BF)
            v_ref[:, 2 * h * hd:(2 * h + 1) * hd] = p_ref[:, o + KW:o + KW + hd].astype(BF)
            v_ref[:, (2 * h + 1) * hd:(2 * h + 2) * hd] = jnp.ones((geo.RT, hd), BF)

    tot = PW + QW + 2 * KW
    return pl.pallas_call(
        body, name=name, grid=(geo.nt,),
        in_specs=[geo.row(tot), _full((1, hd)), _full((1, hd)), geo.row(hd), geo.row(hd)],
        out_specs=[geo.row(QW), geo.row(KW), geo.row(2 * KW)],
        out_shape=[S_((geo.T, QW), BF), S_((geo.T, KW), BF), S_((geo.T, 2 * KW), BF)], compiler_params=_cp())(P, qg, kg, cosf, sinf)


def _qk_prep_bwd(geo, P, dims, qg, kg, cosf, sinf, dpool, dq, dk, dv, name):
    PW, QW, KW, hd = dims

    def body(p_ref, qg_ref, kg_ref, c_ref, s_ref, dpool_ref, dq_ref, dk_ref, dv_ref, dP_ref, dqg_ref, dkg_ref):
        i = pl.program_id(0)
        c, s = c_ref[...], s_ref[...]
        dP_ref[:, :PW] = dpool_ref[...]
        f = lambda x, g: _head_norm_rot(x, g, c, s)
        dqg = jnp.zeros((1, hd), F32)
        for h in range(QW // hd):
            o = PW + h * hd
            _, vjp = jax.vjp(f, p_ref[:, o:o + hd], qg_ref[...])
            dx, dg = vjp(dq_ref[:, h * hd:(h + 1) * hd])
            dP_ref[:, o:o + hd] = dx
            dqg = dqg + dg
        dkg = jnp.zeros((1, hd), F32)
        for h in range(KW // hd):
            o = PW + QW + h * hd
            _, vjp = jax.vjp(f, p_ref[:, o:o + hd], kg_ref[...])
            dx, dg = vjp(dk_ref[:, h * hd:(h + 1) * hd])
            dP_ref[:, o:o + hd] = dx
            dkg = dkg + dg
        dP_ref[:, PW + QW + KW:] = dv_ref[...]
        _acc_rows(dqg_ref, i == 0, [dqg])
        _acc_rows(dkg_ref, i == 0, [dkg])

    tot = PW + QW + 2 * KW
    return pl.pallas_call(
        body, name=name, grid=(geo.nt,),
        in_specs=[geo.row(tot), _full((1, hd)), _full((1, hd)), geo.row(hd), geo.row(hd), geo.row(PW), geo.row(QW), geo.row(KW), geo.row(KW)],
        out_specs=[geo.row(tot), _full((1, hd)), _full((1, hd))],
        out_shape=[S_((geo.T, tot), F32), S_((1, hd), F32), S_((1, hd), F32)], compiler_params=_cp())(
            P, qg, kg, cosf, sinf, dpool, dq, dk, dv)


def _att_tiles(T):
    return _tile(T, 256, 16), _tile(T, 768, 16)


def _stack_heads(ref, G, hd):
    return jnp.concatenate([ref[:, g * hd:(g + 1) * hd] for g in range(G)], axis=0)


def _att_fwd(q, k, v1, hd, name):
    T, QW = q.shape
    KV = k.shape[1] // hd
    G = QW // hd // KV
    tq, tk = _att_tiles(T)
    nk = T // tk

    def body(q_ref, k_ref, v_ref, o_ref, lse_ref, m_scr, acc):
        ki = pl.program_id(2)

        @pl.when(ki == 0)
        def _():
            m_scr[...] = jnp.full(m_scr.shape, -jnp.inf, F32)
            acc[...] = jnp.zeros(acc.shape, F32)

        s = _dot(_stack_heads(q_ref, G, hd), k_ref[...], NT)
        m_prev = m_scr[...]
        m_new = jnp.maximum(m_prev, jnp.max(s, axis=-1, keepdims=True))
        p = jnp.exp2(s - jnp.tile(m_new, (1, tk // hd)))
        acc[...] = jnp.tile(jnp.exp2(m_prev - m_new), (1, 2)) * acc[...] + _dot(p, v_ref[...], NN)
        m_scr[...] = m_new

        @pl.when(ki == nk - 1)
        def _():
            a = acc[...]
            o = a[:, :hd] / a[:, hd:]
            lse = m_scr[...] + jnp.log2(a[:, hd:])
            for g in range(G):
                o_ref[:, g * hd:(g + 1) * hd] = o[g * tq:(g + 1) * tq].astype(BF)
                lse_ref[g] = lse[g * tq:(g + 1) * tq]

    return pl.pallas_call(
        body, name=name, grid=(KV, T // tq, nk),
        in_specs=[pl.BlockSpec((tq, G * hd), lambda a, i, j: (i, a)), pl.BlockSpec((tk, hd), lambda a, i, j: (j, a)),
                  pl.BlockSpec((tk, 2 * hd), lambda a, i, j: (j, a))],
        out_specs=[pl.BlockSpec((tq, G * hd), lambda a, i, j: (i, a)), pl.BlockSpec((G, tq, hd), lambda a, i, j: (a, i, 0))],
        out_shape=[S_((T, QW), BF), S_((QW // hd, T, hd), F32)],
        scratch_shapes=[pltpu.VMEM((G * tq, hd), F32), pltpu.VMEM((G * tq, 2 * hd), F32)],
        compiler_params=_cp())(q, k, v1)


def _att_delta(geo, o, dmix, PW, hd, name):
    QW = o.shape[1]
    nh = QW // hd

    def body(o_ref, do_ref, d_ref):
        for h in range(nh):
            sl = slice(h * hd, (h + 1) * hd)
            d = jnp.sum(o_ref[:, sl].astype(F32) * do_ref[:, PW + h * hd:PW + (h + 1) * hd], axis=-1, keepdims=True)
            d_ref[h] = jnp.broadcast_to(d, (geo.RT, hd))

    return pl.pallas_call(body, name=name, grid=(geo.nt,), in_specs=[geo.row(QW), geo.row(PW + QW)],
                          out_specs=pl.BlockSpec((nh, geo.RT, hd), lambda i: (0, i, 0)), out_shape=S_((nh, geo.T, hd), F32),
                          compiler_params=_cp())(o, dmix)


def _att_bwd(q, k, v1, dmix, PW, lse, delta, hd, name):
    T, QW = q.shape
    KW = k.shape[1]
    KV = KW // hd
    G = QW // hd // KV
    tq, tk = _att_tiles(T)
    nq, nk = T // tq, T // tk
    scale = hd ** -0.5
    pb = PW // hd

    def body(q_ref, k_ref, v_ref, *rest):
        do_refs, (lse_ref, dl_ref, dq_ref, dk_ref, dv_ref, dq_scr) = rest[:G], rest[G:]
        ki, qi = pl.program_id(1), pl.program_id(2)
        q3 = _stack_heads(q_ref, G, hd)
        do3 = jnp.concatenate([r[...] for r in do_refs], axis=0)
        lse = jnp.concatenate([lse_ref[g] for g in range(G)], axis=0)
        dl = jnp.concatenate([dl_ref[g] for g in range(G)], axis=0)
        kk = k_ref[...]
        p = jnp.exp2(_dot(q3, kk, NT) - jnp.tile(lse, (1, tk // hd)))
        ds = p * (_dot(do3, v_ref[:, :hd], NT) - jnp.tile(dl, (1, tk // hd)))
        pv = _dot(p, do3, TN)
        pk = _dot(ds, q3, TN)
        pq = _dot(ds, kk, NN)

        @pl.when(qi == 0)
        def _():
            dk_ref[...] = pk
            dv_ref[...] = pv

        @pl.when(qi > 0)
        def _():
            dk_ref[...] += pk
            dv_ref[...] += pv

        @pl.when(qi == nq - 1)
        def _():
            dk_ref[...] = dk_ref[...] * LN2

        @pl.when(ki == 0)
        def _():
            dq_scr[qi] = pq

        @pl.when(ki > 0)
        def _():
            dq_scr[qi] += pq

        @pl.when(ki == nk - 1)
        def _():
            full = dq_scr[qi]
            for g in range(G):
                dq_ref[:, g * hd:(g + 1) * hd] = full[g * tq:(g + 1) * tq] * scale

    qs = pl.BlockSpec((tq, G * hd), lambda a, j, i: (i, a))
    ks = pl.BlockSpec((tk, hd), lambda a, j, i: (j, a))
    vs = pl.BlockSpec((tk, 2 * hd), lambda a, j, i: (j, a))
    dos = [pl.BlockSpec((tq, hd), lambda a, j, i, g=g: (i, pb + a * G + g)) for g in range(G)]
    st = pl.BlockSpec((G, tq, hd), lambda a, j, i: (a, i, 0))
    dqs = pl.BlockSpec((tq, G * hd), lambda a, j, i: (jnp.where(j == nk - 1, i, 0), a))
    return pl.pallas_call(body, name=name, grid=(KV, nk, nq), in_specs=[qs, ks, vs] + dos + [st, st], out_specs=[dqs, ks, ks],
                          out_shape=[S_((T, QW), F32), S_((T, KW), F32), S_((T, KW), F32)],
                          scratch_shapes=[pltpu.VMEM((nq, G * tq, hd), F32)], compiler_params=_cp())(
                              q, k, v1, *([dmix] * G), lse, delta)


def _pool_cnt(pos, L, w):
    return (jnp.minimum(pos + w // 2, L) - jnp.maximum(pos - w // 2, 0)).astype(F32)


def _pool_mean(ext, gi, w, G, RT, cnt):
    acc = ext[pl.ds(HALO - w // 2, RT), gi * G:(gi + 1) * G]
    for off in range(-w // 2 + 1, w // 2):
        acc = acc + ext[pl.ds(HALO + off, RT), gi * G:(gi + 1) * G]
    return acc / cnt


def _pool_fwd(geo, P, PW, att, pw, ps, name):
    RT = geo.RT
    G = pw.shape[1]
    QW = att.shape[1]

    def body(pp, pc, pn, att_ref, pw_ref, ps_ref, m_ref, ext):
        i = pl.program_id(0)
        _fill_ext(geo, ext, pp, pc, pn, i)
        pos, L = geo.pos(i)
        for gi, w in enumerate(POOL_WINDOWS):
            sl = slice(gi * G, (gi + 1) * G)
            xm = _pool_mean(ext, gi, w, G, RT, _pool_cnt(pos, L, w)) - pc[:, sl]
            m_ref[:, sl] = (_dot(xm, pw_ref[gi], NN) * ps_ref[:, sl]).astype(BF)
        m_ref[:, PW:] = att_ref[...]

    return pl.pallas_call(
        body, name=name, grid=(geo.nt,),
        in_specs=[geo.prev(PW), geo.row(PW), geo.next(PW), geo.row(QW), _full(pw.shape), _full(ps.shape)],
        out_specs=geo.row(PW + QW), out_shape=S_((geo.T, PW + QW), BF),
        scratch_shapes=[pltpu.VMEM((RT + 2 * HALO, PW), F32)], compiler_params=_cp())(P, P, P, att, pw, ps)


def _pool_bwd(geo, P, PW, dmix, pw, ps, name):
    RT = geo.RT
    G = pw.shape[1]
    RE = RT + 2 * HALO

    def body(pp, pc, pn, dp_, dc, dn, pw_ref, ps_ref, dx_ref, dpw, dps, ext, extd, dmc):
        i = pl.program_id(0)
        _fill_ext(geo, ext, pp, pc, pn, i)
        _fill_ext(geo, extd, dp_, dc, dn, i)
        pos, L = geo.pos(i)
        r = lax.broadcasted_iota(jnp.int32, (RE, 1), 0)
        pos_e = pos[0:1, :] - HALO + r
        rows_s = []
        for gi, w in enumerate(POOL_WINDOWS):
            sl = slice(gi * G, (gi + 1) * G)
            xm = _pool_mean(ext, gi, w, G, RT, _pool_cnt(pos, L, w)) - pc[:, sl]
            pre = _dot(xm, pw_ref[gi], NN)
            dout = dc[:, sl]
            rows_s.append(jnp.sum(dout * pre, axis=0, keepdims=True))
            gw = _dot(xm, dout * ps_ref[:, sl], TN)

            @pl.when(i == 0)
            def _(gi=gi, gw=gw):
                dpw[gi] = gw

            @pl.when(i > 0)
            def _(gi=gi, gw=gw):
                dpw[gi] += gw

            dm_e = _dot(extd[:, sl] * ps_ref[:, sl], pw_ref[gi], NT)
            dmc[...] = dm_e / jnp.maximum(_pool_cnt(pos_e, L, w), 1.0)
            acc = -dm_e[HALO:HALO + RT]
            for off in range(-w // 2 + 1, w // 2 + 1):
                acc = acc + dmc[pl.ds(HALO + off, RT), :]
            dx_ref[:, sl] = acc
        _acc_rows(dps, i == 0, [jnp.concatenate(rows_s, axis=-1)])

    return pl.pallas_call(
        body, name=name, grid=(geo.nt,),
        in_specs=[geo.prev(PW), geo.row(PW), geo.next(PW), geo.prev(PW), geo.row(PW), geo.next(PW), _full(pw.shape), _full(ps.shape)],
        out_specs=[geo.row(PW), _full(pw.shape), _full(ps.shape)],
        out_shape=[S_((geo.T, PW), F32), S_(pw.shape, F32), S_(ps.shape, F32)],
        scratch_shapes=[pltpu.VMEM((RE, PW), F32), pltpu.VMEM((RE, PW), F32), pltpu.VMEM((RE, G), F32)],
        compiler_params=_cp())(P, P, P, dmix, dmix, dmix, pw, ps)


def _mod_fwd(A, mod_w, name):
    L, D, MC = mod_w.shape
    tn = _tile(MC, 768, 128)

    def body(a_ref, w_ref, o_ref):
        o_ref[...] = _dot(jax.nn.silu(a_ref[...]), w_ref[...], NN)

    return pl.pallas_call(body, name=name, grid=(L, MC // tn),
                          in_specs=[_full(A.shape), pl.BlockSpec((None, D, tn), lambda l, j: (l, 0, j))],
                          out_specs=pl.BlockSpec((None, 16, tn), lambda l, j: (l, 0, j)), out_shape=S_((L, 16, MC), F32),
                          compiler_params=_cp())(A, mod_w)


def _mod_bwd(A, DM, mod_w, name):
    L, D, MC = mod_w.shape
    tn = _tile(MC, 768, 128)
    nj = MC // tn

    def body(a_ref, dm_ref, w_ref, gw_ref, da_ref, acc):
        l, j = pl.program_id(0), pl.program_id(1)
        sa, vjp = jax.vjp(jax.nn.silu, a_ref[...])
        gw_ref[...] = _dot(sa, dm_ref[...], TN)
        part = _dot(dm_ref[...], w_ref[...], NT)
        first = jnp.logical_and(l == 0, j == 0)

        @pl.when(first)
        def _():
            acc[...] = part

        @pl.when(jnp.logical_not(first))
        def _():
            acc[...] += part

        @pl.when(jnp.logical_and(l == L - 1, j == nj - 1))
        def _():
            da_ref[...] = vjp(acc[...])[0]

    wspec = pl.BlockSpec((None, D, tn), lambda l, j: (l, 0, j))
    return pl.pallas_call(body, name=name, grid=(L, nj),
                          in_specs=[_full(A.shape), pl.BlockSpec((None, 16, tn), lambda l, j: (l, 0, j)), wspec],
                          out_specs=[wspec, _full(A.shape)], out_shape=[S_((L, D, MC), F32), S_(A.shape, F32)],
                          scratch_shapes=[pltpu.VMEM(A.shape, F32)], compiler_params=_cp())(A, DM, mod_w)


PACK_COLS = 1024


def _pack_rows(shape):
    n = 1
    for d in shape:
        n *= d
    return n, -(-n // (8 * PACK_COLS)) * 8


def _pack(arrs):
    parts = []
    for a in arrs:
        n, rows = _pack_rows(a.shape)
        parts.append(jnp.pad(a.reshape(-1).astype(F32), (0, rows * PACK_COLS - n)).reshape(rows, PACK_COLS))
    return jnp.concatenate(parts)


def _unpack(packed, shapes, lead=()):
    out, off = [], 0
    for s in shapes:
        n, rows = _pack_rows(s)
        blk = packed[..., off:off + rows, :].reshape(lead + (rows * PACK_COLS,))
        out.append(blk[..., :n].reshape(lead + tuple(s)))
        off += rows
    return out


def _unshard_last(g):
    g = jnp.moveaxis(g, 0, -2)
    return g.reshape(g.shape[:-2] + (g.shape[-2] * g.shape[-1],))


def _my_shard(a, me):
    n = a.shape[-1] // NDEV
    return lax.dynamic_slice_in_dim(a, me * n, n, axis=a.ndim - 1)


def _rot_tables(S, LC, dk, hd):
    t = jnp.arange(S, dtype=F32)
    n_r = dk // 2
    ang1 = t[:, None] * (RET_THETA ** (-jnp.arange(n_r, dtype=F32) / n_r))
    cos1 = jnp.concatenate([jnp.cos(ang1), jnp.ones((LC, n_r), F32)])
    sin1 = jnp.concatenate([jnp.sin(ang1), jnp.zeros((LC, n_r), F32)])
    n_ax = hd // 4
    f_ax = ROPE_THETA ** (-jnp.arange(n_ax, dtype=F32) / n_ax)
    row = jnp.floor(t / GRID_W)
    col = t - row * GRID_W
    ang2 = jnp.concatenate([row[:, None] * f_ax, col[:, None] * f_ax], axis=-1)
    c2, s2 = jnp.cos(ang2), jnp.sin(ang2)
    cosf = jnp.concatenate([jnp.concatenate([c2, c2], axis=-1), jnp.ones((LC, hd), F32)])
    sinf = jnp.concatenate([jnp.concatenate([-s2, s2], axis=-1), jnp.zeros((LC, hd), F32)])
    return cos1, sin1, cosf, sinf


SMALL = ("c_ctx", "mod_b", "norm_pre", "norm_post", "lru_conv_w", "lru_conv_b", "lru_wa", "lru_ba", "lru_wx", "lru_bx",
         "lru_lambda", "ret_decay_logit", "ret_gn", "pool_w", "pool_scale", "q_norm", "k_norm")
WEIGHTS = ("c_ctx", "mod_w", "mod_b", "norm_pre", "norm_post", "ffn_gate", "ffn_up", "ffn_down", "ev_w_in", "ev_w_out",
           "lru_conv_w", "lru_conv_b", "lru_wa", "lru_ba", "lru_wx", "lru_bx", "lru_lambda", "ret_decay_logit", "ret_gn",
           "od_w_in", "od_w_out", "pool_w", "pool_scale", "q_norm", "k_norm")
INPUTS = ("x", "c", "ctx") + WEIGHTS + ("loss_target",) + tuple("m_" + w for w in WEIGHTS) + tuple("v_" + w for w in WEIGHTS)


def _step(p):
    x, c, ctx = p["x"], p["c"], p["ctx"]
    _, S, D = x.shape
    LC = ctx.shape[1]
    geo = _Geo(S, LC, D)
    T = geo.T
    xi, yi, ci = _me()
    me = 4 * xi + 2 * yi + ci
    L = p["mod_w"].shape[0]
    assert L == 2
    W = p["lru_conv_b"].shape[-1]
    H = p["ret_decay_logit"].shape[-1]
    hd = p["q_norm"].shape[-1]
    G = p["pool_w"].shape[-1]
    PW = G * len(POOL_WINDOWS)
    od_mix = p["od_w_out"].shape[1] * NDEV
    od_in = p["od_w_in"].shape[2] * NDEV
    QW = od_mix - PW
    KW = (od_in - od_mix) // 2
    assert p["ev_w_in"].shape[2] * NDEV == 6 * W and p["ret_gn"].shape[-1] == W and p["ev_w_out"].shape[1] * NDEV == 2 * W
    odims = (PW, QW, KW, hd)
    cos1, sin1, cosf, sinf = _rot_tables(S, LC, W // H, hd)

    sh0 = [(D,), p["norm_pre"].shape, p["norm_post"].shape, p["lru_conv_w"].shape[1:], p["lru_ba"].shape[1:],
           p["lru_bx"].shape[1:], p["lru_lambda"].shape[1:], p["pool_scale"].shape[1:]]
    pack0 = _pack([c, p["norm_pre"], p["norm_post"], p["lru_conv_w"], p["lru_ba"], p["lru_bx"], p["lru_lambda"], p["pool_scale"]])
    (g0,) = _all_gather([pack0], "gather_small")
    c_all, npre, npost, conv_w, ba, bx, lam, pscale = _unpack(g0, sh0, (NDEV,))
    npre, npost, conv_w, ba, bx, lam, pscale = [_unshard_last(a) for a in (npre, npost, conv_w, ba, bx, lam, pscale)]
    pscale = pscale[None]
    conv_b = p["lru_conv_b"]
    wa, wx = p["lru_wa"][0], p["lru_wx"][0]
    gn = p["ret_gn"]
    logits = p["ret_decay_logit"][0].reshape(2, H, 1, 1)
    pool_w = p["pool_w"][0]
    qg, kg = p["q_norm"], p["k_norm"]

    A = jnp.concatenate([c_all, p["c_ctx"][None], jnp.zeros((7, D), F32)])
    M = _mod_fwd(A, p["mod_w"], "mod_fwd")
    (Mg,) = _all_gather([M], "gather_mod")
    MC = M.shape[2]
    tabs = []
    for l in range(L):
        ml = lax.dynamic_index_in_dim(Mg[:, l], me, axis=1, keepdims=False).reshape(NDEV * MC) + p["mod_b"][l]
        mc = Mg[:, l, 8].reshape(NDEV * MC) + p["mod_b"][l]
        tabs.append(jnp.stack([ml.reshape(9, D), mc.reshape(9, D)]))

    def cast2(a, name):
        return _cast_bf16(a.reshape(-1, a.shape[-1]), name).reshape(a.shape)

    loc = {(l, j): [cast2(p[n][l, j], f"cast_{n}_{l}{j}") for n in ("ffn_gate", "ffn_up", "ffn_down")]
           for l in range(L) for j in range(2)}
    loc["ev"] = [cast2(p["ev_w_in"][0], "cast_ev_in"), cast2(p["ev_w_out"][0], "cast_ev_out")]
    loc["od"] = [cast2(p["od_w_in"][0], "cast_od_in"), cast2(p["od_w_out"][0], "cast_od_out")]

    def gather_start(key, name):
        return _xchg_start(True, loc[key], [jnp.broadcast_to(a[None], (NDEV,) + a.shape) for a in loc[key]], name)

    ffn_w = {(0, 0): _all_gather(loc[0, 0], "gather_ffn_00")}

    def gp(a, l, s):
        return a[l, s][None]

    st, tok = gather_start("ev", "gs_ev")
    x0 = _tie(jnp.concatenate([x[0], ctx[0]]), tok)
    (h0,) = _norm_fwd(geo, x0, None, (gp(npre, 0, 0), tabs[0], 0), "pre_00")
    y0, G0, U0 = _ffn_fwd(h0, *ffn_w[0, 0], name="ffn_fwd_00")
    x1, h1 = _norm_fwd(geo, x0, (y0, gp(npost, 0, 0), tabs[0], 0, FFN_STEP), (gp(npre, 0, 1), tabs[0], 1), "post_00")
    ev_in, ev_out = _xchg_wait(st, h1, "gw_ev")
    ev_out_f = ev_out.reshape(2 * W, D)

    st, tok = gather_start((0, 1), "gs_ffn_01")
    Pe = _mm_cols(_tie(h1, tok), ev_in, "ev_in")
    u, a0, b0, a1, b1 = _lru_coef_fwd(geo, Pe, W, conv_w, conv_b, wa, ba, wx, bx, lam, "lru_coef")
    hs0, hp0 = _lru_scan_fwd(geo, a0, b0, 0, "lru_scan_f0")
    hs1, hp1 = _lru_scan_fwd(geo, a1, b1, 1, "lru_scan_f1")
    o0, st0 = _ret_fwd(geo, Pe, W, H, logits[0], cos1, sin1, 0, "ret_f0")
    o1, st1 = _ret_fwd(geo, Pe, W, H, logits[1], cos1, sin1, 1, "ret_f1")
    mixe = _mix_even_fwd(geo, Pe, W, H, hs0, hs1, o0, o1, gn, "mix_even")
    y1 = _mm_full(mixe, ev_out_f, NN, "ev_out")
    x2, h2 = _norm_fwd(geo, x1, (y1, gp(npost, 0, 1), tabs[0], 1, 1.0), (gp(npre, 0, 2), tabs[0], 2), "post_01")
    ffn_w[0, 1] = _xchg_wait(st, h2, "gw_ffn_01")

    st, tok = gather_start((1, 0), "gs_ffn_10")
    y2, G2, U2 = _ffn_fwd(_tie(h2, tok), *ffn_w[0, 1], name="ffn_fwd_01")
    x3, h3 = _norm_fwd(geo, x2, (y2, gp(npost, 0, 2), tabs[0], 2, FFN_STEP), (gp(npre, 1, 0), tabs[1], 0), "post_02")
    ffn_w[1, 0] = _xchg_wait(st, h3, "gw_ffn_10")

    st, tok = gather_start("od", "gs_od")
    y3, G3, U3 = _ffn_fwd(_tie(h3, tok), *ffn_w[1, 0], name="ffn_fwd_10")
    x4, h4 = _norm_fwd(geo, x3, (y3, gp(npost, 1, 0), tabs[1], 0, FFN_STEP), (gp(npre, 1, 1), tabs[1], 1), "post_10")
    od_inw, od_out = _xchg_wait(st, h4, "gw_od")
    od_out_f = od_out.reshape(od_mix, D)

    st, tok = gather_start((1, 1), "gs_ffn_11")
    Po = _mm_cols(_tie(h4, tok), od_inw, "od_in")
    qr, kr, vr = _qk_prep_fwd(geo, Po, odims, qg, kg, cosf, sinf, "qk_prep")
    att, lse = _att_fwd(qr, kr, vr, hd, "att_fwd")
    mixo = _pool_fwd(geo, Po, PW, att, pool_w, pscale, "pool_fwd")
    y4 = _mm_full(mixo, od_out_f, NN, "od_out")
    x5, h5 = _norm_fwd(geo, x4, (y4, gp(npost, 1, 1), tabs[1], 1, 1.0), (gp(npre, 1, 2), tabs[1], 2), "post_11")
    ffn_w[1, 1] = _xchg_wait(st, h5, "gw_ffn_11")

    y5, G5, U5 = _ffn_fwd(h5, *ffn_w[1, 1], name="ffn_fwd_11")
    (x6,) = _norm_fwd(geo, x5, (y5, gp(npost, 1, 2), tabs[1], 2, FFN_STEP), None, "post_12")

    big_g = {}

    def a2a_start(key, srcs, name, dh):
        own = [lax.dynamic_index_in_dim(a, me, 0, keepdims=False) for a in srcs]
        state, token = _xchg_start(False, srcs, [jnp.zeros(a.shape, a.dtype) for a in srcs], name)
        big_g[key] = (state, own)
        return _tie(dh, token)

    def ffn_bwd(dy, h, Gs, Us, key):
        tag = f"{key[0]}{key[1]}"
        dh, dG, dU, Aact = _ffn_bwd_act(dy, Gs, Us, *ffn_w[key], name=f"ffn_bwd_{tag}")
        srcs = [_ffn_wgrad_in(h, dG, f"ffn_wg_{tag}"), _ffn_wgrad_in(h, dU, f"ffn_wu_{tag}"), _ffn_wgrad_out(Aact, dy, f"ffn_wd_{tag}")]
        return a2a_start(key, srcs, f"as_ffn_{tag}", dh)

    loss_p, dx6, dy5, dpost5 = _loss_bwd(geo, x6, p["loss_target"][0], (y5, gp(npost, 1, 2), tabs[1], 2, FFN_STEP), "loss")
    dh5 = ffn_bwd(dy5, h5, G5, U5, (1, 1))
    dx5, dy4, dpre5, dpost4 = _norm_bwd(geo, dx6, dh5, x5, (gp(npre, 1, 2), tabs[1], 2),
                                        (y4, gp(npost, 1, 1), tabs[1], 1, 1.0), "nb_5")

    dmixo = _mm_full(dy4, od_out_f, NT, "od_out_d")
    g_od_out = _mm_tn_rows(mixo, dy4, NDEV, "od_out_w")
    dpool, g_pool_w, g_pscale = _pool_bwd(geo, Po, PW, dmixo, pool_w, pscale, "pool_bwd")
    delta = _att_delta(geo, att, dmixo, PW, hd, "att_delta")
    dq, dk, dv = _att_bwd(qr, kr, vr, dmixo, PW, lse, delta, hd, "att_bwd")
    dPo, g_qn, g_kn = _qk_prep_bwd(geo, Po, odims, qg, kg, cosf, sinf, dpool, dq, dk, dv, "qk_prep_bwd")
    dh4 = _mm_nt_cols(dPo, od_inw, "od_in_d")
    g_od_in = _mm_tn_cols(h4, dPo, NDEV, "od_in_w")
    dh4 = a2a_start("od", [g_od_in, g_od_out], "as_od", dh4)
    dx4, dy3, dpre4, dpost3 = _norm_bwd(geo, dx5, dh4, x4, (gp(npre, 1, 1), tabs[1], 1),
                                        (y3, gp(npost, 1, 0), tabs[1], 0, FFN_STEP), "nb_4")

    dh3 = ffn_bwd(dy3, h3, G3, U3, (1, 0))
    dx3, dy2, dpre3, dpost2 = _norm_bwd(geo, dx4, dh3, x3, (gp(npre, 1, 0), tabs[1], 0),
                                        (y2, gp(npost, 0, 2), tabs[0], 2, FFN_STEP), "nb_3")

    dh2 = ffn_bwd(dy2, h2, G2, U2, (0, 1))
    dx2, dy1, dpre2, dpost1 = _norm_bwd(geo, dx3, dh2, x2, (gp(npre, 0, 2), tabs[0], 2),
                                        (y1, gp(npost, 0, 1), tabs[0], 1, 1.0), "nb_2")

    dmixe = _mm_full(dy1, ev_out_f, NT, "ev_out_d")
    g_ev_out = _mm_tn_rows(mixe, dy1, NDEV, "ev_out_w")
    dg, dhs, dos, dog, g_gn = _mix_even_bwd(geo, Pe, W, H, hs0, hs1, o0, o1, gn, dmixe, "mix_even_bwd")
    da0, db0 = _lru_scan_bwd(geo, dhs, a0, hp0, 0, "lru_scan_b0")
    da1, db1 = _lru_scan_bwd(geo, dhs, a1, hp1, 1, "lru_scan_b1")
    du, g_wa, g_ba, g_wx, g_bx, g_lam = _lru_coef_bwd(geo, u, (da0, da1), (db0, db1), W, wa, ba, wx, bx, lam, "lru_coef_bwd")
    dq0, dk0, dv0, glg0 = _ret_bwd(geo, Pe, W, H, logits[0], cos1, sin1, st0, dos, None, 0, "ret_b0")
    dqe, dke, dve, glg1 = _ret_bwd(geo, Pe, W, H, logits[1], cos1, sin1, st1, dos, (dq0, dk0, dv0), 1, "ret_b1")
    dPe, g_cw, g_cb = _conv_bwd_assemble(geo, Pe, du, (dg, dqe, dke, dve, dog), W, conv_w, "conv_bwd")
    dh1 = _mm_nt_cols(dPe, ev_in, "ev_in_d")
    g_ev_in = _mm_tn_cols(h1, dPe, NDEV, "ev_in_w")
    dh1 = a2a_start("ev", [g_ev_in, g_ev_out], "as_ev", dh1)
    dx1, dy0, dpre1, dpost0 = _norm_bwd(geo, dx2, dh1, x1, (gp(npre, 0, 1), tabs[0], 1),
                                        (y0, gp(npost, 0, 0), tabs[0], 0, FFN_STEP), "nb_1")

    dh0 = ffn_bwd(dy0, h0, G0, U0, (0, 0))
    dx0, dpre0 = _norm_bwd(geo, dx1, dh0, x0, (gp(npre, 0, 0), tabs[0], 0), None, "nb_0")

    dpre = [[dpre0, dpre1, dpre2], [dpre3, dpre4, dpre5]]
    dpost = [[dpost0, dpost1, dpost2], [dpost3, dpost4, dpost5]]
    dtab = jnp.stack([jnp.stack([jnp.stack([dpre[l][s][:, 1], dpre[l][s][:, 2], dpost[l][s][:, 1]], axis=1) for s in range(3)], axis=1)
                      for l in range(L)])
    dtab_p = _pack([jnp.moveaxis(dtab.reshape(L, 2, 9 * D), 1, 0)])
    (dtab_g,) = _all_gather([dtab_p], "gather_dtab")
    dtab_sum = _sum_n(dtab_g, "sum_dtab")
    (dm_all,) = _unpack(dtab_g, [(2, L, 9 * D)], (NDEV,))
    (dm_sum,) = _unpack(dtab_sum, [(2, L, 9 * D)])
    (g_mod_b,) = _unpack(_sum_n(jnp.stack([_pack([dm_sum[0]]), _pack([dm_sum[1]])]), "sum_mod_b"), [(L, 9 * D)])
    dml = lax.dynamic_slice_in_dim(dm_all[:, 0], me * MC, MC, axis=2)
    dmc = lax.dynamic_slice_in_dim(dm_sum[1], me * MC, MC, axis=1)
    DM = jnp.concatenate([jnp.moveaxis(dml, 0, 1), dmc[:, None], jnp.zeros((L, 7, MC), F32)], axis=1)
    g_mod_w, dA = _mod_bwd(A, DM, p["mod_w"], "mod_bwd")

    g_npre = jnp.stack([jnp.stack([dpre[l][s][0, 0] + dpre[l][s][1, 0] for s in range(3)]) for l in range(L)])
    g_npost = jnp.stack([jnp.stack([dpost[l][s][0, 0] + dpost[l][s][1, 0] for s in range(3)]) for l in range(L)])
    g_logit = jnp.stack([glg0.reshape(H), glg1.reshape(H)])
    small_parts = [dA[8], g_npre, g_npost, g_cw, g_cb, g_wa, g_ba, g_wx, g_bx, g_lam, g_logit, g_gn, g_pool_w, g_pscale, g_qn, g_kn]
    (sg,) = _all_gather([_pack(small_parts)], "gather_small_g")
    ssum = _unpack(_sum_n(sg, "sum_small_g"), [a.shape for a in small_parts])
    (g_cctx, g_npre, g_npost, g_cw, g_cb, g_wa, g_ba, g_wx, g_bx, g_lam, g_logit, g_gn, g_pool_w, g_pscale, g_qn, g_kn) = ssum
    small_g = {
        "c_ctx": g_cctx, "mod_b": g_mod_b, "norm_pre": _my_shard(g_npre, me), "norm_post": _my_shard(g_npost, me),
        "lru_conv_w": _my_shard(g_cw, me)[None], "lru_conv_b": g_cb, "lru_wa": g_wa[None], "lru_ba": _my_shard(g_ba, me)[None],
        "lru_wx": g_wx[None], "lru_bx": _my_shard(g_bx, me)[None], "lru_lambda": _my_shard(g_lam, me)[None],
        "ret_decay_logit": g_logit[None], "ret_gn": g_gn, "pool_w": g_pool_w[None], "pool_scale": _my_shard(g_pscale, me),
        "q_norm": g_qn, "k_norm": g_kn,
    }
    shapes = [p[n].shape for n in SMALL]
    s_out = _reduce_adam(_pack([small_g[n] for n in SMALL])[None], _pack([p[n] for n in SMALL]),
                         _pack([p["m_" + n] for n in SMALL]), _pack([p["v_" + n] for n in SMALL]), "adam_small")
    res = {}
    for kind, packed in zip(("g", "d", "m", "v"), s_out):
        for n, a in zip(SMALL, _unpack(packed, shapes)):
            res[kind, n] = a

    def big(name, pieces, own, idx=None):
        w, m, v = p[name], p["m_" + name], p["v_" + name]
        if idx is not None:
            w, m, v = w[idx], m[idx], v[idx]
        shp = w.shape
        tag = name + ("" if idx is None else "_" + "".join(str(i) for i in idx))
        outs = _reduce_adam(pieces.reshape((pieces.shape[0], -1, shp[-1])), w.reshape(-1, shp[-1]), m.reshape(-1, shp[-1]),
                            v.reshape(-1, shp[-1]), "adam_" + tag, None if own is None else own.reshape(-1, shp[-1]))
        return [o.reshape(shp) for o in outs]

    got = {}
    for key in ((1, 1), "od", (1, 0), (0, 1), "ev", (0, 0)):
        state, own = big_g[key]
        tag = key if isinstance(key, str) else f"ffn_{key[0]}{key[1]}"
        got[key] = list(zip(_xchg_wait(state, dx0, "aw_" + tag), own))

    for name, (pieces, own) in (("mod_w", (g_mod_w[None], None)), ("ev_w_in", got["ev"][0]), ("ev_w_out", got["ev"][1]),
                                ("od_w_in", got["od"][0]), ("od_w_out", got["od"][1])):
        outs = big(name, pieces, own, None if name == "mod_w" else (0,))
        for kind, o in zip(("g", "d", "m", "v"), outs):
            res[kind, name] = o if name == "mod_w" else o[None]
    for wi, name in enumerate(("ffn_gate", "ffn_up", "ffn_down")):
        per = {(l, j): big(name, *got[l, j][wi], (l, j)) for l in range(L) for j in range(2)}
        for ki, kind in enumerate(("g", "d", "m", "v")):
            res[kind, name] = jnp.stack([jnp.stack([per[l, j][ki] for j in range(2)]) for l in range(L)])

    loss = lax.psum(loss_p[0, 0], ("x", "y", "c"))
    grad_x = dx0[:S][None]
    return (loss, grad_x) + tuple(res[kind, n] for kind in ("g", "d", "m", "v") for n in WEIGHTS)


def kernel(
        x, c, ctx, c_ctx, mod_w, mod_b, norm_pre, norm_post, ffn_gate, ffn_up, ffn_down, ev_w_in, ev_w_out, lru_conv_w,
        lru_conv_b, lru_wa, lru_ba, lru_wx, lru_bx, lru_lambda, ret_decay_logit, ret_gn, od_w_in, od_w_out, pool_w,
        pool_scale, q_norm, k_norm, loss_target, m_c_ctx, m_mod_w, m_mod_b, m_norm_pre, m_norm_post, m_ffn_gate, m_ffn_up,
        m_ffn_down, m_ev_w_in, m_ev_w_out, m_lru_conv_w, m_lru_conv_b, m_lru_wa, m_lru_ba, m_lru_wx, m_lru_bx, m_lru_lambda,
        m_ret_decay_logit, m_ret_gn, m_od_w_in, m_od_w_out, m_pool_w, m_pool_scale, m_q_norm, m_k_norm, v_c_ctx, v_mod_w,
        v_mod_b, v_norm_pre, v_norm_post, v_ffn_gate, v_ffn_up, v_ffn_down, v_ev_w_in, v_ev_w_out, v_lru_conv_w,
        v_lru_conv_b, v_lru_wa, v_lru_ba, v_lru_wx, v_lru_bx, v_lru_lambda, v_ret_decay_logit, v_ret_gn, v_od_w_in,
        v_od_w_out, v_pool_w, v_pool_scale, v_q_norm, v_k_norm):
    args = locals()
    return _step({n: args[n] for n in INPUTS})
```

```python
import functools

import jax
import jax.numpy as jnp
from jax import lax
from jax.experimental import pallas as pl
from jax.experimental.pallas import tpu as pltpu

F32 = jnp.float32
BF = jnp.bfloat16
S_ = jax.ShapeDtypeStruct
MESH = pl.DeviceIdType.MESH

NDEV = 8
EPS = 1e-6
FFN_STEP = 0.5
LRU_C = 8.0
RET_CHUNK = 128
RET_THETA = 10000.0
ROPE_THETA = 10000.0
GRID_W = 64
POOL_WINDOWS = (2, 4, 8, 16)
ROW_TILE = 256
HALO = 8
VMEM_LIMIT = 58 * 1024 * 1024
FFN_FWD_ROWS = 768
FFN_BWD_ROWS = 528

ADAM_LR = 0.001
ADAM_B1 = 0.9
ADAM_B2 = 0.999
ADAM_EPS = 1e-08
ADAM_WD = 0.01
ADAM_STEP = 10

LOG2E = 1.4426950408889634
LN2 = 0.6931471805599453

NN = ((1,), (0,))
NT = ((1,), (1,))
TN = ((0,), (0,))


def _dot(a, b, dn):
    return lax.dot_general(a.astype(BF), b.astype(BF), (dn, ((), ())), preferred_element_type=F32)


@functools.partial(jax.custom_vjp, nondiff_argnums=(2,))
def _bdot(a, b, mode):
    return _dot(a, b, {"nn": NN, "nt": NT, "tn": TN}[mode])


def _bdot_fwd(a, b, mode):
    return _bdot(a, b, mode), (a, b)


def _bdot_bwd(mode, res, g):
    a, b = res
    if mode == "nn":
        return _dot(g, b, NT), _dot(a, g, TN)
    if mode == "nt":
        return _dot(g, b, NN), _dot(g, a, TN)
    return _dot(b, g, NT), _dot(a, g, NN)


_bdot.defvjp(_bdot_fwd, _bdot_bwd)


@functools.partial(jax.custom_vjp, nondiff_argnums=(1, 2))
def _rollv(x, shift, axis):
    return pltpu.roll(x, shift, axis)


def _rollv_fwd(x, shift, axis):
    return pltpu.roll(x, shift, axis), None


def _rollv_bwd(shift, axis, _, g):
    n = g.shape[axis]
    return (pltpu.roll(g, (n - shift) % n, axis),)


_rollv.defvjp(_rollv_fwd, _rollv_bwd)


def _cp(vmem=VMEM_LIMIT):
    return pltpu.CompilerParams(vmem_limit_bytes=vmem)


def _tile(n, pref, mult=8):
    if n <= pref:
        return n
    for t in range(pref, 0, -1):
        if n % t == 0 and t % mult == 0:
            return t
    return n


def _full(shape):
    nd = len(shape)
    return pl.BlockSpec(tuple(shape), lambda *_: (0,) * nd)


def _me():
    return lax.axis_index("x"), lax.axis_index("y"), lax.axis_index("c")


def _all_gather(arrs, name):
    n = len(arrs)

    def body(*refs):
        xs, outs = refs[:n], refs[n:2 * n]
        send_sems, recv_sems, local_sems = refs[2 * n:]
        x, y, c = _me()
        me, sibling = (x, y, c), (x, y, 1 - c)
        chips = [(1 - x, y), (x, 1 - y), (1 - x, 1 - y)]

        def blk(out, p):
            return out.at[4 * p[0] + 2 * p[1] + p[2]]

        def copy(a, k, block, to, src=None):
            return pltpu.make_async_remote_copy(
                src_ref=blk(outs[a], block) if src is None else src, dst_ref=blk(outs[a], block),
                send_sem=send_sems.at[a, k], recv_sem=recv_sems.at[a, k], device_id=to, device_id_type=MESH)

        mine = [pltpu.make_async_copy(xs[a], blk(outs[a], me), local_sems.at[a]) for a in range(n)]
        for cp in mine:
            cp.start()
        first = []
        for a in range(n):
            first.append(copy(a, 0, me, sibling, src=xs[a]))
            first += [copy(a, 1 + j, me, (*chip, c), src=xs[a]) for j, chip in enumerate(chips)]
        for cp in first:
            cp.start()
        passed = []
        for j, chip in enumerate(chips):
            for a in range(n):
                copy(a, 1 + j, (*chip, c), me).wait_recv()
                fw = copy(a, 4 + j, (*chip, c), sibling)
                fw.start()
                passed.append(fw)
        for a in range(n):
            copy(a, 0, sibling, me).wait_recv()
            for j, chip in enumerate(chips):
                copy(a, 4 + j, (*chip, 1 - c), me).wait_recv()
        for cp in first + passed:
            cp.wait_send()
        for cp in mine:
            cp.wait()

    anyspec = pl.BlockSpec(memory_space=pl.ANY)
    return pl.pallas_call(
        body, name=name,
        out_shape=[S_((NDEV,) + a.shape, a.dtype) for a in arrs],
        in_specs=[anyspec] * n, out_specs=[anyspec] * n,
        scratch_shapes=[pltpu.SemaphoreType.DMA((n, 7)), pltpu.SemaphoreType.DMA((n, 7)), pltpu.SemaphoreType.DMA((n,))],
    )(*arrs)


def _all_to_all(arrs, name):
    n = len(arrs)

    def body(*refs):
        xs, outs = refs[:n], refs[n:2 * n]
        send_sems, recv_sems, local_sems = refs[2 * n:]
        x, y, c = _me()
        me_idx = 4 * x + 2 * y + c
        mine = [pltpu.make_async_copy(xs[a].at[me_idx], outs[a].at[me_idx], local_sems.at[a]) for a in range(n)]
        for cp in mine:
            cp.start()
        copies = []
        for k in range(1, NDEV):
            kx, ky, kc = (k >> 2) & 1, (k >> 1) & 1, k & 1
            px = 1 - x if kx else x
            py = 1 - y if ky else y
            pc = 1 - c if kc else c
            p_idx = 4 * px + 2 * py + pc
            for a in range(n):
                copies.append(pltpu.make_async_remote_copy(
                    src_ref=xs[a].at[p_idx], dst_ref=outs[a].at[me_idx],
                    send_sem=send_sems.at[a, k - 1], recv_sem=recv_sems.at[a, k - 1],
                    device_id=(px, py, pc), device_id_type=MESH))
        for cp in copies:
            cp.start()
        for cp in copies:
            cp.wait_recv()
        for cp in copies:
            cp.wait_send()
        for cp in mine:
            cp.wait()

    anyspec = pl.BlockSpec(memory_space=pl.ANY)
    return pl.pallas_call(
        body, name=name,
        out_shape=[S_(a.shape, a.dtype) for a in arrs],
        in_specs=[anyspec] * n, out_specs=[anyspec] * n,
        scratch_shapes=[pltpu.SemaphoreType.DMA((n, 7)), pltpu.SemaphoreType.DMA((n, 7)), pltpu.SemaphoreType.DMA((n,))],
    )(*arrs)


HBM_SPEC = pl.BlockSpec(memory_space=pltpu.HBM)
SEM_SPEC = pl.BlockSpec(memory_space=pltpu.SEMAPHORE)
EFFECT = pltpu.SideEffectType.DATAFLOW_SIDE_EFFECTING


def _peers():
    x, y, c = _me()
    out = []
    for k in range(1, NDEV):
        px = 1 - x if (k >> 2) & 1 else x
        py = 1 - y if (k >> 1) & 1 else y
        pc = 1 - c if k & 1 else c
        out.append(((px, py, pc), 4 * px + 2 * py + pc))
    return out, 4 * x + 2 * y + c


def _xchg_copies(gather, xs, lands, send, recv):
    peers, me_idx = _peers()
    out = []
    for k, (dev, p_idx) in enumerate(peers):
        for a in range(len(xs)):
            out.append(pltpu.make_async_remote_copy(
                src_ref=xs[a] if gather else xs[a].at[p_idx], dst_ref=lands[a].at[me_idx],
                send_sem=send[a].at[k], recv_sem=recv[a].at[k], device_id=dev, device_id_type=MESH))
    return out


def _xchg_start(gather, xs, lands, name):
    n = len(xs)

    def body(*refs):
        xr, lr = refs[:n], refs[n:2 * n]
        send, recv = refs[2 * n:3 * n], refs[3 * n:4 * n]
        token = refs[6 * n]
        for cp in _xchg_copies(gather, xr, lr, send, recv):
            cp.start()
        token[...] = jnp.zeros(token.shape, F32)

    ops = [pltpu.with_memory_space_constraint(a, pltpu.HBM) for a in list(xs) + list(lands)]
    outs = pl.pallas_call(
        body, name=name,
        out_shape=[pltpu.SemaphoreType.DMA((NDEV - 1,))] * (2 * n) + [pltpu.HBM(a.shape, a.dtype) for a in ops]
        + [S_((8, 128), F32)],
        in_specs=[HBM_SPEC] * (2 * n), out_specs=[SEM_SPEC] * (2 * n) + [HBM_SPEC] * (2 * n) + [pl.BlockSpec(memory_space=pltpu.VMEM)],
        input_output_aliases={i: 2 * n + i for i in range(2 * n)},
        compiler_params=pltpu.CompilerParams(has_side_effects=EFFECT),
    )(*ops)
    return (gather, n, outs[:4 * n]), outs[4 * n]


def _xchg_wait(state, after, name):
    gather, n, st = state
    send, recv, xs, lands = st[:n], st[n:2 * n], st[2 * n:3 * n], st[3 * n:4 * n]

    def body(*refs):
        xr, lr = refs[:n], refs[n:2 * n]
        sr, rr = refs[2 * n:3 * n], refs[3 * n:4 * n]
        for cp in _xchg_copies(gather, xr, lr, sr, rr):
            cp.wait_send()
            cp.wait_recv()

    outs = pl.pallas_call(
        body, name=name,
        out_shape=[pltpu.HBM(a.shape, a.dtype) for a in list(xs) + list(lands)],
        in_specs=[HBM_SPEC] * (2 * n) + [SEM_SPEC] * (2 * n) + [pl.BlockSpec(memory_space=pl.ANY)],
        out_specs=[HBM_SPEC] * (2 * n), input_output_aliases={i: i for i in range(2 * n)},
        compiler_params=pltpu.CompilerParams(has_side_effects=EFFECT),
    )(*xs, *lands, *send, *recv, after)
    return outs[n:]


def _tie(a, token):
    return a + token[0, 0].astype(a.dtype)


def _cast_bf16(a, name):
    R, C = a.shape
    tr = _tile(R, 512, 16)

    def body(a_ref, o_ref):
        o_ref[...] = a_ref[...].astype(BF)

    return pl.pallas_call(body, name=name, grid=(R // tr,), in_specs=[pl.BlockSpec((tr, C), lambda i: (i, 0))],
                          out_specs=pl.BlockSpec((tr, C), lambda i: (i, 0)), out_shape=S_((R, C), BF), compiler_params=_cp())(a)


def _sum_n(a, name):
    n, R, C = a.shape
    tr = _tile(R, 256, 8)

    def body(a_ref, o_ref):
        acc = a_ref[0].astype(F32)
        for i in range(1, n):
            acc = acc + a_ref[i].astype(F32)
        o_ref[...] = acc

    return pl.pallas_call(body, name=name, grid=(R // tr,), in_specs=[pl.BlockSpec((n, tr, C), lambda i: (0, i, 0))],
                          out_specs=pl.BlockSpec((tr, C), lambda i: (i, 0)), out_shape=S_((R, C), F32), compiler_params=_cp())(a)


def _adam_math(w, g, m, v):
    m = ADAM_B1 * m + (1.0 - ADAM_B1) * g
    v = ADAM_B2 * v + (1.0 - ADAM_B2) * jnp.square(g)
    m_hat = m / (1.0 - ADAM_B1 ** ADAM_STEP)
    v_hat = v / (1.0 - ADAM_B2 ** ADAM_STEP)
    delta = -ADAM_LR * (m_hat / (jnp.sqrt(v_hat) + ADAM_EPS) + ADAM_WD * w)
    return delta, m, v


def _reduce_adam(pieces, w, m, v, name, own=None):
    n, R, C = pieces.shape
    tr = _tile(R, 256, 8)

    def body(*refs):
        p_ref, w_ref, m_ref, v_ref = refs[:4]
        g_ref, d_ref, mo_ref, vo_ref = refs[-4:]
        g = p_ref[0].astype(F32)
        for i in range(1, n):
            g = g + p_ref[i].astype(F32)
        if own is not None:
            g = g + refs[4][...].astype(F32)
        d, mn, vn = _adam_math(w_ref[...], g, m_ref[...], v_ref[...])
        g_ref[...] = g
        d_ref[...] = d
        mo_ref[...] = mn
        vo_ref[...] = vn

    row = pl.BlockSpec((tr, C), lambda i: (i, 0))
    ins = [pieces, w, m, v] + ([] if own is None else [own])
    return pl.pallas_call(body, name=name, grid=(R // tr,),
                          in_specs=[pl.BlockSpec((n, tr, C), lambda i: (0, i, 0))] + [row] * (len(ins) - 1),
                          out_specs=[row] * 4, out_shape=[S_((R, C), F32)] * 4, compiler_params=_cp())(*ins)


def _mm_cols(a, wb, name, out_dtype=F32, dep=None):
    M, K = a.shape
    NB, _, nb = wb.shape
    tm = _tile(M, 768, 16)

    def body(*refs):
        refs[-1][...] = _dot(refs[0][...], refs[1][...], NN).astype(out_dtype)

    deps = [] if dep is None else [dep]
    return pl.pallas_call(body, name=name, grid=(M // tm, NB),
                          in_specs=[pl.BlockSpec((tm, K), lambda i, j: (i, 0)), pl.BlockSpec((None, K, nb), lambda i, j: (j, 0, 0))]
                          + [_full(d.shape) for d in deps],
                          out_specs=pl.BlockSpec((tm, nb), lambda i, j: (i, j)), out_shape=S_((M, NB * nb), out_dtype),
                          compiler_params=_cp())(a, wb, *deps)


def _mm_nt_cols(g, wb, name):
    M = g.shape[0]
    NB, K, nb = wb.shape
    tm = _tile(M, 768, 16)

    def body(g_ref, w_ref, o_ref):
        j = pl.program_id(1)
        part = _dot(g_ref[...], w_ref[...], NT)

        @pl.when(j == 0)
        def _():
            o_ref[...] = part

        @pl.when(j > 0)
        def _():
            o_ref[...] += part

    return pl.pallas_call(body, name=name, grid=(M // tm, NB),
                          in_specs=[pl.BlockSpec((tm, nb), lambda i, j: (i, j)), pl.BlockSpec((None, K, nb), lambda i, j: (j, 0, 0))],
                          out_specs=pl.BlockSpec((tm, K), lambda i, j: (i, 0)), out_shape=S_((M, K), F32),
                          compiler_params=_cp())(g, wb)


def _mm_full(a, w, dn, name, out_dtype=F32):
    M, K = a.shape
    N = w.shape[1] if dn == NN else w.shape[0]
    tm = _tile(M, 768, 16)

    def body(a_ref, w_ref, o_ref):
        o_ref[...] = _dot(a_ref[...], w_ref[...], dn).astype(out_dtype)

    return pl.pallas_call(body, name=name, grid=(M // tm,),
                          in_specs=[pl.BlockSpec((tm, K), lambda i: (i, 0)), _full(w.shape)],
                          out_specs=pl.BlockSpec((tm, N), lambda i: (i, 0)), out_shape=S_((M, N), out_dtype),
                          compiler_params=_cp())(a, w)


def _mm_tn_cols(a, g, NB, name):
    M, K = a.shape
    nb = g.shape[1] // NB
    tk = _tile(M, 768, 16)
    nk = M // tk

    def body(a_ref, g_ref, o_ref, acc):
        k = pl.program_id(1)
        part = _dot(a_ref[...], g_ref[...], TN)

        @pl.when(k == 0)
        def _():
            acc[...] = part

        @pl.when(k > 0)
        def _():
            acc[...] += part

        @pl.when(k == nk - 1)
        def _():
            o_ref[...] = acc[...].astype(BF)

    return pl.pallas_call(body, name=name, grid=(NB, nk),
                          in_specs=[pl.BlockSpec((tk, K), lambda b, k: (k, 0)), pl.BlockSpec((tk, nb), lambda b, k: (k, b))],
                          out_specs=pl.BlockSpec((None, K, nb), lambda b, k: (b, 0, 0)), out_shape=S_((NB, K, nb), BF),
                          scratch_shapes=[pltpu.VMEM((K, nb), F32)], compiler_params=_cp())(a, g)


def _mm_tn_rows(a, g, NB, name):
    M = a.shape[0]
    kb = a.shape[1] // NB
    N = g.shape[1]
    tk = _tile(M, 768, 16)
    nk = M // tk

    def body(a_ref, g_ref, o_ref, acc):
        k = pl.program_id(1)
        part = _dot(a_ref[...], g_ref[...], TN)

        @pl.when(k == 0)
        def _():
            acc[...] = part

        @pl.when(k > 0)
        def _():
            acc[...] += part

        @pl.when(k == nk - 1)
        def _():
            o_ref[...] = acc[...].astype(BF)

    return pl.pallas_call(body, name=name, grid=(NB, nk),
                          in_specs=[pl.BlockSpec((tk, kb), lambda b, k: (k, b)), pl.BlockSpec((tk, N), lambda b, k: (k, 0))],
                          out_specs=pl.BlockSpec((None, kb, N), lambda b, k: (b, 0, 0)), out_shape=S_((NB, kb, N), BF),
                          scratch_shapes=[pltpu.VMEM((kb, N), F32)], compiler_params=_cp())(a, g)


def _ffn_fwd(h, wg, wu, wd, name, dep=None):
    T, D = h.shape
    NB, _, nb = wg.shape
    tm = _tile(T, FFN_FWD_ROWS, 16)

    def body(*refs):
        h_ref, wg_ref, wu_ref, wd_ref = refs[:4]
        y_ref, g_ref, u_ref = refs[-3:]
        b = pl.program_id(1)
        hh = h_ref[...]
        g = _dot(hh, wg_ref[...], NN).astype(BF)
        u = _dot(hh, wu_ref[...], NN).astype(BF)
        g_ref[...] = g
        u_ref[...] = u
        gf = g.astype(F32)
        part = _dot(gf * jax.nn.sigmoid(gf) * u.astype(F32), wd_ref[...], NN)

        @pl.when(b == 0)
        def _():
            y_ref[...] = part

        @pl.when(b > 0)
        def _():
            y_ref[...] += part

    deps = [] if dep is None else [dep]
    wcol = pl.BlockSpec((None, D, nb), lambda i, b: (b, 0, 0))
    act = pl.BlockSpec((None, tm, nb), lambda i, b: (b, i, 0))
    return pl.pallas_call(
        body, name=name, grid=(T // tm, NB),
        in_specs=[pl.BlockSpec((tm, D), lambda i, b: (i, 0)), wcol, wcol, pl.BlockSpec((None, nb, D), lambda i, b: (b, 0, 0))]
        + [_full(d.shape) for d in deps],
        out_specs=[pl.BlockSpec((tm, D), lambda i, b: (i, 0)), act, act],
        out_shape=[S_((T, D), F32), S_((NB, T, nb), BF), S_((NB, T, nb), BF)], compiler_params=_cp())(h, wg, wu, wd, *deps)


def _ffn_bwd_act(dy, G, U, wg, wu, wd, name):
    T, D = dy.shape
    NB, _, nb = wg.shape
    tm = _tile(T, FFN_BWD_ROWS, 16)

    def body(dy_ref, g_ref, u_ref, wg_ref, wu_ref, wd_ref, dh_ref, dg_ref, du_ref, a_ref):
        b = pl.program_id(1)
        g, u = g_ref[...].astype(F32), u_ref[...].astype(F32)
        da = _dot(dy_ref[...], wd_ref[...], NT)
        s = jax.nn.sigmoid(g)
        silu = g * s
        du = da * silu
        dg = da * u * (s * (1.0 + g * (1.0 - s)))
        dg_ref[...] = dg.astype(BF)
        du_ref[...] = du.astype(BF)
        a_ref[...] = (silu * u).astype(BF)
        part = _dot(dg, wg_ref[...], NT) + _dot(du, wu_ref[...], NT)

        @pl.when(b == 0)
        def _():
            dh_ref[...] = part

        @pl.when(b > 0)
        def _():
            dh_ref[...] += part

    wcol = pl.BlockSpec((None, D, nb), lambda i, b: (b, 0, 0))
    act = pl.BlockSpec((None, tm, nb), lambda i, b: (b, i, 0))
    row = pl.BlockSpec((tm, D), lambda i, b: (i, 0))
    return pl.pallas_call(
        body, name=name, grid=(T // tm, NB),
        in_specs=[row, act, act, wcol, wcol, pl.BlockSpec((None, nb, D), lambda i, b: (b, 0, 0))],
        out_specs=[row, act, act, act],
        out_shape=[S_((T, D), F32)] + [S_((NB, T, nb), BF)] * 3, compiler_params=_cp())(dy, G, U, wg, wu, wd)


def _ffn_wgrad_in(h, dact, name):
    T, D = h.shape
    NB, _, nb = dact.shape
    tk = _tile(T, 768, 16)
    nk = T // tk

    def body(h_ref, d_ref, o_ref, acc):
        k = pl.program_id(1)
        part = _dot(h_ref[...], d_ref[...], TN)

        @pl.when(k == 0)
        def _():
            acc[...] = part

        @pl.when(k > 0)
        def _():
            acc[...] += part

        @pl.when(k == nk - 1)
        def _():
            o_ref[...] = acc[...].astype(BF)

    return pl.pallas_call(body, name=name, grid=(NB, nk),
                          in_specs=[pl.BlockSpec((tk, D), lambda b, k: (k, 0)), pl.BlockSpec((None, tk, nb), lambda b, k: (b, k, 0))],
                          out_specs=pl.BlockSpec((None, D, nb), lambda b, k: (b, 0, 0)), out_shape=S_((NB, D, nb), BF),
                          scratch_shapes=[pltpu.VMEM((D, nb), F32)], compiler_params=_cp())(h, dact)


def _ffn_wgrad_out(act, dy, name):
    NB, T, nb = act.shape
    D = dy.shape[1]
    tk = _tile(T, 768, 16)
    nk = T // tk

    def body(a_ref, d_ref, o_ref, acc):
        k = pl.program_id(1)
        part = _dot(a_ref[...], d_ref[...], TN)

        @pl.when(k == 0)
        def _():
            acc[...] = part

        @pl.when(k > 0)
        def _():
            acc[...] += part

        @pl.when(k == nk - 1)
        def _():
            o_ref[...] = acc[...].astype(BF)

    return pl.pallas_call(body, name=name, grid=(NB, nk),
                          in_specs=[pl.BlockSpec((None, tk, nb), lambda b, k: (b, k, 0)), pl.BlockSpec((tk, D), lambda b, k: (k, 0))],
                          out_specs=pl.BlockSpec((None, nb, D), lambda b, k: (b, 0, 0)), out_shape=S_((NB, nb, D), BF),
                          scratch_shapes=[pltpu.VMEM((nb, D), F32)], compiler_params=_cp())(act, dy)


class _Geo:
    def __init__(self, S, LC, D):
        self.S, self.LC, self.D, self.T = S, LC, D, S + LC
        self.RT = _tile(LC, ROW_TILE, 8)
        assert S % self.RT == 0 and self.RT >= 2 * HALO
        self.nlat, self.nctx = S // self.RT, LC // self.RT
        self.nt = self.nlat + self.nctx

    def row(self, C, cb=0):
        return pl.BlockSpec((self.RT, C), lambda i: (i, cb))

    def prev(self, C, cb=0):
        return pl.BlockSpec((self.RT, C), lambda i: (jnp.maximum(i - 1, 0), cb))

    def next(self, C, cb=0):
        nt = self.nt
        return pl.BlockSpec((self.RT, C), lambda i: (jnp.minimum(i + 1, nt - 1), cb))

    def seg(self, r, C):
        nlat = self.nlat
        return pl.BlockSpec((None, r, C), lambda i: (jnp.minimum(i // nlat, 1), 0, 0))

    def first_of_seg(self, i):
        return jnp.logical_or(i == 0, i == self.nlat)

    def prev_ok(self, i):
        return jnp.logical_and(i != 0, i != self.nlat)

    def next_ok(self, i):
        return jnp.logical_and(i != self.nlat - 1, i != self.nt - 1)

    def pos(self, i):
        r = lax.broadcasted_iota(jnp.int32, (self.RT, 1), 0)
        is_ctx = i >= self.nlat
        base = jnp.where(is_ctx, (i - self.nlat) * self.RT, i * self.RT)
        return base + r, jnp.where(is_ctx, self.LC, self.S)


def _rms(x):
    return x * lax.rsqrt(jnp.mean(x * x, axis=-1, keepdims=True) + EPS)


def _modulate(x, g, shift, scale):
    return (_rms(x) * g) * (1 + scale) + shift


def _post(x, y, g, gate, w):
    return x + w * gate * (_rms(y) * g)


def _norm_fwd(geo, x, post, pre, name):
    D = geo.D
    ins, specs = [x], [geo.row(D)]
    if post is not None:
        ins += [post[0], post[1], post[2]]
        specs += [geo.row(D), _full((1, D)), geo.seg(9, D)]
    if pre is not None:
        ins += [pre[0], pre[1]]
        specs += [_full((1, D)), geo.seg(9, D)]

    def body(*refs):
        it = iter(refs)
        xv = next(it)[...]
        if post is not None:
            y_ref, gp_ref, tab_ref = next(it), next(it), next(it)
        if pre is not None:
            gq_ref, tabn_ref = next(it), next(it)
        if post is not None:
            r = 3 * post[3] + 2
            xv = _post(xv, y_ref[...], gp_ref[...], tab_ref[r:r + 1, :], post[4])
            next(it)[...] = xv
        if pre is not None:
            r = 3 * pre[2]
            next(it)[...] = _modulate(xv, gq_ref[...], tabn_ref[r:r + 1, :], tabn_ref[r + 1:r + 2, :]).astype(BF)

    outs, ospecs = [], []
    if post is not None:
        outs.append(S_((geo.T, D), F32))
        ospecs.append(geo.row(D))
    if pre is not None:
        outs.append(S_((geo.T, D), BF))
        ospecs.append(geo.row(D))
    return pl.pallas_call(body, name=name, grid=(geo.nt,), in_specs=specs, out_specs=ospecs, out_shape=outs,
                          compiler_params=_cp())(*ins)


def _acc_rows(ref, first, rows):
    for k, v in enumerate(rows):
        @pl.when(first)
        def _(k=k, v=v):
            ref[k:k + 1, :] = v

        @pl.when(jnp.logical_not(first))
        def _(k=k, v=v):
            ref[k:k + 1, :] += v


def _norm_bwd(geo, dxo, dh, x, pre, post, name):
    D = geo.D
    ins = [dxo, dh, x, pre[0], pre[1]]
    specs = [geo.row(D)] * 3 + [_full((1, D)), geo.seg(9, D)]
    if post is not None:
        ins += [post[0], post[1], post[2]]
        specs += [geo.row(D), _full((1, D)), geo.seg(9, D)]

    def body(*refs):
        i = pl.program_id(0)
        first = geo.first_of_seg(i)
        dxo_ref, dh_ref, x_ref, gq_ref, tab_ref = refs[:5]
        k = 5
        if post is not None:
            y_ref, gp_ref, tabp_ref = refs[5:8]
            k = 8
        outs = refs[k:]
        r = 3 * pre[2]
        _, vjp = jax.vjp(_modulate, x_ref[...], gq_ref[...], tab_ref[r:r + 1, :], tab_ref[r + 1:r + 2, :])
        dx, dg, dsh, dsc = vjp(dh_ref[...].astype(F32))
        dx = dx + dxo_ref[...]
        outs[0][...] = dx
        if post is None:
            _acc_rows(outs[1], first, [dg, dsh, dsc])
            return
        _acc_rows(outs[2], first, [dg, dsh, dsc])
        rp = 3 * post[3] + 2
        w = post[4]
        _, vjp2 = jax.vjp(lambda yy, gg, ga: w * ga * (_rms(yy) * gg), y_ref[...], gp_ref[...], tabp_ref[rp:rp + 1, :])
        dy, dgp, dga = vjp2(dx)
        outs[1][...] = dy
        _acc_rows(outs[3], first, [dgp, dga])

    if post is None:
        outs, ospecs = [S_((geo.T, D), F32), S_((2, 3, D), F32)], [geo.row(D), geo.seg(3, D)]
    else:
        outs = [S_((geo.T, D), F32), S_((geo.T, D), F32), S_((2, 3, D), F32), S_((2, 2, D), F32)]
        ospecs = [geo.row(D), geo.row(D), geo.seg(3, D), geo.seg(2, D)]
    return pl.pallas_call(body, name=name, grid=(geo.nt,), in_specs=specs, out_specs=ospecs, out_shape=outs,
                          compiler_params=_cp())(*ins)


def _loss_bwd(geo, xf, tgt, post, name):
    D = geo.D
    nlat = geo.nlat

    def body(x_ref, t_ref, y_ref, gp_ref, tabp_ref, loss_ref, dx_ref, dy_ref, dpost_ref):
        i = pl.program_id(0)
        first = geo.first_of_seg(i)
        lat = i < nlat
        diff = x_ref[...] - t_ref[...]
        part = jnp.where(lat, 0.5 * jnp.sum(jnp.mean(diff * diff, axis=-1, keepdims=True), axis=0, keepdims=True), 0.0)

        @pl.when(i == 0)
        def _():
            loss_ref[...] = part

        @pl.when(i > 0)
        def _():
            loss_ref[...] += part

        dx = jnp.where(lat, diff * (1.0 / D), 0.0)
        dx_ref[...] = dx
        rp = 3 * post[3] + 2
        w = post[4]
        _, vjp2 = jax.vjp(lambda yy, gg, ga: w * ga * (_rms(yy) * gg), y_ref[...], gp_ref[...], tabp_ref[rp:rp + 1, :])
        dy, dgp, dga = vjp2(dx)
        dy_ref[...] = dy
        _acc_rows(dpost_ref, first, [dgp, dga])

    tspec = pl.BlockSpec((geo.RT, D), lambda i: (jnp.minimum(i, nlat - 1), 0))
    return pl.pallas_call(
        body, name=name, grid=(geo.nt,),
        in_specs=[geo.row(D), tspec, geo.row(D), _full((1, D)), geo.seg(9, D)],
        out_specs=[_full((1, 1)), geo.row(D), geo.row(D), geo.seg(2, D)],
        out_shape=[S_((1, 1), F32), S_((geo.T, D), F32), S_((geo.T, D), F32), S_((2, 2, D), F32)],
        compiler_params=_cp())(xf, tgt, post[0], post[1], post[2])


def _fill_ext(geo, ext, prev_ref, cur_ref, next_ref, i):
    RT = geo.RT
    ext[0:HALO, :] = jnp.where(geo.prev_ok(i), prev_ref[RT - HALO:RT, :], 0.0).astype(ext.dtype)
    ext[HALO:HALO + RT, :] = cur_ref[...].astype(ext.dtype)
    ext[HALO + RT:2 * HALO + RT, :] = jnp.where(geo.next_ok(i), next_ref[0:HALO, :], 0.0).astype(ext.dtype)


CONV_W = 4
CONV_LEFT = 2


def _lru_gates(u, za, zx, lam):
    r = jax.nn.sigmoid(za)
    i = jax.nn.sigmoid(zx)
    log_a = -LRU_C * r * jax.nn.softplus(-lam)
    a = jnp.exp(log_a)
    return a, jnp.sqrt(1.0 - jnp.exp(2.0 * log_a)) * (i * u)


def _lru_coef_fwd(geo, P, W, cw, cb, wa, ba, wx, bx, lam, name):
    RT = geo.RT
    nblk, LB = wa.shape[1], wa.shape[2]

    def body(pp, pc, pn, cw_ref, cb_ref, wa_ref, ba_ref, wx_ref, bx_ref, lam_ref, u_ref, a0, b0, a1, b1, ext):
        i = pl.program_id(0)
        _fill_ext(geo, ext, pp, pc, pn, i)
        u = cb_ref[...] + ext[pl.ds(HALO - CONV_LEFT, RT), :] * cw_ref[0:1, :]
        for k in range(1, CONV_W):
            u = u + ext[pl.ds(HALO - CONV_LEFT + k, RT), :] * cw_ref[k:k + 1, :]
        u_ref[...] = u
        for d, (a_ref, b_ref) in enumerate(((a0, b0), (a1, b1))):
            for n in range(nblk):
                sl = slice(n * LB, (n + 1) * LB)
                un = u[:, sl]
                za = _dot(un, wa_ref[d, n], NN) + ba_ref[d:d + 1, sl]
                zx = _dot(un, wx_ref[d, n], NN) + bx_ref[d:d + 1, sl]
                a, b = _lru_gates(un, za, zx, lam_ref[d:d + 1, sl])
                a_ref[:, sl] = a
                b_ref[:, sl] = b

    return pl.pallas_call(
        body, name=name, grid=(geo.nt,),
        in_specs=[geo.prev(W, 1), geo.row(W, 1), geo.next(W, 1), _full(cw.shape), _full(cb.shape), _full(wa.shape),
                  _full(ba.shape), _full(wx.shape), _full(bx.shape), _full(lam.shape)],
        out_specs=[geo.row(W)] * 5, out_shape=[S_((geo.T, W), F32)] * 5,
        scratch_shapes=[pltpu.VMEM((RT + 2 * HALO, W), F32)], compiler_params=_cp())(P, P, P, cw, cb, wa, ba, wx, bx, lam)


def _lru_coef_bwd(geo, u, da, db, W, wa, ba, wx, bx, lam, name):
    nblk, LB = wa.shape[1], wa.shape[2]

    def body(u_ref, da0, db0, da1, db1, wa_ref, ba_ref, wx_ref, bx_ref, lam_ref, du_ref, dwa, dba, dwx, dbx, dlam):
        i = pl.program_id(0)

        @pl.when(i == 0)
        def _():
            for r in (dwa, dba, dwx, dbx, dlam):
                r[...] = jnp.zeros(r.shape, F32)

        u = u_ref[...]
        for n in range(nblk):
            sl = slice(n * LB, (n + 1) * LB)
            un = u[:, sl]
            dun = jnp.zeros_like(un)
            for d, (da_ref, db_ref) in enumerate(((da0, db0), (da1, db1))):
                za = _dot(un, wa_ref[d, n], NN) + ba_ref[d:d + 1, sl]
                zx = _dot(un, wx_ref[d, n], NN) + bx_ref[d:d + 1, sl]
                _, vjp = jax.vjp(_lru_gates, un, za, zx, lam_ref[d:d + 1, sl])
                du_e, dza, dzx, dl = vjp((da_ref[:, sl], db_ref[:, sl]))
                dun = dun + du_e + _dot(dza, wa_ref[d, n], NT) + _dot(dzx, wx_ref[d, n], NT)
                dwa[d, n] += _dot(un, dza, TN)
                dwx[d, n] += _dot(un, dzx, TN)
                dba[d:d + 1, sl] += jnp.sum(dza, axis=0, keepdims=True)
                dbx[d:d + 1, sl] += jnp.sum(dzx, axis=0, keepdims=True)
                dlam[d:d + 1, sl] += dl
            du_ref[:, sl] = dun

    row = geo.row(W)
    return pl.pallas_call(
        body, name=name, grid=(geo.nt,),
        in_specs=[row] * 5 + [_full(wa.shape), _full(ba.shape), _full(wx.shape), _full(bx.shape), _full(lam.shape)],
        out_specs=[row, _full(wa.shape), _full(ba.shape), _full(wx.shape), _full(bx.shape), _full(lam.shape)],
        out_shape=[S_((geo.T, W), F32), S_(wa.shape, F32), S_(ba.shape, F32), S_(wx.shape, F32), S_(bx.shape, F32), S_(lam.shape, F32)],
        compiler_params=_cp())(u, da[0], db[0], da[1], db[1], wa, ba, wx, bx, lam)


def _scan_order(d, k, nlat, nctx):
    if d == 0:
        return jnp.where(k < nctx, nlat + k, k - nctx)
    return jnp.where(k < nctx, nlat + nctx - 1 - k, nlat - 1 - (k - nctx))


def _chunk_scan(a, b, reverse):
    n = a.shape[0]
    row = lax.broadcasted_iota(jnp.int32, a.shape, 0)
    s = 1
    while s < n:
        if reverse:
            ok = row < n - s
            a_s, b_s = pltpu.roll(a, n - s, 0), pltpu.roll(b, n - s, 0)
        else:
            ok = row >= s
            a_s, b_s = pltpu.roll(a, s, 0), pltpu.roll(b, s, 0)
        b = a * jnp.where(ok, b_s, 0.0) + b
        a = a * jnp.where(ok, a_s, 1.0)
        s *= 2
    return a, b


def _shift1(x, reverse, fill):
    n = x.shape[0]
    row = lax.broadcasted_iota(jnp.int32, x.shape, 0)
    if reverse:
        return jnp.where(row == n - 1, fill, pltpu.roll(x, n - 1, 0))
    return jnp.where(row == 0, fill, pltpu.roll(x, 1, 0))


def _lru_scan_fwd(geo, a, b, d, name):
    W = a.shape[1]
    tw = _tile(W, 256, 128)
    RT, nlat, nctx = geo.RT, geo.nlat, geo.nctx
    rev = d == 1

    def body(a_ref, b_ref, h_ref, hp_ref, carry):
        k = pl.program_id(1)

        @pl.when(k == 0)
        def _():
            carry[...] = jnp.zeros(carry.shape, F32)

        ac, bc = _chunk_scan(a_ref[...], b_ref[...], rev)
        h = bc + ac * carry[...]
        h_ref[...] = h
        hp_ref[...] = _shift1(h, rev, carry[...])
        carry[...] = h[0:1, :] if rev else h[RT - 1:RT, :]

    spec = pl.BlockSpec((RT, tw), lambda j, k: (_scan_order(d, k, nlat, nctx), j))
    return pl.pallas_call(body, name=name, grid=(W // tw, geo.nt), in_specs=[spec, spec], out_specs=[spec, spec],
                          out_shape=[S_((geo.T, W), F32)] * 2, scratch_shapes=[pltpu.VMEM((1, tw), F32)],
                          compiler_params=_cp())(a, b)


def _lru_scan_bwd(geo, dh, a, hprev, d, name):
    W = a.shape[1]
    tw = _tile(W, 256, 128)
    RT, nlat, nctx, nt = geo.RT, geo.nlat, geo.nctx, geo.nt
    rev = d == 1

    def body(dh_ref, a_ref, hp_ref, da_ref, db_ref, carry):
        k = pl.program_id(1)

        @pl.when(k == 0)
        def _():
            carry[...] = jnp.zeros(carry.shape, F32)

        av = a_ref[...]
        a_next = _shift1(av, not rev, jnp.ones((1, tw), F32))
        ac, bc = _chunk_scan(a_next, dh_ref[...], not rev)
        lam = bc + ac * carry[...]
        db_ref[...] = lam
        da_ref[...] = lam * hp_ref[...]
        first = (av * lam)[RT - 1:RT, :] if rev else (av * lam)[0:1, :]
        carry[...] = first

    spec = pl.BlockSpec((RT, tw), lambda j, k: (_scan_order(d, nt - 1 - k, nlat, nctx), j))
    return pl.pallas_call(body, name=name, grid=(W // tw, nt), in_specs=[spec] * 3, out_specs=[spec] * 2,
                          out_shape=[S_((geo.T, W), F32)] * 2, scratch_shapes=[pltpu.VMEM((1, tw), F32)],
                          compiler_params=_cp())(dh, a, hprev)


def _conv_bwd_assemble(geo, P, du, pieces, W, cw, name):
    RT = geo.RT

    def body(pp, pc, pn, dup, duc, dun, dg, dq, dk, dv, dog, cw_ref, dP_ref, dcw, dcb, ext_r, ext_d):
        i = pl.program_id(0)
        _fill_ext(geo, ext_r, pp, pc, pn, i)
        _fill_ext(geo, ext_d, dup, duc, dun, i)
        du_c = duc[...]
        rows = []
        dr = None
        for k in range(CONV_W):
            rows.append(jnp.sum(du_c * ext_r[pl.ds(HALO - CONV_LEFT + k, RT), :], axis=0, keepdims=True))
            t = ext_d[pl.ds(HALO + CONV_LEFT - k, RT), :] * cw_ref[k:k + 1, :]
            dr = t if dr is None else dr + t
        first = i == 0
        _acc_rows(dcw, first, rows)
        _acc_rows(dcb, first, [jnp.sum(du_c, axis=0, keepdims=True)])
        for j, v in enumerate((dg[...], dr, dq[...], dk[...], dv[...], dog[...])):
            dP_ref[:, j * W:(j + 1) * W] = v

    row = geo.row(W)
    return pl.pallas_call(
        body, name=name, grid=(geo.nt,),
        in_specs=[geo.prev(W, 1), geo.row(W, 1), geo.next(W, 1), geo.prev(W), row, geo.next(W)] + [row] * 5 + [_full(cw.shape)],
        out_specs=[geo.row(6 * W), _full((CONV_W, W)), _full((1, W))],
        out_shape=[S_((geo.T, 6 * W), F32), S_((CONV_W, W), F32), S_((1, W), F32)],
        scratch_shapes=[pltpu.VMEM((RT + 2 * HALO, W), F32)] * 2, compiler_params=_cp())(P, P, P, du, du, du, *pieces, cw)


def _rot_half(x, cos, sin):
    hw = x.shape[1] // 2
    x1, x2 = x[:, :hw], x[:, hw:]
    return jnp.concatenate([x1 * cos - x2 * sin, x1 * sin + x2 * cos], axis=-1)


def _ret_chunk(q, k, v, s, logit, cos, sin, rev):
    C, dk = q.shape
    lg = -jax.nn.softplus(-logit)
    q = _rot_half(q, cos, sin)
    k = _rot_half(k, cos, sin) * (dk ** -0.5)
    i = lax.broadcasted_iota(jnp.int32, (C, 1), 0).astype(F32)
    j = lax.broadcasted_iota(jnp.int32, (1, C), 1).astype(F32)
    if rev:
        diff, qe, ke = j - i, C - i, i
    else:
        diff, qe, ke = i - j, i + 1.0, C - 1.0 - i
    intra = jnp.where(diff >= 0, jnp.exp(lg * jnp.maximum(diff, 0.0)), 0.0)
    scores = _bdot(q, k, "nt") * intra
    o = _bdot(scores, v, "nn") + _bdot(q * jnp.exp(lg * qe), s, "nn")
    s_new = s * jnp.exp(lg * C) + _bdot(k * jnp.exp(lg * ke), v, "tn")
    return o, s_new


def _ret_specs(P, W, H, d, nlc, ncc, order_of_step):
    C = RET_CHUNK
    dk = W // H

    def cidx(k):
        return _scan_order(d, order_of_step(k), nlc, ncc)

    per_w = W // dk
    q = pl.BlockSpec((C, dk), lambda h, k: (cidx(k), 2 * per_w + h))
    kk = pl.BlockSpec((C, dk), lambda h, k: (cidx(k), 3 * per_w + h))
    v = pl.BlockSpec((C, dk), lambda h, k: (cidx(k), 4 * per_w + h))
    tab = pl.BlockSpec((C, dk // 2), lambda h, k: (cidx(k), 0))
    logit = pl.BlockSpec((None, 1, 1), lambda h, k: (h, 0, 0))
    o = pl.BlockSpec((C, dk), lambda h, k: (cidx(k), h))
    return q, kk, v, tab, logit, o


def _ret_fwd(geo, P, W, H, logit_d, cos, sin, d, name):
    C = RET_CHUNK
    dk = W // H
    nlc, ncc = geo.S // C, geo.LC // C
    nch = nlc + ncc
    rev = d == 1

    def body(q_ref, k_ref, v_ref, cos_ref, sin_ref, lg_ref, o_ref, st_ref, s_scr):
        k = pl.program_id(1)

        @pl.when(k == 0)
        def _():
            s_scr[...] = jnp.zeros(s_scr.shape, F32)

        st_ref[...] = s_scr[...]
        o, s_new = _ret_chunk(q_ref[...], k_ref[...], v_ref[...], s_scr[...], lg_ref[...], cos_ref[...], sin_ref[...], rev)
        o_ref[...] = o
        s_scr[...] = s_new

    q, kk, v, tab, logit, o = _ret_specs(P, W, H, d, nlc, ncc, lambda k: k)
    st = pl.BlockSpec((None, None, dk, dk), lambda h, k: (h, k, 0, 0))
    return pl.pallas_call(body, name=name, grid=(H, nch), in_specs=[q, kk, v, tab, tab, logit], out_specs=[o, st],
                          out_shape=[S_((geo.T, W), F32), S_((H, nch, dk, dk), F32)],
                          scratch_shapes=[pltpu.VMEM((dk, dk), F32)], compiler_params=_cp())(P, P, P, cos, sin, logit_d)


def _ret_bwd(geo, P, W, H, logit_d, cos, sin, states, do, prev, d, name):
    C = RET_CHUNK
    dk = W // H
    nlc, ncc = geo.S // C, geo.LC // C
    nch = nlc + ncc
    rev = d == 1
    np_ = 0 if prev is None else 3

    def body(*refs):
        q_ref, k_ref, v_ref, cos_ref, sin_ref, lg_ref, st_ref, do_ref = refs[:8]
        prevs = refs[8:8 + np_]
        dq_ref, dk_ref, dv_ref, dlg_ref, ds_scr = refs[8 + np_:]
        k = pl.program_id(1)

        @pl.when(k == 0)
        def _():
            ds_scr[...] = jnp.zeros(ds_scr.shape, F32)
            dlg_ref[...] = jnp.zeros(dlg_ref.shape, F32)

        cos, sin = cos_ref[...], sin_ref[...]
        _, vjp = jax.vjp(lambda a, b, c, s, lg: _ret_chunk(a, b, c, s, lg, cos, sin, rev),
                         q_ref[...], k_ref[...], v_ref[...], st_ref[...], lg_ref[...])
        dq, dkk, dv, ds, dlg = vjp((do_ref[...], ds_scr[...]))
        if prev is not None:
            dq, dkk, dv = dq + prevs[0][...], dkk + prevs[1][...], dv + prevs[2][...]
        dq_ref[...] = dq
        dk_ref[...] = dkk
        dv_ref[...] = dv
        ds_scr[...] = ds
        dlg_ref[...] += dlg

    q, kk, v, tab, logit, o = _ret_specs(P, W, H, d, nlc, ncc, lambda k: nch - 1 - k)
    st = pl.BlockSpec((None, None, dk, dk), lambda h, k: (h, nch - 1 - k, 0, 0))
    ins = [P, P, P, cos, sin, logit_d, states, do] + ([] if prev is None else list(prev))
    return pl.pallas_call(body, name=name, grid=(H, nch), in_specs=[q, kk, v, tab, tab, logit, st, o] + [o] * np_,
                          out_specs=[o, o, o, logit], out_shape=[S_((geo.T, W), F32)] * 3 + [S_((H, 1, 1), F32)],
                          scratch_shapes=[pltpu.VMEM((dk, dk), F32)], compiler_params=_cp())(*ins)


def _mix_even_math(g, h0, h1, o0, o1, og, gn, H):
    lru = jax.nn.gelu(g) * (h0 + h1)
    o = o0 + o1
    dv = o.shape[1] // H
    parts = []
    for h in range(H):
        oh = o[:, h * dv:(h + 1) * dv]
        mu = jnp.mean(oh, axis=-1, keepdims=True)
        var = jnp.mean(jnp.square(oh - mu), axis=-1, keepdims=True)
        parts.append((oh - mu) * lax.rsqrt(var + EPS))
    y = jnp.concatenate(parts, axis=-1) * gn
    return lru, y * jax.nn.silu(og)


def _mix_even_fwd(geo, P, W, H, h0, h1, o0, o1, gn, name):
    def body(g_ref, h0r, h1r, o0r, o1r, og_ref, gn_ref, m_ref):
        lru, ret = _mix_even_math(g_ref[...], h0r[...], h1r[...], o0r[...], o1r[...], og_ref[...], gn_ref[...], H)
        m_ref[:, :W] = lru.astype(BF)
        m_ref[:, W:] = ret.astype(BF)

    row = geo.row(W)
    return pl.pallas_call(body, name=name, grid=(geo.nt,), in_specs=[geo.row(W, 0), row, row, row, row, geo.row(W, 5), _full((1, W))],
                          out_specs=geo.row(2 * W), out_shape=S_((geo.T, 2 * W), BF), compiler_params=_cp())(P, h0, h1, o0, o1, P, gn)


def _mix_even_bwd(geo, P, W, H, h0, h1, o0, o1, gn, dmix, name):
    def body(g_ref, h0r, h1r, o0r, o1r, og_ref, gn_ref, dl_ref, dr_ref, dg_ref, dh_ref, do_ref, dog_ref, dgn_ref):
        i = pl.program_id(0)
        _, vjp = jax.vjp(lambda g, hs, os_, og, gn_: _mix_even_math(g, hs, 0.0, os_, 0.0, og, gn_, H),
                         g_ref[...], h0r[...] + h1r[...], o0r[...] + o1r[...], og_ref[...], gn_ref[...])
        dg, dh, do, dog, dgn = vjp((dl_ref[...], dr_ref[...]))
        dg_ref[...] = dg
        dh_ref[...] = dh
        do_ref[...] = do
        dog_ref[...] = dog
        _acc_rows(dgn_ref, i == 0, [dgn])

    row = geo.row(W)
    return pl.pallas_call(
        body, name=name, grid=(geo.nt,),
        in_specs=[geo.row(W, 0), row, row, row, row, geo.row(W, 5), _full((1, W)), geo.row(W, 0), geo.row(W, 1)],
        out_specs=[row] * 4 + [_full((1, W))], out_shape=[S_((geo.T, W), F32)] * 4 + [S_((1, W), F32)],
        compiler_params=_cp())(P, h0, h1, o0, o1, P, gn, dmix, dmix)


def _head_norm_rot(x, g, cosf, sinf):
    xn = _rms(x) * g
    return xn * cosf + _rollv(xn, x.shape[1] // 2, 1) * sinf


def _qk_prep_fwd(geo, P, dims, qg, kg, cosf, sinf, name):
    PW, QW, KW, hd = dims
    qscale = hd ** -0.5 * LOG2E

    def body(p_ref, qg_ref, kg_ref, c_ref, s_ref, q_ref, k_ref, v_ref):
        c, s = c_ref[...], s_ref[...]
        for h in range(QW // hd):
            qh = _head_norm_rot(p_ref[:, PW + h * hd:PW + (h + 1) * hd], qg_ref[...], c, s)
            q_ref[:, h * hd:(h + 1) * hd] = (qh * qscale).astype(BF)
        for h in range(KW // hd):
            o = PW + QW + h * hd
            k_ref[:, h * hd:(h + 1) * hd] = _head_norm_rot(p_ref[:, o:o + hd], kg_ref[...], c, s).astype(BF)
            v_ref[:, 2 * h * hd:(2 * h + 1) * hd] = p_ref[:, o + KW:o + KW + hd].astype(BF)
            v_ref[:, (2 * h + 1) * hd:(2 * h + 2) * hd] = jnp.ones((geo.RT, hd), BF)

    tot = PW + QW + 2 * KW
    return pl.pallas_call(
        body, name=name, grid=(geo.nt,),
        in_specs=[geo.row(tot), _full((1, hd)), _full((1, hd)), geo.row(hd), geo.row(hd)],
        out_specs=[geo.row(QW), geo.row(KW), geo.row(2 * KW)],
        out_shape=[S_((geo.T, QW), BF), S_((geo.T, KW), BF), S_((geo.T, 2 * KW), BF)], compiler_params=_cp())(P, qg, kg, cosf, sinf)


def _qk_prep_bwd(geo, P, dims, qg, kg, cosf, sinf, dpool, dq, dk, dv, name):
    PW, QW, KW, hd = dims

    def body(p_ref, qg_ref, kg_ref, c_ref, s_ref, dpool_ref, dq_ref, dk_ref, dv_ref, dP_ref, dqg_ref, dkg_ref):
        i = pl.program_id(0)
        c, s = c_ref[...], s_ref[...]
        dP_ref[:, :PW] = dpool_ref[...]
        f = lambda x, g: _head_norm_rot(x, g, c, s)
        dqg = jnp.zeros((1, hd), F32)
        for h in range(QW // hd):
            o = PW + h * hd
            _, vjp = jax.vjp(f, p_ref[:, o:o + hd], qg_ref[...])
            dx, dg = vjp(dq_ref[:, h * hd:(h + 1) * hd])
            dP_ref[:, o:o + hd] = dx
            dqg = dqg + dg
        dkg = jnp.zeros((1, hd), F32)
        for h in range(KW // hd):
            o = PW + QW + h * hd
            _, vjp = jax.vjp(f, p_ref[:, o:o + hd], kg_ref[...])
            dx, dg = vjp(dk_ref[:, h * hd:(h + 1) * hd])
            dP_ref[:, o:o + hd] = dx
            dkg = dkg + dg
        dP_ref[:, PW + QW + KW:] = dv_ref[...]
        _acc_rows(dqg_ref, i == 0, [dqg])
        _acc_rows(dkg_ref, i == 0, [dkg])

    tot = PW + QW + 2 * KW
    return pl.pallas_call(
        body, name=name, grid=(geo.nt,),
        in_specs=[geo.row(tot), _full((1, hd)), _full((1, hd)), geo.row(hd), geo.row(hd), geo.row(PW), geo.row(QW), geo.row(KW), geo.row(KW)],
        out_specs=[geo.row(tot), _full((1, hd)), _full((1, hd))],
        out_shape=[S_((geo.T, tot), F32), S_((1, hd), F32), S_((1, hd), F32)], compiler_params=_cp())(
            P, qg, kg, cosf, sinf, dpool, dq, dk, dv)


def _att_tiles(T):
    return _tile(T, 256, 16), _tile(T, 768, 16)


def _stack_heads(ref, G, hd):
    return jnp.concatenate([ref[:, g * hd:(g + 1) * hd] for g in range(G)], axis=0)


def _att_fwd(q, k, v1, hd, name):
    T, QW = q.shape
    KV = k.shape[1] // hd
    G = QW // hd // KV
    tq, tk = _att_tiles(T)
    nk = T // tk

    def body(q_ref, k_ref, v_ref, o_ref, lse_ref, m_scr, acc):
        ki = pl.program_id(2)

        @pl.when(ki == 0)
        def _():
            m_scr[...] = jnp.full(m_scr.shape, -jnp.inf, F32)
            acc[...] = jnp.zeros(acc.shape, F32)

        s = _dot(_stack_heads(q_ref, G, hd), k_ref[...], NT)
        m_prev = m_scr[...]
        m_new = jnp.maximum(m_prev, jnp.max(s, axis=-1, keepdims=True))
        p = jnp.exp2(s - jnp.tile(m_new, (1, tk // hd)))
        acc[...] = jnp.tile(jnp.exp2(m_prev - m_new), (1, 2)) * acc[...] + _dot(p, v_ref[...], NN)
        m_scr[...] = m_new

        @pl.when(ki == nk - 1)
        def _():
            a = acc[...]
            o = a[:, :hd] / a[:, hd:]
            lse = m_scr[...] + jnp.log2(a[:, hd:])
            for g in range(G):
                o_ref[:, g * hd:(g + 1) * hd] = o[g * tq:(g + 1) * tq].astype(BF)
                lse_ref[g] = lse[g * tq:(g + 1) * tq]

    return pl.pallas_call(
        body, name=name, grid=(KV, T // tq, nk),
        in_specs=[pl.BlockSpec((tq, G * hd), lambda a, i, j: (i, a)), pl.BlockSpec((tk, hd), lambda a, i, j: (j, a)),
                  pl.BlockSpec((tk, 2 * hd), lambda a, i, j: (j, a))],
        out_specs=[pl.BlockSpec((tq, G * hd), lambda a, i, j: (i, a)), pl.BlockSpec((G, tq, hd), lambda a, i, j: (a, i, 0))],
        out_shape=[S_((T, QW), BF), S_((QW // hd, T, hd), F32)],
        scratch_shapes=[pltpu.VMEM((G * tq, hd), F32), pltpu.VMEM((G * tq, 2 * hd), F32)],
        compiler_params=_cp())(q, k, v1)


def _att_delta(geo, o, dmix, PW, hd, name):
    QW = o.shape[1]
    nh = QW // hd

    def body(o_ref, do_ref, d_ref):
        for h in range(nh):
            sl = slice(h * hd, (h + 1) * hd)
            d = jnp.sum(o_ref[:, sl].astype(F32) * do_ref[:, PW + h * hd:PW + (h + 1) * hd], axis=-1, keepdims=True)
            d_ref[h] = jnp.broadcast_to(d, (geo.RT, hd))

    return pl.pallas_call(body, name=name, grid=(geo.nt,), in_specs=[geo.row(QW), geo.row(PW + QW)],
                          out_specs=pl.BlockSpec((nh, geo.RT, hd), lambda i: (0, i, 0)), out_shape=S_((nh, geo.T, hd), F32),
                          compiler_params=_cp())(o, dmix)


def _att_bwd(q, k, v1, dmix, PW, lse, delta, hd, name):
    T, QW = q.shape
    KW = k.shape[1]
    KV = KW // hd
    G = QW // hd // KV
    tq, tk = _att_tiles(T)
    nq, nk = T // tq, T // tk
    scale = hd ** -0.5
    pb = PW // hd

    def body(q_ref, k_ref, v_ref, *rest):
        do_refs, (lse_ref, dl_ref, dq_ref, dk_ref, dv_ref, dq_scr) = rest[:G], rest[G:]
        ki, qi = pl.program_id(1), pl.program_id(2)
        q3 = _stack_heads(q_ref, G, hd)
        do3 = jnp.concatenate([r[...] for r in do_refs], axis=0)
        lse = jnp.concatenate([lse_ref[g] for g in range(G)], axis=0)
        dl = jnp.concatenate([dl_ref[g] for g in range(G)], axis=0)
        kk = k_ref[...]
        p = jnp.exp2(_dot(q3, kk, NT) - jnp.tile(lse, (1, tk // hd)))
        ds = p * (_dot(do3, v_ref[:, :hd], NT) - jnp.tile(dl, (1, tk // hd)))
        pv = _dot(p, do3, TN)
        pk = _dot(ds, q3, TN)
        pq = _dot(ds, kk, NN)

        @pl.when(qi == 0)
        def _():
            dk_ref[...] = pk
            dv_ref[...] = pv

        @pl.when(qi > 0)
        def _():
            dk_ref[...] += pk
            dv_ref[...] += pv

        @pl.when(qi == nq - 1)
        def _():
            dk_ref[...] = dk_ref[...] * LN2

        @pl.when(ki == 0)
        def _():
            dq_scr[qi] = pq

        @pl.when(ki > 0)
        def _():
            dq_scr[qi] += pq

        @pl.when(ki == nk - 1)
        def _():
            full = dq_scr[qi]
            for g in range(G):
                dq_ref[:, g * hd:(g + 1) * hd] = full[g * tq:(g + 1) * tq] * scale

    qs = pl.BlockSpec((tq, G * hd), lambda a, j, i: (i, a))
    ks = pl.BlockSpec((tk, hd), lambda a, j, i: (j, a))
    vs = pl.BlockSpec((tk, 2 * hd), lambda a, j, i: (j, a))
    dos = [pl.BlockSpec((tq, hd), lambda a, j, i, g=g: (i, pb + a * G + g)) for g in range(G)]
    st = pl.BlockSpec((G, tq, hd), lambda a, j, i: (a, i, 0))
    dqs = pl.BlockSpec((tq, G * hd), lambda a, j, i: (jnp.where(j == nk - 1, i, 0), a))
    return pl.pallas_call(body, name=name, grid=(KV, nk, nq), in_specs=[qs, ks, vs] + dos + [st, st], out_specs=[dqs, ks, ks],
                          out_shape=[S_((T, QW), F32), S_((T, KW), F32), S_((T, KW), F32)],
                          scratch_shapes=[pltpu.VMEM((nq, G * tq, hd), F32)], compiler_params=_cp())(
                              q, k, v1, *([dmix] * G), lse, delta)


def _pool_cnt(pos, L, w):
    return (jnp.minimum(pos + w // 2, L) - jnp.maximum(pos - w // 2, 0)).astype(F32)


def _pool_mean(ext, gi, w, G, RT, cnt):
    acc = ext[pl.ds(HALO - w // 2, RT), gi * G:(gi + 1) * G]
    for off in range(-w // 2 + 1, w // 2):
        acc = acc + ext[pl.ds(HALO + off, RT), gi * G:(gi + 1) * G]
    return acc / cnt


def _pool_fwd(geo, P, PW, att, pw, ps, name):
    RT = geo.RT
    G = pw.shape[1]
    QW = att.shape[1]

    def body(pp, pc, pn, att_ref, pw_ref, ps_ref, m_ref, ext):
        i = pl.program_id(0)
        _fill_ext(geo, ext, pp, pc, pn, i)
        pos, L = geo.pos(i)
        for gi, w in enumerate(POOL_WINDOWS):
            sl = slice(gi * G, (gi + 1) * G)
            xm = _pool_mean(ext, gi, w, G, RT, _pool_cnt(pos, L, w)) - pc[:, sl]
            m_ref[:, sl] = (_dot(xm, pw_ref[gi], NN) * ps_ref[:, sl]).astype(BF)
        m_ref[:, PW:] = att_ref[...]

    return pl.pallas_call(
        body, name=name, grid=(geo.nt,),
        in_specs=[geo.prev(PW), geo.row(PW), geo.next(PW), geo.row(QW), _full(pw.shape), _full(ps.shape)],
        out_specs=geo.row(PW + QW), out_shape=S_((geo.T, PW + QW), BF),
        scratch_shapes=[pltpu.VMEM((RT + 2 * HALO, PW), F32)], compiler_params=_cp())(P, P, P, att, pw, ps)


def _pool_bwd(geo, P, PW, dmix, pw, ps, name):
    RT = geo.RT
    G = pw.shape[1]
    RE = RT + 2 * HALO

    def body(pp, pc, pn, dp_, dc, dn, pw_ref, ps_ref, dx_ref, dpw, dps, ext, extd, dmc):
        i = pl.program_id(0)
        _fill_ext(geo, ext, pp, pc, pn, i)
        _fill_ext(geo, extd, dp_, dc, dn, i)
        pos, L = geo.pos(i)
        r = lax.broadcasted_iota(jnp.int32, (RE, 1), 0)
        pos_e = pos[0:1, :] - HALO + r
        rows_s = []
        for gi, w in enumerate(POOL_WINDOWS):
            sl = slice(gi * G, (gi + 1) * G)
            xm = _pool_mean(ext, gi, w, G, RT, _pool_cnt(pos, L, w)) - pc[:, sl]
            pre = _dot(xm, pw_ref[gi], NN)
            dout = dc[:, sl]
            rows_s.append(jnp.sum(dout * pre, axis=0, keepdims=True))
            gw = _dot(xm, dout * ps_ref[:, sl], TN)

            @pl.when(i == 0)
            def _(gi=gi, gw=gw):
                dpw[gi] = gw

            @pl.when(i > 0)
            def _(gi=gi, gw=gw):
                dpw[gi] += gw

            dm_e = _dot(extd[:, sl] * ps_ref[:, sl], pw_ref[gi], NT)
            dmc[...] = dm_e / jnp.maximum(_pool_cnt(pos_e, L, w), 1.0)
            acc = -dm_e[HALO:HALO + RT]
            for off in range(-w // 2 + 1, w // 2 + 1):
                acc = acc + dmc[pl.ds(HALO + off, RT), :]
            dx_ref[:, sl] = acc
        _acc_rows(dps, i == 0, [jnp.concatenate(rows_s, axis=-1)])

    return pl.pallas_call(
        body, name=name, grid=(geo.nt,),
        in_specs=[geo.prev(PW), geo.row(PW), geo.next(PW), geo.prev(PW), geo.row(PW), geo.next(PW), _full(pw.shape), _full(ps.shape)],
        out_specs=[geo.row(PW), _full(pw.shape), _full(ps.shape)],
        out_shape=[S_((geo.T, PW), F32), S_(pw.shape, F32), S_(ps.shape, F32)],
        scratch_shapes=[pltpu.VMEM((RE, PW), F32), pltpu.VMEM((RE, PW), F32), pltpu.VMEM((RE, G), F32)],
        compiler_params=_cp())(P, P, P, dmix, dmix, dmix, pw, ps)


def _mod_fwd(A, mod_w, name):
    L, D, MC = mod_w.shape
    tn = _tile(MC, 768, 128)

    def body(a_ref, w_ref, o_ref):
        o_ref[...] = _dot(jax.nn.silu(a_ref[...]), w_ref[...], NN)

    return pl.pallas_call(body, name=name, grid=(L, MC // tn),
                          in_specs=[_full(A.shape), pl.BlockSpec((None, D, tn), lambda l, j: (l, 0, j))],
                          out_specs=pl.BlockSpec((None, 16, tn), lambda l, j: (l, 0, j)), out_shape=S_((L, 16, MC), F32),
                          compiler_params=_cp())(A, mod_w)


def _mod_bwd(A, DM, mod_w, name):
    L, D, MC = mod_w.shape
    tn = _tile(MC, 768, 128)
    nj = MC // tn

    def body(a_ref, dm_ref, w_ref, gw_ref, da_ref, acc):
        l, j = pl.program_id(0), pl.program_id(1)
        sa, vjp = jax.vjp(jax.nn.silu, a_ref[...])
        gw_ref[...] = _dot(sa, dm_ref[...], TN)
        part = _dot(dm_ref[...], w_ref[...], NT)
        first = jnp.logical_and(l == 0, j == 0)

        @pl.when(first)
        def _():
            acc[...] = part

        @pl.when(jnp.logical_not(first))
        def _():
            acc[...] += part

        @pl.when(jnp.logical_and(l == L - 1, j == nj - 1))
        def _():
            da_ref[...] = vjp(acc[...])[0]

    wspec = pl.BlockSpec((None, D, tn), lambda l, j: (l, 0, j))
    return pl.pallas_call(body, name=name, grid=(L, nj),
                          in_specs=[_full(A.shape), pl.BlockSpec((None, 16, tn), lambda l, j: (l, 0, j)), wspec],
                          out_specs=[wspec, _full(A.shape)], out_shape=[S_((L, D, MC), F32), S_(A.shape, F32)],
                          scratch_shapes=[pltpu.VMEM(A.shape, F32)], compiler_params=_cp())(A, DM, mod_w)


PACK_COLS = 1024


def _pack_rows(shape):
    n = 1
    for d in shape:
        n *= d
    return n, -(-n // (8 * PACK_COLS)) * 8


def _pack(arrs):
    parts = []
    for a in arrs:
        n, rows = _pack_rows(a.shape)
        parts.append(jnp.pad(a.reshape(-1).astype(F32), (0, rows * PACK_COLS - n)).reshape(rows, PACK_COLS))
    return jnp.concatenate(parts)


def _unpack(packed, shapes, lead=()):
    out, off = [], 0
    for s in shapes:
        n, rows = _pack_rows(s)
        blk = packed[..., off:off + rows, :].reshape(lead + (rows * PACK_COLS,))
        out.append(blk[..., :n].reshape(lead + tuple(s)))
        off += rows
    return out


def _unshard_last(g):
    g = jnp.moveaxis(g, 0, -2)
    return g.reshape(g.shape[:-2] + (g.shape[-2] * g.shape[-1],))


def _my_shard(a, me):
    n = a.shape[-1] // NDEV
    return lax.dynamic_slice_in_dim(a, me * n, n, axis=a.ndim - 1)


def _rot_tables(S, LC, dk, hd):
    t = jnp.arange(S, dtype=F32)
    n_r = dk // 2
    ang1 = t[:, None] * (RET_THETA ** (-jnp.arange(n_r, dtype=F32) / n_r))
    cos1 = jnp.concatenate([jnp.cos(ang1), jnp.ones((LC, n_r), F32)])
    sin1 = jnp.concatenate([jnp.sin(ang1), jnp.zeros((LC, n_r), F32)])
    n_ax = hd // 4
    f_ax = ROPE_THETA ** (-jnp.arange(n_ax, dtype=F32) / n_ax)
    row = jnp.floor(t / GRID_W)
    col = t - row * GRID_W
    ang2 = jnp.concatenate([row[:, None] * f_ax, col[:, None] * f_ax], axis=-1)
    c2, s2 = jnp.cos(ang2), jnp.sin(ang2)
    cosf = jnp.concatenate([jnp.concatenate([c2, c2], axis=-1), jnp.ones((LC, hd), F32)])
    sinf = jnp.concatenate([jnp.concatenate([-s2, s2], axis=-1), jnp.zeros((LC, hd), F32)])
    return cos1, sin1, cosf, sinf


SMALL = ("c_ctx", "mod_b", "norm_pre", "norm_post", "lru_conv_w", "lru_conv_b", "lru_wa", "lru_ba", "lru_wx", "lru_bx",
         "lru_lambda", "ret_decay_logit", "ret_gn", "pool_w", "pool_scale", "q_norm", "k_norm")
WEIGHTS = ("c_ctx", "mod_w", "mod_b", "norm_pre", "norm_post", "ffn_gate", "ffn_up", "ffn_down", "ev_w_in", "ev_w_out",
           "lru_conv_w", "lru_conv_b", "lru_wa", "lru_ba", "lru_wx", "lru_bx", "lru_lambda", "ret_decay_logit", "ret_gn",
           "od_w_in", "od_w_out", "pool_w", "pool_scale", "q_norm", "k_norm")
INPUTS = ("x", "c", "ctx") + WEIGHTS + ("loss_target",) + tuple("m_" + w for w in WEIGHTS) + tuple("v_" + w for w in WEIGHTS)


def _step(p):
    x, c, ctx = p["x"], p["c"], p["ctx"]
    _, S, D = x.shape
    LC = ctx.shape[1]
    geo = _Geo(S, LC, D)
    T = geo.T
    xi, yi, ci = _me()
    me = 4 * xi + 2 * yi + ci
    L = p["mod_w"].shape[0]
    assert L == 2
    W = p["lru_conv_b"].shape[-1]
    H = p["ret_decay_logit"].shape[-1]
    hd = p["q_norm"].shape[-1]
    G = p["pool_w"].shape[-1]
    PW = G * len(POOL_WINDOWS)
    od_mix = p["od_w_out"].shape[1] * NDEV
    od_in = p["od_w_in"].shape[2] * NDEV
    QW = od_mix - PW
    KW = (od_in - od_mix) // 2
    assert p["ev_w_in"].shape[2] * NDEV == 6 * W and p["ret_gn"].shape[-1] == W and p["ev_w_out"].shape[1] * NDEV == 2 * W
    odims = (PW, QW, KW, hd)
    cos1, sin1, cosf, sinf = _rot_tables(S, LC, W // H, hd)

    sh0 = [(D,), p["norm_pre"].shape, p["norm_post"].shape, p["lru_conv_w"].shape[1:], p["lru_ba"].shape[1:],
           p["lru_bx"].shape[1:], p["lru_lambda"].shape[1:], p["pool_scale"].shape[1:]]
    pack0 = _pack([c, p["norm_pre"], p["norm_post"], p["lru_conv_w"], p["lru_ba"], p["lru_bx"], p["lru_lambda"], p["pool_scale"]])
    (g0,) = _all_gather([pack0], "gather_small")
    c_all, npre, npost, conv_w, ba, bx, lam, pscale = _unpack(g0, sh0, (NDEV,))
    npre, npost, conv_w, ba, bx, lam, pscale = [_unshard_last(a) for a in (npre, npost, conv_w, ba, bx, lam, pscale)]
    pscale = pscale[None]
    conv_b = p["lru_conv_b"]
    wa, wx = p["lru_wa"][0], p["lru_wx"][0]
    gn = p["ret_gn"]
    logits = p["ret_decay_logit"][0].reshape(2, H, 1, 1)
    pool_w = p["pool_w"][0]
    qg, kg = p["q_norm"], p["k_norm"]

    A = jnp.concatenate([c_all, p["c_ctx"][None], jnp.zeros((7, D), F32)])
    M = _mod_fwd(A, p["mod_w"], "mod_fwd")
    (Mg,) = _all_gather([M], "gather_mod")
    MC = M.shape[2]
    tabs = []
    for l in range(L):
        ml = lax.dynamic_index_in_dim(Mg[:, l], me, axis=1, keepdims=False).reshape(NDEV * MC) + p["mod_b"][l]
        mc = Mg[:, l, 8].reshape(NDEV * MC) + p["mod_b"][l]
        tabs.append(jnp.stack([ml.reshape(9, D), mc.reshape(9, D)]))

    def cast2(a, name):
        return _cast_bf16(a.reshape(-1, a.shape[-1]), name).reshape(a.shape)

    loc = {(l, j): [cast2(p[n][l, j], f"cast_{n}_{l}{j}") for n in ("ffn_gate", "ffn_up", "ffn_down")]
           for l in range(L) for j in range(2)}
    loc["ev"] = [cast2(p["ev_w_in"][0], "cast_ev_in"), cast2(p["ev_w_out"][0], "cast_ev_out")]
    loc["od"] = [cast2(p["od_w_in"][0], "cast_od_in"), cast2(p["od_w_out"][0], "cast_od_out")]

    def gather_start(key, name):
        return _xchg_start(True, loc[key], [jnp.broadcast_to(a[None], (NDEV,) + a.shape) for a in loc[key]], name)

    ffn_w = {(0, 0): _all_gather(loc[0, 0], "gather_ffn_00")}

    def gp(a, l, s):
        return a[l, s][None]

    st, tok = gather_start("ev", "gs_ev")
    x0 = jnp.concatenate([x[0], ctx[0]])
    (h0,) = _norm_fwd(geo, x0, None, (_tie(gp(npre, 0, 0), tok), tabs[0], 0), "pre_00")
    y0, G0, U0 = _ffn_fwd(h0, *ffn_w[0, 0], name="ffn_fwd_00")
    x1, h1 = _norm_fwd(geo, x0, (y0, gp(npost, 0, 0), tabs[0], 0, FFN_STEP), (gp(npre, 0, 1), tabs[0], 1), "post_00")
    ev_in, ev_out = _xchg_wait(st, h1, "gw_ev")
    ev_out_f = ev_out.reshape(2 * W, D)

    st, tok = gather_start((0, 1), "gs_ffn_01")
    Pe = _mm_cols(h1, ev_in, "ev_in", dep=tok)
    u, a0, b0, a1, b1 = _lru_coef_fwd(geo, Pe, W, conv_w, conv_b, wa, ba, wx, bx, lam, "lru_coef")
    hs0, hp0 = _lru_scan_fwd(geo, a0, b0, 0, "lru_scan_f0")
    hs1, hp1 = _lru_scan_fwd(geo, a1, b1, 1, "lru_scan_f1")
    o0, st0 = _ret_fwd(geo, Pe, W, H, logits[0], cos1, sin1, 0, "ret_f0")
    o1, st1 = _ret_fwd(geo, Pe, W, H, logits[1], cos1, sin1, 1, "ret_f1")
    mixe = _mix_even_fwd(geo, Pe, W, H, hs0, hs1, o0, o1, gn, "mix_even")
    y1 = _mm_full(mixe, ev_out_f, NN, "ev_out")
    x2, h2 = _norm_fwd(geo, x1, (y1, gp(npost, 0, 1), tabs[0], 1, 1.0), (gp(npre, 0, 2), tabs[0], 2), "post_01")
    ffn_w[0, 1] = _xchg_wait(st, h2, "gw_ffn_01")

    st, tok = gather_start((1, 0), "gs_ffn_10")
    y2, G2, U2 = _ffn_fwd(h2, *ffn_w[0, 1], name="ffn_fwd_01", dep=tok)
    x3, h3 = _norm_fwd(geo, x2, (y2, gp(npost, 0, 2), tabs[0], 2, FFN_STEP), (gp(npre, 1, 0), tabs[1], 0), "post_02")
    ffn_w[1, 0] = _xchg_wait(st, h3, "gw_ffn_10")

    st, tok = gather_start("od", "gs_od")
    y3, G3, U3 = _ffn_fwd(h3, *ffn_w[1, 0], name="ffn_fwd_10", dep=tok)
    x4, h4 = _norm_fwd(geo, x3, (y3, gp(npost, 1, 0), tabs[1], 0, FFN_STEP), (gp(npre, 1, 1), tabs[1], 1), "post_10")
    od_inw, od_out = _xchg_wait(st, h4, "gw_od")
    od_out_f = od_out.reshape(od_mix, D)

    st, tok = gather_start((1, 1), "gs_ffn_11")
    Po = _mm_cols(h4, od_inw, "od_in", dep=tok)
    qr, kr, vr = _qk_prep_fwd(geo, Po, odims, qg, kg, cosf, sinf, "qk_prep")
    att, lse = _att_fwd(qr, kr, vr, hd, "att_fwd")
    mixo = _pool_fwd(geo, Po, PW, att, pool_w, pscale, "pool_fwd")
    y4 = _mm_full(mixo, od_out_f, NN, "od_out")
    x5, h5 = _norm_fwd(geo, x4, (y4, gp(npost, 1, 1), tabs[1], 1, 1.0), (gp(npre, 1, 2), tabs[1], 2), "post_11")
    ffn_w[1, 1] = _xchg_wait(st, h5, "gw_ffn_11")

    y5, G5, U5 = _ffn_fwd(h5, *ffn_w[1, 1], name="ffn_fwd_11")
    (x6,) = _norm_fwd(geo, x5, (y5, gp(npost, 1, 2), tabs[1], 2, FFN_STEP), None, "post_12")

    big_g = {}
    tokbox = [None]

    def gpt(l, s):
        return _tie(gp(npre, l, s), tokbox[0])

    def a2a_start(key, srcs, name, dh):
        own = [lax.dynamic_index_in_dim(a, me, 0, keepdims=False) for a in srcs]
        state, token = _xchg_start(False, srcs, [jnp.zeros(a.shape, a.dtype) for a in srcs], name)
        big_g[key] = (state, own)
        tokbox[0] = token
        return dh

    def ffn_bwd(dy, h, Gs, Us, key):
        tag = f"{key[0]}{key[1]}"
        dh, dG, dU, Aact = _ffn_bwd_act(dy, Gs, Us, *ffn_w[key], name=f"ffn_bwd_{tag}")
        srcs = [_ffn_wgrad_in(h, dG, f"ffn_wg_{tag}"), _ffn_wgrad_in(h, dU, f"ffn_wu_{tag}"), _ffn_wgrad_out(Aact, dy, f"ffn_wd_{tag}")]
        return a2a_start(key, srcs, f"as_ffn_{tag}", dh)

    loss_p, dx6, dy5, dpost5 = _loss_bwd(geo, x6, p["loss_target"][0], (y5, gp(npost, 1, 2), tabs[1], 2, FFN_STEP), "loss")
    dh5 = ffn_bwd(dy5, h5, G5, U5, (1, 1))
    dx5, dy4, dpre5, dpost4 = _norm_bwd(geo, dx6, dh5, x5, (gpt(1, 2), tabs[1], 2),
                                        (y4, gp(npost, 1, 1), tabs[1], 1, 1.0), "nb_5")

    dmixo = _mm_full(dy4, od_out_f, NT, "od_out_d")
    g_od_out = _mm_tn_rows(mixo, dy4, NDEV, "od_out_w")
    dpool, g_pool_w, g_pscale = _pool_bwd(geo, Po, PW, dmixo, pool_w, pscale, "pool_bwd")
    delta = _att_delta(geo, att, dmixo, PW, hd, "att_delta")
    dq, dk, dv = _att_bwd(qr, kr, vr, dmixo, PW, lse, delta, hd, "att_bwd")
    dPo, g_qn, g_kn = _qk_prep_bwd(geo, Po, odims, qg, kg, cosf, sinf, dpool, dq, dk, dv, "qk_prep_bwd")
    dh4 = _mm_nt_cols(dPo, od_inw, "od_in_d")
    g_od_in = _mm_tn_cols(h4, dPo, NDEV, "od_in_w")
    dh4 = a2a_start("od", [g_od_in, g_od_out], "as_od", dh4)
    dx4, dy3, dpre4, dpost3 = _norm_bwd(geo, dx5, dh4, x4, (gpt(1, 1), tabs[1], 1),
                                        (y3, gp(npost, 1, 0), tabs[1], 0, FFN_STEP), "nb_4")

    dh3 = ffn_bwd(dy3, h3, G3, U3, (1, 0))
    dx3, dy2, dpre3, dpost2 = _norm_bwd(geo, dx4, dh3, x3, (gpt(1, 0), tabs[1], 0),
                                        (y2, gp(npost, 0, 2), tabs[0], 2, FFN_STEP), "nb_3")

    dh2 = ffn_bwd(dy2, h2, G2, U2, (0, 1))
    dx2, dy1, dpre2, dpost1 = _norm_bwd(geo, dx3, dh2, x2, (gpt(0, 2), tabs[0], 2),
                                        (y1, gp(npost, 0, 1), tabs[0], 1, 1.0), "nb_2")

    dmixe = _mm_full(dy1, ev_out_f, NT, "ev_out_d")
    g_ev_out = _mm_tn_rows(mixe, dy1, NDEV, "ev_out_w")
    dg, dhs, dos, dog, g_gn = _mix_even_bwd(geo, Pe, W, H, hs0, hs1, o0, o1, gn, dmixe, "mix_even_bwd")
    da0, db0 = _lru_scan_bwd(geo, dhs, a0, hp0, 0, "lru_scan_b0")
    da1, db1 = _lru_scan_bwd(geo, dhs, a1, hp1, 1, "lru_scan_b1")
    du, g_wa, g_ba, g_wx, g_bx, g_lam = _lru_coef_bwd(geo, u, (da0, da1), (db0, db1), W, wa, ba, wx, bx, lam, "lru_coef_bwd")
    dq0, dk0, dv0, glg0 = _ret_bwd(geo, Pe, W, H, logits[0], cos1, sin1, st0, dos, None, 0, "ret_b0")
    dqe, dke, dve, glg1 = _ret_bwd(geo, Pe, W, H, logits[1], cos1, sin1, st1, dos, (dq0, dk0, dv0), 1, "ret_b1")
    dPe, g_cw, g_cb = _conv_bwd_assemble(geo, Pe, du, (dg, dqe, dke, dve, dog), W, conv_w, "conv_bwd")
    dh1 = _mm_nt_cols(dPe, ev_in, "ev_in_d")
    g_ev_in = _mm_tn_cols(h1, dPe, NDEV, "ev_in_w")
    dh1 = a2a_start("ev", [g_ev_in, g_ev_out], "as_ev", dh1)
    dx1, dy0, dpre1, dpost0 = _norm_bwd(geo, dx2, dh1, x1, (gpt(0, 1), tabs[0], 1),
                                        (y0, gp(npost, 0, 0), tabs[0], 0, FFN_STEP), "nb_1")

    dh0 = ffn_bwd(dy0, h0, G0, U0, (0, 0))
    dx0, dpre0 = _norm_bwd(geo, dx1, dh0, x0, (gpt(0, 0), tabs[0], 0), None, "nb_0")

    dpre = [[dpre0, dpre1, dpre2], [dpre3, dpre4, dpre5]]
    dpost = [[dpost0, dpost1, dpost2], [dpost3, dpost4, dpost5]]
    dtab = jnp.stack([jnp.stack([jnp.stack([dpre[l][s][:, 1], dpre[l][s][:, 2], dpost[l][s][:, 1]], axis=1) for s in range(3)], axis=1)
                      for l in range(L)])
    dtab_p = _pack([jnp.moveaxis(dtab.reshape(L, 2, 9 * D), 1, 0)])
    (dtab_g,) = _all_gather([dtab_p], "gather_dtab")
    dtab_sum = _sum_n(dtab_g, "sum_dtab")
    (dm_all,) = _unpack(dtab_g, [(2, L, 9 * D)], (NDEV,))
    (dm_sum,) = _unpack(dtab_sum, [(2, L, 9 * D)])
    (g_mod_b,) = _unpack(_sum_n(jnp.stack([_pack([dm_sum[0]]), _pack([dm_sum[1]])]), "sum_mod_b"), [(L, 9 * D)])
    dml = lax.dynamic_slice_in_dim(dm_all[:, 0], me * MC, MC, axis=2)
    dmc = lax.dynamic_slice_in_dim(dm_sum[1], me * MC, MC, axis=1)
    DM = jnp.concatenate([jnp.moveaxis(dml, 0, 1), dmc[:, None], jnp.zeros((L, 7, MC), F32)], axis=1)
    g_mod_w, dA = _mod_bwd(A, DM, p["mod_w"], "mod_bwd")

    g_npre = jnp.stack([jnp.stack([dpre[l][s][0, 0] + dpre[l][s][1, 0] for s in range(3)]) for l in range(L)])
    g_npost = jnp.stack([jnp.stack([dpost[l][s][0, 0] + dpost[l][s][1, 0] for s in range(3)]) for l in range(L)])
    g_logit = jnp.stack([glg0.reshape(H), glg1.reshape(H)])
    small_parts = [dA[8], g_npre, g_npost, g_cw, g_cb, g_wa, g_ba, g_wx, g_bx, g_lam, g_logit, g_gn, g_pool_w, g_pscale, g_qn, g_kn]
    (sg,) = _all_gather([_pack(small_parts)], "gather_small_g")
    ssum = _unpack(_sum_n(sg, "sum_small_g"), [a.shape for a in small_parts])
    (g_cctx, g_npre, g_npost, g_cw, g_cb, g_wa, g_ba, g_wx, g_bx, g_lam, g_logit, g_gn, g_pool_w, g_pscale, g_qn, g_kn) = ssum
    small_g = {
        "c_ctx": g_cctx, "mod_b": g_mod_b, "norm_pre": _my_shard(g_npre, me), "norm_post": _my_shard(g_npost, me),
        "lru_conv_w": _my_shard(g_cw, me)[None], "lru_conv_b": g_cb, "lru_wa": g_wa[None], "lru_ba": _my_shard(g_ba, me)[None],
        "lru_wx": g_wx[None], "lru_bx": _my_shard(g_bx, me)[None], "lru_lambda": _my_shard(g_lam, me)[None],
        "ret_decay_logit": g_logit[None], "ret_gn": g_gn, "pool_w": g_pool_w[None], "pool_scale": _my_shard(g_pscale, me),
        "q_norm": g_qn, "k_norm": g_kn,
    }
    shapes = [p[n].shape for n in SMALL]
    s_out = _reduce_adam(_pack([small_g[n] for n in SMALL])[None], _pack([p[n] for n in SMALL]),
                         _pack([p["m_" + n] for n in SMALL]), _pack([p["v_" + n] for n in SMALL]), "adam_small")
    res = {}
    for kind, packed in zip(("g", "d", "m", "v"), s_out):
        for n, a in zip(SMALL, _unpack(packed, shapes)):
            res[kind, n] = a

    def big(name, pieces, own, idx=None):
        w, m, v = p[name], p["m_" + name], p["v_" + name]
        if idx is not None:
            w, m, v = w[idx], m[idx], v[idx]
        shp = w.shape
        tag = name + ("" if idx is None else "_" + "".join(str(i) for i in idx))
        outs = _reduce_adam(pieces.reshape((pieces.shape[0], -1, shp[-1])), w.reshape(-1, shp[-1]), m.reshape(-1, shp[-1]),
                            v.reshape(-1, shp[-1]), "adam_" + tag, None if own is None else own.reshape(-1, shp[-1]))
        return [o.reshape(shp) for o in outs]

    got = {}
    for key in ((1, 1), "od", (1, 0), (0, 1), "ev", (0, 0)):
        state, own = big_g[key]
        tag = key if isinstance(key, str) else f"ffn_{key[0]}{key[1]}"
        got[key] = list(zip(_xchg_wait(state, dx0, "aw_" + tag), own))

    for name, (pieces, own) in (("mod_w", (g_mod_w[None], None)), ("ev_w_in", got["ev"][0]), ("ev_w_out", got["ev"][1]),
                                ("od_w_in", got["od"][0]), ("od_w_out", got["od"][1])):
        outs = big(name, pieces, own, None if name == "mod_w" else (0,))
        for kind, o in zip(("g", "d", "m", "v"), outs):
            res[kind, name] = o if name == "mod_w" else o[None]
    for wi, name in enumerate(("ffn_gate", "ffn_up", "ffn_down")):
        per = {(l, j): big(name, *got[l, j][wi], (l, j)) for l in range(L) for j in range(2)}
        for ki, kind in enumerate(("g", "d", "m", "v")):
            res[kind, name] = jnp.stack([jnp.stack([per[l, j][ki] for j in range(2)]) for l in range(L)])

    loss = lax.psum(loss_p[0, 0], ("x", "y", "c"))
    grad_x = dx0[:S][None]
    return (loss, grad_x) + tuple(res[kind, n] for kind in ("g", "d", "m", "v") for n in WEIGHTS)


def kernel(
        x, c, ctx, c_ctx, mod_w, mod_b, norm_pre, norm_post, ffn_gate, ffn_up, ffn_down, ev_w_in, ev_w_out, lru_conv_w,
        lru_conv_b, lru_wa, lru_ba, lru_wx, lru_bx, lru_lambda, ret_decay_logit, ret_gn, od_w_in, od_w_out, pool_w,
        pool_scale, q_norm, k_norm, loss_target, m_c_ctx, m_mod_w, m_mod_b, m_norm_pre, m_norm_post, m_ffn_gate, m_ffn_up,
        m_ffn_down, m_ev_w_in, m_ev_w_out, m_lru_conv_w, m_lru_conv_b, m_lru_wa, m_lru_ba, m_lru_wx, m_lru_bx, m_lru_lambda,
        m_ret_decay_logit, m_ret_gn, m_od_w_in, m_od_w_out, m_pool_w, m_pool_scale, m_q_norm, m_k_norm, v_c_ctx, v_mod_w,
        v_mod_b, v_norm_pre, v_norm_post, v_ffn_gate, v_ffn_up, v_ffn_down, v_ev_w_in, v_ev_w_out, v_lru_conv_w,
        v_lru_conv_b, v_lru_wa, v_lru_ba, v_lru_wx, v_lru_bx, v_lru_lambda, v_ret_decay_logit, v_ret_gn, v_od_w_in,
        v_od_w_out, v_pool_w, v_pool_scale, v_q_norm, v_k_norm):
    args = locals()
    return _step({n: args[n] for n in INPUTS})
```

```python
import functools

import jax
import jax.numpy as jnp
from jax import lax
from jax.experimental import pallas as pl
from jax.experimental.pallas import tpu as pltpu

F32 = jnp.float32
BF = jnp.bfloat16
S_ = jax.ShapeDtypeStruct
MESH = pl.DeviceIdType.MESH

NDEV = 8
EPS = 1e-6
FFN_STEP = 0.5
LRU_C = 8.0
RET_CHUNK = 128
RET_THETA = 10000.0
ROPE_THETA = 10000.0
GRID_W = 64
POOL_WINDOWS = (2, 4, 8, 16)
ROW_TILE = 256
HALO = 8
VMEM_LIMIT = 58 * 1024 * 1024
FFN_FWD_ROWS = 768
FFN_BWD_ROWS = 528
WGRAD_ROWS = 1408

ADAM_LR = 0.001
ADAM_B1 = 0.9
ADAM_B2 = 0.999
ADAM_EPS = 1e-08
ADAM_WD = 0.01
ADAM_STEP = 10

LOG2E = 1.4426950408889634
LN2 = 0.6931471805599453

NN = ((1,), (0,))
NT = ((1,), (1,))
TN = ((0,), (0,))


def _dot(a, b, dn):
    return lax.dot_general(a.astype(BF), b.astype(BF), (dn, ((), ())), preferred_element_type=F32)


@functools.partial(jax.custom_vjp, nondiff_argnums=(2,))
def _bdot(a, b, mode):
    return _dot(a, b, {"nn": NN, "nt": NT, "tn": TN}[mode])


def _bdot_fwd(a, b, mode):
    return _bdot(a, b, mode), (a, b)


def _bdot_bwd(mode, res, g):
    a, b = res
    if mode == "nn":
        return _dot(g, b, NT), _dot(a, g, TN)
    if mode == "nt":
        return _dot(g, b, NN), _dot(g, a, TN)
    return _dot(b, g, NT), _dot(a, g, NN)


_bdot.defvjp(_bdot_fwd, _bdot_bwd)


@functools.partial(jax.custom_vjp, nondiff_argnums=(1, 2))
def _rollv(x, shift, axis):
    return pltpu.roll(x, shift, axis)


def _rollv_fwd(x, shift, axis):
    return pltpu.roll(x, shift, axis), None


def _rollv_bwd(shift, axis, _, g):
    n = g.shape[axis]
    return (pltpu.roll(g, (n - shift) % n, axis),)


_rollv.defvjp(_rollv_fwd, _rollv_bwd)


def _cp(vmem=VMEM_LIMIT):
    return pltpu.CompilerParams(vmem_limit_bytes=vmem)


def _tile(n, pref, mult=8):
    if n <= pref:
        return n
    for t in range(pref, 0, -1):
        if n % t == 0 and t % mult == 0:
            return t
    return n


def _full(shape):
    nd = len(shape)
    return pl.BlockSpec(tuple(shape), lambda *_: (0,) * nd)


def _me():
    return lax.axis_index("x"), lax.axis_index("y"), lax.axis_index("c")


def _all_gather(arrs, name):
    n = len(arrs)

    def body(*refs):
        xs, outs = refs[:n], refs[n:2 * n]
        send_sems, recv_sems, local_sems = refs[2 * n:]
        x, y, c = _me()
        me, sibling = (x, y, c), (x, y, 1 - c)
        chips = [(1 - x, y), (x, 1 - y), (1 - x, 1 - y)]

        def blk(out, p):
            return out.at[4 * p[0] + 2 * p[1] + p[2]]

        def copy(a, k, block, to, src=None):
            return pltpu.make_async_remote_copy(
                src_ref=blk(outs[a], block) if src is None else src, dst_ref=blk(outs[a], block),
                send_sem=send_sems.at[a, k], recv_sem=recv_sems.at[a, k], device_id=to, device_id_type=MESH)

        mine = [pltpu.make_async_copy(xs[a], blk(outs[a], me), local_sems.at[a]) for a in range(n)]
        for cp in mine:
            cp.start()
        first = []
        for a in range(n):
            first.append(copy(a, 0, me, sibling, src=xs[a]))
            first += [copy(a, 1 + j, me, (*chip, c), src=xs[a]) for j, chip in enumerate(chips)]
        for cp in first:
            cp.start()
        passed = []
        for j, chip in enumerate(chips):
            for a in range(n):
                copy(a, 1 + j, (*chip, c), me).wait_recv()
                fw = copy(a, 4 + j, (*chip, c), sibling)
                fw.start()
                passed.append(fw)
        for a in range(n):
            copy(a, 0, sibling, me).wait_recv()
            for j, chip in enumerate(chips):
                copy(a, 4 + j, (*chip, 1 - c), me).wait_recv()
        for cp in first + passed:
            cp.wait_send()
        for cp in mine:
            cp.wait()

    anyspec = pl.BlockSpec(memory_space=pl.ANY)
    return pl.pallas_call(
        body, name=name,
        out_shape=[S_((NDEV,) + a.shape, a.dtype) for a in arrs],
        in_specs=[anyspec] * n, out_specs=[anyspec] * n,
        scratch_shapes=[pltpu.SemaphoreType.DMA((n, 7)), pltpu.SemaphoreType.DMA((n, 7)), pltpu.SemaphoreType.DMA((n,))],
    )(*arrs)


def _all_to_all(arrs, name):
    n = len(arrs)

    def body(*refs):
        xs, outs = refs[:n], refs[n:2 * n]
        send_sems, recv_sems, local_sems = refs[2 * n:]
        x, y, c = _me()
        me_idx = 4 * x + 2 * y + c
        mine = [pltpu.make_async_copy(xs[a].at[me_idx], outs[a].at[me_idx], local_sems.at[a]) for a in range(n)]
        for cp in mine:
            cp.start()
        copies = []
        for k in range(1, NDEV):
            kx, ky, kc = (k >> 2) & 1, (k >> 1) & 1, k & 1
            px = 1 - x if kx else x
            py = 1 - y if ky else y
            pc = 1 - c if kc else c
            p_idx = 4 * px + 2 * py + pc
            for a in range(n):
                copies.append(pltpu.make_async_remote_copy(
                    src_ref=xs[a].at[p_idx], dst_ref=outs[a].at[me_idx],
                    send_sem=send_sems.at[a, k - 1], recv_sem=recv_sems.at[a, k - 1],
                    device_id=(px, py, pc), device_id_type=MESH))
        for cp in copies:
            cp.start()
        for cp in copies:
            cp.wait_recv()
        for cp in copies:
            cp.wait_send()
        for cp in mine:
            cp.wait()

    anyspec = pl.BlockSpec(memory_space=pl.ANY)
    return pl.pallas_call(
        body, name=name,
        out_shape=[S_(a.shape, a.dtype) for a in arrs],
        in_specs=[anyspec] * n, out_specs=[anyspec] * n,
        scratch_shapes=[pltpu.SemaphoreType.DMA((n, 7)), pltpu.SemaphoreType.DMA((n, 7)), pltpu.SemaphoreType.DMA((n,))],
    )(*arrs)


HBM_SPEC = pl.BlockSpec(memory_space=pltpu.HBM)
SEM_SPEC = pl.BlockSpec(memory_space=pltpu.SEMAPHORE)
EFFECT = pltpu.SideEffectType.DATAFLOW_SIDE_EFFECTING


def _peers():
    x, y, c = _me()
    out = []
    for k in range(1, NDEV):
        px = 1 - x if (k >> 2) & 1 else x
        py = 1 - y if (k >> 1) & 1 else y
        pc = 1 - c if k & 1 else c
        out.append(((px, py, pc), 4 * px + 2 * py + pc))
    return out, 4 * x + 2 * y + c


def _xchg_copies(gather, xs, lands, send, recv):
    peers, me_idx = _peers()
    out = []
    for k, (dev, p_idx) in enumerate(peers):
        for a in range(len(xs)):
            out.append(pltpu.make_async_remote_copy(
                src_ref=xs[a] if gather else xs[a].at[p_idx], dst_ref=lands[a].at[me_idx],
                send_sem=send[a].at[k], recv_sem=recv[a].at[k], device_id=dev, device_id_type=MESH))
    return out


def _xchg_start(gather, xs, lands, name, after=()):
    n = len(xs)
    na = len(after)

    def body(*refs):
        xr, lr = refs[:n], refs[n:2 * n]
        outs = refs[2 * n + na:]
        for cp in _xchg_copies(gather, xr, lr, outs[:n], outs[n:2 * n]):
            cp.start()
        outs[4 * n][...] = jnp.zeros(outs[4 * n].shape, F32)

    ops = [pltpu.with_memory_space_constraint(a, pltpu.HBM) for a in list(xs) + list(lands)]
    outs = pl.pallas_call(
        body, name=name,
        out_shape=[pltpu.SemaphoreType.DMA((NDEV - 1,))] * (2 * n) + [pltpu.HBM(a.shape, a.dtype) for a in ops]
        + [S_((8, 128), F32)],
        in_specs=[HBM_SPEC] * (2 * n) + [pl.BlockSpec(memory_space=pl.ANY)] * na,
        out_specs=[SEM_SPEC] * (2 * n) + [HBM_SPEC] * (2 * n) + [pl.BlockSpec(memory_space=pltpu.VMEM)],
        input_output_aliases={i: 2 * n + i for i in range(2 * n)},
        compiler_params=pltpu.CompilerParams(has_side_effects=EFFECT),
    )(*ops, *after)
    return (gather, n, outs[:4 * n]), outs[4 * n]


def _xchg_wait(state, after, name):
    gather, n, st = state
    send, recv, xs, lands = st[:n], st[n:2 * n], st[2 * n:3 * n], st[3 * n:4 * n]
    after = list(after) if isinstance(after, (list, tuple)) else [after]

    def body(*refs):
        xr, lr = refs[:n], refs[n:2 * n]
        sr, rr = refs[2 * n:3 * n], refs[3 * n:4 * n]
        for cp in _xchg_copies(gather, xr, lr, sr, rr):
            cp.wait_send()
            cp.wait_recv()

    outs = pl.pallas_call(
        body, name=name,
        out_shape=[pltpu.HBM(a.shape, a.dtype) for a in list(xs) + list(lands)],
        in_specs=[HBM_SPEC] * (2 * n) + [SEM_SPEC] * (2 * n) + [pl.BlockSpec(memory_space=pl.ANY)] * len(after),
        out_specs=[HBM_SPEC] * (2 * n), input_output_aliases={i: i for i in range(2 * n)},
        compiler_params=pltpu.CompilerParams(has_side_effects=EFFECT),
    )(*xs, *lands, *send, *recv, *after)
    return outs[n:]


def _tie(a, token):
    return a + token[0, 0].astype(a.dtype)


def _cast_bf16(a, name):
    R, C = a.shape
    tr = _tile(R, 512, 16)

    def body(a_ref, o_ref):
        o_ref[...] = a_ref[...].astype(BF)

    return pl.pallas_call(body, name=name, grid=(R // tr,), in_specs=[pl.BlockSpec((tr, C), lambda i: (i, 0))],
                          out_specs=pl.BlockSpec((tr, C), lambda i: (i, 0)), out_shape=S_((R, C), BF), compiler_params=_cp())(a)


def _sum_n(a, name):
    n, R, C = a.shape
    tr = _tile(R, 256, 8)

    def body(a_ref, o_ref):
        acc = a_ref[0].astype(F32)
        for i in range(1, n):
            acc = acc + a_ref[i].astype(F32)
        o_ref[...] = acc

    return pl.pallas_call(body, name=name, grid=(R // tr,), in_specs=[pl.BlockSpec((n, tr, C), lambda i: (0, i, 0))],
                          out_specs=pl.BlockSpec((tr, C), lambda i: (i, 0)), out_shape=S_((R, C), F32), compiler_params=_cp())(a)


def _adam_math(w, g, m, v):
    m = ADAM_B1 * m + (1.0 - ADAM_B1) * g
    v = ADAM_B2 * v + (1.0 - ADAM_B2) * jnp.square(g)
    m_hat = m / (1.0 - ADAM_B1 ** ADAM_STEP)
    v_hat = v / (1.0 - ADAM_B2 ** ADAM_STEP)
    delta = -ADAM_LR * (m_hat / (jnp.sqrt(v_hat) + ADAM_EPS) + ADAM_WD * w)
    return delta, m, v


def _reduce_adam(pieces, w, m, v, name, own=None):
    n, R, C = pieces.shape
    tr = _tile(R, 256, 8)

    def body(*refs):
        p_ref, w_ref, m_ref, v_ref = refs[:4]
        g_ref, d_ref, mo_ref, vo_ref = refs[-4:]
        g = p_ref[0].astype(F32)
        for i in range(1, n):
            g = g + p_ref[i].astype(F32)
        if own is not None:
            g = g + refs[4][...].astype(F32)
        d, mn, vn = _adam_math(w_ref[...], g, m_ref[...], v_ref[...])
        g_ref[...] = g
        d_ref[...] = d
        mo_ref[...] = mn
        vo_ref[...] = vn

    row = pl.BlockSpec((tr, C), lambda i: (i, 0))
    ins = [pieces, w, m, v] + ([] if own is None else [own])
    return pl.pallas_call(body, name=name, grid=(R // tr,),
                          in_specs=[pl.BlockSpec((n, tr, C), lambda i: (0, i, 0))] + [row] * (len(ins) - 1),
                          out_specs=[row] * 4, out_shape=[S_((R, C), F32)] * 4, compiler_params=_cp())(*ins)


def _reduce_adam_stack(pieces, owns, w, m, v, name):
    F = len(pieces)
    n, R, C = pieces[0].shape
    tr = _tile(R, max(8, 131072 // C), 8)

    def body(*refs):
        w_ref, m_ref, v_ref = refs[2 * F:2 * F + 3]
        g_ref, d_ref, mo_ref, vo_ref = refs[-4:]
        f = pl.program_id(0)
        for ff in range(F):
            @pl.when(f == ff)
            def _(ff=ff):
                g = refs[ff][0].astype(F32)
                for i in range(1, n):
                    g = g + refs[ff][i].astype(F32)
                g = g + refs[F + ff][...].astype(F32)
                d, mn, vn = _adam_math(w_ref[...], g, m_ref[...], v_ref[...])
                g_ref[...] = g
                d_ref[...] = d
                mo_ref[...] = mn
                vo_ref[...] = vn

    pspecs = [pl.BlockSpec((n, tr, C), lambda f, r, ff=ff: (0, jnp.where(f == ff, r, 0), 0)) for ff in range(F)]
    ospecs = [pl.BlockSpec((tr, C), lambda f, r, ff=ff: (jnp.where(f == ff, r, 0), 0)) for ff in range(F)]
    st = pl.BlockSpec((None, tr, C), lambda f, r: (f, r, 0))
    return pl.pallas_call(body, name=name, grid=(F, R // tr), in_specs=pspecs + ospecs + [st] * 3, out_specs=[st] * 4,
                          out_shape=[S_((F, R, C), F32)] * 4, compiler_params=_cp())(*pieces, *owns, w, m, v)


def _mm_cols(a, wb, name, out_dtype=F32, dep=None):
    M, K = a.shape
    NB, _, nb = wb.shape
    tm = _tile(M, 768, 16)

    def body(*refs):
        refs[-1][...] = _dot(refs[0][...], refs[1][...], NN).astype(out_dtype)

    deps = [] if dep is None else [dep]
    return pl.pallas_call(body, name=name, grid=(M // tm, NB),
                          in_specs=[pl.BlockSpec((tm, K), lambda i, j: (i, 0)), pl.BlockSpec((None, K, nb), lambda i, j: (j, 0, 0))]
                          + [_full(d.shape) for d in deps],
                          out_specs=pl.BlockSpec((tm, nb), lambda i, j: (i, j)), out_shape=S_((M, NB * nb), out_dtype),
                          compiler_params=_cp())(a, wb, *deps)


def _mm_nt_cols(g, wb, name):
    M = g.shape[0]
    NB, K, nb = wb.shape
    tm = _tile(M, 768, 16)

    def body(g_ref, w_ref, o_ref):
        j = pl.program_id(1)
        part = _dot(g_ref[...], w_ref[...], NT)

        @pl.when(j == 0)
        def _():
            o_ref[...] = part

        @pl.when(j > 0)
        def _():
            o_ref[...] += part

    return pl.pallas_call(body, name=name, grid=(M // tm, NB),
                          in_specs=[pl.BlockSpec((tm, nb), lambda i, j: (i, j)), pl.BlockSpec((None, K, nb), lambda i, j: (j, 0, 0))],
                          out_specs=pl.BlockSpec((tm, K), lambda i, j: (i, 0)), out_shape=S_((M, K), F32),
                          compiler_params=_cp())(g, wb)


def _mm_full(a, w, dn, name, out_dtype=F32):
    M, K = a.shape
    N = w.shape[1] if dn == NN else w.shape[0]
    tm = _tile(M, 768, 16)

    def body(a_ref, w_ref, o_ref):
        o_ref[...] = _dot(a_ref[...], w_ref[...], dn).astype(out_dtype)

    return pl.pallas_call(body, name=name, grid=(M // tm,),
                          in_specs=[pl.BlockSpec((tm, K), lambda i: (i, 0)), _full(w.shape)],
                          out_specs=pl.BlockSpec((tm, N), lambda i: (i, 0)), out_shape=S_((M, N), out_dtype),
                          compiler_params=_cp())(a, w)


def _mm_tn_cols(at, g, NB, name):
    K, M = at.shape
    nb = g.shape[1] // NB
    tk = _tile(M, WGRAD_ROWS, 128)
    nk = M // tk

    def body(a_ref, g_ref, o_ref, acc):
        k = pl.program_id(1)
        part = _dot(a_ref[...], g_ref[...], NN)

        @pl.when(k == 0)
        def _():
            acc[...] = part

        @pl.when(k > 0)
        def _():
            acc[...] += part

        @pl.when(k == nk - 1)
        def _():
            o_ref[...] = acc[...].astype(BF)

    return pl.pallas_call(body, name=name, grid=(NB, nk),
                          in_specs=[pl.BlockSpec((K, tk), lambda b, k: (0, k)), pl.BlockSpec((tk, nb), lambda b, k: (k, b))],
                          out_specs=pl.BlockSpec((None, K, nb), lambda b, k: (b, 0, 0)), out_shape=S_((NB, K, nb), BF),
                          scratch_shapes=[pltpu.VMEM((K, nb), F32)], compiler_params=_cp())(at, g)


def _mm_tn_rows(a, g, NB, name):
    M = a.shape[0]
    kb = a.shape[1] // NB
    N = g.shape[1]
    tk = _tile(M, WGRAD_ROWS, 128)
    nk = M // tk

    def body(a_ref, g_ref, o_ref, acc):
        k = pl.program_id(1)
        part = _dot(a_ref[...], g_ref[...], TN)

        @pl.when(k == 0)
        def _():
            acc[...] = part

        @pl.when(k > 0)
        def _():
            acc[...] += part

        @pl.when(k == nk - 1)
        def _():
            o_ref[...] = acc[...].astype(BF)

    return pl.pallas_call(body, name=name, grid=(NB, nk),
                          in_specs=[pl.BlockSpec((tk, kb), lambda b, k: (k, b)), pl.BlockSpec((tk, N), lambda b, k: (k, 0))],
                          out_specs=pl.BlockSpec((None, kb, N), lambda b, k: (b, 0, 0)), out_shape=S_((NB, kb, N), BF),
                          scratch_shapes=[pltpu.VMEM((kb, N), F32)], compiler_params=_cp())(a, g)


def _ffn_fwd(h, wg, wu, wd, name, dep=None):
    T, D = h.shape
    NB, _, nb = wg.shape
    tm = _tile(T, FFN_FWD_ROWS, 16)

    def body(*refs):
        h_ref, wg_ref, wu_ref, wd_ref = refs[:4]
        y_ref, g_ref, u_ref = refs[-3:]
        b = pl.program_id(1)
        hh = h_ref[...]
        g = _dot(hh, wg_ref[...], NN).astype(BF)
        u = _dot(hh, wu_ref[...], NN).astype(BF)
        g_ref[...] = g
        u_ref[...] = u
        gf = g.astype(F32)
        part = _dot(gf * jax.nn.sigmoid(gf) * u.astype(F32), wd_ref[...], NN)

        @pl.when(b == 0)
        def _():
            y_ref[...] = part

        @pl.when(b > 0)
        def _():
            y_ref[...] += part

    deps = [] if dep is None else [dep]
    wcol = pl.BlockSpec((None, D, nb), lambda i, b: (b, 0, 0))
    act = pl.BlockSpec((None, tm, nb), lambda i, b: (b, i, 0))
    return pl.pallas_call(
        body, name=name, grid=(T // tm, NB),
        in_specs=[pl.BlockSpec((tm, D), lambda i, b: (i, 0)), wcol, wcol, pl.BlockSpec((None, nb, D), lambda i, b: (b, 0, 0))]
        + [_full(d.shape) for d in deps],
        out_specs=[pl.BlockSpec((tm, D), lambda i, b: (i, 0)), act, act],
        out_shape=[S_((T, D), F32), S_((NB, T, nb), BF), S_((NB, T, nb), BF)], compiler_params=_cp())(h, wg, wu, wd, *deps)


def _ffn_bwd_act(dy, G, U, wg, wu, wd, name):
    T, D = dy.shape
    NB, _, nb = wg.shape
    tm = _tile(T, FFN_BWD_ROWS, 16)

    def body(dy_ref, g_ref, u_ref, wg_ref, wu_ref, wd_ref, dh_ref, dg_ref, du_ref, a_ref):
        b = pl.program_id(1)
        g, u = g_ref[...].astype(F32), u_ref[...].astype(F32)
        da = _dot(dy_ref[...], wd_ref[...], NT)
        s = jax.nn.sigmoid(g)
        silu = g * s
        du = da * silu
        dg = da * u * (s * (1.0 + g * (1.0 - s)))
        dg_ref[...] = dg.astype(BF)
        du_ref[...] = du.astype(BF)
        a_ref[...] = (silu * u).astype(BF)
        part = _dot(dg, wg_ref[...], NT) + _dot(du, wu_ref[...], NT)

        @pl.when(b == 0)
        def _():
            dh_ref[...] = part

        @pl.when(b > 0)
        def _():
            dh_ref[...] += part

    wcol = pl.BlockSpec((None, D, nb), lambda i, b: (b, 0, 0))
    act = pl.BlockSpec((None, tm, nb), lambda i, b: (b, i, 0))
    row = pl.BlockSpec((tm, D), lambda i, b: (i, 0))
    return pl.pallas_call(
        body, name=name, grid=(T // tm, NB),
        in_specs=[row, act, act, wcol, wcol, pl.BlockSpec((None, nb, D), lambda i, b: (b, 0, 0))],
        out_specs=[row, act, act, act],
        out_shape=[S_((T, D), F32)] + [S_((NB, T, nb), BF)] * 3, compiler_params=_cp())(dy, G, U, wg, wu, wd)


def _ffn_wgrad_in(ht, dact, name):
    D, T = ht.shape
    NB, _, nb = dact.shape
    tk = _tile(T, WGRAD_ROWS, 128)
    nk = T // tk

    def body(h_ref, d_ref, o_ref, acc):
        k = pl.program_id(1)
        part = _dot(h_ref[...], d_ref[...], NN)

        @pl.when(k == 0)
        def _():
            acc[...] = part

        @pl.when(k > 0)
        def _():
            acc[...] += part

        @pl.when(k == nk - 1)
        def _():
            o_ref[...] = acc[...].astype(BF)

    return pl.pallas_call(body, name=name, grid=(NB, nk),
                          in_specs=[pl.BlockSpec((D, tk), lambda b, k: (0, k)), pl.BlockSpec((None, tk, nb), lambda b, k: (b, k, 0))],
                          out_specs=pl.BlockSpec((None, D, nb), lambda b, k: (b, 0, 0)), out_shape=S_((NB, D, nb), BF),
                          scratch_shapes=[pltpu.VMEM((D, nb), F32)], compiler_params=_cp())(ht, dact)


def _ffn_wgrad_out(act, dy, name):
    NB, T, nb = act.shape
    D = dy.shape[1]
    tk = _tile(T, WGRAD_ROWS, 128)
    nk = T // tk

    def body(a_ref, d_ref, o_ref, acc):
        k = pl.program_id(1)
        part = _dot(a_ref[...], d_ref[...], TN)

        @pl.when(k == 0)
        def _():
            acc[...] = part

        @pl.when(k > 0)
        def _():
            acc[...] += part

        @pl.when(k == nk - 1)
        def _():
            o_ref[...] = acc[...].astype(BF)

    return pl.pallas_call(body, name=name, grid=(NB, nk),
                          in_specs=[pl.BlockSpec((None, tk, nb), lambda b, k: (b, k, 0)), pl.BlockSpec((tk, D), lambda b, k: (k, 0))],
                          out_specs=pl.BlockSpec((None, nb, D), lambda b, k: (b, 0, 0)), out_shape=S_((NB, nb, D), BF),
                          scratch_shapes=[pltpu.VMEM((nb, D), F32)], compiler_params=_cp())(act, dy)


class _Geo:
    def __init__(self, S, LC, D):
        self.S, self.LC, self.D, self.T = S, LC, D, S + LC
        self.RT = _tile(LC, ROW_TILE, 8)
        assert S % self.RT == 0 and self.RT >= 2 * HALO
        self.nlat, self.nctx = S // self.RT, LC // self.RT
        self.nt = self.nlat + self.nctx

    def row(self, C, cb=0):
        return pl.BlockSpec((self.RT, C), lambda i: (i, cb))

    def prev(self, C, cb=0):
        return pl.BlockSpec((self.RT, C), lambda i: (jnp.maximum(i - 1, 0), cb))

    def next(self, C, cb=0):
        nt = self.nt
        return pl.BlockSpec((self.RT, C), lambda i: (jnp.minimum(i + 1, nt - 1), cb))

    def seg(self, r, C):
        nlat = self.nlat
        return pl.BlockSpec((None, r, C), lambda i: (jnp.minimum(i // nlat, 1), 0, 0))

    def first_of_seg(self, i):
        return jnp.logical_or(i == 0, i == self.nlat)

    def prev_ok(self, i):
        return jnp.logical_and(i != 0, i != self.nlat)

    def next_ok(self, i):
        return jnp.logical_and(i != self.nlat - 1, i != self.nt - 1)

    def pos(self, i):
        r = lax.broadcasted_iota(jnp.int32, (self.RT, 1), 0)
        is_ctx = i >= self.nlat
        base = jnp.where(is_ctx, (i - self.nlat) * self.RT, i * self.RT)
        return base + r, jnp.where(is_ctx, self.LC, self.S)


def _rms(x):
    return x * lax.rsqrt(jnp.mean(x * x, axis=-1, keepdims=True) + EPS)


def _modulate(x, g, shift, scale):
    return (_rms(x) * g) * (1 + scale) + shift


def _post(x, y, g, gate, w):
    return x + w * gate * (_rms(y) * g)


def _norm_fwd(geo, x, post, pre, name):
    D = geo.D
    ins, specs = [x], [geo.row(D)]
    if post is not None:
        ins += [post[0], post[1], post[2]]
        specs += [geo.row(D), _full((1, D)), geo.seg(9, D)]
    if pre is not None:
        ins += [pre[0], pre[1]]
        specs += [_full((1, D)), geo.seg(9, D)]

    def body(*refs):
        it = iter(refs)
        xv = next(it)[...]
        if post is not None:
            y_ref, gp_ref, tab_ref = next(it), next(it), next(it)
        if pre is not None:
            gq_ref, tabn_ref = next(it), next(it)
        if post is not None:
            r = 3 * post[3] + 2
            xv = _post(xv, y_ref[...], gp_ref[...], tab_ref[r:r + 1, :], post[4])
            next(it)[...] = xv
        if pre is not None:
            r = 3 * pre[2]
            h = _modulate(xv, gq_ref[...], tabn_ref[r:r + 1, :], tabn_ref[r + 1:r + 2, :]).astype(BF)
            next(it)[...] = h
            next(it)[...] = h.T

    outs, ospecs = [], []
    if post is not None:
        outs.append(S_((geo.T, D), F32))
        ospecs.append(geo.row(D))
    if pre is not None:
        outs += [S_((geo.T, D), BF), S_((D, geo.T), BF)]
        ospecs += [geo.row(D), pl.BlockSpec((D, geo.RT), lambda i: (0, i))]
    return pl.pallas_call(body, name=name, grid=(geo.nt,), in_specs=specs, out_specs=ospecs, out_shape=outs,
                          compiler_params=_cp())(*ins)


def _acc_rows(ref, first, rows):
    for k, v in enumerate(rows):
        @pl.when(first)
        def _(k=k, v=v):
            ref[k:k + 1, :] = v

        @pl.when(jnp.logical_not(first))
        def _(k=k, v=v):
            ref[k:k + 1, :] += v


def _norm_bwd(geo, dxo, dh, x, pre, post, name):
    D = geo.D
    ins = [dxo, dh, x, pre[0], pre[1]]
    specs = [geo.row(D)] * 3 + [_full((1, D)), geo.seg(9, D)]
    if post is not None:
        ins += [post[0], post[1], post[2]]
        specs += [geo.row(D), _full((1, D)), geo.seg(9, D)]

    def body(*refs):
        i = pl.program_id(0)
        first = geo.first_of_seg(i)
        dxo_ref, dh_ref, x_ref, gq_ref, tab_ref = refs[:5]
        k = 5
        if post is not None:
            y_ref, gp_ref, tabp_ref = refs[5:8]
            k = 8
        outs = refs[k:]
        r = 3 * pre[2]
        _, vjp = jax.vjp(_modulate, x_ref[...], gq_ref[...], tab_ref[r:r + 1, :], tab_ref[r + 1:r + 2, :])
        dx, dg, dsh, dsc = vjp(dh_ref[...].astype(F32))
        dx = dx + dxo_ref[...]
        outs[0][...] = dx
        if post is None:
            _acc_rows(outs[1], first, [dg, dsh, dsc])
            return
        _acc_rows(outs[2], first, [dg, dsh, dsc])
        rp = 3 * post[3] + 2
        w = post[4]
        _, vjp2 = jax.vjp(lambda yy, gg, ga: w * ga * (_rms(yy) * gg), y_ref[...], gp_ref[...], tabp_ref[rp:rp + 1, :])
        dy, dgp, dga = vjp2(dx)
        outs[1][...] = dy
        _acc_rows(outs[3], first, [dgp, dga])

    if post is None:
        outs, ospecs = [S_((geo.T, D), F32), S_((2, 3, D), F32)], [geo.row(D), geo.seg(3, D)]
    else:
        outs = [S_((geo.T, D), F32), S_((geo.T, D), F32), S_((2, 3, D), F32), S_((2, 2, D), F32)]
        ospecs = [geo.row(D), geo.row(D), geo.seg(3, D), geo.seg(2, D)]
    return pl.pallas_call(body, name=name, grid=(geo.nt,), in_specs=specs, out_specs=ospecs, out_shape=outs,
                          compiler_params=_cp())(*ins)


def _loss_bwd(geo, xf, tgt, post, name):
    D = geo.D
    nlat = geo.nlat

    def body(x_ref, t_ref, y_ref, gp_ref, tabp_ref, loss_ref, dx_ref, dy_ref, dpost_ref):
        i = pl.program_id(0)
        first = geo.first_of_seg(i)
        lat = i < nlat
        diff = x_ref[...] - t_ref[...]
        part = jnp.where(lat, 0.5 * jnp.sum(jnp.mean(diff * diff, axis=-1, keepdims=True), axis=0, keepdims=True), 0.0)

        @pl.when(i == 0)
        def _():
            loss_ref[...] = part

        @pl.when(i > 0)
        def _():
            loss_ref[...] += part

        dx = jnp.where(lat, diff * (1.0 / D), 0.0)
        dx_ref[...] = dx
        rp = 3 * post[3] + 2
        w = post[4]
        _, vjp2 = jax.vjp(lambda yy, gg, ga: w * ga * (_rms(yy) * gg), y_ref[...], gp_ref[...], tabp_ref[rp:rp + 1, :])
        dy, dgp, dga = vjp2(dx)
        dy_ref[...] = dy
        _acc_rows(dpost_ref, first, [dgp, dga])

    tspec = pl.BlockSpec((geo.RT, D), lambda i: (jnp.minimum(i, nlat - 1), 0))
    return pl.pallas_call(
        body, name=name, grid=(geo.nt,),
        in_specs=[geo.row(D), tspec, geo.row(D), _full((1, D)), geo.seg(9, D)],
        out_specs=[_full((1, 1)), geo.row(D), geo.row(D), geo.seg(2, D)],
        out_shape=[S_((1, 1), F32), S_((geo.T, D), F32), S_((geo.T, D), F32), S_((2, 2, D), F32)],
        compiler_params=_cp())(xf, tgt, post[0], post[1], post[2])


def _fill_ext(geo, ext, prev_ref, cur_ref, next_ref, i):
    RT = geo.RT
    ext[0:HALO, :] = jnp.where(geo.prev_ok(i), prev_ref[RT - HALO:RT, :], 0.0).astype(ext.dtype)
    ext[HALO:HALO + RT, :] = cur_ref[...].astype(ext.dtype)
    ext[HALO + RT:2 * HALO + RT, :] = jnp.where(geo.next_ok(i), next_ref[0:HALO, :], 0.0).astype(ext.dtype)


CONV_W = 4
CONV_LEFT = 2


def _lru_gates(u, za, zx, lam):
    r = jax.nn.sigmoid(za)
    i = jax.nn.sigmoid(zx)
    log_a = -LRU_C * r * jax.nn.softplus(-lam)
    a = jnp.exp(log_a)
    return a, jnp.sqrt(1.0 - jnp.exp(2.0 * log_a)) * (i * u)


def _lru_coef_fwd(geo, P, W, cw, cb, wa, ba, wx, bx, lam, name):
    RT = geo.RT
    nblk, LB = wa.shape[1], wa.shape[2]

    def body(pp, pc, pn, cw_ref, cb_ref, wa_ref, ba_ref, wx_ref, bx_ref, lam_ref, u_ref, a0, b0, a1, b1, ext):
        i = pl.program_id(0)
        _fill_ext(geo, ext, pp, pc, pn, i)
        u = cb_ref[...] + ext[pl.ds(HALO - CONV_LEFT, RT), :] * cw_ref[0:1, :]
        for k in range(1, CONV_W):
            u = u + ext[pl.ds(HALO - CONV_LEFT + k, RT), :] * cw_ref[k:k + 1, :]
        u_ref[...] = u
        for d, (a_ref, b_ref) in enumerate(((a0, b0), (a1, b1))):
            for n in range(nblk):
                sl = slice(n * LB, (n + 1) * LB)
                un = u[:, sl]
                za = _dot(un, wa_ref[d, n], NN) + ba_ref[d:d + 1, sl]
                zx = _dot(un, wx_ref[d, n], NN) + bx_ref[d:d + 1, sl]
                a, b = _lru_gates(un, za, zx, lam_ref[d:d + 1, sl])
                a_ref[:, sl] = a
                b_ref[:, sl] = b

    return pl.pallas_call(
        body, name=name, grid=(geo.nt,),
        in_specs=[geo.prev(W, 1), geo.row(W, 1), geo.next(W, 1), _full(cw.shape), _full(cb.shape), _full(wa.shape),
                  _full(ba.shape), _full(wx.shape), _full(bx.shape), _full(lam.shape)],
        out_specs=[geo.row(W)] * 5, out_shape=[S_((geo.T, W), F32)] * 5,
        scratch_shapes=[pltpu.VMEM((RT + 2 * HALO, W), F32)], compiler_params=_cp())(P, P, P, cw, cb, wa, ba, wx, bx, lam)


def _lru_coef_bwd(geo, u, da, db, W, wa, ba, wx, bx, lam, name):
    nblk, LB = wa.shape[1], wa.shape[2]

    def body(u_ref, da0, db0, da1, db1, wa_ref, ba_ref, wx_ref, bx_ref, lam_ref, du_ref, dwa, dba, dwx, dbx, dlam):
        i = pl.program_id(0)

        @pl.when(i == 0)
        def _():
            for r in (dwa, dba, dwx, dbx, dlam):
                r[...] = jnp.zeros(r.shape, F32)

        u = u_ref[...]
        for n in range(nblk):
            sl = slice(n * LB, (n + 1) * LB)
            un = u[:, sl]
            dun = jnp.zeros_like(un)
            for d, (da_ref, db_ref) in enumerate(((da0, db0), (da1, db1))):
                za = _dot(un, wa_ref[d, n], NN) + ba_ref[d:d + 1, sl]
                zx = _dot(un, wx_ref[d, n], NN) + bx_ref[d:d + 1, sl]
                _, vjp = jax.vjp(_lru_gates, un, za, zx, lam_ref[d:d + 1, sl])
                du_e, dza, dzx, dl = vjp((da_ref[:, sl], db_ref[:, sl]))
                dun = dun + du_e + _dot(dza, wa_ref[d, n], NT) + _dot(dzx, wx_ref[d, n], NT)
                dwa[d, n] += _dot(un, dza, TN)
                dwx[d, n] += _dot(un, dzx, TN)
                dba[d:d + 1, sl] += jnp.sum(dza, axis=0, keepdims=True)
                dbx[d:d + 1, sl] += jnp.sum(dzx, axis=0, keepdims=True)
                dlam[d:d + 1, sl] += dl
            du_ref[:, sl] = dun

    row = geo.row(W)
    return pl.pallas_call(
        body, name=name, grid=(geo.nt,),
        in_specs=[row] * 5 + [_full(wa.shape), _full(ba.shape), _full(wx.shape), _full(bx.shape), _full(lam.shape)],
        out_specs=[row, _full(wa.shape), _full(ba.shape), _full(wx.shape), _full(bx.shape), _full(lam.shape)],
        out_shape=[S_((geo.T, W), F32), S_(wa.shape, F32), S_(ba.shape, F32), S_(wx.shape, F32), S_(bx.shape, F32), S_(lam.shape, F32)],
        compiler_params=_cp())(u, da[0], db[0], da[1], db[1], wa, ba, wx, bx, lam)


def _scan_order(d, k, nlat, nctx):
    if d == 0:
        return jnp.where(k < nctx, nlat + k, k - nctx)
    return jnp.where(k < nctx, nlat + nctx - 1 - k, nlat - 1 - (k - nctx))


def _chunk_scan(a, b, reverse):
    n = a.shape[0]
    row = lax.broadcasted_iota(jnp.int32, a.shape, 0)
    s = 1
    while s < n:
        if reverse:
            ok = row < n - s
            a_s, b_s = pltpu.roll(a, n - s, 0), pltpu.roll(b, n - s, 0)
        else:
            ok = row >= s
            a_s, b_s = pltpu.roll(a, s, 0), pltpu.roll(b, s, 0)
        b = a * jnp.where(ok, b_s, 0.0) + b
        a = a * jnp.where(ok, a_s, 1.0)
        s *= 2
    return a, b


def _shift1(x, reverse, fill):
    n = x.shape[0]
    row = lax.broadcasted_iota(jnp.int32, x.shape, 0)
    if reverse:
        return jnp.where(row == n - 1, fill, pltpu.roll(x, n - 1, 0))
    return jnp.where(row == 0, fill, pltpu.roll(x, 1, 0))


def _lru_scan_fwd(geo, a, b, d, name):
    W = a.shape[1]
    tw = _tile(W, 256, 128)
    RT, nlat, nctx = geo.RT, geo.nlat, geo.nctx
    rev = d == 1

    def body(a_ref, b_ref, h_ref, hp_ref, carry):
        k = pl.program_id(1)

        @pl.when(k == 0)
        def _():
            carry[...] = jnp.zeros(carry.shape, F32)

        ac, bc = _chunk_scan(a_ref[...], b_ref[...], rev)
        h = bc + ac * carry[...]
        h_ref[...] = h
        hp_ref[...] = _shift1(h, rev, carry[...])
        carry[...] = h[0:1, :] if rev else h[RT - 1:RT, :]

    spec = pl.BlockSpec((RT, tw), lambda j, k: (_scan_order(d, k, nlat, nctx), j))
    return pl.pallas_call(body, name=name, grid=(W // tw, geo.nt), in_specs=[spec, spec], out_specs=[spec, spec],
                          out_shape=[S_((geo.T, W), F32)] * 2, scratch_shapes=[pltpu.VMEM((1, tw), F32)],
                          compiler_params=_cp())(a, b)


def _lru_scan_bwd(geo, dh, a, hprev, d, name):
    W = a.shape[1]
    tw = _tile(W, 256, 128)
    RT, nlat, nctx, nt = geo.RT, geo.nlat, geo.nctx, geo.nt
    rev = d == 1

    def body(dh_ref, a_ref, hp_ref, da_ref, db_ref, carry):
        k = pl.program_id(1)

        @pl.when(k == 0)
        def _():
            carry[...] = jnp.zeros(carry.shape, F32)

        av = a_ref[...]
        a_next = _shift1(av, not rev, jnp.ones((1, tw), F32))
        ac, bc = _chunk_scan(a_next, dh_ref[...], not rev)
        lam = bc + ac * carry[...]
        db_ref[...] = lam
        da_ref[...] = lam * hp_ref[...]
        first = (av * lam)[RT - 1:RT, :] if rev else (av * lam)[0:1, :]
        carry[...] = first

    spec = pl.BlockSpec((RT, tw), lambda j, k: (_scan_order(d, nt - 1 - k, nlat, nctx), j))
    return pl.pallas_call(body, name=name, grid=(W // tw, nt), in_specs=[spec] * 3, out_specs=[spec] * 2,
                          out_shape=[S_((geo.T, W), F32)] * 2, scratch_shapes=[pltpu.VMEM((1, tw), F32)],
                          compiler_params=_cp())(dh, a, hprev)


def _conv_bwd_assemble(geo, P, du, pieces, W, cw, name):
    RT = geo.RT

    def body(pp, pc, pn, dup, duc, dun, dg, dq, dk, dv, dog, cw_ref, dP_ref, dcw, dcb, ext_r, ext_d):
        i = pl.program_id(0)
        _fill_ext(geo, ext_r, pp, pc, pn, i)
        _fill_ext(geo, ext_d, dup, duc, dun, i)
        du_c = duc[...]
        rows = []
        dr = None
        for k in range(CONV_W):
            rows.append(jnp.sum(du_c * ext_r[pl.ds(HALO - CONV_LEFT + k, RT), :], axis=0, keepdims=True))
            t = ext_d[pl.ds(HALO + CONV_LEFT - k, RT), :] * cw_ref[k:k + 1, :]
            dr = t if dr is None else dr + t
        first = i == 0
        _acc_rows(dcw, first, rows)
        _acc_rows(dcb, first, [jnp.sum(du_c, axis=0, keepdims=True)])
        for j, v in enumerate((dg[...], dr, dq[...], dk[...], dv[...], dog[...])):
            dP_ref[:, j * W:(j + 1) * W] = v

    row = geo.row(W)
    return pl.pallas_call(
        body, name=name, grid=(geo.nt,),
        in_specs=[geo.prev(W, 1), geo.row(W, 1), geo.next(W, 1), geo.prev(W), row, geo.next(W)] + [row] * 5 + [_full(cw.shape)],
        out_specs=[geo.row(6 * W), _full((CONV_W, W)), _full((1, W))],
        out_shape=[S_((geo.T, 6 * W), F32), S_((CONV_W, W), F32), S_((1, W), F32)],
        scratch_shapes=[pltpu.VMEM((RT + 2 * HALO, W), F32)] * 2, compiler_params=_cp())(P, P, P, du, du, du, *pieces, cw)


def _rot_half(x, cos, sin):
    hw = x.shape[1] // 2
    x1, x2 = x[:, :hw], x[:, hw:]
    return jnp.concatenate([x1 * cos - x2 * sin, x1 * sin + x2 * cos], axis=-1)


def _ret_chunk(q, k, v, s, logit, cos, sin, rev):
    C, dk = q.shape
    lg = -jax.nn.softplus(-logit)
    q = _rot_half(q, cos, sin)
    k = _rot_half(k, cos, sin) * (dk ** -0.5)
    i = lax.broadcasted_iota(jnp.int32, (C, 1), 0).astype(F32)
    j = lax.broadcasted_iota(jnp.int32, (1, C), 1).astype(F32)
    if rev:
        diff, qe, ke = j - i, C - i, i
    else:
        diff, qe, ke = i - j, i + 1.0, C - 1.0 - i
    intra = jnp.where(diff >= 0, jnp.exp(lg * jnp.maximum(diff, 0.0)), 0.0)
    scores = _bdot(q, k, "nt") * intra
    o = _bdot(scores, v, "nn") + _bdot(q * jnp.exp(lg * qe), s, "nn")
    s_new = s * jnp.exp(lg * C) + _bdot(k * jnp.exp(lg * ke), v, "tn")
    return o, s_new


def _ret_specs(P, W, H, d, nlc, ncc, order_of_step):
    C = RET_CHUNK
    dk = W // H

    def cidx(k):
        return _scan_order(d, order_of_step(k), nlc, ncc)

    per_w = W // dk
    q = pl.BlockSpec((C, dk), lambda h, k: (cidx(k), 2 * per_w + h))
    kk = pl.BlockSpec((C, dk), lambda h, k: (cidx(k), 3 * per_w + h))
    v = pl.BlockSpec((C, dk), lambda h, k: (cidx(k), 4 * per_w + h))
    tab = pl.BlockSpec((C, dk // 2), lambda h, k: (cidx(k), 0))
    logit = pl.BlockSpec((None, 1, 1), lambda h, k: (h, 0, 0))
    o = pl.BlockSpec((C, dk), lambda h, k: (cidx(k), h))
    return q, kk, v, tab, logit, o


def _ret_fwd(geo, P, W, H, logit_d, cos, sin, d, name):
    C = RET_CHUNK
    dk = W // H
    nlc, ncc = geo.S // C, geo.LC // C
    nch = nlc + ncc
    rev = d == 1

    def body(q_ref, k_ref, v_ref, cos_ref, sin_ref, lg_ref, o_ref, st_ref, s_scr):
        k = pl.program_id(1)

        @pl.when(k == 0)
        def _():
            s_scr[...] = jnp.zeros(s_scr.shape, F32)

        st_ref[...] = s_scr[...]
        o, s_new = _ret_chunk(q_ref[...], k_ref[...], v_ref[...], s_scr[...], lg_ref[...], cos_ref[...], sin_ref[...], rev)
        o_ref[...] = o
        s_scr[...] = s_new

    q, kk, v, tab, logit, o = _ret_specs(P, W, H, d, nlc, ncc, lambda k: k)
    st = pl.BlockSpec((None, None, dk, dk), lambda h, k: (h, k, 0, 0))
    return pl.pallas_call(body, name=name, grid=(H, nch), in_specs=[q, kk, v, tab, tab, logit], out_specs=[o, st],
                          out_shape=[S_((geo.T, W), F32), S_((H, nch, dk, dk), F32)],
                          scratch_shapes=[pltpu.VMEM((dk, dk), F32)], compiler_params=_cp())(P, P, P, cos, sin, logit_d)


def _ret_bwd(geo, P, W, H, logit_d, cos, sin, states, do, prev, d, name):
    C = RET_CHUNK
    dk = W // H
    nlc, ncc = geo.S // C, geo.LC // C
    nch = nlc + ncc
    rev = d == 1
    np_ = 0 if prev is None else 3

    def body(*refs):
        q_ref, k_ref, v_ref, cos_ref, sin_ref, lg_ref, st_ref, do_ref = refs[:8]
        prevs = refs[8:8 + np_]
        dq_ref, dk_ref, dv_ref, dlg_ref, ds_scr = refs[8 + np_:]
        k = pl.program_id(1)

        @pl.when(k == 0)
        def _():
            ds_scr[...] = jnp.zeros(ds_scr.shape, F32)
            dlg_ref[...] = jnp.zeros(dlg_ref.shape, F32)

        cos, sin = cos_ref[...], sin_ref[...]
        _, vjp = jax.vjp(lambda a, b, c, s, lg: _ret_chunk(a, b, c, s, lg, cos, sin, rev),
                         q_ref[...], k_ref[...], v_ref[...], st_ref[...], lg_ref[...])
        dq, dkk, dv, ds, dlg = vjp((do_ref[...], ds_scr[...]))
        if prev is not None:
            dq, dkk, dv = dq + prevs[0][...], dkk + prevs[1][...], dv + prevs[2][...]
        dq_ref[...] = dq
        dk_ref[...] = dkk
        dv_ref[...] = dv
        ds_scr[...] = ds
        dlg_ref[...] += dlg

    q, kk, v, tab, logit, o = _ret_specs(P, W, H, d, nlc, ncc, lambda k: nch - 1 - k)
    st = pl.BlockSpec((None, None, dk, dk), lambda h, k: (h, nch - 1 - k, 0, 0))
    ins = [P, P, P, cos, sin, logit_d, states, do] + ([] if prev is None else list(prev))
    return pl.pallas_call(body, name=name, grid=(H, nch), in_specs=[q, kk, v, tab, tab, logit, st, o] + [o] * np_,
                          out_specs=[o, o, o, logit], out_shape=[S_((geo.T, W), F32)] * 3 + [S_((H, 1, 1), F32)],
                          scratch_shapes=[pltpu.VMEM((dk, dk), F32)], compiler_params=_cp())(*ins)


def _mix_even_math(g, h0, h1, o0, o1, og, gn, H):
    lru = jax.nn.gelu(g) * (h0 + h1)
    o = o0 + o1
    dv = o.shape[1] // H
    parts = []
    for h in range(H):
        oh = o[:, h * dv:(h + 1) * dv]
        mu = jnp.mean(oh, axis=-1, keepdims=True)
        var = jnp.mean(jnp.square(oh - mu), axis=-1, keepdims=True)
        parts.append((oh - mu) * lax.rsqrt(var + EPS))
    y = jnp.concatenate(parts, axis=-1) * gn
    return lru, y * jax.nn.silu(og)


def _mix_even_fwd(geo, P, W, H, h0, h1, o0, o1, gn, name):
    def body(g_ref, h0r, h1r, o0r, o1r, og_ref, gn_ref, m_ref):
        lru, ret = _mix_even_math(g_ref[...], h0r[...], h1r[...], o0r[...], o1r[...], og_ref[...], gn_ref[...], H)
        m_ref[:, :W] = lru.astype(BF)
        m_ref[:, W:] = ret.astype(BF)

    row = geo.row(W)
    return pl.pallas_call(body, name=name, grid=(geo.nt,), in_specs=[geo.row(W, 0), row, row, row, row, geo.row(W, 5), _full((1, W))],
                          out_specs=geo.row(2 * W), out_shape=S_((geo.T, 2 * W), BF), compiler_params=_cp())(P, h0, h1, o0, o1, P, gn)


def _mix_even_bwd(geo, P, W, H, h0, h1, o0, o1, gn, dmix, name):
    def body(g_ref, h0r, h1r, o0r, o1r, og_ref, gn_ref, dl_ref, dr_ref, dg_ref, dh_ref, do_ref, dog_ref, dgn_ref):
        i = pl.program_id(0)
        _, vjp = jax.vjp(lambda g, hs, os_, og, gn_: _mix_even_math(g, hs, 0.0, os_, 0.0, og, gn_, H),
                         g_ref[...], h0r[...] + h1r[...], o0r[...] + o1r[...], og_ref[...], gn_ref[...])
        dg, dh, do, dog, dgn = vjp((dl_ref[...], dr_ref[...]))
        dg_ref[...] = dg
        dh_ref[...] = dh
        do_ref[...] = do
        dog_ref[...] = dog
        _acc_rows(dgn_ref, i == 0, [dgn])

    row = geo.row(W)
    return pl.pallas_call(
        body, name=name, grid=(geo.nt,),
        in_specs=[geo.row(W, 0), row, row, row, row, geo.row(W, 5), _full((1, W)), geo.row(W, 0), geo.row(W, 1)],
        out_specs=[row] * 4 + [_full((1, W))], out_shape=[S_((geo.T, W), F32)] * 4 + [S_((1, W), F32)],
        compiler_params=_cp())(P, h0, h1, o0, o1, P, gn, dmix, dmix)


def _head_norm_rot(x, g, cosf, sinf):
    xn = _rms(x) * g
    return xn * cosf + _rollv(xn, x.shape[1] // 2, 1) * sinf


def _qk_prep_fwd(geo, P, dims, qg, kg, cosf, sinf, name):
    PW, QW, KW, hd = dims
    qscale = hd ** -0.5 * LOG2E

    def body(p_ref, qg_ref, kg_ref, c_ref, s_ref, q_ref, k_ref, v_ref):
        c, s = c_ref[...], s_ref[...]
        for h in range(QW // hd):
            qh = _head_norm_rot(p_ref[:, PW + h * hd:PW + (h + 1) * hd], qg_ref[...], c, s)
            q_ref[:, h * hd:(h + 1) * hd] = (qh * qscale).astype(BF)
        for h in range(KW // hd):
            o = PW + QW + h * hd
            k_ref[:, h * hd:(h + 1) * hd] = _head_norm_rot(p_ref[:, o:o + hd], kg_ref[...], c, s).astype(BF)
            v_ref[:, 2 * h * hd:(2 * h + 1) * hd] = p_ref[:, o + KW:o + KW + hd].astype(BF)
            v_ref[:, (2 * h + 1) * hd:(2 * h + 2) * hd] = jnp.ones((geo.RT, hd), BF)

    tot = PW + QW + 2 * KW
    return pl.pallas_call(
        body, name=name, grid=(geo.nt,),
        in_specs=[geo.row(tot), _full((1, hd)), _full((1, hd)), geo.row(hd), geo.row(hd)],
        out_specs=[geo.row(QW), geo.row(KW), geo.row(2 * KW)],
        out_shape=[S_((geo.T, QW), BF), S_((geo.T, KW), BF), S_((geo.T, 2 * KW), BF)], compiler_params=_cp())(P, qg, kg, cosf, sinf)


def _qk_prep_bwd(geo, P, dims, qg, kg, cosf, sinf, dpool, dq, dk, dv, name):
    PW, QW, KW, hd = dims

    def body(p_ref, qg_ref, kg_ref, c_ref, s_ref, dpool_ref, dq_ref, dk_ref, dv_ref, dP_ref, dqg_ref, dkg_ref):
        i = pl.program_id(0)
        c, s = c_ref[...], s_ref[...]
        dP_ref[:, :PW] = dpool_ref[...]
        f = lambda x, g: _head_norm_rot(x, g, c, s)
        dqg = jnp.zeros((1, hd), F32)
        for h in range(QW // hd):
            o = PW + h * hd
            _, vjp = jax.vjp(f, p_ref[:, o:o + hd], qg_ref[...])
            dx, dg = vjp(dq_ref[:, h * hd:(h + 1) * hd])
            dP_ref[:, o:o + hd] = dx
            dqg = dqg + dg
        dkg = jnp.zeros((1, hd), F32)
        for h in range(KW // hd):
            o = PW + QW + h * hd
            _, vjp = jax.vjp(f, p_ref[:, o:o + hd], kg_ref[...])
            dx, dg = vjp(dk_ref[:, h * hd:(h + 1) * hd])
            dP_ref[:, o:o + hd] = dx
            dkg = dkg + dg
        dP_ref[:, PW + QW + KW:] = dv_ref[...]
        _acc_rows(dqg_ref, i == 0, [dqg])
        _acc_rows(dkg_ref, i == 0, [dkg])

    tot = PW + QW + 2 * KW
    return pl.pallas_call(
        body, name=name, grid=(geo.nt,),
        in_specs=[geo.row(tot), _full((1, hd)), _full((1, hd)), geo.row(hd), geo.row(hd), geo.row(PW), geo.row(QW), geo.row(KW), geo.row(KW)],
        out_specs=[geo.row(tot), _full((1, hd)), _full((1, hd))],
        out_shape=[S_((geo.T, tot), F32), S_((1, hd), F32), S_((1, hd), F32)], compiler_params=_cp())(
            P, qg, kg, cosf, sinf, dpool, dq, dk, dv)


def _att_tiles(T):
    return _tile(T, 256, 16), _tile(T, 768, 16)


def _stack_heads(ref, G, hd):
    return jnp.concatenate([ref[:, g * hd:(g + 1) * hd] for g in range(G)], axis=0)


def _att_fwd(q, k, v1, hd, name):
    T, QW = q.shape
    KV = k.shape[1] // hd
    G = QW // hd // KV
    tq, tk = _att_tiles(T)
    nk = T // tk

    def body(q_ref, k_ref, v_ref, o_ref, lse_ref, m_scr, acc):
        ki = pl.program_id(2)

        @pl.when(ki == 0)
        def _():
            m_scr[...] = jnp.full(m_scr.shape, -jnp.inf, F32)
            acc[...] = jnp.zeros(acc.shape, F32)

        s = _dot(_stack_heads(q_ref, G, hd), k_ref[...], NT)
        m_prev = m_scr[...]
        m_new = jnp.maximum(m_prev, jnp.max(s, axis=-1, keepdims=True))
        p = jnp.exp2(s - jnp.tile(m_new, (1, tk // hd)))
        acc[...] = jnp.tile(jnp.exp2(m_prev - m_new), (1, 2)) * acc[...] + _dot(p, v_ref[...], NN)
        m_scr[...] = m_new

        @pl.when(ki == nk - 1)
        def _():
            a = acc[...]
            o = a[:, :hd] / a[:, hd:]
            lse = m_scr[...] + jnp.log2(a[:, hd:])
            for g in range(G):
                o_ref[:, g * hd:(g + 1) * hd] = o[g * tq:(g + 1) * tq].astype(BF)
                lse_ref[g] = lse[g * tq:(g + 1) * tq]

    return pl.pallas_call(
        body, name=name, grid=(KV, T // tq, nk),
        in_specs=[pl.BlockSpec((tq, G * hd), lambda a, i, j: (i, a)), pl.BlockSpec((tk, hd), lambda a, i, j: (j, a)),
                  pl.BlockSpec((tk, 2 * hd), lambda a, i, j: (j, a))],
        out_specs=[pl.BlockSpec((tq, G * hd), lambda a, i, j: (i, a)), pl.BlockSpec((G, tq, hd), lambda a, i, j: (a, i, 0))],
        out_shape=[S_((T, QW), BF), S_((QW // hd, T, hd), F32)],
        scratch_shapes=[pltpu.VMEM((G * tq, hd), F32), pltpu.VMEM((G * tq, 2 * hd), F32)],
        compiler_params=_cp())(q, k, v1)


def _att_delta(geo, o, dmix, PW, hd, name):
    QW = o.shape[1]
    nh = QW // hd

    def body(o_ref, do_ref, d_ref):
        for h in range(nh):
            sl = slice(h * hd, (h + 1) * hd)
            d = jnp.sum(o_ref[:, sl].astype(F32) * do_ref[:, PW + h * hd:PW + (h + 1) * hd], axis=-1, keepdims=True)
            d_ref[h] = jnp.broadcast_to(d, (geo.RT, hd))

    return pl.pallas_call(body, name=name, grid=(geo.nt,), in_specs=[geo.row(QW), geo.row(PW + QW)],
                          out_specs=pl.BlockSpec((nh, geo.RT, hd), lambda i: (0, i, 0)), out_shape=S_((nh, geo.T, hd), F32),
                          compiler_params=_cp())(o, dmix)


def _att_bwd(q, k, v1, dmix, PW, lse, delta, hd, name):
    T, QW = q.shape
    KW = k.shape[1]
    KV = KW // hd
    G = QW // hd // KV
    tq, tk = _att_tiles(T)
    nq, nk = T // tq, T // tk
    scale = hd ** -0.5
    pb = PW // hd

    def body(q_ref, k_ref, v_ref, *rest):
        do_refs, (lse_ref, dl_ref, dq_ref, dk_ref, dv_ref, dq_scr) = rest[:G], rest[G:]
        ki, qi = pl.program_id(1), pl.program_id(2)
        q3 = _stack_heads(q_ref, G, hd)
        do3 = jnp.concatenate([r[...] for r in do_refs], axis=0)
        lse = jnp.concatenate([lse_ref[g] for g in range(G)], axis=0)
        dl = jnp.concatenate([dl_ref[g] for g in range(G)], axis=0)
        kk = k_ref[...]
        p = jnp.exp2(_dot(q3, kk, NT) - jnp.tile(lse, (1, tk // hd)))
        ds = p * (_dot(do3, v_ref[:, :hd], NT) - jnp.tile(dl, (1, tk // hd)))
        pv = _dot(p, do3, TN)
        pk = _dot(ds, q3, TN)
        pq = _dot(ds, kk, NN)

        @pl.when(qi == 0)
        def _():
            dk_ref[...] = pk
            dv_ref[...] = pv

        @pl.when(qi > 0)
        def _():
            dk_ref[...] += pk
            dv_ref[...] += pv

        @pl.when(qi == nq - 1)
        def _():
            dk_ref[...] = dk_ref[...] * LN2

        @pl.when(ki == 0)
        def _():
            dq_scr[qi] = pq

        @pl.when(ki > 0)
        def _():
            dq_scr[qi] += pq

        @pl.when(ki == nk - 1)
        def _():
            full = dq_scr[qi]
            for g in range(G):
                dq_ref[:, g * hd:(g + 1) * hd] = full[g * tq:(g + 1) * tq] * scale

    qs = pl.BlockSpec((tq, G * hd), lambda a, j, i: (i, a))
    ks = pl.BlockSpec((tk, hd), lambda a, j, i: (j, a))
    vs = pl.BlockSpec((tk, 2 * hd), lambda a, j, i: (j, a))
    dos = [pl.BlockSpec((tq, hd), lambda a, j, i, g=g: (i, pb + a * G + g)) for g in range(G)]
    st = pl.BlockSpec((G, tq, hd), lambda a, j, i: (a, i, 0))
    dqs = pl.BlockSpec((tq, G * hd), lambda a, j, i: (jnp.where(j == nk - 1, i, 0), a))
    return pl.pallas_call(body, name=name, grid=(KV, nk, nq), in_specs=[qs, ks, vs] + dos + [st, st], out_specs=[dqs, ks, ks],
                          out_shape=[S_((T, QW), F32), S_((T, KW), F32), S_((T, KW), F32)],
                          scratch_shapes=[pltpu.VMEM((nq, G * tq, hd), F32)], compiler_params=_cp())(
                              q, k, v1, *([dmix] * G), lse, delta)


def _pool_cnt(pos, L, w):
    return (jnp.minimum(pos + w // 2, L) - jnp.maximum(pos - w // 2, 0)).astype(F32)


def _pool_mean(ext, gi, w, G, RT, cnt):
    acc = ext[pl.ds(HALO - w // 2, RT), gi * G:(gi + 1) * G]
    for off in range(-w // 2 + 1, w // 2):
        acc = acc + ext[pl.ds(HALO + off, RT), gi * G:(gi + 1) * G]
    return acc / cnt


def _pool_fwd(geo, P, PW, att, pw, ps, name):
    RT = geo.RT
    G = pw.shape[1]
    QW = att.shape[1]

    def body(pp, pc, pn, att_ref, pw_ref, ps_ref, m_ref, ext):
        i = pl.program_id(0)
        _fill_ext(geo, ext, pp, pc, pn, i)
        pos, L = geo.pos(i)
        for gi, w in enumerate(POOL_WINDOWS):
            sl = slice(gi * G, (gi + 1) * G)
            xm = _pool_mean(ext, gi, w, G, RT, _pool_cnt(pos, L, w)) - pc[:, sl]
            m_ref[:, sl] = (_dot(xm, pw_ref[gi], NN) * ps_ref[:, sl]).astype(BF)
        m_ref[:, PW:] = att_ref[...]

    return pl.pallas_call(
        body, name=name, grid=(geo.nt,),
        in_specs=[geo.prev(PW), geo.row(PW), geo.next(PW), geo.row(QW), _full(pw.shape), _full(ps.shape)],
        out_specs=geo.row(PW + QW), out_shape=S_((geo.T, PW + QW), BF),
        scratch_shapes=[pltpu.VMEM((RT + 2 * HALO, PW), F32)], compiler_params=_cp())(P, P, P, att, pw, ps)


def _pool_bwd(geo, P, PW, dmix, pw, ps, name):
    RT = geo.RT
    G = pw.shape[1]
    RE = RT + 2 * HALO

    def body(pp, pc, pn, dp_, dc, dn, pw_ref, ps_ref, dx_ref, dpw, dps, ext, extd, dmc):
        i = pl.program_id(0)
        _fill_ext(geo, ext, pp, pc, pn, i)
        _fill_ext(geo, extd, dp_, dc, dn, i)
        pos, L = geo.pos(i)
        r = lax.broadcasted_iota(jnp.int32, (RE, 1), 0)
        pos_e = pos[0:1, :] - HALO + r
        rows_s = []
        for gi, w in enumerate(POOL_WINDOWS):
            sl = slice(gi * G, (gi + 1) * G)
            xm = _pool_mean(ext, gi, w, G, RT, _pool_cnt(pos, L, w)) - pc[:, sl]
            pre = _dot(xm, pw_ref[gi], NN)
            dout = dc[:, sl]
            rows_s.append(jnp.sum(dout * pre, axis=0, keepdims=True))
            gw = _dot(xm, dout * ps_ref[:, sl], TN)

            @pl.when(i == 0)
            def _(gi=gi, gw=gw):
                dpw[gi] = gw

            @pl.when(i > 0)
            def _(gi=gi, gw=gw):
                dpw[gi] += gw

            dm_e = _dot(extd[:, sl] * ps_ref[:, sl], pw_ref[gi], NT)
            dmc[...] = dm_e / jnp.maximum(_pool_cnt(pos_e, L, w), 1.0)
            acc = -dm_e[HALO:HALO + RT]
            for off in range(-w // 2 + 1, w // 2 + 1):
                acc = acc + dmc[pl.ds(HALO + off, RT), :]
            dx_ref[:, sl] = acc
        _acc_rows(dps, i == 0, [jnp.concatenate(rows_s, axis=-1)])

    return pl.pallas_call(
        body, name=name, grid=(geo.nt,),
        in_specs=[geo.prev(PW), geo.row(PW), geo.next(PW), geo.prev(PW), geo.row(PW), geo.next(PW), _full(pw.shape), _full(ps.shape)],
        out_specs=[geo.row(PW), _full(pw.shape), _full(ps.shape)],
        out_shape=[S_((geo.T, PW), F32), S_(pw.shape, F32), S_(ps.shape, F32)],
        scratch_shapes=[pltpu.VMEM((RE, PW), F32), pltpu.VMEM((RE, PW), F32), pltpu.VMEM((RE, G), F32)],
        compiler_params=_cp())(P, P, P, dmix, dmix, dmix, pw, ps)


def _mod_fwd(A, mod_w, name):
    L, D, MC = mod_w.shape
    tn = _tile(MC, 768, 128)

    def body(a_ref, w_ref, o_ref):
        o_ref[...] = _dot(jax.nn.silu(a_ref[...]), w_ref[...], NN)

    return pl.pallas_call(body, name=name, grid=(L, MC // tn),
                          in_specs=[_full(A.shape), pl.BlockSpec((None, D, tn), lambda l, j: (l, 0, j))],
                          out_specs=pl.BlockSpec((None, 16, tn), lambda l, j: (l, 0, j)), out_shape=S_((L, 16, MC), F32),
                          compiler_params=_cp())(A, mod_w)


def _mod_bwd(A, DM, mod_w, name):
    L, D, MC = mod_w.shape
    tn = _tile(MC, 768, 128)
    nj = MC // tn

    def body(a_ref, dm_ref, w_ref, gw_ref, da_ref, acc):
        l, j = pl.program_id(0), pl.program_id(1)
        sa, vjp = jax.vjp(jax.nn.silu, a_ref[...])
        gw_ref[...] = _dot(sa, dm_ref[...], TN)
        part = _dot(dm_ref[...], w_ref[...], NT)
        first = jnp.logical_and(l == 0, j == 0)

        @pl.when(first)
        def _():
            acc[...] = part

        @pl.when(jnp.logical_not(first))
        def _():
            acc[...] += part

        @pl.when(jnp.logical_and(l == L - 1, j == nj - 1))
        def _():
            da_ref[...] = vjp(acc[...])[0]

    wspec = pl.BlockSpec((None, D, tn), lambda l, j: (l, 0, j))
    return pl.pallas_call(body, name=name, grid=(L, nj),
                          in_specs=[_full(A.shape), pl.BlockSpec((None, 16, tn), lambda l, j: (l, 0, j)), wspec],
                          out_specs=[wspec, _full(A.shape)], out_shape=[S_((L, D, MC), F32), S_(A.shape, F32)],
                          scratch_shapes=[pltpu.VMEM(A.shape, F32)], compiler_params=_cp())(A, DM, mod_w)


PACK_COLS = 1024


def _pack_rows(shape):
    n = 1
    for d in shape:
        n *= d
    return n, -(-n // (8 * PACK_COLS)) * 8


def _pack(arrs):
    parts = []
    for a in arrs:
        n, rows = _pack_rows(a.shape)
        parts.append(jnp.pad(a.reshape(-1).astype(F32), (0, rows * PACK_COLS - n)).reshape(rows, PACK_COLS))
    return jnp.concatenate(parts)


def _unpack(packed, shapes, lead=()):
    out, off = [], 0
    for s in shapes:
        n, rows = _pack_rows(s)
        blk = packed[..., off:off + rows, :].reshape(lead + (rows * PACK_COLS,))
        out.append(blk[..., :n].reshape(lead + tuple(s)))
        off += rows
    return out


def _unshard_last(g):
    g = jnp.moveaxis(g, 0, -2)
    return g.reshape(g.shape[:-2] + (g.shape[-2] * g.shape[-1],))


def _my_shard(a, me):
    n = a.shape[-1] // NDEV
    return lax.dynamic_slice_in_dim(a, me * n, n, axis=a.ndim - 1)


def _rot_tables(S, LC, dk, hd):
    t = jnp.arange(S, dtype=F32)
    n_r = dk // 2
    ang1 = t[:, None] * (RET_THETA ** (-jnp.arange(n_r, dtype=F32) / n_r))
    cos1 = jnp.concatenate([jnp.cos(ang1), jnp.ones((LC, n_r), F32)])
    sin1 = jnp.concatenate([jnp.sin(ang1), jnp.zeros((LC, n_r), F32)])
    n_ax = hd // 4
    f_ax = ROPE_THETA ** (-jnp.arange(n_ax, dtype=F32) / n_ax)
    row = jnp.floor(t / GRID_W)
    col = t - row * GRID_W
    ang2 = jnp.concatenate([row[:, None] * f_ax, col[:, None] * f_ax], axis=-1)
    c2, s2 = jnp.cos(ang2), jnp.sin(ang2)
    cosf = jnp.concatenate([jnp.concatenate([c2, c2], axis=-1), jnp.ones((LC, hd), F32)])
    sinf = jnp.concatenate([jnp.concatenate([-s2, s2], axis=-1), jnp.zeros((LC, hd), F32)])
    return cos1, sin1, cosf, sinf


SMALL = ("c_ctx", "mod_b", "norm_pre", "norm_post", "lru_conv_w", "lru_conv_b", "lru_wa", "lru_ba", "lru_wx", "lru_bx",
         "lru_lambda", "ret_decay_logit", "ret_gn", "pool_w", "pool_scale", "q_norm", "k_norm")
WEIGHTS = ("c_ctx", "mod_w", "mod_b", "norm_pre", "norm_post", "ffn_gate", "ffn_up", "ffn_down", "ev_w_in", "ev_w_out",
           "lru_conv_w", "lru_conv_b", "lru_wa", "lru_ba", "lru_wx", "lru_bx", "lru_lambda", "ret_decay_logit", "ret_gn",
           "od_w_in", "od_w_out", "pool_w", "pool_scale", "q_norm", "k_norm")
INPUTS = ("x", "c", "ctx") + WEIGHTS + ("loss_target",) + tuple("m_" + w for w in WEIGHTS) + tuple("v_" + w for w in WEIGHTS)


def _step(p):
    x, c, ctx = p["x"], p["c"], p["ctx"]
    _, S, D = x.shape
    LC = ctx.shape[1]
    geo = _Geo(S, LC, D)
    T = geo.T
    xi, yi, ci = _me()
    me = 4 * xi + 2 * yi + ci
    L = p["mod_w"].shape[0]
    assert L == 2
    W = p["lru_conv_b"].shape[-1]
    H = p["ret_decay_logit"].shape[-1]
    hd = p["q_norm"].shape[-1]
    G = p["pool_w"].shape[-1]
    PW = G * len(POOL_WINDOWS)
    od_mix = p["od_w_out"].shape[1] * NDEV
    od_in = p["od_w_in"].shape[2] * NDEV
    QW = od_mix - PW
    KW = (od_in - od_mix) // 2
    assert p["ev_w_in"].shape[2] * NDEV == 6 * W and p["ret_gn"].shape[-1] == W and p["ev_w_out"].shape[1] * NDEV == 2 * W
    odims = (PW, QW, KW, hd)
    cos1, sin1, cosf, sinf = _rot_tables(S, LC, W // H, hd)

    sh0 = [(D,), p["norm_pre"].shape, p["norm_post"].shape, p["lru_conv_w"].shape[1:], p["lru_ba"].shape[1:],
           p["lru_bx"].shape[1:], p["lru_lambda"].shape[1:], p["pool_scale"].shape[1:]]
    pack0 = _pack([c, p["norm_pre"], p["norm_post"], p["lru_conv_w"], p["lru_ba"], p["lru_bx"], p["lru_lambda"], p["pool_scale"]])
    (g0,) = _all_gather([pack0], "gather_small")
    c_all, npre, npost, conv_w, ba, bx, lam, pscale = _unpack(g0, sh0, (NDEV,))
    npre, npost, conv_w, ba, bx, lam, pscale = [_unshard_last(a) for a in (npre, npost, conv_w, ba, bx, lam, pscale)]
    pscale = pscale[None]
    conv_b = p["lru_conv_b"]
    wa, wx = p["lru_wa"][0], p["lru_wx"][0]
    gn = p["ret_gn"]
    logits = p["ret_decay_logit"][0].reshape(2, H, 1, 1)
    pool_w = p["pool_w"][0]
    qg, kg = p["q_norm"], p["k_norm"]

    A = jnp.concatenate([c_all, p["c_ctx"][None], jnp.zeros((7, D), F32)])
    M = _mod_fwd(A, p["mod_w"], "mod_fwd")
    (Mg,) = _all_gather([M], "gather_mod")
    MC = M.shape[2]
    tabs = []
    for l in range(L):
        ml = lax.dynamic_index_in_dim(Mg[:, l], me, axis=1, keepdims=False).reshape(NDEV * MC) + p["mod_b"][l]
        mc = Mg[:, l, 8].reshape(NDEV * MC) + p["mod_b"][l]
        tabs.append(jnp.stack([ml.reshape(9, D), mc.reshape(9, D)]))

    def cast2(a, name):
        return _cast_bf16(a.reshape(-1, a.shape[-1]), name).reshape(a.shape)

    loc = {(l, j): [cast2(p[n][l, j], f"cast_{n}_{l}{j}") for n in ("ffn_gate", "ffn_up", "ffn_down")]
           for l in range(L) for j in range(2)}
    loc["ev"] = [cast2(p["ev_w_in"][0], "cast_ev_in"), cast2(p["ev_w_out"][0], "cast_ev_out")]
    loc["od"] = [cast2(p["od_w_in"][0], "cast_od_in"), cast2(p["od_w_out"][0], "cast_od_out")]

    def gather_start(key, name, after):
        return _xchg_start(True, loc[key], [jnp.broadcast_to(a[None], (NDEV,) + a.shape) for a in loc[key]], name, [after])

    ffn_w = {(0, 0): _all_gather(loc[0, 0], "gather_ffn_00")}

    def gp(a, l, s):
        return a[l, s][None]

    st, tok = gather_start("ev", "gs_ev", ffn_w[0, 0][0])
    x0 = jnp.concatenate([x[0], ctx[0]])
    h0, h0t = _norm_fwd(geo, x0, None, (_tie(gp(npre, 0, 0), tok), tabs[0], 0), "pre_00")
    y0, G0, U0 = _ffn_fwd(h0, *ffn_w[0, 0], name="ffn_fwd_00")
    x1, h1, h1t = _norm_fwd(geo, x0, (y0, gp(npost, 0, 0), tabs[0], 0, FFN_STEP), (gp(npre, 0, 1), tabs[0], 1), "post_00")
    ev_in, ev_out = _xchg_wait(st, h1, "gw_ev")
    ev_out_f = ev_out.reshape(2 * W, D)

    st, tok = gather_start((0, 1), "gs_ffn_01", h1)
    Pe = _mm_cols(h1, ev_in, "ev_in", dep=tok)
    u, a0, b0, a1, b1 = _lru_coef_fwd(geo, Pe, W, conv_w, conv_b, wa, ba, wx, bx, lam, "lru_coef")
    hs0, hp0 = _lru_scan_fwd(geo, a0, b0, 0, "lru_scan_f0")
    hs1, hp1 = _lru_scan_fwd(geo, a1, b1, 1, "lru_scan_f1")
    o0, st0 = _ret_fwd(geo, Pe, W, H, logits[0], cos1, sin1, 0, "ret_f0")
    o1, st1 = _ret_fwd(geo, Pe, W, H, logits[1], cos1, sin1, 1, "ret_f1")
    mixe = _mix_even_fwd(geo, Pe, W, H, hs0, hs1, o0, o1, gn, "mix_even")
    y1 = _mm_full(mixe, ev_out_f, NN, "ev_out")
    x2, h2, h2t = _norm_fwd(geo, x1, (y1, gp(npost, 0, 1), tabs[0], 1, 1.0), (gp(npre, 0, 2), tabs[0], 2), "post_01")
    ffn_w[0, 1] = _xchg_wait(st, h2, "gw_ffn_01")

    st, tok = gather_start((1, 0), "gs_ffn_10", h2)
    y2, G2, U2 = _ffn_fwd(h2, *ffn_w[0, 1], name="ffn_fwd_01", dep=tok)
    x3, h3, h3t = _norm_fwd(geo, x2, (y2, gp(npost, 0, 2), tabs[0], 2, FFN_STEP), (gp(npre, 1, 0), tabs[1], 0), "post_02")
    ffn_w[1, 0] = _xchg_wait(st, h3, "gw_ffn_10")

    st, tok = gather_start("od", "gs_od", h3)
    y3, G3, U3 = _ffn_fwd(h3, *ffn_w[1, 0], name="ffn_fwd_10", dep=tok)
    x4, h4, h4t = _norm_fwd(geo, x3, (y3, gp(npost, 1, 0), tabs[1], 0, FFN_STEP), (gp(npre, 1, 1), tabs[1], 1), "post_10")
    od_inw, od_out = _xchg_wait(st, h4, "gw_od")
    od_out_f = od_out.reshape(od_mix, D)

    st, tok = gather_start((1, 1), "gs_ffn_11", h4)
    Po = _mm_cols(h4, od_inw, "od_in", dep=tok)
    qr, kr, vr = _qk_prep_fwd(geo, Po, odims, qg, kg, cosf, sinf, "qk_prep")
    att, lse = _att_fwd(qr, kr, vr, hd, "att_fwd")
    mixo = _pool_fwd(geo, Po, PW, att, pool_w, pscale, "pool_fwd")
    y4 = _mm_full(mixo, od_out_f, NN, "od_out")
    x5, h5, h5t = _norm_fwd(geo, x4, (y4, gp(npost, 1, 1), tabs[1], 1, 1.0), (gp(npre, 1, 2), tabs[1], 2), "post_11")
    ffn_w[1, 1] = _xchg_wait(st, h5, "gw_ffn_11")

    y5, G5, U5 = _ffn_fwd(h5, *ffn_w[1, 1], name="ffn_fwd_11")
    (x6,) = _norm_fwd(geo, x5, (y5, gp(npost, 1, 2), tabs[1], 2, FFN_STEP), None, "post_12")

    big_g = {}
    tokbox = [None]

    def gpt(l, s):
        return _tie(gp(npre, l, s), tokbox[0])

    def a2a_start(key, srcs, name, dh):
        own = [lax.dynamic_index_in_dim(a, me, 0, keepdims=False) for a in srcs]
        state, token = _xchg_start(False, srcs, [jnp.zeros(a.shape, a.dtype) for a in srcs], name)
        big_g[key] = (state, own)
        tokbox[0] = token
        return dh

    def ffn_bwd(dy, h, Gs, Us, key):
        tag = f"{key[0]}{key[1]}"
        dh, dG, dU, Aact = _ffn_bwd_act(dy, Gs, Us, *ffn_w[key], name=f"ffn_bwd_{tag}")
        srcs = [_ffn_wgrad_in(h, dG, f"ffn_wg_{tag}"), _ffn_wgrad_in(h, dU, f"ffn_wu_{tag}"), _ffn_wgrad_out(Aact, dy, f"ffn_wd_{tag}")]
        return a2a_start(key, srcs, f"as_ffn_{tag}", dh)

    loss_p, dx6, dy5, dpost5 = _loss_bwd(geo, x6, p["loss_target"][0], (y5, gp(npost, 1, 2), tabs[1], 2, FFN_STEP), "loss")
    dh5 = ffn_bwd(dy5, h5t, G5, U5, (1, 1))
    dx5, dy4, dpre5, dpost4 = _norm_bwd(geo, dx6, dh5, x5, (gpt(1, 2), tabs[1], 2),
                                        (y4, gp(npost, 1, 1), tabs[1], 1, 1.0), "nb_5")

    dmixo = _mm_full(dy4, od_out_f, NT, "od_out_d")
    g_od_out = _mm_tn_rows(mixo, dy4, NDEV, "od_out_w")
    dpool, g_pool_w, g_pscale = _pool_bwd(geo, Po, PW, dmixo, pool_w, pscale, "pool_bwd")
    delta = _att_delta(geo, att, dmixo, PW, hd, "att_delta")
    dq, dk, dv = _att_bwd(qr, kr, vr, dmixo, PW, lse, delta, hd, "att_bwd")
    dPo, g_qn, g_kn = _qk_prep_bwd(geo, Po, odims, qg, kg, cosf, sinf, dpool, dq, dk, dv, "qk_prep_bwd")
    dh4 = _mm_nt_cols(dPo, od_inw, "od_in_d")
    g_od_in = _mm_tn_cols(h4t, dPo, NDEV, "od_in_w")
    dh4 = a2a_start("od", [g_od_in, g_od_out], "as_od", dh4)
    dx4, dy3, dpre4, dpost3 = _norm_bwd(geo, dx5, dh4, x4, (gpt(1, 1), tabs[1], 1),
                                        (y3, gp(npost, 1, 0), tabs[1], 0, FFN_STEP), "nb_4")

    dh3 = ffn_bwd(dy3, h3t, G3, U3, (1, 0))
    dx3, dy2, dpre3, dpost2 = _norm_bwd(geo, dx4, dh3, x3, (gpt(1, 0), tabs[1], 0),
                                        (y2, gp(npost, 0, 2), tabs[0], 2, FFN_STEP), "nb_3")

    dh2 = ffn_bwd(dy2, h2t, G2, U2, (0, 1))
    dx2, dy1, dpre2, dpost1 = _norm_bwd(geo, dx3, dh2, x2, (gpt(0, 2), tabs[0], 2),
                                        (y1, gp(npost, 0, 1), tabs[0], 1, 1.0), "nb_2")

    dmixe = _mm_full(dy1, ev_out_f, NT, "ev_out_d")
    g_ev_out = _mm_tn_rows(mixe, dy1, NDEV, "ev_out_w")
    dg, dhs, dos, dog, g_gn = _mix_even_bwd(geo, Pe, W, H, hs0, hs1, o0, o1, gn, dmixe, "mix_even_bwd")
    da0, db0 = _lru_scan_bwd(geo, dhs, a0, hp0, 0, "lru_scan_b0")
    da1, db1 = _lru_scan_bwd(geo, dhs, a1, hp1, 1, "lru_scan_b1")
    du, g_wa, g_ba, g_wx, g_bx, g_lam = _lru_coef_bwd(geo, u, (da0, da1), (db0, db1), W, wa, ba, wx, bx, lam, "lru_coef_bwd")
    dq0, dk0, dv0, glg0 = _ret_bwd(geo, Pe, W, H, logits[0], cos1, sin1, st0, dos, None, 0, "ret_b0")
    dqe, dke, dve, glg1 = _ret_bwd(geo, Pe, W, H, logits[1], cos1, sin1, st1, dos, (dq0, dk0, dv0), 1, "ret_b1")
    dPe, g_cw, g_cb = _conv_bwd_assemble(geo, Pe, du, (dg, dqe, dke, dve, dog), W, conv_w, "conv_bwd")
    dh1 = _mm_nt_cols(dPe, ev_in, "ev_in_d")
    g_ev_in = _mm_tn_cols(h1t, dPe, NDEV, "ev_in_w")
    dh1 = a2a_start("ev", [g_ev_in, g_ev_out], "as_ev", dh1)
    dx1, dy0, dpre1, dpost0 = _norm_bwd(geo, dx2, dh1, x1, (gpt(0, 1), tabs[0], 1),
                                        (y0, gp(npost, 0, 0), tabs[0], 0, FFN_STEP), "nb_1")

    dh0 = ffn_bwd(dy0, h0t, G0, U0, (0, 0))
    dx0, dpre0 = _norm_bwd(geo, dx1, dh0, x0, (gpt(0, 0), tabs[0], 0), None, "nb_0")

    dpre = [[dpre0, dpre1, dpre2], [dpre3, dpre4, dpre5]]
    dpost = [[dpost0, dpost1, dpost2], [dpost3, dpost4, dpost5]]
    dtab = jnp.stack([jnp.stack([jnp.stack([dpre[l][s][:, 1], dpre[l][s][:, 2], dpost[l][s][:, 1]], axis=1) for s in range(3)], axis=1)
                      for l in range(L)])
    dtab_p = _pack([jnp.moveaxis(dtab.reshape(L, 2, 9 * D), 1, 0)])
    (dtab_g,) = _all_gather([dtab_p], "gather_dtab")
    dtab_sum = _sum_n(dtab_g, "sum_dtab")
    (dm_all,) = _unpack(dtab_g, [(2, L, 9 * D)], (NDEV,))
    (dm_sum,) = _unpack(dtab_sum, [(2, L, 9 * D)])
    (g_mod_b,) = _unpack(_sum_n(jnp.stack([_pack([dm_sum[0]]), _pack([dm_sum[1]])]), "sum_mod_b"), [(L, 9 * D)])
    dml = lax.dynamic_slice_in_dim(dm_all[:, 0], me * MC, MC, axis=2)
    dmc = lax.dynamic_slice_in_dim(dm_sum[1], me * MC, MC, axis=1)
    DM = jnp.concatenate([jnp.moveaxis(dml, 0, 1), dmc[:, None], jnp.zeros((L, 7, MC), F32)], axis=1)
    g_mod_w, dA = _mod_bwd(A, DM, p["mod_w"], "mod_bwd")

    g_npre = jnp.stack([jnp.stack([dpre[l][s][0, 0] + dpre[l][s][1, 0] for s in range(3)]) for l in range(L)])
    g_npost = jnp.stack([jnp.stack([dpost[l][s][0, 0] + dpost[l][s][1, 0] for s in range(3)]) for l in range(L)])
    g_logit = jnp.stack([glg0.reshape(H), glg1.reshape(H)])
    small_parts = [dA[8], g_npre, g_npost, g_cw, g_cb, g_wa, g_ba, g_wx, g_bx, g_lam, g_logit, g_gn, g_pool_w, g_pscale, g_qn, g_kn]
    (sg,) = _all_gather([_pack(small_parts)], "gather_small_g")
    ssum = _unpack(_sum_n(sg, "sum_small_g"), [a.shape for a in small_parts])
    (g_cctx, g_npre, g_npost, g_cw, g_cb, g_wa, g_ba, g_wx, g_bx, g_lam, g_logit, g_gn, g_pool_w, g_pscale, g_qn, g_kn) = ssum
    small_g = {
        "c_ctx": g_cctx, "mod_b": g_mod_b, "norm_pre": _my_shard(g_npre, me), "norm_post": _my_shard(g_npost, me),
        "lru_conv_w": _my_shard(g_cw, me)[None], "lru_conv_b": g_cb, "lru_wa": g_wa[None], "lru_ba": _my_shard(g_ba, me)[None],
        "lru_wx": g_wx[None], "lru_bx": _my_shard(g_bx, me)[None], "lru_lambda": _my_shard(g_lam, me)[None],
        "ret_decay_logit": g_logit[None], "ret_gn": g_gn, "pool_w": g_pool_w[None], "pool_scale": _my_shard(g_pscale, me),
        "q_norm": g_qn, "k_norm": g_kn,
    }
    shapes = [p[n].shape for n in SMALL]
    s_out = _reduce_adam(_pack([small_g[n] for n in SMALL])[None], _pack([p[n] for n in SMALL]),
                         _pack([p["m_" + n] for n in SMALL]), _pack([p["v_" + n] for n in SMALL]), "adam_small")
    res = {}
    for kind, packed in zip(("g", "d", "m", "v"), s_out):
        for n, a in zip(SMALL, _unpack(packed, shapes)):
            res[kind, n] = a

    def big(name, pieces, own, idx=None):
        w, m, v = p[name], p["m_" + name], p["v_" + name]
        if idx is not None:
            w, m, v = w[idx], m[idx], v[idx]
        shp = w.shape
        tag = name + ("" if idx is None else "_" + "".join(str(i) for i in idx))
        outs = _reduce_adam(pieces.reshape((pieces.shape[0], -1, shp[-1])), w.reshape(-1, shp[-1]), m.reshape(-1, shp[-1]),
                            v.reshape(-1, shp[-1]), "adam_" + tag, None if own is None else own.reshape(-1, shp[-1]))
        return [o.reshape(shp) for o in outs]

    got = {}
    after = [dx0]
    for key in ((1, 1), "od", (1, 0), (0, 1), "ev", (0, 0)):
        state, own = big_g[key]
        tag = key if isinstance(key, str) else f"ffn_{key[0]}{key[1]}"
        if key == (0, 0):
            for name, (pieces, own_) in (("mod_w", (g_mod_w[None], None)), ("ev_w_in", got["ev"][0]), ("ev_w_out", got["ev"][1]),
                                         ("od_w_in", got["od"][0]), ("od_w_out", got["od"][1])):
                outs = big(name, pieces, own_, None if name == "mod_w" else (0,))
                for kind, o in zip(("g", "d", "m", "v"), outs):
                    res[kind, name] = o if name == "mod_w" else o[None]
            after = [s_out[0]] + [res["g", n] for n in ("mod_w", "ev_w_in", "ev_w_out", "od_w_in", "od_w_out")]
        lands = _xchg_wait(state, after, "aw_" + tag)
        got[key] = list(zip(lands, own))
        after = [lands[0]]

    keys = [(l, j) for l in range(L) for j in range(2)]
    for wi, name in enumerate(("ffn_gate", "ffn_up", "ffn_down")):
        shp = p[name].shape
        st3 = (len(keys), shp[-2], shp[-1])
        outs = _reduce_adam_stack([got[k][wi][0] for k in keys], [got[k][wi][1] for k in keys], p[name].reshape(st3),
                                  p["m_" + name].reshape(st3), p["v_" + name].reshape(st3), "adam_" + name)
        for kind, o in zip(("g", "d", "m", "v"), outs):
            res[kind, name] = o.reshape(shp)

    loss = lax.psum(loss_p[0, 0], ("x", "y", "c"))
    grad_x = dx0[:S][None]
    return (loss, grad_x) + tuple(res[kind, n] for kind in ("g", "d", "m", "v") for n in WEIGHTS)


def kernel(
        x, c, ctx, c_ctx, mod_w, mod_b, norm_pre, norm_post, ffn_gate, ffn_up, ffn_down, ev_w_in, ev_w_out, lru_conv_w,
        lru_conv_b, lru_wa, lru_ba, lru_wx, lru_bx, lru_lambda, ret_decay_logit, ret_gn, od_w_in, od_w_out, pool_w,
        pool_scale, q_norm, k_norm, loss_target, m_c_ctx, m_mod_w, m_mod_b, m_norm_pre, m_norm_post, m_ffn_gate, m_ffn_up,
        m_ffn_down, m_ev_w_in, m_ev_w_out, m_lru_conv_w, m_lru_conv_b, m_lru_wa, m_lru_ba, m_lru_wx, m_lru_bx, m_lru_lambda,
        m_ret_decay_logit, m_ret_gn, m_od_w_in, m_od_w_out, m_pool_w, m_pool_scale, m_q_norm, m_k_norm, v_c_ctx, v_mod_w,
        v_mod_b, v_norm_pre, v_norm_post, v_ffn_gate, v_ffn_up, v_ffn_down, v_ev_w_in, v_ev_w_out, v_lru_conv_w,
        v_lru_conv_b, v_lru_wa, v_lru_ba, v_lru_wx, v_lru_bx, v_lru_lambda, v_ret_decay_logit, v_ret_gn, v_od_w_in,
        v_od_w_out, v_pool_w, v_pool_scale, v_q_norm, v_k_norm):
    args = locals()
    return _step({n: args[n] for n in INPUTS})
```

```python
import functools

import jax
import jax.numpy as jnp
from jax import lax
from jax.experimental import pallas as pl
from jax.experimental.pallas import tpu as pltpu

F32 = jnp.float32
BF = jnp.bfloat16
S_ = jax.ShapeDtypeStruct
MESH = pl.DeviceIdType.MESH

NDEV = 8
EPS = 1e-6
FFN_STEP = 0.5
LRU_C = 8.0
RET_CHUNK = 128
RET_THETA = 10000.0
ROPE_THETA = 10000.0
GRID_W = 64
POOL_WINDOWS = (2, 4, 8, 16)
ROW_TILE = 256
HALO = 8
VMEM_LIMIT = 58 * 1024 * 1024
FFN_FWD_ROWS = 768
FFN_BWD_ROWS = 528
WGRAD_ROWS = 1408

ADAM_LR = 0.001
ADAM_B1 = 0.9
ADAM_B2 = 0.999
ADAM_EPS = 1e-08
ADAM_WD = 0.01
ADAM_STEP = 10

LOG2E = 1.4426950408889634
LN2 = 0.6931471805599453

NN = ((1,), (0,))
NT = ((1,), (1,))
TN = ((0,), (0,))


def _dot(a, b, dn):
    return lax.dot_general(a.astype(BF), b.astype(BF), (dn, ((), ())), preferred_element_type=F32)


@functools.partial(jax.custom_vjp, nondiff_argnums=(2,))
def _bdot(a, b, mode):
    return _dot(a, b, {"nn": NN, "nt": NT, "tn": TN}[mode])


def _bdot_fwd(a, b, mode):
    return _bdot(a, b, mode), (a, b)


def _bdot_bwd(mode, res, g):
    a, b = res
    if mode == "nn":
        return _dot(g, b, NT), _dot(a, g, TN)
    if mode == "nt":
        return _dot(g, b, NN), _dot(g, a, TN)
    return _dot(b, g, NT), _dot(a, g, NN)


_bdot.defvjp(_bdot_fwd, _bdot_bwd)


@functools.partial(jax.custom_vjp, nondiff_argnums=(1, 2))
def _rollv(x, shift, axis):
    return pltpu.roll(x, shift, axis)


def _rollv_fwd(x, shift, axis):
    return pltpu.roll(x, shift, axis), None


def _rollv_bwd(shift, axis, _, g):
    n = g.shape[axis]
    return (pltpu.roll(g, (n - shift) % n, axis),)


_rollv.defvjp(_rollv_fwd, _rollv_bwd)


def _cp(vmem=VMEM_LIMIT):
    return pltpu.CompilerParams(vmem_limit_bytes=vmem)


def _tile(n, pref, mult=8):
    if n <= pref:
        return n
    for t in range(pref, 0, -1):
        if n % t == 0 and t % mult == 0:
            return t
    return n


def _full(shape):
    nd = len(shape)
    return pl.BlockSpec(tuple(shape), lambda *_: (0,) * nd)


def _me():
    return lax.axis_index("x"), lax.axis_index("y"), lax.axis_index("c")


def _all_gather(arrs, name):
    n = len(arrs)

    def body(*refs):
        xs, outs = refs[:n], refs[n:2 * n]
        send_sems, recv_sems, local_sems = refs[2 * n:]
        x, y, c = _me()
        me, sibling = (x, y, c), (x, y, 1 - c)
        chips = [(1 - x, y), (x, 1 - y), (1 - x, 1 - y)]

        def blk(out, p):
            return out.at[4 * p[0] + 2 * p[1] + p[2]]

        def copy(a, k, block, to, src=None):
            return pltpu.make_async_remote_copy(
                src_ref=blk(outs[a], block) if src is None else src, dst_ref=blk(outs[a], block),
                send_sem=send_sems.at[a, k], recv_sem=recv_sems.at[a, k], device_id=to, device_id_type=MESH)

        mine = [pltpu.make_async_copy(xs[a], blk(outs[a], me), local_sems.at[a]) for a in range(n)]
        for cp in mine:
            cp.start()
        first = []
        for a in range(n):
            first.append(copy(a, 0, me, sibling, src=xs[a]))
            first += [copy(a, 1 + j, me, (*chip, c), src=xs[a]) for j, chip in enumerate(chips)]
        for cp in first:
            cp.start()
        passed = []
        for j, chip in enumerate(chips):
            for a in range(n):
                copy(a, 1 + j, (*chip, c), me).wait_recv()
                fw = copy(a, 4 + j, (*chip, c), sibling)
                fw.start()
                passed.append(fw)
        for a in range(n):
            copy(a, 0, sibling, me).wait_recv()
            for j, chip in enumerate(chips):
                copy(a, 4 + j, (*chip, 1 - c), me).wait_recv()
        for cp in first + passed:
            cp.wait_send()
        for cp in mine:
            cp.wait()

    anyspec = pl.BlockSpec(memory_space=pl.ANY)
    return pl.pallas_call(
        body, name=name,
        out_shape=[S_((NDEV,) + a.shape, a.dtype) for a in arrs],
        in_specs=[anyspec] * n, out_specs=[anyspec] * n,
        scratch_shapes=[pltpu.SemaphoreType.DMA((n, 7)), pltpu.SemaphoreType.DMA((n, 7)), pltpu.SemaphoreType.DMA((n,))],
    )(*arrs)


def _all_to_all(arrs, name):
    n = len(arrs)

    def body(*refs):
        xs, outs = refs[:n], refs[n:2 * n]
        send_sems, recv_sems, local_sems = refs[2 * n:]
        x, y, c = _me()
        me_idx = 4 * x + 2 * y + c
        mine = [pltpu.make_async_copy(xs[a].at[me_idx], outs[a].at[me_idx], local_sems.at[a]) for a in range(n)]
        for cp in mine:
            cp.start()
        copies = []
        for k in range(1, NDEV):
            kx, ky, kc = (k >> 2) & 1, (k >> 1) & 1, k & 1
            px = 1 - x if kx else x
            py = 1 - y if ky else y
            pc = 1 - c if kc else c
            p_idx = 4 * px + 2 * py + pc
            for a in range(n):
                copies.append(pltpu.make_async_remote_copy(
                    src_ref=xs[a].at[p_idx], dst_ref=outs[a].at[me_idx],
                    send_sem=send_sems.at[a, k - 1], recv_sem=recv_sems.at[a, k - 1],
                    device_id=(px, py, pc), device_id_type=MESH))
        for cp in copies:
            cp.start()
        for cp in copies:
            cp.wait_recv()
        for cp in copies:
            cp.wait_send()
        for cp in mine:
            cp.wait()

    anyspec = pl.BlockSpec(memory_space=pl.ANY)
    return pl.pallas_call(
        body, name=name,
        out_shape=[S_(a.shape, a.dtype) for a in arrs],
        in_specs=[anyspec] * n, out_specs=[anyspec] * n,
        scratch_shapes=[pltpu.SemaphoreType.DMA((n, 7)), pltpu.SemaphoreType.DMA((n, 7)), pltpu.SemaphoreType.DMA((n,))],
    )(*arrs)


HBM_SPEC = pl.BlockSpec(memory_space=pltpu.HBM)
SEM_SPEC = pl.BlockSpec(memory_space=pltpu.SEMAPHORE)
EFFECT = pltpu.SideEffectType.DATAFLOW_SIDE_EFFECTING


def _peers():
    x, y, c = _me()
    out = []
    for k in range(1, NDEV):
        px = 1 - x if (k >> 2) & 1 else x
        py = 1 - y if (k >> 1) & 1 else y
        pc = 1 - c if k & 1 else c
        out.append(((px, py, pc), 4 * px + 2 * py + pc))
    return out, 4 * x + 2 * y + c


def _xchg_copies(gather, xs, lands, send, recv):
    peers, me_idx = _peers()
    out = []
    for k, (dev, p_idx) in enumerate(peers):
        for a in range(len(xs)):
            out.append(pltpu.make_async_remote_copy(
                src_ref=xs[a] if gather else xs[a].at[p_idx], dst_ref=lands[a].at[me_idx],
                send_sem=send[a].at[k], recv_sem=recv[a].at[k], device_id=dev, device_id_type=MESH))
    return out


def _xchg_start(gather, xs, lands, name, after=()):
    n = len(xs)
    na = len(after)

    def body(*refs):
        xr, lr = refs[:n], refs[n:2 * n]
        outs = refs[2 * n + na:]
        for cp in _xchg_copies(gather, xr, lr, outs[:n], outs[n:2 * n]):
            cp.start()
        outs[4 * n][...] = jnp.zeros(outs[4 * n].shape, F32)

    ops = [pltpu.with_memory_space_constraint(a, pltpu.HBM) for a in list(xs) + list(lands)]
    outs = pl.pallas_call(
        body, name=name,
        out_shape=[pltpu.SemaphoreType.DMA((NDEV - 1,))] * (2 * n) + [pltpu.HBM(a.shape, a.dtype) for a in ops]
        + [S_((8, 128), F32)],
        in_specs=[HBM_SPEC] * (2 * n) + [pl.BlockSpec(memory_space=pl.ANY)] * na,
        out_specs=[SEM_SPEC] * (2 * n) + [HBM_SPEC] * (2 * n) + [pl.BlockSpec(memory_space=pltpu.VMEM)],
        input_output_aliases={i: 2 * n + i for i in range(2 * n)},
        compiler_params=pltpu.CompilerParams(has_side_effects=EFFECT),
    )(*ops, *after)
    return (gather, n, outs[:4 * n]), outs[4 * n]


def _xchg_wait(state, after, name):
    gather, n, st = state
    send, recv, xs, lands = st[:n], st[n:2 * n], st[2 * n:3 * n], st[3 * n:4 * n]
    after = list(after) if isinstance(after, (list, tuple)) else [after]

    def body(*refs):
        xr, lr = refs[:n], refs[n:2 * n]
        sr, rr = refs[2 * n:3 * n], refs[3 * n:4 * n]
        for cp in _xchg_copies(gather, xr, lr, sr, rr):
            cp.wait_send()
            cp.wait_recv()

    outs = pl.pallas_call(
        body, name=name,
        out_shape=[pltpu.HBM(a.shape, a.dtype) for a in list(xs) + list(lands)],
        in_specs=[HBM_SPEC] * (2 * n) + [SEM_SPEC] * (2 * n) + [pl.BlockSpec(memory_space=pl.ANY)] * len(after),
        out_specs=[HBM_SPEC] * (2 * n), input_output_aliases={i: i for i in range(2 * n)},
        compiler_params=pltpu.CompilerParams(has_side_effects=EFFECT),
    )(*xs, *lands, *send, *recv, *after)
    return outs[n:]


def _tie(a, token):
    return a + token[0, 0].astype(a.dtype)


def _cast_bf16(a, name):
    R, C = a.shape
    tr = _tile(R, 512, 16)

    def body(a_ref, o_ref):
        o_ref[...] = a_ref[...].astype(BF)

    return pl.pallas_call(body, name=name, grid=(R // tr,), in_specs=[pl.BlockSpec((tr, C), lambda i: (i, 0))],
                          out_specs=pl.BlockSpec((tr, C), lambda i: (i, 0)), out_shape=S_((R, C), BF), compiler_params=_cp())(a)


def _sum_n(a, name):
    n, R, C = a.shape
    tr = _tile(R, 256, 8)

    def body(a_ref, o_ref):
        acc = a_ref[0].astype(F32)
        for i in range(1, n):
            acc = acc + a_ref[i].astype(F32)
        o_ref[...] = acc

    return pl.pallas_call(body, name=name, grid=(R // tr,), in_specs=[pl.BlockSpec((n, tr, C), lambda i: (0, i, 0))],
                          out_specs=pl.BlockSpec((tr, C), lambda i: (i, 0)), out_shape=S_((R, C), F32), compiler_params=_cp())(a)


def _adam_math(w, g, m, v):
    m = ADAM_B1 * m + (1.0 - ADAM_B1) * g
    v = ADAM_B2 * v + (1.0 - ADAM_B2) * jnp.square(g)
    m_hat = m / (1.0 - ADAM_B1 ** ADAM_STEP)
    v_hat = v / (1.0 - ADAM_B2 ** ADAM_STEP)
    delta = -ADAM_LR * (m_hat / (jnp.sqrt(v_hat) + ADAM_EPS) + ADAM_WD * w)
    return delta, m, v


def _reduce_adam(pieces, w, m, v, name, own=None):
    n, R, C = pieces.shape
    tr = _tile(R, 256, 8)

    def body(*refs):
        p_ref, w_ref, m_ref, v_ref = refs[:4]
        g_ref, d_ref, mo_ref, vo_ref = refs[-4:]
        g = p_ref[0].astype(F32)
        for i in range(1, n):
            g = g + p_ref[i].astype(F32)
        if own is not None:
            g = g + refs[4][...].astype(F32)
        d, mn, vn = _adam_math(w_ref[...], g, m_ref[...], v_ref[...])
        g_ref[...] = g
        d_ref[...] = d
        mo_ref[...] = mn
        vo_ref[...] = vn

    row = pl.BlockSpec((tr, C), lambda i: (i, 0))
    ins = [pieces, w, m, v] + ([] if own is None else [own])
    return pl.pallas_call(body, name=name, grid=(R // tr,),
                          in_specs=[pl.BlockSpec((n, tr, C), lambda i: (0, i, 0))] + [row] * (len(ins) - 1),
                          out_specs=[row] * 4, out_shape=[S_((R, C), F32)] * 4, compiler_params=_cp())(*ins)


def _reduce_adam_stack(pieces, owns, w, m, v, name):
    F = len(pieces)
    n, R, C = pieces[0].shape
    tr = _tile(R, max(8, 131072 // C), 8)

    def body(*refs):
        w_ref, m_ref, v_ref = refs[2 * F:2 * F + 3]
        g_ref, d_ref, mo_ref, vo_ref = refs[-4:]
        f = pl.program_id(0)
        for ff in range(F):
            @pl.when(f == ff)
            def _(ff=ff):
                g = refs[ff][0].astype(F32)
                for i in range(1, n):
                    g = g + refs[ff][i].astype(F32)
                g = g + refs[F + ff][...].astype(F32)
                d, mn, vn = _adam_math(w_ref[...], g, m_ref[...], v_ref[...])
                g_ref[...] = g
                d_ref[...] = d
                mo_ref[...] = mn
                vo_ref[...] = vn

    pspecs = [pl.BlockSpec((n, tr, C), lambda f, r, ff=ff: (0, jnp.where(f == ff, r, 0), 0)) for ff in range(F)]
    ospecs = [pl.BlockSpec((tr, C), lambda f, r, ff=ff: (jnp.where(f == ff, r, 0), 0)) for ff in range(F)]
    st = pl.BlockSpec((None, tr, C), lambda f, r: (f, r, 0))
    return pl.pallas_call(body, name=name, grid=(F, R // tr), in_specs=pspecs + ospecs + [st] * 3, out_specs=[st] * 4,
                          out_shape=[S_((F, R, C), F32)] * 4, compiler_params=_cp())(*pieces, *owns, w, m, v)


def _mm_cols(a, wb, name, out_dtype=F32, dep=None):
    M, K = a.shape
    NB, _, nb = wb.shape
    tm = _tile(M, 768, 16)

    def body(*refs):
        refs[-1][...] = _dot(refs[0][...], refs[1][...], NN).astype(out_dtype)

    deps = [] if dep is None else [dep]
    return pl.pallas_call(body, name=name, grid=(M // tm, NB),
                          in_specs=[pl.BlockSpec((tm, K), lambda i, j: (i, 0)), pl.BlockSpec((None, K, nb), lambda i, j: (j, 0, 0))]
                          + [_full(d.shape) for d in deps],
                          out_specs=pl.BlockSpec((tm, nb), lambda i, j: (i, j)), out_shape=S_((M, NB * nb), out_dtype),
                          compiler_params=_cp())(a, wb, *deps)


def _mm_nt_cols(g, wb, name):
    M = g.shape[0]
    NB, K, nb = wb.shape
    tm = _tile(M, 768, 16)

    def body(g_ref, w_ref, o_ref):
        j = pl.program_id(1)
        part = _dot(g_ref[...], w_ref[...], NT)

        @pl.when(j == 0)
        def _():
            o_ref[...] = part

        @pl.when(j > 0)
        def _():
            o_ref[...] += part

    return pl.pallas_call(body, name=name, grid=(M // tm, NB),
                          in_specs=[pl.BlockSpec((tm, nb), lambda i, j: (i, j)), pl.BlockSpec((None, K, nb), lambda i, j: (j, 0, 0))],
                          out_specs=pl.BlockSpec((tm, K), lambda i, j: (i, 0)), out_shape=S_((M, K), F32),
                          compiler_params=_cp())(g, wb)


def _mm_full(a, w, dn, name, out_dtype=F32):
    M, K = a.shape
    N = w.shape[1] if dn == NN else w.shape[0]
    tm = _tile(M, 768, 16)

    def body(a_ref, w_ref, o_ref):
        o_ref[...] = _dot(a_ref[...], w_ref[...], dn).astype(out_dtype)

    return pl.pallas_call(body, name=name, grid=(M // tm,),
                          in_specs=[pl.BlockSpec((tm, K), lambda i: (i, 0)), _full(w.shape)],
                          out_specs=pl.BlockSpec((tm, N), lambda i: (i, 0)), out_shape=S_((M, N), out_dtype),
                          compiler_params=_cp())(a, w)


def _mm_tn_cols(at, g, NB, name):
    K, M = at.shape
    nb = g.shape[1] // NB
    tk = _tile(M, WGRAD_ROWS, 128)
    nk = M // tk

    def body(a_ref, g_ref, o_ref, acc):
        k = pl.program_id(1)
        part = _dot(a_ref[...], g_ref[...], NN)

        @pl.when(k == 0)
        def _():
            acc[...] = part

        @pl.when(k > 0)
        def _():
            acc[...] += part

        @pl.when(k == nk - 1)
        def _():
            o_ref[...] = acc[...].astype(BF)

    return pl.pallas_call(body, name=name, grid=(NB, nk),
                          in_specs=[pl.BlockSpec((K, tk), lambda b, k: (0, k)), pl.BlockSpec((tk, nb), lambda b, k: (k, b))],
                          out_specs=pl.BlockSpec((None, K, nb), lambda b, k: (b, 0, 0)), out_shape=S_((NB, K, nb), BF),
                          scratch_shapes=[pltpu.VMEM((K, nb), F32)], compiler_params=_cp())(at, g)


def _mm_tn_rows(a, g, NB, name):
    M = a.shape[0]
    kb = a.shape[1] // NB
    N = g.shape[1]
    tk = _tile(M, WGRAD_ROWS, 128)
    nk = M // tk

    def body(a_ref, g_ref, o_ref, acc):
        k = pl.program_id(1)
        part = _dot(a_ref[...], g_ref[...], TN)

        @pl.when(k == 0)
        def _():
            acc[...] = part

        @pl.when(k > 0)
        def _():
            acc[...] += part

        @pl.when(k == nk - 1)
        def _():
            o_ref[...] = acc[...].astype(BF)

    return pl.pallas_call(body, name=name, grid=(NB, nk),
                          in_specs=[pl.BlockSpec((tk, kb), lambda b, k: (k, b)), pl.BlockSpec((tk, N), lambda b, k: (k, 0))],
                          out_specs=pl.BlockSpec((None, kb, N), lambda b, k: (b, 0, 0)), out_shape=S_((NB, kb, N), BF),
                          scratch_shapes=[pltpu.VMEM((kb, N), F32)], compiler_params=_cp())(a, g)


def _ffn_fwd(h, wg, wu, wd, name, dep=None):
    T, D = h.shape
    NB, _, nb = wg.shape
    tm = _tile(T, FFN_FWD_ROWS, 16)

    def body(*refs):
        h_ref, wg_ref, wu_ref, wd_ref = refs[:4]
        y_ref, g_ref, u_ref = refs[-3:]
        b = pl.program_id(1)
        hh = h_ref[...]
        g = _dot(hh, wg_ref[...], NN).astype(BF)
        u = _dot(hh, wu_ref[...], NN).astype(BF)
        g_ref[...] = g
        u_ref[...] = u
        gf = g.astype(F32)
        part = _dot(gf * jax.nn.sigmoid(gf) * u.astype(F32), wd_ref[...], NN)

        @pl.when(b == 0)
        def _():
            y_ref[...] = part

        @pl.when(b > 0)
        def _():
            y_ref[...] += part

    deps = [] if dep is None else [dep]
    wcol = pl.BlockSpec((None, D, nb), lambda i, b: (b, 0, 0))
    act = pl.BlockSpec((None, tm, nb), lambda i, b: (b, i, 0))
    return pl.pallas_call(
        body, name=name, grid=(T // tm, NB),
        in_specs=[pl.BlockSpec((tm, D), lambda i, b: (i, 0)), wcol, wcol, pl.BlockSpec((None, nb, D), lambda i, b: (b, 0, 0))]
        + [_full(d.shape) for d in deps],
        out_specs=[pl.BlockSpec((tm, D), lambda i, b: (i, 0)), act, act],
        out_shape=[S_((T, D), F32), S_((NB, T, nb), BF), S_((NB, T, nb), BF)], compiler_params=_cp())(h, wg, wu, wd, *deps)


def _ffn_bwd_act(dy, G, U, wg, wu, wd, name):
    T, D = dy.shape
    NB, _, nb = wg.shape
    tm = _tile(T, FFN_BWD_ROWS, 16)

    def body(dy_ref, g_ref, u_ref, wg_ref, wu_ref, wd_ref, dh_ref, dg_ref, du_ref, a_ref):
        b = pl.program_id(1)
        g, u = g_ref[...].astype(F32), u_ref[...].astype(F32)
        da = _dot(dy_ref[...], wd_ref[...], NT)
        s = jax.nn.sigmoid(g)
        silu = g * s
        du = da * silu
        dg = da * u * (s * (1.0 + g * (1.0 - s)))
        dg_ref[...] = dg.astype(BF)
        du_ref[...] = du.astype(BF)
        a_ref[...] = (silu * u).astype(BF)
        part = _dot(dg, wg_ref[...], NT) + _dot(du, wu_ref[...], NT)

        @pl.when(b == 0)
        def _():
            dh_ref[...] = part

        @pl.when(b > 0)
        def _():
            dh_ref[...] += part

    wcol = pl.BlockSpec((None, D, nb), lambda i, b: (b, 0, 0))
    act = pl.BlockSpec((None, tm, nb), lambda i, b: (b, i, 0))
    row = pl.BlockSpec((tm, D), lambda i, b: (i, 0))
    return pl.pallas_call(
        body, name=name, grid=(T // tm, NB),
        in_specs=[row, act, act, wcol, wcol, pl.BlockSpec((None, nb, D), lambda i, b: (b, 0, 0))],
        out_specs=[row, act, act, act],
        out_shape=[S_((T, D), F32)] + [S_((NB, T, nb), BF)] * 3, compiler_params=_cp())(dy, G, U, wg, wu, wd)


def _ffn_wgrad_in(ht, dact, name, dep=None):
    D, T = ht.shape
    NB, _, nb = dact.shape
    tk = _tile(T, WGRAD_ROWS, 128)
    nk = T // tk
    deps = [] if dep is None else [dep]

    def body(*refs):
        h_ref, d_ref = refs[:2]
        o_ref, acc = refs[-2:]
        k = pl.program_id(1)
        part = _dot(h_ref[...], d_ref[...], NN)

        @pl.when(k == 0)
        def _():
            acc[...] = part

        @pl.when(k > 0)
        def _():
            acc[...] += part

        @pl.when(k == nk - 1)
        def _():
            o_ref[...] = acc[...].astype(BF)

    return pl.pallas_call(body, name=name, grid=(NB, nk),
                          in_specs=[pl.BlockSpec((D, tk), lambda b, k: (0, k)), pl.BlockSpec((None, tk, nb), lambda b, k: (b, k, 0))]
                          + [_full(d.shape) for d in deps],
                          out_specs=pl.BlockSpec((None, D, nb), lambda b, k: (b, 0, 0)), out_shape=S_((NB, D, nb), BF),
                          scratch_shapes=[pltpu.VMEM((D, nb), F32)], compiler_params=_cp())(ht, dact, *deps)


def _ffn_wgrad_out(act, dy, name):
    NB, T, nb = act.shape
    D = dy.shape[1]
    tk = _tile(T, WGRAD_ROWS, 128)
    nk = T // tk

    def body(a_ref, d_ref, o_ref, acc):
        k = pl.program_id(1)
        part = _dot(a_ref[...], d_ref[...], TN)

        @pl.when(k == 0)
        def _():
            acc[...] = part

        @pl.when(k > 0)
        def _():
            acc[...] += part

        @pl.when(k == nk - 1)
        def _():
            o_ref[...] = acc[...].astype(BF)

    return pl.pallas_call(body, name=name, grid=(NB, nk),
                          in_specs=[pl.BlockSpec((None, tk, nb), lambda b, k: (b, k, 0)), pl.BlockSpec((tk, D), lambda b, k: (k, 0))],
                          out_specs=pl.BlockSpec((None, nb, D), lambda b, k: (b, 0, 0)), out_shape=S_((NB, nb, D), BF),
                          scratch_shapes=[pltpu.VMEM((nb, D), F32)], compiler_params=_cp())(act, dy)


class _Geo:
    def __init__(self, S, LC, D):
        self.S, self.LC, self.D, self.T = S, LC, D, S + LC
        self.RT = _tile(LC, ROW_TILE, 8)
        assert S % self.RT == 0 and self.RT >= 2 * HALO
        self.nlat, self.nctx = S // self.RT, LC // self.RT
        self.nt = self.nlat + self.nctx

    def row(self, C, cb=0):
        return pl.BlockSpec((self.RT, C), lambda i: (i, cb))

    def prev(self, C, cb=0):
        return pl.BlockSpec((self.RT, C), lambda i: (jnp.maximum(i - 1, 0), cb))

    def next(self, C, cb=0):
        nt = self.nt
        return pl.BlockSpec((self.RT, C), lambda i: (jnp.minimum(i + 1, nt - 1), cb))

    def seg(self, r, C):
        nlat = self.nlat
        return pl.BlockSpec((None, r, C), lambda i: (jnp.minimum(i // nlat, 1), 0, 0))

    def first_of_seg(self, i):
        return jnp.logical_or(i == 0, i == self.nlat)

    def prev_ok(self, i):
        return jnp.logical_and(i != 0, i != self.nlat)

    def next_ok(self, i):
        return jnp.logical_and(i != self.nlat - 1, i != self.nt - 1)

    def pos(self, i):
        r = lax.broadcasted_iota(jnp.int32, (self.RT, 1), 0)
        is_ctx = i >= self.nlat
        base = jnp.where(is_ctx, (i - self.nlat) * self.RT, i * self.RT)
        return base + r, jnp.where(is_ctx, self.LC, self.S)


def _rms(x):
    return x * lax.rsqrt(jnp.mean(x * x, axis=-1, keepdims=True) + EPS)


def _modulate(x, g, shift, scale):
    return (_rms(x) * g) * (1 + scale) + shift


def _post(x, y, g, gate, w):
    return x + w * gate * (_rms(y) * g)


def _norm_fwd(geo, x, post, pre, name):
    D = geo.D
    ins, specs = [x], [geo.row(D)]
    if post is not None:
        ins += [post[0], post[1], post[2]]
        specs += [geo.row(D), _full((1, D)), geo.seg(9, D)]
    if pre is not None:
        ins += [pre[0], pre[1]]
        specs += [_full((1, D)), geo.seg(9, D)]

    def body(*refs):
        it = iter(refs)
        xv = next(it)[...]
        if post is not None:
            y_ref, gp_ref, tab_ref = next(it), next(it), next(it)
        if pre is not None:
            gq_ref, tabn_ref = next(it), next(it)
        if post is not None:
            r = 3 * post[3] + 2
            xv = _post(xv, y_ref[...], gp_ref[...], tab_ref[r:r + 1, :], post[4])
            next(it)[...] = xv
        if pre is not None:
            r = 3 * pre[2]
            h = _modulate(xv, gq_ref[...], tabn_ref[r:r + 1, :], tabn_ref[r + 1:r + 2, :]).astype(BF)
            next(it)[...] = h
            next(it)[...] = h.T

    outs, ospecs = [], []
    if post is not None:
        outs.append(S_((geo.T, D), F32))
        ospecs.append(geo.row(D))
    if pre is not None:
        outs += [S_((geo.T, D), BF), S_((D, geo.T), BF)]
        ospecs += [geo.row(D), pl.BlockSpec((D, geo.RT), lambda i: (0, i))]
    return pl.pallas_call(body, name=name, grid=(geo.nt,), in_specs=specs, out_specs=ospecs, out_shape=outs,
                          compiler_params=_cp())(*ins)


def _acc_rows(ref, first, rows):
    for k, v in enumerate(rows):
        @pl.when(first)
        def _(k=k, v=v):
            ref[k:k + 1, :] = v

        @pl.when(jnp.logical_not(first))
        def _(k=k, v=v):
            ref[k:k + 1, :] += v


def _norm_bwd(geo, dxo, dh, x, pre, post, name):
    D = geo.D
    ins = [dxo, dh, x, pre[0], pre[1]]
    specs = [geo.row(D)] * 3 + [_full((1, D)), geo.seg(9, D)]
    if post is not None:
        ins += [post[0], post[1], post[2]]
        specs += [geo.row(D), _full((1, D)), geo.seg(9, D)]

    def body(*refs):
        i = pl.program_id(0)
        first = geo.first_of_seg(i)
        dxo_ref, dh_ref, x_ref, gq_ref, tab_ref = refs[:5]
        k = 5
        if post is not None:
            y_ref, gp_ref, tabp_ref = refs[5:8]
            k = 8
        outs = refs[k:]
        r = 3 * pre[2]
        _, vjp = jax.vjp(_modulate, x_ref[...], gq_ref[...], tab_ref[r:r + 1, :], tab_ref[r + 1:r + 2, :])
        dx, dg, dsh, dsc = vjp(dh_ref[...].astype(F32))
        dx = dx + dxo_ref[...]
        outs[0][...] = dx
        if post is None:
            _acc_rows(outs[1], first, [dg, dsh, dsc])
            return
        _acc_rows(outs[2], first, [dg, dsh, dsc])
        rp = 3 * post[3] + 2
        w = post[4]
        _, vjp2 = jax.vjp(lambda yy, gg, ga: w * ga * (_rms(yy) * gg), y_ref[...], gp_ref[...], tabp_ref[rp:rp + 1, :])
        dy, dgp, dga = vjp2(dx)
        outs[1][...] = dy
        _acc_rows(outs[3], first, [dgp, dga])

    if post is None:
        outs, ospecs = [S_((geo.T, D), F32), S_((2, 3, D), F32)], [geo.row(D), geo.seg(3, D)]
    else:
        outs = [S_((geo.T, D), F32), S_((geo.T, D), F32), S_((2, 3, D), F32), S_((2, 2, D), F32)]
        ospecs = [geo.row(D), geo.row(D), geo.seg(3, D), geo.seg(2, D)]
    return pl.pallas_call(body, name=name, grid=(geo.nt,), in_specs=specs, out_specs=ospecs, out_shape=outs,
                          compiler_params=_cp())(*ins)


def _loss_bwd(geo, xf, tgt, post, name):
    D = geo.D
    nlat = geo.nlat

    def body(x_ref, t_ref, y_ref, gp_ref, tabp_ref, loss_ref, dx_ref, dy_ref, dpost_ref):
        i = pl.program_id(0)
        first = geo.first_of_seg(i)
        lat = i < nlat
        diff = x_ref[...] - t_ref[...]
        part = jnp.where(lat, 0.5 * jnp.sum(jnp.mean(diff * diff, axis=-1, keepdims=True), axis=0, keepdims=True), 0.0)

        @pl.when(i == 0)
        def _():
            loss_ref[...] = part

        @pl.when(i > 0)
        def _():
            loss_ref[...] += part

        dx = jnp.where(lat, diff * (1.0 / D), 0.0)
        dx_ref[...] = dx
        rp = 3 * post[3] + 2
        w = post[4]
        _, vjp2 = jax.vjp(lambda yy, gg, ga: w * ga * (_rms(yy) * gg), y_ref[...], gp_ref[...], tabp_ref[rp:rp + 1, :])
        dy, dgp, dga = vjp2(dx)
        dy_ref[...] = dy
        _acc_rows(dpost_ref, first, [dgp, dga])

    tspec = pl.BlockSpec((geo.RT, D), lambda i: (jnp.minimum(i, nlat - 1), 0))
    return pl.pallas_call(
        body, name=name, grid=(geo.nt,),
        in_specs=[geo.row(D), tspec, geo.row(D), _full((1, D)), geo.seg(9, D)],
        out_specs=[_full((1, 1)), geo.row(D), geo.row(D), geo.seg(2, D)],
        out_shape=[S_((1, 1), F32), S_((geo.T, D), F32), S_((geo.T, D), F32), S_((2, 2, D), F32)],
        compiler_params=_cp())(xf, tgt, post[0], post[1], post[2])


def _fill_ext(geo, ext, prev_ref, cur_ref, next_ref, i):
    RT = geo.RT
    ext[0:HALO, :] = jnp.where(geo.prev_ok(i), prev_ref[RT - HALO:RT, :], 0.0).astype(ext.dtype)
    ext[HALO:HALO + RT, :] = cur_ref[...].astype(ext.dtype)
    ext[HALO + RT:2 * HALO + RT, :] = jnp.where(geo.next_ok(i), next_ref[0:HALO, :], 0.0).astype(ext.dtype)


CONV_W = 4
CONV_LEFT = 2


def _lru_gates(u, za, zx, lam):
    r = jax.nn.sigmoid(za)
    i = jax.nn.sigmoid(zx)
    log_a = -LRU_C * r * jax.nn.softplus(-lam)
    a = jnp.exp(log_a)
    return a, jnp.sqrt(1.0 - jnp.exp(2.0 * log_a)) * (i * u)


def _lru_coef_fwd(geo, P, W, cw, cb, wa, ba, wx, bx, lam, name):
    RT = geo.RT
    nblk, LB = wa.shape[1], wa.shape[2]

    def body(pp, pc, pn, cw_ref, cb_ref, wa_ref, ba_ref, wx_ref, bx_ref, lam_ref, u_ref, a0, b0, a1, b1, ext):
        i = pl.program_id(0)
        _fill_ext(geo, ext, pp, pc, pn, i)
        u = cb_ref[...] + ext[pl.ds(HALO - CONV_LEFT, RT), :] * cw_ref[0:1, :]
        for k in range(1, CONV_W):
            u = u + ext[pl.ds(HALO - CONV_LEFT + k, RT), :] * cw_ref[k:k + 1, :]
        u_ref[...] = u
        for d, (a_ref, b_ref) in enumerate(((a0, b0), (a1, b1))):
            for n in range(nblk):
                sl = slice(n * LB, (n + 1) * LB)
                un = u[:, sl]
                za = _dot(un, wa_ref[d, n], NN) + ba_ref[d:d + 1, sl]
                zx = _dot(un, wx_ref[d, n], NN) + bx_ref[d:d + 1, sl]
                a, b = _lru_gates(un, za, zx, lam_ref[d:d + 1, sl])
                a_ref[:, sl] = a
                b_ref[:, sl] = b

    return pl.pallas_call(
        body, name=name, grid=(geo.nt,),
        in_specs=[geo.prev(W, 1), geo.row(W, 1), geo.next(W, 1), _full(cw.shape), _full(cb.shape), _full(wa.shape),
                  _full(ba.shape), _full(wx.shape), _full(bx.shape), _full(lam.shape)],
        out_specs=[geo.row(W)] * 5, out_shape=[S_((geo.T, W), F32)] * 5,
        scratch_shapes=[pltpu.VMEM((RT + 2 * HALO, W), F32)], compiler_params=_cp())(P, P, P, cw, cb, wa, ba, wx, bx, lam)


def _lru_coef_bwd(geo, u, da, db, W, wa, ba, wx, bx, lam, name):
    nblk, LB = wa.shape[1], wa.shape[2]

    def body(u_ref, da0, db0, da1, db1, wa_ref, ba_ref, wx_ref, bx_ref, lam_ref, du_ref, dwa, dba, dwx, dbx, dlam):
        i = pl.program_id(0)

        @pl.when(i == 0)
        def _():
            for r in (dwa, dba, dwx, dbx, dlam):
                r[...] = jnp.zeros(r.shape, F32)

        u = u_ref[...]
        for n in range(nblk):
            sl = slice(n * LB, (n + 1) * LB)
            un = u[:, sl]
            dun = jnp.zeros_like(un)
            for d, (da_ref, db_ref) in enumerate(((da0, db0), (da1, db1))):
                za = _dot(un, wa_ref[d, n], NN) + ba_ref[d:d + 1, sl]
                zx = _dot(un, wx_ref[d, n], NN) + bx_ref[d:d + 1, sl]
                _, vjp = jax.vjp(_lru_gates, un, za, zx, lam_ref[d:d + 1, sl])
                du_e, dza, dzx, dl = vjp((da_ref[:, sl], db_ref[:, sl]))
                dun = dun + du_e + _dot(dza, wa_ref[d, n], NT) + _dot(dzx, wx_ref[d, n], NT)
                dwa[d, n] += _dot(un, dza, TN)
                dwx[d, n] += _dot(un, dzx, TN)
                dba[d:d + 1, sl] += jnp.sum(dza, axis=0, keepdims=True)
                dbx[d:d + 1, sl] += jnp.sum(dzx, axis=0, keepdims=True)
                dlam[d:d + 1, sl] += dl
            du_ref[:, sl] = dun

    row = geo.row(W)
    return pl.pallas_call(
        body, name=name, grid=(geo.nt,),
        in_specs=[row] * 5 + [_full(wa.shape), _full(ba.shape), _full(wx.shape), _full(bx.shape), _full(lam.shape)],
        out_specs=[row, _full(wa.shape), _full(ba.shape), _full(wx.shape), _full(bx.shape), _full(lam.shape)],
        out_shape=[S_((geo.T, W), F32), S_(wa.shape, F32), S_(ba.shape, F32), S_(wx.shape, F32), S_(bx.shape, F32), S_(lam.shape, F32)],
        compiler_params=_cp())(u, da[0], db[0], da[1], db[1], wa, ba, wx, bx, lam)


def _scan_order(d, k, nlat, nctx):
    if d == 0:
        return jnp.where(k < nctx, nlat + k, k - nctx)
    return jnp.where(k < nctx, nlat + nctx - 1 - k, nlat - 1 - (k - nctx))


def _chunk_scan(a, b, reverse):
    n = a.shape[0]
    row = lax.broadcasted_iota(jnp.int32, a.shape, 0)
    s = 1
    while s < n:
        if reverse:
            ok = row < n - s
            a_s, b_s = pltpu.roll(a, n - s, 0), pltpu.roll(b, n - s, 0)
        else:
            ok = row >= s
            a_s, b_s = pltpu.roll(a, s, 0), pltpu.roll(b, s, 0)
        b = a * jnp.where(ok, b_s, 0.0) + b
        a = a * jnp.where(ok, a_s, 1.0)
        s *= 2
    return a, b


def _shift1(x, reverse, fill):
    n = x.shape[0]
    row = lax.broadcasted_iota(jnp.int32, x.shape, 0)
    if reverse:
        return jnp.where(row == n - 1, fill, pltpu.roll(x, n - 1, 0))
    return jnp.where(row == 0, fill, pltpu.roll(x, 1, 0))


def _lru_scan_fwd(geo, a, b, d, name):
    W = a.shape[1]
    tw = _tile(W, 256, 128)
    RT, nlat, nctx = geo.RT, geo.nlat, geo.nctx
    rev = d == 1

    def body(a_ref, b_ref, h_ref, hp_ref, carry):
        k = pl.program_id(1)

        @pl.when(k == 0)
        def _():
            carry[...] = jnp.zeros(carry.shape, F32)

        ac, bc = _chunk_scan(a_ref[...], b_ref[...], rev)
        h = bc + ac * carry[...]
        h_ref[...] = h
        hp_ref[...] = _shift1(h, rev, carry[...])
        carry[...] = h[0:1, :] if rev else h[RT - 1:RT, :]

    spec = pl.BlockSpec((RT, tw), lambda j, k: (_scan_order(d, k, nlat, nctx), j))
    return pl.pallas_call(body, name=name, grid=(W // tw, geo.nt), in_specs=[spec, spec], out_specs=[spec, spec],
                          out_shape=[S_((geo.T, W), F32)] * 2, scratch_shapes=[pltpu.VMEM((1, tw), F32)],
                          compiler_params=_cp())(a, b)


def _lru_scan_bwd(geo, dh, a, hprev, d, name):
    W = a.shape[1]
    tw = _tile(W, 256, 128)
    RT, nlat, nctx, nt = geo.RT, geo.nlat, geo.nctx, geo.nt
    rev = d == 1

    def body(dh_ref, a_ref, hp_ref, da_ref, db_ref, carry):
        k = pl.program_id(1)

        @pl.when(k == 0)
        def _():
            carry[...] = jnp.zeros(carry.shape, F32)

        av = a_ref[...]
        a_next = _shift1(av, not rev, jnp.ones((1, tw), F32))
        ac, bc = _chunk_scan(a_next, dh_ref[...], not rev)
        lam = bc + ac * carry[...]
        db_ref[...] = lam
        da_ref[...] = lam * hp_ref[...]
        first = (av * lam)[RT - 1:RT, :] if rev else (av * lam)[0:1, :]
        carry[...] = first

    spec = pl.BlockSpec((RT, tw), lambda j, k: (_scan_order(d, nt - 1 - k, nlat, nctx), j))
    return pl.pallas_call(body, name=name, grid=(W // tw, nt), in_specs=[spec] * 3, out_specs=[spec] * 2,
                          out_shape=[S_((geo.T, W), F32)] * 2, scratch_shapes=[pltpu.VMEM((1, tw), F32)],
                          compiler_params=_cp())(dh, a, hprev)


def _conv_bwd_assemble(geo, P, du, pieces, W, cw, name):
    RT = geo.RT

    def body(pp, pc, pn, dup, duc, dun, dg, dq, dk, dv, dq1, dk1, dv1, dog, cw_ref, dP_ref, dcw, dcb, ext_r, ext_d):
        i = pl.program_id(0)
        _fill_ext(geo, ext_r, pp, pc, pn, i)
        _fill_ext(geo, ext_d, dup, duc, dun, i)
        du_c = duc[...]
        rows = []
        dr = None
        for k in range(CONV_W):
            rows.append(jnp.sum(du_c * ext_r[pl.ds(HALO - CONV_LEFT + k, RT), :], axis=0, keepdims=True))
            t = ext_d[pl.ds(HALO + CONV_LEFT - k, RT), :] * cw_ref[k:k + 1, :]
            dr = t if dr is None else dr + t
        first = i == 0
        _acc_rows(dcw, first, rows)
        _acc_rows(dcb, first, [jnp.sum(du_c, axis=0, keepdims=True)])
        for j, v in enumerate((dg[...], dr, dq[...] + dq1[...], dk[...] + dk1[...], dv[...] + dv1[...], dog[...])):
            dP_ref[:, j * W:(j + 1) * W] = v

    row = geo.row(W)
    return pl.pallas_call(
        body, name=name, grid=(geo.nt,),
        in_specs=[geo.prev(W, 1), geo.row(W, 1), geo.next(W, 1), geo.prev(W), row, geo.next(W)] + [row] * 8 + [_full(cw.shape)],
        out_specs=[geo.row(6 * W), _full((CONV_W, W)), _full((1, W))],
        out_shape=[S_((geo.T, 6 * W), F32), S_((CONV_W, W), F32), S_((1, W), F32)],
        scratch_shapes=[pltpu.VMEM((RT + 2 * HALO, W), F32)] * 2, compiler_params=_cp())(P, P, P, du, du, du, *pieces, cw)


def _rot_half(x, cos, sin):
    hw = x.shape[1] // 2
    x1, x2 = x[:, :hw], x[:, hw:]
    return jnp.concatenate([x1 * cos - x2 * sin, x1 * sin + x2 * cos], axis=-1)


def _ret_chunk(q, k, v, s, logit, cos, sin, rev):
    C, dk = q.shape
    lg = -jax.nn.softplus(-logit)
    q = _rot_half(q, cos, sin)
    k = _rot_half(k, cos, sin) * (dk ** -0.5)
    i = lax.broadcasted_iota(jnp.int32, (C, 1), 0).astype(F32)
    j = lax.broadcasted_iota(jnp.int32, (1, C), 1).astype(F32)
    if rev:
        diff, qe, ke = j - i, C - i, i
    else:
        diff, qe, ke = i - j, i + 1.0, C - 1.0 - i
    intra = jnp.where(diff >= 0, jnp.exp(lg * jnp.maximum(diff, 0.0)), 0.0)
    scores = _bdot(q, k, "nt") * intra
    o = _bdot(scores, v, "nn") + _bdot(q * jnp.exp(lg * qe), s, "nn")
    s_new = s * jnp.exp(lg * C) + _bdot(k * jnp.exp(lg * ke), v, "tn")
    return o, s_new


def _ret_specs(P, W, H, d, nlc, ncc, order_of_step):
    C = RET_CHUNK
    dk = W // H

    def cidx(k):
        return _scan_order(d, order_of_step(k), nlc, ncc)

    per_w = W // dk
    q = pl.BlockSpec((C, dk), lambda h, k: (cidx(k), 2 * per_w + h))
    kk = pl.BlockSpec((C, dk), lambda h, k: (cidx(k), 3 * per_w + h))
    v = pl.BlockSpec((C, dk), lambda h, k: (cidx(k), 4 * per_w + h))
    tab = pl.BlockSpec((C, dk // 2), lambda h, k: (cidx(k), 0))
    logit = pl.BlockSpec((None, 1, 1), lambda h, k: (h, 0, 0))
    o = pl.BlockSpec((C, dk), lambda h, k: (cidx(k), h))
    return q, kk, v, tab, logit, o


def _ret_fwd(geo, P, W, H, logits, cos, sin, name):
    C = RET_CHUNK
    dk = W // H
    nlc, ncc = geo.S // C, geo.LC // C
    nch = nlc + ncc

    def body(*refs):
        ins, outs, scrs = refs[:12], refs[12:16], refs[16:]
        k = pl.program_id(1)
        for d in range(2):
            q_ref, k_ref, v_ref, cos_ref, sin_ref, lg_ref = ins[6 * d:6 * d + 6]
            o_ref, st_ref = outs[2 * d:2 * d + 2]
            s_scr = scrs[d]

            @pl.when(k == 0)
            def _(s_scr=s_scr):
                s_scr[...] = jnp.zeros(s_scr.shape, F32)

            st_ref[...] = s_scr[...]
            o, s_new = _ret_chunk(q_ref[...], k_ref[...], v_ref[...], s_scr[...], lg_ref[...], cos_ref[...], sin_ref[...], d == 1)
            o_ref[...] = o
            s_scr[...] = s_new

    in_specs, out_specs = [], []
    st = pl.BlockSpec((None, None, dk, dk), lambda h, k: (h, k, 0, 0))
    for d in range(2):
        q, kk, v, tab, logit, o = _ret_specs(P, W, H, d, nlc, ncc, lambda k: k)
        in_specs += [q, kk, v, tab, tab, logit]
        out_specs += [o, st]
    return pl.pallas_call(body, name=name, grid=(H, nch), in_specs=in_specs, out_specs=out_specs,
                          out_shape=[S_((geo.T, W), F32), S_((H, nch, dk, dk), F32)] * 2,
                          scratch_shapes=[pltpu.VMEM((dk, dk), F32)] * 2, compiler_params=_cp())(
                              P, P, P, cos, sin, logits[0], P, P, P, cos, sin, logits[1])


def _ret_bwd(geo, P, W, H, logits, cos, sin, states, do, name):
    C = RET_CHUNK
    dk = W // H
    nlc, ncc = geo.S // C, geo.LC // C
    nch = nlc + ncc

    def body(*refs):
        ins, outs, scrs = refs[:16], refs[16:24], refs[24:]
        k = pl.program_id(1)
        for d in range(2):
            q_ref, k_ref, v_ref, cos_ref, sin_ref, lg_ref, st_ref, do_ref = ins[8 * d:8 * d + 8]
            dq_ref, dk_ref, dv_ref, dlg_ref = outs[4 * d:4 * d + 4]
            ds_scr = scrs[d]

            @pl.when(k == 0)
            def _(ds_scr=ds_scr, dlg_ref=dlg_ref):
                ds_scr[...] = jnp.zeros(ds_scr.shape, F32)
                dlg_ref[...] = jnp.zeros(dlg_ref.shape, F32)

            cos, sin = cos_ref[...], sin_ref[...]
            _, vjp = jax.vjp(lambda a, b, c, s, lg, cos=cos, sin=sin, d=d: _ret_chunk(a, b, c, s, lg, cos, sin, d == 1),
                             q_ref[...], k_ref[...], v_ref[...], st_ref[...], lg_ref[...])
            dq, dkk, dv, ds, dlg = vjp((do_ref[...], ds_scr[...]))
            dq_ref[...] = dq
            dk_ref[...] = dkk
            dv_ref[...] = dv
            ds_scr[...] = ds
            dlg_ref[...] += dlg

    in_specs, out_specs, ops = [], [], []
    st = pl.BlockSpec((None, None, dk, dk), lambda h, k: (h, nch - 1 - k, 0, 0))
    for d in range(2):
        q, kk, v, tab, logit, o = _ret_specs(P, W, H, d, nlc, ncc, lambda k: nch - 1 - k)
        in_specs += [q, kk, v, tab, tab, logit, st, o]
        out_specs += [o, o, o, logit]
        ops += [P, P, P, cos, sin, logits[d], states[d], do]
    return pl.pallas_call(body, name=name, grid=(H, nch), in_specs=in_specs, out_specs=out_specs,
                          out_shape=([S_((geo.T, W), F32)] * 3 + [S_((H, 1, 1), F32)]) * 2,
                          scratch_shapes=[pltpu.VMEM((dk, dk), F32)] * 2, compiler_params=_cp())(*ops)


def _mix_even_math(g, h0, h1, o0, o1, og, gn, H):
    lru = jax.nn.gelu(g) * (h0 + h1)
    o = o0 + o1
    dv = o.shape[1] // H
    parts = []
    for h in range(H):
        oh = o[:, h * dv:(h + 1) * dv]
        mu = jnp.mean(oh, axis=-1, keepdims=True)
        var = jnp.mean(jnp.square(oh - mu), axis=-1, keepdims=True)
        parts.append((oh - mu) * lax.rsqrt(var + EPS))
    y = jnp.concatenate(parts, axis=-1) * gn
    return lru, y * jax.nn.silu(og)


def _mix_even_fwd(geo, P, W, H, h0, h1, o0, o1, gn, name):
    def body(g_ref, h0r, h1r, o0r, o1r, og_ref, gn_ref, m_ref):
        lru, ret = _mix_even_math(g_ref[...], h0r[...], h1r[...], o0r[...], o1r[...], og_ref[...], gn_ref[...], H)
        m_ref[:, :W] = lru.astype(BF)
        m_ref[:, W:] = ret.astype(BF)

    row = geo.row(W)
    return pl.pallas_call(body, name=name, grid=(geo.nt,), in_specs=[geo.row(W, 0), row, row, row, row, geo.row(W, 5), _full((1, W))],
                          out_specs=geo.row(2 * W), out_shape=S_((geo.T, 2 * W), BF), compiler_params=_cp())(P, h0, h1, o0, o1, P, gn)


def _mix_even_bwd(geo, P, W, H, h0, h1, o0, o1, gn, dmix, name):
    def body(g_ref, h0r, h1r, o0r, o1r, og_ref, gn_ref, dl_ref, dr_ref, dg_ref, dh_ref, do_ref, dog_ref, dgn_ref):
        i = pl.program_id(0)
        _, vjp = jax.vjp(lambda g, hs, os_, og, gn_: _mix_even_math(g, hs, 0.0, os_, 0.0, og, gn_, H),
                         g_ref[...], h0r[...] + h1r[...], o0r[...] + o1r[...], og_ref[...], gn_ref[...])
        dg, dh, do, dog, dgn = vjp((dl_ref[...], dr_ref[...]))
        dg_ref[...] = dg
        dh_ref[...] = dh
        do_ref[...] = do
        dog_ref[...] = dog
        _acc_rows(dgn_ref, i == 0, [dgn])

    row = geo.row(W)
    return pl.pallas_call(
        body, name=name, grid=(geo.nt,),
        in_specs=[geo.row(W, 0), row, row, row, row, geo.row(W, 5), _full((1, W)), geo.row(W, 0), geo.row(W, 1)],
        out_specs=[row] * 4 + [_full((1, W))], out_shape=[S_((geo.T, W), F32)] * 4 + [S_((1, W), F32)],
        compiler_params=_cp())(P, h0, h1, o0, o1, P, gn, dmix, dmix)


def _head_norm_rot(x, g, cosf, sinf):
    xn = _rms(x) * g
    return xn * cosf + _rollv(xn, x.shape[1] // 2, 1) * sinf


def _qk_prep_fwd(geo, P, dims, qg, kg, cosf, sinf, name):
    PW, QW, KW, hd = dims
    qscale = hd ** -0.5 * LOG2E

    def body(p_ref, qg_ref, kg_ref, c_ref, s_ref, q_ref, k_ref, v_ref):
        c, s = c_ref[...], s_ref[...]
        for h in range(QW // hd):
            qh = _head_norm_rot(p_ref[:, PW + h * hd:PW + (h + 1) * hd], qg_ref[...], c, s)
            q_ref[:, h * hd:(h + 1) * hd] = (qh * qscale).astype(BF)
        for h in range(KW // hd):
            o = PW + QW + h * hd
            k_ref[:, h * hd:(h + 1) * hd] = _head_norm_rot(p_ref[:, o:o + hd], kg_ref[...], c, s).astype(BF)
            v_ref[:, 2 * h * hd:(2 * h + 1) * hd] = p_ref[:, o + KW:o + KW + hd].astype(BF)
            v_ref[:, (2 * h + 1) * hd:(2 * h + 2) * hd] = jnp.ones((geo.RT, hd), BF)

    tot = PW + QW + 2 * KW
    return pl.pallas_call(
        body, name=name, grid=(geo.nt,),
        in_specs=[geo.row(tot), _full((1, hd)), _full((1, hd)), geo.row(hd), geo.row(hd)],
        out_specs=[geo.row(QW), geo.row(KW), geo.row(2 * KW)],
        out_shape=[S_((geo.T, QW), BF), S_((geo.T, KW), BF), S_((geo.T, 2 * KW), BF)], compiler_params=_cp())(P, qg, kg, cosf, sinf)


def _qk_prep_bwd(geo, P, dims, qg, kg, cosf, sinf, dpool, dq, dk, dv, name):
    PW, QW, KW, hd = dims

    def body(p_ref, qg_ref, kg_ref, c_ref, s_ref, dpool_ref, dq_ref, dk_ref, dv_ref, dP_ref, dqg_ref, dkg_ref):
        i = pl.program_id(0)
        c, s = c_ref[...], s_ref[...]
        dP_ref[:, :PW] = dpool_ref[...]
        f = lambda x, g: _head_norm_rot(x, g, c, s)
        dqg = jnp.zeros((1, hd), F32)
        for h in range(QW // hd):
            o = PW + h * hd
            _, vjp = jax.vjp(f, p_ref[:, o:o + hd], qg_ref[...])
            dx, dg = vjp(dq_ref[:, h * hd:(h + 1) * hd])
            dP_ref[:, o:o + hd] = dx
            dqg = dqg + dg
        dkg = jnp.zeros((1, hd), F32)
        for h in range(KW // hd):
            o = PW + QW + h * hd
            _, vjp = jax.vjp(f, p_ref[:, o:o + hd], kg_ref[...])
            dx, dg = vjp(dk_ref[:, h * hd:(h + 1) * hd])
            dP_ref[:, o:o + hd] = dx
            dkg = dkg + dg
        dP_ref[:, PW + QW + KW:] = dv_ref[...]
        _acc_rows(dqg_ref, i == 0, [dqg])
        _acc_rows(dkg_ref, i == 0, [dkg])

    tot = PW + QW + 2 * KW
    return pl.pallas_call(
        body, name=name, grid=(geo.nt,),
        in_specs=[geo.row(tot), _full((1, hd)), _full((1, hd)), geo.row(hd), geo.row(hd), geo.row(PW), geo.row(QW), geo.row(KW), geo.row(KW)],
        out_specs=[geo.row(tot), _full((1, hd)), _full((1, hd))],
        out_shape=[S_((geo.T, tot), F32), S_((1, hd), F32), S_((1, hd), F32)], compiler_params=_cp())(
            P, qg, kg, cosf, sinf, dpool, dq, dk, dv)


def _att_tiles(T):
    return _tile(T, 384, 16), _tile(T, 768, 16)


def _stack_heads(ref, G, hd):
    return jnp.concatenate([ref[:, g * hd:(g + 1) * hd] for g in range(G)], axis=0)


def _att_fwd(q, k, v1, hd, name):
    T, QW = q.shape
    KV = k.shape[1] // hd
    G = QW // hd // KV
    tq, tk = _att_tiles(T)
    nk = T // tk

    def body(q_ref, k_ref, v_ref, o_ref, lse_ref, m_scr, acc):
        ki = pl.program_id(2)

        @pl.when(ki == 0)
        def _():
            m_scr[...] = jnp.full(m_scr.shape, -jnp.inf, F32)
            acc[...] = jnp.zeros(acc.shape, F32)

        s = _dot(_stack_heads(q_ref, G, hd), k_ref[...], NT)
        m_prev = m_scr[...]
        m_new = jnp.maximum(m_prev, jnp.max(s, axis=-1, keepdims=True))
        p = jnp.exp2(s - jnp.tile(m_new, (1, tk // hd)))
        acc[...] = jnp.tile(jnp.exp2(m_prev - m_new), (1, 2)) * acc[...] + _dot(p, v_ref[...], NN)
        m_scr[...] = m_new

        @pl.when(ki == nk - 1)
        def _():
            a = acc[...]
            o = a[:, :hd] / a[:, hd:]
            lse = m_scr[...] + jnp.log2(a[:, hd:])
            for g in range(G):
                o_ref[:, g * hd:(g + 1) * hd] = o[g * tq:(g + 1) * tq].astype(BF)
                lse_ref[g] = lse[g * tq:(g + 1) * tq]

    return pl.pallas_call(
        body, name=name, grid=(KV, T // tq, nk),
        in_specs=[pl.BlockSpec((tq, G * hd), lambda a, i, j: (i, a)), pl.BlockSpec((tk, hd), lambda a, i, j: (j, a)),
                  pl.BlockSpec((tk, 2 * hd), lambda a, i, j: (j, a))],
        out_specs=[pl.BlockSpec((tq, G * hd), lambda a, i, j: (i, a)), pl.BlockSpec((G, tq, hd), lambda a, i, j: (a, i, 0))],
        out_shape=[S_((T, QW), BF), S_((QW // hd, T, hd), F32)],
        scratch_shapes=[pltpu.VMEM((G * tq, hd), F32), pltpu.VMEM((G * tq, 2 * hd), F32)],
        compiler_params=_cp())(q, k, v1)


def _att_delta(geo, o, dmix, PW, hd, name):
    QW = o.shape[1]
    nh = QW // hd

    def body(o_ref, do_ref, d_ref):
        for h in range(nh):
            sl = slice(h * hd, (h + 1) * hd)
            d = jnp.sum(o_ref[:, sl].astype(F32) * do_ref[:, PW + h * hd:PW + (h + 1) * hd], axis=-1, keepdims=True)
            d_ref[h] = jnp.broadcast_to(d, (geo.RT, hd))

    return pl.pallas_call(body, name=name, grid=(geo.nt,), in_specs=[geo.row(QW), geo.row(PW + QW)],
                          out_specs=pl.BlockSpec((nh, geo.RT, hd), lambda i: (0, i, 0)), out_shape=S_((nh, geo.T, hd), F32),
                          compiler_params=_cp())(o, dmix)


def _att_bwd(q, k, v1, dmix, PW, lse, delta, hd, name):
    T, QW = q.shape
    KW = k.shape[1]
    KV = KW // hd
    G = QW // hd // KV
    tq, tk = _att_tiles(T)
    nq, nk = T // tq, T // tk
    scale = hd ** -0.5
    pb = PW // hd

    def body(q_ref, k_ref, v_ref, *rest):
        do_refs, (lse_ref, dl_ref, dq_ref, dk_ref, dv_ref, dq_scr) = rest[:G], rest[G:]
        ki, qi = pl.program_id(1), pl.program_id(2)
        q3 = _stack_heads(q_ref, G, hd)
        do3 = jnp.concatenate([r[...] for r in do_refs], axis=0)
        lse = jnp.concatenate([lse_ref[g] for g in range(G)], axis=0)
        dl = jnp.concatenate([dl_ref[g] for g in range(G)], axis=0)
        kk = k_ref[...]
        p = jnp.exp2(_dot(q3, kk, NT) - jnp.tile(lse, (1, tk // hd)))
        ds = p * (_dot(do3, v_ref[:, :hd], NT) - jnp.tile(dl, (1, tk // hd)))
        pv = _dot(p, do3, TN)
        pk = _dot(ds, q3, TN)
        pq = _dot(ds, kk, NN)

        @pl.when(qi == 0)
        def _():
            dk_ref[...] = pk
            dv_ref[...] = pv

        @pl.when(qi > 0)
        def _():
            dk_ref[...] += pk
            dv_ref[...] += pv

        @pl.when(qi == nq - 1)
        def _():
            dk_ref[...] = dk_ref[...] * LN2

        @pl.when(ki == 0)
        def _():
            dq_scr[qi] = pq

        @pl.when(ki > 0)
        def _():
            dq_scr[qi] += pq

        @pl.when(ki == nk - 1)
        def _():
            full = dq_scr[qi]
            for g in range(G):
                dq_ref[:, g * hd:(g + 1) * hd] = full[g * tq:(g + 1) * tq] * scale

    qs = pl.BlockSpec((tq, G * hd), lambda a, j, i: (i, a))
    ks = pl.BlockSpec((tk, hd), lambda a, j, i: (j, a))
    vs = pl.BlockSpec((tk, 2 * hd), lambda a, j, i: (j, a))
    dos = [pl.BlockSpec((tq, hd), lambda a, j, i, g=g: (i, pb + a * G + g)) for g in range(G)]
    st = pl.BlockSpec((G, tq, hd), lambda a, j, i: (a, i, 0))
    dqs = pl.BlockSpec((tq, G * hd), lambda a, j, i: (jnp.where(j == nk - 1, i, 0), a))
    return pl.pallas_call(body, name=name, grid=(KV, nk, nq), in_specs=[qs, ks, vs] + dos + [st, st], out_specs=[dqs, ks, ks],
                          out_shape=[S_((T, QW), F32), S_((T, KW), F32), S_((T, KW), F32)],
                          scratch_shapes=[pltpu.VMEM((nq, G * tq, hd), F32)], compiler_params=_cp())(
                              q, k, v1, *([dmix] * G), lse, delta)


def _pool_cnt(pos, L, w):
    return (jnp.minimum(pos + w // 2, L) - jnp.maximum(pos - w // 2, 0)).astype(F32)


def _pool_mean(ext, gi, w, G, RT, cnt):
    acc = ext[pl.ds(HALO - w // 2, RT), gi * G:(gi + 1) * G]
    for off in range(-w // 2 + 1, w // 2):
        acc = acc + ext[pl.ds(HALO + off, RT), gi * G:(gi + 1) * G]
    return acc / cnt


def _pool_fwd(geo, P, PW, att, pw, ps, name):
    RT = geo.RT
    G = pw.shape[1]
    QW = att.shape[1]

    def body(pp, pc, pn, att_ref, pw_ref, ps_ref, m_ref, ext):
        i = pl.program_id(0)
        _fill_ext(geo, ext, pp, pc, pn, i)
        pos, L = geo.pos(i)
        for gi, w in enumerate(POOL_WINDOWS):
            sl = slice(gi * G, (gi + 1) * G)
            xm = _pool_mean(ext, gi, w, G, RT, _pool_cnt(pos, L, w)) - pc[:, sl]
            m_ref[:, sl] = (_dot(xm, pw_ref[gi], NN) * ps_ref[:, sl]).astype(BF)
        m_ref[:, PW:] = att_ref[...]

    return pl.pallas_call(
        body, name=name, grid=(geo.nt,),
        in_specs=[geo.prev(PW), geo.row(PW), geo.next(PW), geo.row(QW), _full(pw.shape), _full(ps.shape)],
        out_specs=geo.row(PW + QW), out_shape=S_((geo.T, PW + QW), BF),
        scratch_shapes=[pltpu.VMEM((RT + 2 * HALO, PW), F32)], compiler_params=_cp())(P, P, P, att, pw, ps)


def _pool_bwd(geo, P, PW, dmix, pw, ps, name):
    RT = geo.RT
    G = pw.shape[1]
    RE = RT + 2 * HALO

    def body(pp, pc, pn, dp_, dc, dn, pw_ref, ps_ref, dx_ref, dpw, dps, ext, extd, dmc):
        i = pl.program_id(0)
        _fill_ext(geo, ext, pp, pc, pn, i)
        _fill_ext(geo, extd, dp_, dc, dn, i)
        pos, L = geo.pos(i)
        r = lax.broadcasted_iota(jnp.int32, (RE, 1), 0)
        pos_e = pos[0:1, :] - HALO + r
        rows_s = []
        for gi, w in enumerate(POOL_WINDOWS):
            sl = slice(gi * G, (gi + 1) * G)
            xm = _pool_mean(ext, gi, w, G, RT, _pool_cnt(pos, L, w)) - pc[:, sl]
            pre = _dot(xm, pw_ref[gi], NN)
            dout = dc[:, sl]
            rows_s.append(jnp.sum(dout * pre, axis=0, keepdims=True))
            gw = _dot(xm, dout * ps_ref[:, sl], TN)

            @pl.when(i == 0)
            def _(gi=gi, gw=gw):
                dpw[gi] = gw

            @pl.when(i > 0)
            def _(gi=gi, gw=gw):
                dpw[gi] += gw

            dm_e = _dot(extd[:, sl] * ps_ref[:, sl], pw_ref[gi], NT)
            dmc[...] = dm_e / jnp.maximum(_pool_cnt(pos_e, L, w), 1.0)
            acc = -dm_e[HALO:HALO + RT]
            for off in range(-w // 2 + 1, w // 2 + 1):
                acc = acc + dmc[pl.ds(HALO + off, RT), :]
            dx_ref[:, sl] = acc
        _acc_rows(dps, i == 0, [jnp.concatenate(rows_s, axis=-1)])

    return pl.pallas_call(
        body, name=name, grid=(geo.nt,),
        in_specs=[geo.prev(PW), geo.row(PW), geo.next(PW), geo.prev(PW), geo.row(PW), geo.next(PW), _full(pw.shape), _full(ps.shape)],
        out_specs=[geo.row(PW), _full(pw.shape), _full(ps.shape)],
        out_shape=[S_((geo.T, PW), F32), S_(pw.shape, F32), S_(ps.shape, F32)],
        scratch_shapes=[pltpu.VMEM((RE, PW), F32), pltpu.VMEM((RE, PW), F32), pltpu.VMEM((RE, G), F32)],
        compiler_params=_cp())(P, P, P, dmix, dmix, dmix, pw, ps)


def _mod_fwd(A, mod_w, name):
    L, D, MC = mod_w.shape
    tn = _tile(MC, 768, 128)

    def body(a_ref, w_ref, o_ref):
        o_ref[...] = _dot(jax.nn.silu(a_ref[...]), w_ref[...], NN)

    return pl.pallas_call(body, name=name, grid=(L, MC // tn),
                          in_specs=[_full(A.shape), pl.BlockSpec((None, D, tn), lambda l, j: (l, 0, j))],
                          out_specs=pl.BlockSpec((None, 16, tn), lambda l, j: (l, 0, j)), out_shape=S_((L, 16, MC), F32),
                          compiler_params=_cp())(A, mod_w)


def _mod_bwd(A, DM, mod_w, name):
    L, D, MC = mod_w.shape
    tn = _tile(MC, 768, 128)
    nj = MC // tn

    def body(a_ref, dm_ref, w_ref, gw_ref, da_ref, acc):
        l, j = pl.program_id(0), pl.program_id(1)
        sa, vjp = jax.vjp(jax.nn.silu, a_ref[...])
        gw_ref[...] = _dot(sa, dm_ref[...], TN)
        part = _dot(dm_ref[...], w_ref[...], NT)
        first = jnp.logical_and(l == 0, j == 0)

        @pl.when(first)
        def _():
            acc[...] = part

        @pl.when(jnp.logical_not(first))
        def _():
            acc[...] += part

        @pl.when(jnp.logical_and(l == L - 1, j == nj - 1))
        def _():
            da_ref[...] = vjp(acc[...])[0]

    wspec = pl.BlockSpec((None, D, tn), lambda l, j: (l, 0, j))
    return pl.pallas_call(body, name=name, grid=(L, nj),
                          in_specs=[_full(A.shape), pl.BlockSpec((None, 16, tn), lambda l, j: (l, 0, j)), wspec],
                          out_specs=[wspec, _full(A.shape)], out_shape=[S_((L, D, MC), F32), S_(A.shape, F32)],
                          scratch_shapes=[pltpu.VMEM(A.shape, F32)], compiler_params=_cp())(A, DM, mod_w)


PACK_COLS = 1024


def _pack_rows(shape):
    n = 1
    for d in shape:
        n *= d
    return n, -(-n // (8 * PACK_COLS)) * 8


def _pack(arrs):
    parts = []
    for a in arrs:
        n, rows = _pack_rows(a.shape)
        parts.append(jnp.pad(a.reshape(-1).astype(F32), (0, rows * PACK_COLS - n)).reshape(rows, PACK_COLS))
    return jnp.concatenate(parts)


def _unpack(packed, shapes, lead=()):
    out, off = [], 0
    for s in shapes:
        n, rows = _pack_rows(s)
        blk = packed[..., off:off + rows, :].reshape(lead + (rows * PACK_COLS,))
        out.append(blk[..., :n].reshape(lead + tuple(s)))
        off += rows
    return out


def _unshard_last(g):
    g = jnp.moveaxis(g, 0, -2)
    return g.reshape(g.shape[:-2] + (g.shape[-2] * g.shape[-1],))


def _my_shard(a, me):
    n = a.shape[-1] // NDEV
    return lax.dynamic_slice_in_dim(a, me * n, n, axis=a.ndim - 1)


def _rot_tables(S, LC, dk, hd):
    t = jnp.arange(S, dtype=F32)
    n_r = dk // 2
    ang1 = t[:, None] * (RET_THETA ** (-jnp.arange(n_r, dtype=F32) / n_r))
    cos1 = jnp.concatenate([jnp.cos(ang1), jnp.ones((LC, n_r), F32)])
    sin1 = jnp.concatenate([jnp.sin(ang1), jnp.zeros((LC, n_r), F32)])
    n_ax = hd // 4
    f_ax = ROPE_THETA ** (-jnp.arange(n_ax, dtype=F32) / n_ax)
    row = jnp.floor(t / GRID_W)
    col = t - row * GRID_W
    ang2 = jnp.concatenate([row[:, None] * f_ax, col[:, None] * f_ax], axis=-1)
    c2, s2 = jnp.cos(ang2), jnp.sin(ang2)
    cosf = jnp.concatenate([jnp.concatenate([c2, c2], axis=-1), jnp.ones((LC, hd), F32)])
    sinf = jnp.concatenate([jnp.concatenate([-s2, s2], axis=-1), jnp.zeros((LC, hd), F32)])
    return cos1, sin1, cosf, sinf


SMALL = ("c_ctx", "mod_b", "norm_pre", "norm_post", "lru_conv_w", "lru_conv_b", "lru_wa", "lru_ba", "lru_wx", "lru_bx",
         "lru_lambda", "ret_decay_logit", "ret_gn", "pool_w", "pool_scale", "q_norm", "k_norm")
WEIGHTS = ("c_ctx", "mod_w", "mod_b", "norm_pre", "norm_post", "ffn_gate", "ffn_up", "ffn_down", "ev_w_in", "ev_w_out",
           "lru_conv_w", "lru_conv_b", "lru_wa", "lru_ba", "lru_wx", "lru_bx", "lru_lambda", "ret_decay_logit", "ret_gn",
           "od_w_in", "od_w_out", "pool_w", "pool_scale", "q_norm", "k_norm")
INPUTS = ("x", "c", "ctx") + WEIGHTS + ("loss_target",) + tuple("m_" + w for w in WEIGHTS) + tuple("v_" + w for w in WEIGHTS)


def _step(p):
    x, c, ctx = p["x"], p["c"], p["ctx"]
    _, S, D = x.shape
    LC = ctx.shape[1]
    geo = _Geo(S, LC, D)
    T = geo.T
    xi, yi, ci = _me()
    me = 4 * xi + 2 * yi + ci
    L = p["mod_w"].shape[0]
    assert L == 2
    W = p["lru_conv_b"].shape[-1]
    H = p["ret_decay_logit"].shape[-1]
    hd = p["q_norm"].shape[-1]
    G = p["pool_w"].shape[-1]
    PW = G * len(POOL_WINDOWS)
    od_mix = p["od_w_out"].shape[1] * NDEV
    od_in = p["od_w_in"].shape[2] * NDEV
    QW = od_mix - PW
    KW = (od_in - od_mix) // 2
    assert p["ev_w_in"].shape[2] * NDEV == 6 * W and p["ret_gn"].shape[-1] == W and p["ev_w_out"].shape[1] * NDEV == 2 * W
    odims = (PW, QW, KW, hd)
    cos1, sin1, cosf, sinf = _rot_tables(S, LC, W // H, hd)

    sh0 = [(D,), p["norm_pre"].shape, p["norm_post"].shape, p["lru_conv_w"].shape[1:], p["lru_ba"].shape[1:],
           p["lru_bx"].shape[1:], p["lru_lambda"].shape[1:], p["pool_scale"].shape[1:]]
    pack0 = _pack([c, p["norm_pre"], p["norm_post"], p["lru_conv_w"], p["lru_ba"], p["lru_bx"], p["lru_lambda"], p["pool_scale"]])
    (g0,) = _all_gather([pack0], "gather_small")
    c_all, npre, npost, conv_w, ba, bx, lam, pscale = _unpack(g0, sh0, (NDEV,))
    npre, npost, conv_w, ba, bx, lam, pscale = [_unshard_last(a) for a in (npre, npost, conv_w, ba, bx, lam, pscale)]
    pscale = pscale[None]
    conv_b = p["lru_conv_b"]
    wa, wx = p["lru_wa"][0], p["lru_wx"][0]
    gn = p["ret_gn"]
    logits = p["ret_decay_logit"][0].reshape(2, H, 1, 1)
    pool_w = p["pool_w"][0]
    qg, kg = p["q_norm"], p["k_norm"]

    A = jnp.concatenate([c_all, p["c_ctx"][None], jnp.zeros((7, D), F32)])
    M = _mod_fwd(A, p["mod_w"], "mod_fwd")
    (Mg,) = _all_gather([M], "gather_mod")
    MC = M.shape[2]
    tabs = []
    for l in range(L):
        ml = lax.dynamic_index_in_dim(Mg[:, l], me, axis=1, keepdims=False).reshape(NDEV * MC) + p["mod_b"][l]
        mc = Mg[:, l, 8].reshape(NDEV * MC) + p["mod_b"][l]
        tabs.append(jnp.stack([ml.reshape(9, D), mc.reshape(9, D)]))

    def cast2(a, name):
        return _cast_bf16(a.reshape(-1, a.shape[-1]), name).reshape(a.shape)

    loc = {(l, j): [cast2(p[n][l, j], f"cast_{n}_{l}{j}") for n in ("ffn_gate", "ffn_up", "ffn_down")]
           for l in range(L) for j in range(2)}
    loc["ev"] = [cast2(p["ev_w_in"][0], "cast_ev_in"), cast2(p["ev_w_out"][0], "cast_ev_out")]
    loc["od"] = [cast2(p["od_w_in"][0], "cast_od_in"), cast2(p["od_w_out"][0], "cast_od_out")]

    def gather_start(key, name, after):
        return _xchg_start(True, loc[key], [jnp.broadcast_to(a[None], (NDEV,) + a.shape) for a in loc[key]], name, [after])

    ffn_w = {(0, 0): _all_gather(loc[0, 0], "gather_ffn_00")}

    def gp(a, l, s):
        return a[l, s][None]

    st, tok = gather_start("ev", "gs_ev", ffn_w[0, 0][0])
    x0 = jnp.concatenate([x[0], ctx[0]])
    h0, h0t = _norm_fwd(geo, x0, None, (_tie(gp(npre, 0, 0), tok), tabs[0], 0), "pre_00")
    y0, G0, U0 = _ffn_fwd(h0, *ffn_w[0, 0], name="ffn_fwd_00")
    x1, h1, h1t = _norm_fwd(geo, x0, (y0, gp(npost, 0, 0), tabs[0], 0, FFN_STEP), (gp(npre, 0, 1), tabs[0], 1), "post_00")
    ev_in, ev_out = _xchg_wait(st, h1, "gw_ev")
    ev_out_f = ev_out.reshape(2 * W, D)

    st, tok = gather_start((0, 1), "gs_ffn_01", h1)
    Pe = _mm_cols(h1, ev_in, "ev_in", dep=tok)
    u, a0, b0, a1, b1 = _lru_coef_fwd(geo, Pe, W, conv_w, conv_b, wa, ba, wx, bx, lam, "lru_coef")
    hs0, hp0 = _lru_scan_fwd(geo, a0, b0, 0, "lru_scan_f0")
    hs1, hp1 = _lru_scan_fwd(geo, a1, b1, 1, "lru_scan_f1")
    o0, st0, o1, st1 = _ret_fwd(geo, Pe, W, H, logits, cos1, sin1, "ret_fwd")
    mixe = _mix_even_fwd(geo, Pe, W, H, hs0, hs1, o0, o1, gn, "mix_even")
    y1 = _mm_full(mixe, ev_out_f, NN, "ev_out")
    x2, h2, h2t = _norm_fwd(geo, x1, (y1, gp(npost, 0, 1), tabs[0], 1, 1.0), (gp(npre, 0, 2), tabs[0], 2), "post_01")
    ffn_w[0, 1] = _xchg_wait(st, h2, "gw_ffn_01")

    st, tok = gather_start((1, 0), "gs_ffn_10", h2)
    y2, G2, U2 = _ffn_fwd(h2, *ffn_w[0, 1], name="ffn_fwd_01", dep=tok)
    x3, h3, h3t = _norm_fwd(geo, x2, (y2, gp(npost, 0, 2), tabs[0], 2, FFN_STEP), (gp(npre, 1, 0), tabs[1], 0), "post_02")
    ffn_w[1, 0] = _xchg_wait(st, h3, "gw_ffn_10")

    st, tok = gather_start("od", "gs_od", h3)
    y3, G3, U3 = _ffn_fwd(h3, *ffn_w[1, 0], name="ffn_fwd_10", dep=tok)
    x4, h4, h4t = _norm_fwd(geo, x3, (y3, gp(npost, 1, 0), tabs[1], 0, FFN_STEP), (gp(npre, 1, 1), tabs[1], 1), "post_10")
    od_inw, od_out = _xchg_wait(st, h4, "gw_od")
    od_out_f = od_out.reshape(od_mix, D)

    st, tok = gather_start((1, 1), "gs_ffn_11", h4)
    Po = _mm_cols(h4, od_inw, "od_in", dep=tok)
    qr, kr, vr = _qk_prep_fwd(geo, Po, odims, qg, kg, cosf, sinf, "qk_prep")
    att, lse = _att_fwd(qr, kr, vr, hd, "att_fwd")
    mixo = _pool_fwd(geo, Po, PW, att, pool_w, pscale, "pool_fwd")
    y4 = _mm_full(mixo, od_out_f, NN, "od_out")
    x5, h5, h5t = _norm_fwd(geo, x4, (y4, gp(npost, 1, 1), tabs[1], 1, 1.0), (gp(npre, 1, 2), tabs[1], 2), "post_11")
    ffn_w[1, 1] = _xchg_wait(st, h5, "gw_ffn_11")

    y5, G5, U5 = _ffn_fwd(h5, *ffn_w[1, 1], name="ffn_fwd_11")
    (x6,) = _norm_fwd(geo, x5, (y5, gp(npost, 1, 2), tabs[1], 2, FFN_STEP), None, "post_12")

    big_g = {}
    tokbox = [None]

    def gpt(l, s):
        return _tie(gp(npre, l, s), tokbox[0])

    def a2a_start(key, srcs, name, dh):
        own = [lax.dynamic_index_in_dim(a, me, 0, keepdims=False) for a in srcs]
        state, token = _xchg_start(False, srcs, [jnp.zeros(a.shape, a.dtype) for a in srcs], name)
        big_g[key] = (state, own)
        tokbox[0] = token
        return dh

    def ffn_bwd(dy, h, Gs, Us, key):
        tag = f"{key[0]}{key[1]}"
        dh, dG, dU, Aact = _ffn_bwd_act(dy, Gs, Us, *ffn_w[key], name=f"ffn_bwd_{tag}")
        if key != (0, 0):
            srcs = [_ffn_wgrad_in(h, dG, f"ffn_wg_{tag}"), _ffn_wgrad_in(h, dU, f"ffn_wu_{tag}"), _ffn_wgrad_out(Aact, dy, f"ffn_wd_{tag}")]
            return a2a_start(key, srcs, f"as_ffn_{tag}", dh)
        parts = [None] * 3
        a2a_start(key, [_ffn_wgrad_out(Aact, dy, f"ffn_wd_{tag}")], f"as_ffn_{tag}_d", dh)
        parts[2] = big_g[key]
        a2a_start(key, [_ffn_wgrad_in(h, dG, f"ffn_wg_{tag}", dep=tokbox[0])], f"as_ffn_{tag}_g", dh)
        parts[0] = big_g[key]
        a2a_start(key, [_ffn_wgrad_in(h, dU, f"ffn_wu_{tag}", dep=tokbox[0])], f"as_ffn_{tag}_u", dh)
        parts[1] = big_g[key]
        big_g[key] = parts
        return dh

    loss_p, dx6, dy5, dpost5 = _loss_bwd(geo, x6, p["loss_target"][0], (y5, gp(npost, 1, 2), tabs[1], 2, FFN_STEP), "loss")
    dh5 = ffn_bwd(dy5, h5t, G5, U5, (1, 1))
    dx5, dy4, dpre5, dpost4 = _norm_bwd(geo, dx6, dh5, x5, (gpt(1, 2), tabs[1], 2),
                                        (y4, gp(npost, 1, 1), tabs[1], 1, 1.0), "nb_5")

    dmixo = _mm_full(dy4, od_out_f, NT, "od_out_d")
    g_od_out = _mm_tn_rows(mixo, dy4, NDEV, "od_out_w")
    dpool, g_pool_w, g_pscale = _pool_bwd(geo, Po, PW, dmixo, pool_w, pscale, "pool_bwd")
    delta = _att_delta(geo, att, dmixo, PW, hd, "att_delta")
    dq, dk, dv = _att_bwd(qr, kr, vr, dmixo, PW, lse, delta, hd, "att_bwd")
    dPo, g_qn, g_kn = _qk_prep_bwd(geo, Po, odims, qg, kg, cosf, sinf, dpool, dq, dk, dv, "qk_prep_bwd")
    dh4 = _mm_nt_cols(dPo, od_inw, "od_in_d")
    g_od_in = _mm_tn_cols(h4t, dPo, NDEV, "od_in_w")
    dh4 = a2a_start("od", [g_od_in, g_od_out], "as_od", dh4)
    dx4, dy3, dpre4, dpost3 = _norm_bwd(geo, dx5, dh4, x4, (gpt(1, 1), tabs[1], 1),
                                        (y3, gp(npost, 1, 0), tabs[1], 0, FFN_STEP), "nb_4")

    dh3 = ffn_bwd(dy3, h3t, G3, U3, (1, 0))
    dx3, dy2, dpre3, dpost2 = _norm_bwd(geo, dx4, dh3, x3, (gpt(1, 0), tabs[1], 0),
                                        (y2, gp(npost, 0, 2), tabs[0], 2, FFN_STEP), "nb_3")

    dh2 = ffn_bwd(dy2, h2t, G2, U2, (0, 1))
    dx2, dy1, dpre2, dpost1 = _norm_bwd(geo, dx3, dh2, x2, (gpt(0, 2), tabs[0], 2),
                                        (y1, gp(npost, 0, 1), tabs[0], 1, 1.0), "nb_2")

    dmixe = _mm_full(dy1, ev_out_f, NT, "ev_out_d")
    g_ev_out = _mm_tn_rows(mixe, dy1, NDEV, "ev_out_w")
    dg, dhs, dos, dog, g_gn = _mix_even_bwd(geo, Pe, W, H, hs0, hs1, o0, o1, gn, dmixe, "mix_even_bwd")
    da0, db0 = _lru_scan_bwd(geo, dhs, a0, hp0, 0, "lru_scan_b0")
    da1, db1 = _lru_scan_bwd(geo, dhs, a1, hp1, 1, "lru_scan_b1")
    du, g_wa, g_ba, g_wx, g_bx, g_lam = _lru_coef_bwd(geo, u, (da0, da1), (db0, db1), W, wa, ba, wx, bx, lam, "lru_coef_bwd")
    dq0, dk0, dv0, glg0, dq1, dk1, dv1, glg1 = _ret_bwd(geo, Pe, W, H, logits, cos1, sin1, (st0, st1), dos, "ret_bwd")
    dPe, g_cw, g_cb = _conv_bwd_assemble(geo, Pe, du, (dg, dq0, dk0, dv0, dq1, dk1, dv1, dog), W, conv_w, "conv_bwd")
    dh1 = _mm_nt_cols(dPe, ev_in, "ev_in_d")
    g_ev_in = _mm_tn_cols(h1t, dPe, NDEV, "ev_in_w")
    dh1 = a2a_start("ev", [g_ev_in, g_ev_out], "as_ev", dh1)
    dx1, dy0, dpre1, dpost0 = _norm_bwd(geo, dx2, dh1, x1, (gpt(0, 1), tabs[0], 1),
                                        (y0, gp(npost, 0, 0), tabs[0], 0, FFN_STEP), "nb_1")

    dh0 = ffn_bwd(dy0, h0t, G0, U0, (0, 0))
    dx0, dpre0 = _norm_bwd(geo, dx1, dh0, x0, (gpt(0, 0), tabs[0], 0), None, "nb_0")

    dpre = [[dpre0, dpre1, dpre2], [dpre3, dpre4, dpre5]]
    dpost = [[dpost0, dpost1, dpost2], [dpost3, dpost4, dpost5]]
    dtab = jnp.stack([jnp.stack([jnp.stack([dpre[l][s][:, 1], dpre[l][s][:, 2], dpost[l][s][:, 1]], axis=1) for s in range(3)], axis=1)
                      for l in range(L)])
    dtab_p = _pack([jnp.moveaxis(dtab.reshape(L, 2, 9 * D), 1, 0)])
    (dtab_g,) = _all_gather([dtab_p], "gather_dtab")
    dtab_sum = _sum_n(dtab_g, "sum_dtab")
    (dm_all,) = _unpack(dtab_g, [(2, L, 9 * D)], (NDEV,))
    (dm_sum,) = _unpack(dtab_sum, [(2, L, 9 * D)])
    (g_mod_b,) = _unpack(_sum_n(jnp.stack([_pack([dm_sum[0]]), _pack([dm_sum[1]])]), "sum_mod_b"), [(L, 9 * D)])
    dml = lax.dynamic_slice_in_dim(dm_all[:, 0], me * MC, MC, axis=2)
    dmc = lax.dynamic_slice_in_dim(dm_sum[1], me * MC, MC, axis=1)
    DM = jnp.concatenate([jnp.moveaxis(dml, 0, 1), dmc[:, None], jnp.zeros((L, 7, MC), F32)], axis=1)
    g_mod_w, dA = _mod_bwd(A, DM, p["mod_w"], "mod_bwd")

    g_npre = jnp.stack([jnp.stack([dpre[l][s][0, 0] + dpre[l][s][1, 0] for s in range(3)]) for l in range(L)])
    g_npost = jnp.stack([jnp.stack([dpost[l][s][0, 0] + dpost[l][s][1, 0] for s in range(3)]) for l in range(L)])
    g_logit = jnp.stack([glg0.reshape(H), glg1.reshape(H)])
    small_parts = [dA[8], g_npre, g_npost, g_cw, g_cb, g_wa, g_ba, g_wx, g_bx, g_lam, g_logit, g_gn, g_pool_w, g_pscale, g_qn, g_kn]
    (sg,) = _all_gather([_pack(small_parts)], "gather_small_g")
    ssum = _unpack(_sum_n(sg, "sum_small_g"), [a.shape for a in small_parts])
    (g_cctx, g_npre, g_npost, g_cw, g_cb, g_wa, g_ba, g_wx, g_bx, g_lam, g_logit, g_gn, g_pool_w, g_pscale, g_qn, g_kn) = ssum
    small_g = {
        "c_ctx": g_cctx, "mod_b": g_mod_b, "norm_pre": _my_shard(g_npre, me), "norm_post": _my_shard(g_npost, me),
        "lru_conv_w": _my_shard(g_cw, me)[None], "lru_conv_b": g_cb, "lru_wa": g_wa[None], "lru_ba": _my_shard(g_ba, me)[None],
        "lru_wx": g_wx[None], "lru_bx": _my_shard(g_bx, me)[None], "lru_lambda": _my_shard(g_lam, me)[None],
        "ret_decay_logit": g_logit[None], "ret_gn": g_gn, "pool_w": g_pool_w[None], "pool_scale": _my_shard(g_pscale, me),
        "q_norm": g_qn, "k_norm": g_kn,
    }
    shapes = [p[n].shape for n in SMALL]
    s_out = _reduce_adam(_pack([small_g[n] for n in SMALL])[None], _pack([p[n] for n in SMALL]),
                         _pack([p["m_" + n] for n in SMALL]), _pack([p["v_" + n] for n in SMALL]), "adam_small")
    res = {}
    for kind, packed in zip(("g", "d", "m", "v"), s_out):
        for n, a in zip(SMALL, _unpack(packed, shapes)):
            res[kind, n] = a

    def big(name, pieces, own, idx=None):
        w, m, v = p[name], p["m_" + name], p["v_" + name]
        if idx is not None:
            w, m, v = w[idx], m[idx], v[idx]
        shp = w.shape
        tag = name + ("" if idx is None else "_" + "".join(str(i) for i in idx))
        outs = _reduce_adam(pieces.reshape((pieces.shape[0], -1, shp[-1])), w.reshape(-1, shp[-1]), m.reshape(-1, shp[-1]),
                            v.reshape(-1, shp[-1]), "adam_" + tag, None if own is None else own.reshape(-1, shp[-1]))
        return [o.reshape(shp) for o in outs]

    got = {}
    after = [dx0]
    for key in ((1, 1), "od", (1, 0), (0, 1), "ev", (0, 0)):
        tag = key if isinstance(key, str) else f"ffn_{key[0]}{key[1]}"
        if key != (0, 0):
            state, own = big_g[key]
        else:
            for name, (pieces, own_) in (("mod_w", (g_mod_w[None], None)), ("ev_w_in", got["ev"][0]), ("ev_w_out", got["ev"][1]),
                                         ("od_w_in", got["od"][0]), ("od_w_out", got["od"][1])):
                outs = big(name, pieces, own_, None if name == "mod_w" else (0,))
                for kind, o in zip(("g", "d", "m", "v"), outs):
                    res[kind, name] = o if name == "mod_w" else o[None]
            after = [s_out[0]] + [res["g", n] for n in ("mod_w", "ev_w_in", "ev_w_out", "od_w_in", "od_w_out")]
            got[key] = [None] * 3
            for wi in (2, 0, 1):
                st_w, own_w = big_g[key][wi]
                lands = _xchg_wait(st_w, after, f"aw_{tag}_{wi}")
                got[key][wi] = (lands[0], own_w[0])
                after = [lands[0]]
            continue
        lands = _xchg_wait(state, after, "aw_" + tag)
        got[key] = list(zip(lands, own))
        after = [lands[0]]

    keys = [(l, j) for l in range(L) for j in range(2)]
    for wi, name in enumerate(("ffn_gate", "ffn_up", "ffn_down")):
        shp = p[name].shape
        st3 = (len(keys), shp[-2], shp[-1])
        outs = _reduce_adam_stack([got[k][wi][0] for k in keys], [got[k][wi][1] for k in keys], p[name].reshape(st3),
                                  p["m_" + name].reshape(st3), p["v_" + name].reshape(st3), "adam_" + name)
        for kind, o in zip(("g", "d", "m", "v"), outs):
            res[kind, name] = o.reshape(shp)

    loss = lax.psum(loss_p[0, 0], ("x", "y", "c"))
    grad_x = dx0[:S][None]
    return (loss, grad_x) + tuple(res[kind, n] for kind in ("g", "d", "m", "v") for n in WEIGHTS)


def kernel(
        x, c, ctx, c_ctx, mod_w, mod_b, norm_pre, norm_post, ffn_gate, ffn_up, ffn_down, ev_w_in, ev_w_out, lru_conv_w,
        lru_conv_b, lru_wa, lru_ba, lru_wx, lru_bx, lru_lambda, ret_decay_logit, ret_gn, od_w_in, od_w_out, pool_w,
        pool_scale, q_norm, k_norm, loss_target, m_c_ctx, m_mod_w, m_mod_b, m_norm_pre, m_norm_post, m_ffn_gate, m_ffn_up,
        m_ffn_down, m_ev_w_in, m_ev_w_out, m_lru_conv_w, m_lru_conv_b, m_lru_wa, m_lru_ba, m_lru_wx, m_lru_bx, m_lru_lambda,
        m_ret_decay_logit, m_ret_gn, m_od_w_in, m_od_w_out, m_pool_w, m_pool_scale, m_q_norm, m_k_norm, v_c_ctx, v_mod_w,
        v_mod_b, v_norm_pre, v_norm_post, v_ffn_gate, v_ffn_up, v_ffn_down, v_ev_w_in, v_ev_w_out, v_lru_conv_w,
        v_lru_conv_b, v_lru_wa, v_lru_ba, v_lru_wx, v_lru_bx, v_lru_lambda, v_ret_decay_logit, v_ret_gn, v_od_w_in,
        v_od_w_out, v_pool_w, v_pool_scale, v_q_norm, v_k_norm):
    args = locals()
    return _step({n: args[n] for n in INPUTS})
```

```python
import functools

import jax
import jax.numpy as jnp
from jax import lax
from jax.experimental import pallas as pl
from jax.experimental.pallas import tpu as pltpu

F32 = jnp.float32
BF = jnp.bfloat16
S_ = jax.ShapeDtypeStruct
MESH = pl.DeviceIdType.MESH

NDEV = 8
EPS = 1e-6
FFN_STEP = 0.5
LRU_C = 8.0
RET_CHUNK = 128
RET_THETA = 10000.0
ROPE_THETA = 10000.0
GRID_W = 64
POOL_WINDOWS = (2, 4, 8, 16)
ROW_TILE = 256
HALO = 8
VMEM_LIMIT = 58 * 1024 * 1024
FFN_FWD_ROWS = 768
FFN_BWD_ROWS = 528
FFN_FWD_SPLIT = 1
FFN_BWD_SPLIT = 1
WGRAD_ROWS = 1408

ADAM_LR = 0.001
ADAM_B1 = 0.9
ADAM_B2 = 0.999
ADAM_EPS = 1e-08
ADAM_WD = 0.01
ADAM_STEP = 10

LOG2E = 1.4426950408889634
LN2 = 0.6931471805599453

NN = ((1,), (0,))
NT = ((1,), (1,))
TN = ((0,), (0,))


def _dot(a, b, dn):
    return lax.dot_general(a.astype(BF), b.astype(BF), (dn, ((), ())), preferred_element_type=F32)


@functools.partial(jax.custom_vjp, nondiff_argnums=(2,))
def _bdot(a, b, mode):
    return _dot(a, b, {"nn": NN, "nt": NT, "tn": TN}[mode])


def _bdot_fwd(a, b, mode):
    return _bdot(a, b, mode), (a, b)


def _bdot_bwd(mode, res, g):
    a, b = res
    if mode == "nn":
        return _dot(g, b, NT), _dot(a, g, TN)
    if mode == "nt":
        return _dot(g, b, NN), _dot(g, a, TN)
    return _dot(b, g, NT), _dot(a, g, NN)


_bdot.defvjp(_bdot_fwd, _bdot_bwd)


@functools.partial(jax.custom_vjp, nondiff_argnums=(1, 2))
def _rollv(x, shift, axis):
    return pltpu.roll(x, shift, axis)


def _rollv_fwd(x, shift, axis):
    return pltpu.roll(x, shift, axis), None


def _rollv_bwd(shift, axis, _, g):
    n = g.shape[axis]
    return (pltpu.roll(g, (n - shift) % n, axis),)


_rollv.defvjp(_rollv_fwd, _rollv_bwd)


def _cp(vmem=VMEM_LIMIT):
    return pltpu.CompilerParams(vmem_limit_bytes=vmem)


def _tile(n, pref, mult=8):
    if n <= pref:
        return n
    for t in range(pref, 0, -1):
        if n % t == 0 and t % mult == 0:
            return t
    return n


def _full(shape):
    nd = len(shape)
    return pl.BlockSpec(tuple(shape), lambda *_: (0,) * nd)


def _me():
    return lax.axis_index("x"), lax.axis_index("y"), lax.axis_index("c")


def _all_gather(arrs, name):
    n = len(arrs)

    def body(*refs):
        xs, outs = refs[:n], refs[n:2 * n]
        send_sems, recv_sems, local_sems = refs[2 * n:]
        x, y, c = _me()
        me, sibling = (x, y, c), (x, y, 1 - c)
        chips = [(1 - x, y), (x, 1 - y), (1 - x, 1 - y)]

        def blk(out, p):
            return out.at[4 * p[0] + 2 * p[1] + p[2]]

        def copy(a, k, block, to, src=None):
            return pltpu.make_async_remote_copy(
                src_ref=blk(outs[a], block) if src is None else src, dst_ref=blk(outs[a], block),
                send_sem=send_sems.at[a, k], recv_sem=recv_sems.at[a, k], device_id=to, device_id_type=MESH)

        mine = [pltpu.make_async_copy(xs[a], blk(outs[a], me), local_sems.at[a]) for a in range(n)]
        for cp in mine:
            cp.start()
        first = []
        for a in range(n):
            first.append(copy(a, 0, me, sibling, src=xs[a]))
            first += [copy(a, 1 + j, me, (*chip, c), src=xs[a]) for j, chip in enumerate(chips)]
        for cp in first:
            cp.start()
        passed = []
        for j, chip in enumerate(chips):
            for a in range(n):
                copy(a, 1 + j, (*chip, c), me).wait_recv()
                fw = copy(a, 4 + j, (*chip, c), sibling)
                fw.start()
                passed.append(fw)
        for a in range(n):
            copy(a, 0, sibling, me).wait_recv()
            for j, chip in enumerate(chips):
                copy(a, 4 + j, (*chip, 1 - c), me).wait_recv()
        for cp in first + passed:
            cp.wait_send()
        for cp in mine:
            cp.wait()

    anyspec = pl.BlockSpec(memory_space=pl.ANY)
    return pl.pallas_call(
        body, name=name,
        out_shape=[S_((NDEV,) + a.shape, a.dtype) for a in arrs],
        in_specs=[anyspec] * n, out_specs=[anyspec] * n,
        scratch_shapes=[pltpu.SemaphoreType.DMA((n, 7)), pltpu.SemaphoreType.DMA((n, 7)), pltpu.SemaphoreType.DMA((n,))],
    )(*arrs)


def _all_to_all(arrs, name):
    n = len(arrs)

    def body(*refs):
        xs, outs = refs[:n], refs[n:2 * n]
        send_sems, recv_sems, local_sems = refs[2 * n:]
        x, y, c = _me()
        me_idx = 4 * x + 2 * y + c
        mine = [pltpu.make_async_copy(xs[a].at[me_idx], outs[a].at[me_idx], local_sems.at[a]) for a in range(n)]
        for cp in mine:
            cp.start()
        copies = []
        for k in range(1, NDEV):
            kx, ky, kc = (k >> 2) & 1, (k >> 1) & 1, k & 1
            px = 1 - x if kx else x
            py = 1 - y if ky else y
            pc = 1 - c if kc else c
            p_idx = 4 * px + 2 * py + pc
            for a in range(n):
                copies.append(pltpu.make_async_remote_copy(
                    src_ref=xs[a].at[p_idx], dst_ref=outs[a].at[me_idx],
                    send_sem=send_sems.at[a, k - 1], recv_sem=recv_sems.at[a, k - 1],
                    device_id=(px, py, pc), device_id_type=MESH))
        for cp in copies:
            cp.start()
        for cp in copies:
            cp.wait_recv()
        for cp in copies:
            cp.wait_send()
        for cp in mine:
            cp.wait()

    anyspec = pl.BlockSpec(memory_space=pl.ANY)
    return pl.pallas_call(
        body, name=name,
        out_shape=[S_(a.shape, a.dtype) for a in arrs],
        in_specs=[anyspec] * n, out_specs=[anyspec] * n,
        scratch_shapes=[pltpu.SemaphoreType.DMA((n, 7)), pltpu.SemaphoreType.DMA((n, 7)), pltpu.SemaphoreType.DMA((n,))],
    )(*arrs)


HBM_SPEC = pl.BlockSpec(memory_space=pltpu.HBM)
SEM_SPEC = pl.BlockSpec(memory_space=pltpu.SEMAPHORE)
EFFECT = pltpu.SideEffectType.DATAFLOW_SIDE_EFFECTING


def _peers():
    x, y, c = _me()
    out = []
    for k in range(1, NDEV):
        px = 1 - x if (k >> 2) & 1 else x
        py = 1 - y if (k >> 1) & 1 else y
        pc = 1 - c if k & 1 else c
        out.append(((px, py, pc), 4 * px + 2 * py + pc))
    return out, 4 * x + 2 * y + c


def _xchg_copies(gather, xs, lands, send, recv):
    peers, me_idx = _peers()
    out = []
    for k, (dev, p_idx) in enumerate(peers):
        for a in range(len(xs)):
            out.append(pltpu.make_async_remote_copy(
                src_ref=xs[a] if gather else xs[a].at[p_idx], dst_ref=lands[a].at[me_idx],
                send_sem=send[a].at[k], recv_sem=recv[a].at[k], device_id=dev, device_id_type=MESH))
    return out


def _xchg_start(gather, xs, lands, name, after=()):
    n = len(xs)
    na = len(after)

    def body(*refs):
        xr, lr = refs[:n], refs[n:2 * n]
        outs = refs[2 * n + na:]
        for cp in _xchg_copies(gather, xr, lr, outs[:n], outs[n:2 * n]):
            cp.start()
        outs[4 * n][...] = jnp.zeros(outs[4 * n].shape, F32)

    ops = [pltpu.with_memory_space_constraint(a, pltpu.HBM) for a in list(xs) + list(lands)]
    outs = pl.pallas_call(
        body, name=name,
        out_shape=[pltpu.SemaphoreType.DMA((NDEV - 1,))] * (2 * n) + [pltpu.HBM(a.shape, a.dtype) for a in ops]
        + [S_((8, 128), F32)],
        in_specs=[HBM_SPEC] * (2 * n) + [pl.BlockSpec(memory_space=pl.ANY)] * na,
        out_specs=[SEM_SPEC] * (2 * n) + [HBM_SPEC] * (2 * n) + [pl.BlockSpec(memory_space=pltpu.VMEM)],
        input_output_aliases={i: 2 * n + i for i in range(2 * n)},
        compiler_params=pltpu.CompilerParams(has_side_effects=EFFECT),
    )(*ops, *after)
    return (gather, n, outs[:4 * n]), outs[4 * n]


def _xchg_wait(state, after, name):
    gather, n, st = state
    send, recv, xs, lands = st[:n], st[n:2 * n], st[2 * n:3 * n], st[3 * n:4 * n]
    after = list(after) if isinstance(after, (list, tuple)) else [after]

    def body(*refs):
        xr, lr = refs[:n], refs[n:2 * n]
        sr, rr = refs[2 * n:3 * n], refs[3 * n:4 * n]
        for cp in _xchg_copies(gather, xr, lr, sr, rr):
            cp.wait_send()
            cp.wait_recv()

    outs = pl.pallas_call(
        body, name=name,
        out_shape=[pltpu.HBM(a.shape, a.dtype) for a in list(xs) + list(lands)],
        in_specs=[HBM_SPEC] * (2 * n) + [SEM_SPEC] * (2 * n) + [pl.BlockSpec(memory_space=pl.ANY)] * len(after),
        out_specs=[HBM_SPEC] * (2 * n), input_output_aliases={i: i for i in range(2 * n)},
        compiler_params=pltpu.CompilerParams(has_side_effects=EFFECT),
    )(*xs, *lands, *send, *recv, *after)
    return outs[n:]


def _tie(a, token):
    return a + token[0, 0].astype(a.dtype)


def _cast_bf16(a, name):
    R, C = a.shape
    tr = _tile(R, 512, 16)

    def body(a_ref, o_ref):
        o_ref[...] = a_ref[...].astype(BF)

    return pl.pallas_call(body, name=name, grid=(R // tr,), in_specs=[pl.BlockSpec((tr, C), lambda i: (i, 0))],
                          out_specs=pl.BlockSpec((tr, C), lambda i: (i, 0)), out_shape=S_((R, C), BF), compiler_params=_cp())(a)


def _sum_n(a, name):
    n, R, C = a.shape
    tr = _tile(R, 256, 8)

    def body(a_ref, o_ref):
        acc = a_ref[0].astype(F32)
        for i in range(1, n):
            acc = acc + a_ref[i].astype(F32)
        o_ref[...] = acc

    return pl.pallas_call(body, name=name, grid=(R // tr,), in_specs=[pl.BlockSpec((n, tr, C), lambda i: (0, i, 0))],
                          out_specs=pl.BlockSpec((tr, C), lambda i: (i, 0)), out_shape=S_((R, C), F32), compiler_params=_cp())(a)


def _adam_math(w, g, m, v):
    m = ADAM_B1 * m + (1.0 - ADAM_B1) * g
    v = ADAM_B2 * v + (1.0 - ADAM_B2) * jnp.square(g)
    m_hat = m / (1.0 - ADAM_B1 ** ADAM_STEP)
    v_hat = v / (1.0 - ADAM_B2 ** ADAM_STEP)
    delta = -ADAM_LR * (m_hat / (jnp.sqrt(v_hat) + ADAM_EPS) + ADAM_WD * w)
    return delta, m, v


def _reduce_adam(pieces, w, m, v, name, own=None):
    n, R, C = pieces.shape
    tr = _tile(R, 256, 8)

    def body(*refs):
        p_ref, w_ref, m_ref, v_ref = refs[:4]
        g_ref, d_ref, mo_ref, vo_ref = refs[-4:]
        g = p_ref[0].astype(F32)
        for i in range(1, n):
            g = g + p_ref[i].astype(F32)
        if own is not None:
            g = g + refs[4][...].astype(F32)
        d, mn, vn = _adam_math(w_ref[...], g, m_ref[...], v_ref[...])
        g_ref[...] = g
        d_ref[...] = d
        mo_ref[...] = mn
        vo_ref[...] = vn

    row = pl.BlockSpec((tr, C), lambda i: (i, 0))
    ins = [pieces, w, m, v] + ([] if own is None else [own])
    return pl.pallas_call(body, name=name, grid=(R // tr,),
                          in_specs=[pl.BlockSpec((n, tr, C), lambda i: (0, i, 0))] + [row] * (len(ins) - 1),
                          out_specs=[row] * 4, out_shape=[S_((R, C), F32)] * 4, compiler_params=_cp())(*ins)


def _reduce_adam_stack(pieces, owns, w, m, v, name):
    F = len(pieces)
    n, R, C = pieces[0].shape
    tr = _tile(R, max(8, 131072 // C), 8)

    def body(*refs):
        w_ref, m_ref, v_ref = refs[2 * F:2 * F + 3]
        g_ref, d_ref, mo_ref, vo_ref = refs[-4:]
        f = pl.program_id(0)
        for ff in range(F):
            @pl.when(f == ff)
            def _(ff=ff):
                g = refs[ff][0].astype(F32)
                for i in range(1, n):
                    g = g + refs[ff][i].astype(F32)
                g = g + refs[F + ff][...].astype(F32)
                d, mn, vn = _adam_math(w_ref[...], g, m_ref[...], v_ref[...])
                g_ref[...] = g
                d_ref[...] = d
                mo_ref[...] = mn
                vo_ref[...] = vn

    pspecs = [pl.BlockSpec((n, tr, C), lambda f, r, ff=ff: (0, jnp.where(f == ff, r, 0), 0)) for ff in range(F)]
    ospecs = [pl.BlockSpec((tr, C), lambda f, r, ff=ff: (jnp.where(f == ff, r, 0), 0)) for ff in range(F)]
    st = pl.BlockSpec((None, tr, C), lambda f, r: (f, r, 0))
    return pl.pallas_call(body, name=name, grid=(F, R // tr), in_specs=pspecs + ospecs + [st] * 3, out_specs=[st] * 4,
                          out_shape=[S_((F, R, C), F32)] * 4, compiler_params=_cp())(*pieces, *owns, w, m, v)


def _mm_cols(a, wb, name, out_dtype=F32, dep=None):
    M, K = a.shape
    NB, _, nb = wb.shape
    tm = _tile(M, 768, 16)

    def body(*refs):
        refs[-1][...] = _dot(refs[0][...], refs[1][...], NN).astype(out_dtype)

    deps = [] if dep is None else [dep]
    return pl.pallas_call(body, name=name, grid=(M // tm, NB),
                          in_specs=[pl.BlockSpec((tm, K), lambda i, j: (i, 0)), pl.BlockSpec((None, K, nb), lambda i, j: (j, 0, 0))]
                          + [_full(d.shape) for d in deps],
                          out_specs=pl.BlockSpec((tm, nb), lambda i, j: (i, j)), out_shape=S_((M, NB * nb), out_dtype),
                          compiler_params=_cp())(a, wb, *deps)


def _mm_nt_cols(g, wb, name):
    M = g.shape[0]
    NB, K, nb = wb.shape
    tm = _tile(M, 768, 16)

    def body(g_ref, w_ref, o_ref):
        j = pl.program_id(1)
        part = _dot(g_ref[...], w_ref[...], NT)

        @pl.when(j == 0)
        def _():
            o_ref[...] = part

        @pl.when(j > 0)
        def _():
            o_ref[...] += part

    return pl.pallas_call(body, name=name, grid=(M // tm, NB),
                          in_specs=[pl.BlockSpec((tm, nb), lambda i, j: (i, j)), pl.BlockSpec((None, K, nb), lambda i, j: (j, 0, 0))],
                          out_specs=pl.BlockSpec((tm, K), lambda i, j: (i, 0)), out_shape=S_((M, K), F32),
                          compiler_params=_cp())(g, wb)


def _mm_full(a, w, dn, name, out_dtype=F32):
    M, K = a.shape
    N = w.shape[1] if dn == NN else w.shape[0]
    tm = _tile(M, 768, 16)

    def body(a_ref, w_ref, o_ref):
        o_ref[...] = _dot(a_ref[...], w_ref[...], dn).astype(out_dtype)

    return pl.pallas_call(body, name=name, grid=(M // tm,),
                          in_specs=[pl.BlockSpec((tm, K), lambda i: (i, 0)), _full(w.shape)],
                          out_specs=pl.BlockSpec((tm, N), lambda i: (i, 0)), out_shape=S_((M, N), out_dtype),
                          compiler_params=_cp())(a, w)


def _mm_tn_cols(at, g, NB, name):
    K, M = at.shape
    nb = g.shape[1] // NB
    tk = _tile(M, WGRAD_ROWS, 128)
    nk = M // tk

    def body(a_ref, g_ref, o_ref, acc):
        k = pl.program_id(1)
        part = _dot(a_ref[...], g_ref[...], NN)

        @pl.when(k == 0)
        def _():
            acc[...] = part

        @pl.when(k > 0)
        def _():
            acc[...] += part

        @pl.when(k == nk - 1)
        def _():
            o_ref[...] = acc[...].astype(BF)

    return pl.pallas_call(body, name=name, grid=(NB, nk),
                          in_specs=[pl.BlockSpec((K, tk), lambda b, k: (0, k)), pl.BlockSpec((tk, nb), lambda b, k: (k, b))],
                          out_specs=pl.BlockSpec((None, K, nb), lambda b, k: (b, 0, 0)), out_shape=S_((NB, K, nb), BF),
                          scratch_shapes=[pltpu.VMEM((K, nb), F32)], compiler_params=_cp())(at, g)


def _mm_tn_rows(a, g, NB, name):
    M = a.shape[0]
    kb = a.shape[1] // NB
    N = g.shape[1]
    tk = _tile(M, WGRAD_ROWS, 128)
    nk = M // tk

    def body(a_ref, g_ref, o_ref, acc):
        k = pl.program_id(1)
        part = _dot(a_ref[...], g_ref[...], TN)

        @pl.when(k == 0)
        def _():
            acc[...] = part

        @pl.when(k > 0)
        def _():
            acc[...] += part

        @pl.when(k == nk - 1)
        def _():
            o_ref[...] = acc[...].astype(BF)

    return pl.pallas_call(body, name=name, grid=(NB, nk),
                          in_specs=[pl.BlockSpec((tk, kb), lambda b, k: (k, b)), pl.BlockSpec((tk, N), lambda b, k: (k, 0))],
                          out_specs=pl.BlockSpec((None, kb, N), lambda b, k: (b, 0, 0)), out_shape=S_((NB, kb, N), BF),
                          scratch_shapes=[pltpu.VMEM((kb, N), F32)], compiler_params=_cp())(a, g)


def _ffn_fwd(h, wg, wu, wd, name, dep=None):
    T, D = h.shape
    NB, _, nb = wg.shape
    tm = _tile(T, FFN_FWD_ROWS, 16)

    def body(*refs):
        h_ref, wg_ref, wu_ref, wd_ref = refs[:4]
        y_ref, g_ref, u_ref = refs[-3:]
        b = pl.program_id(1)
        parts = []
        for r in range(FFN_FWD_SPLIT):
            rows = pl.ds(r * (tm // FFN_FWD_SPLIT), tm // FFN_FWD_SPLIT)
            hh = h_ref[rows, :]
            g = _dot(hh, wg_ref[...], NN).astype(BF)
            u = _dot(hh, wu_ref[...], NN).astype(BF)
            g_ref[rows, :] = g
            u_ref[rows, :] = u
            gf = g.astype(F32)
            parts.append((rows, _dot(gf * jax.nn.sigmoid(gf) * u.astype(F32), wd_ref[...], NN)))

        @pl.when(b == 0)
        def _():
            for rows, part in parts:
                y_ref[rows, :] = part

        @pl.when(b > 0)
        def _():
            for rows, part in parts:
                y_ref[rows, :] += part

    deps = [] if dep is None else [dep]
    wcol = pl.BlockSpec((None, D, nb), lambda i, b: (b, 0, 0))
    act = pl.BlockSpec((None, tm, nb), lambda i, b: (b, i, 0))
    return pl.pallas_call(
        body, name=name, grid=(T // tm, NB),
        in_specs=[pl.BlockSpec((tm, D), lambda i, b: (i, 0)), wcol, wcol, pl.BlockSpec((None, nb, D), lambda i, b: (b, 0, 0))]
        + [_full(d.shape) for d in deps],
        out_specs=[pl.BlockSpec((tm, D), lambda i, b: (i, 0)), act, act],
        out_shape=[S_((T, D), F32), S_((NB, T, nb), BF), S_((NB, T, nb), BF)], compiler_params=_cp())(h, wg, wu, wd, *deps)


def _ffn_bwd_act(dy, G, U, wg, wu, wd, name):
    T, D = dy.shape
    NB, _, nb = wg.shape
    tm = _tile(T, FFN_BWD_ROWS, 16)

    def body(dy_ref, g_ref, u_ref, wg_ref, wu_ref, wd_ref, dh_ref, dg_ref, du_ref, a_ref):
        b = pl.program_id(1)
        parts = []
        for r in range(FFN_BWD_SPLIT):
            rows = pl.ds(r * (tm // FFN_BWD_SPLIT), tm // FFN_BWD_SPLIT)
            g, u = g_ref[rows, :].astype(F32), u_ref[rows, :].astype(F32)
            da = _dot(dy_ref[rows, :], wd_ref[...], NT)
            s = jax.nn.sigmoid(g)
            silu = g * s
            du = da * silu
            dg = da * u * (s * (1.0 + g * (1.0 - s)))
            dg_ref[rows, :] = dg.astype(BF)
            du_ref[rows, :] = du.astype(BF)
            a_ref[rows, :] = (silu * u).astype(BF)
            parts.append((rows, _dot(dg, wg_ref[...], NT) + _dot(du, wu_ref[...], NT)))

        @pl.when(b == 0)
        def _():
            for rows, part in parts:
                dh_ref[rows, :] = part

        @pl.when(b > 0)
        def _():
            for rows, part in parts:
                dh_ref[rows, :] += part

    wcol = pl.BlockSpec((None, D, nb), lambda i, b: (b, 0, 0))
    wrow = pl.BlockSpec((None, nb, D), lambda i, b: (b, 0, 0))
    act = pl.BlockSpec((None, tm, nb), lambda i, b: (b, i, 0))
    row = pl.BlockSpec((tm, D), lambda i, b: (i, 0))
    return pl.pallas_call(
        body, name=name, grid=(T // tm, NB),
        in_specs=[row, act, act, wcol, wcol, wrow],
        out_specs=[row, act, act, act],
        out_shape=[S_((T, D), F32)] + [S_((NB, T, nb), BF)] * 3, compiler_params=_cp())(dy, G, U, wg, wu, wd)


def _ffn_wgrad_in(ht, dact, name, dep=None):
    D, T = ht.shape
    NB, _, nb = dact.shape
    tk = _tile(T, WGRAD_ROWS, 128)
    nk = T // tk
    deps = [] if dep is None else [dep]

    def body(*refs):
        h_ref, d_ref = refs[:2]
        o_ref, acc = refs[-2:]
        k = pl.program_id(1)
        part = _dot(h_ref[...], d_ref[...], NN)

        @pl.when(k == 0)
        def _():
            acc[...] = part

        @pl.when(k > 0)
        def _():
            acc[...] += part

        @pl.when(k == nk - 1)
        def _():
            o_ref[...] = acc[...].astype(BF)

    return pl.pallas_call(body, name=name, grid=(NB, nk),
                          in_specs=[pl.BlockSpec((D, tk), lambda b, k: (0, k)), pl.BlockSpec((None, tk, nb), lambda b, k: (b, k, 0))]
                          + [_full(d.shape) for d in deps],
                          out_specs=pl.BlockSpec((None, D, nb), lambda b, k: (b, 0, 0)), out_shape=S_((NB, D, nb), BF),
                          scratch_shapes=[pltpu.VMEM((D, nb), F32)], compiler_params=_cp())(ht, dact, *deps)


def _ffn_wgrad_out(act, dy, name, dep=None):
    NB, T, nb = act.shape
    D = dy.shape[1]
    tk = _tile(T, WGRAD_ROWS, 128)
    nk = T // tk
    deps = [] if dep is None else [dep]

    def body(*refs):
        a_ref, d_ref = refs[:2]
        o_ref, acc = refs[-2:]
        k = pl.program_id(1)
        part = _dot(a_ref[...], d_ref[...], TN)

        @pl.when(k == 0)
        def _():
            acc[...] = part

        @pl.when(k > 0)
        def _():
            acc[...] += part

        @pl.when(k == nk - 1)
        def _():
            o_ref[...] = acc[...].astype(BF)

    return pl.pallas_call(body, name=name, grid=(NB, nk),
                          in_specs=[pl.BlockSpec((None, tk, nb), lambda b, k: (b, k, 0)), pl.BlockSpec((tk, D), lambda b, k: (k, 0))]
                          + [_full(d.shape) for d in deps],
                          out_specs=pl.BlockSpec((None, nb, D), lambda b, k: (b, 0, 0)), out_shape=S_((NB, nb, D), BF),
                          scratch_shapes=[pltpu.VMEM((nb, D), F32)], compiler_params=_cp())(act, dy, *deps)


class _Geo:
    def __init__(self, S, LC, D):
        self.S, self.LC, self.D, self.T = S, LC, D, S + LC
        self.RT = _tile(LC, ROW_TILE, 8)
        assert S % self.RT == 0 and self.RT >= 2 * HALO
        self.nlat, self.nctx = S // self.RT, LC // self.RT
        self.nt = self.nlat + self.nctx

    def row(self, C, cb=0):
        return pl.BlockSpec((self.RT, C), lambda i: (i, cb))

    def prev(self, C, cb=0):
        return pl.BlockSpec((self.RT, C), lambda i: (jnp.maximum(i - 1, 0), cb))

    def next(self, C, cb=0):
        nt = self.nt
        return pl.BlockSpec((self.RT, C), lambda i: (jnp.minimum(i + 1, nt - 1), cb))

    def seg(self, r, C):
        nlat = self.nlat
        return pl.BlockSpec((None, r, C), lambda i: (jnp.minimum(i // nlat, 1), 0, 0))

    def first_of_seg(self, i):
        return jnp.logical_or(i == 0, i == self.nlat)

    def prev_ok(self, i):
        return jnp.logical_and(i != 0, i != self.nlat)

    def next_ok(self, i):
        return jnp.logical_and(i != self.nlat - 1, i != self.nt - 1)

    def pos(self, i):
        r = lax.broadcasted_iota(jnp.int32, (self.RT, 1), 0)
        is_ctx = i >= self.nlat
        base = jnp.where(is_ctx, (i - self.nlat) * self.RT, i * self.RT)
        return base + r, jnp.where(is_ctx, self.LC, self.S)


def _rms(x):
    return x * lax.rsqrt(jnp.mean(x * x, axis=-1, keepdims=True) + EPS)


def _modulate(x, g, shift, scale):
    return (_rms(x) * g) * (1 + scale) + shift


def _post(x, y, g, gate, w):
    return x + w * gate * (_rms(y) * g)


def _norm_fwd(geo, x, post, pre, name):
    D = geo.D
    ins, specs = [x], [geo.row(D)]
    if post is not None:
        ins += [post[0], post[1], post[2]]
        specs += [geo.row(D), _full((1, D)), geo.seg(9, D)]
    if pre is not None:
        ins += [pre[0], pre[1]]
        specs += [_full((1, D)), geo.seg(9, D)]

    def body(*refs):
        it = iter(refs)
        xv = next(it)[...]
        if post is not None:
            y_ref, gp_ref, tab_ref = next(it), next(it), next(it)
        if pre is not None:
            gq_ref, tabn_ref = next(it), next(it)
        if post is not None:
            r = 3 * post[3] + 2
            xv = _post(xv, y_ref[...], gp_ref[...], tab_ref[r:r + 1, :], post[4])
            next(it)[...] = xv
        if pre is not None:
            r = 3 * pre[2]
            h = _modulate(xv, gq_ref[...], tabn_ref[r:r + 1, :], tabn_ref[r + 1:r + 2, :]).astype(BF)
            next(it)[...] = h
            next(it)[...] = h.T

    outs, ospecs = [], []
    if post is not None:
        outs.append(S_((geo.T, D), F32))
        ospecs.append(geo.row(D))
    if pre is not None:
        outs += [S_((geo.T, D), BF), S_((D, geo.T), BF)]
        ospecs += [geo.row(D), pl.BlockSpec((D, geo.RT), lambda i: (0, i))]
    return pl.pallas_call(body, name=name, grid=(geo.nt,), in_specs=specs, out_specs=ospecs, out_shape=outs,
                          compiler_params=_cp())(*ins)


def _acc_rows(ref, first, rows):
    for k, v in enumerate(rows):
        @pl.when(first)
        def _(k=k, v=v):
            ref[k:k + 1, :] = v

        @pl.when(jnp.logical_not(first))
        def _(k=k, v=v):
            ref[k:k + 1, :] += v


def _norm_bwd(geo, dxo, dh, x, pre, post, name):
    D = geo.D
    ins = [dxo, dh, x, pre[0], pre[1]]
    specs = [geo.row(D)] * 3 + [_full((1, D)), geo.seg(9, D)]
    if post is not None:
        ins += [post[0], post[1], post[2]]
        specs += [geo.row(D), _full((1, D)), geo.seg(9, D)]

    def body(*refs):
        i = pl.program_id(0)
        first = geo.first_of_seg(i)
        dxo_ref, dh_ref, x_ref, gq_ref, tab_ref = refs[:5]
        k = 5
        if post is not None:
            y_ref, gp_ref, tabp_ref = refs[5:8]
            k = 8
        outs = refs[k:]
        r = 3 * pre[2]
        _, vjp = jax.vjp(_modulate, x_ref[...], gq_ref[...], tab_ref[r:r + 1, :], tab_ref[r + 1:r + 2, :])
        dx, dg, dsh, dsc = vjp(dh_ref[...].astype(F32))
        dx = dx + dxo_ref[...]
        outs[0][...] = dx
        if post is None:
            _acc_rows(outs[1], first, [dg, dsh, dsc])
            return
        _acc_rows(outs[2], first, [dg, dsh, dsc])
        rp = 3 * post[3] + 2
        w = post[4]
        _, vjp2 = jax.vjp(lambda yy, gg, ga: w * ga * (_rms(yy) * gg), y_ref[...], gp_ref[...], tabp_ref[rp:rp + 1, :])
        dy, dgp, dga = vjp2(dx)
        outs[1][...] = dy
        _acc_rows(outs[3], first, [dgp, dga])

    if post is None:
        outs, ospecs = [S_((geo.T, D), F32), S_((2, 3, D), F32)], [geo.row(D), geo.seg(3, D)]
    else:
        outs = [S_((geo.T, D), F32), S_((geo.T, D), F32), S_((2, 3, D), F32), S_((2, 2, D), F32)]
        ospecs = [geo.row(D), geo.row(D), geo.seg(3, D), geo.seg(2, D)]
    return pl.pallas_call(body, name=name, grid=(geo.nt,), in_specs=specs, out_specs=ospecs, out_shape=outs,
                          compiler_params=_cp())(*ins)


def _loss_bwd(geo, xf, tgt, post, name):
    D = geo.D
    nlat = geo.nlat

    def body(x_ref, t_ref, y_ref, gp_ref, tabp_ref, loss_ref, dx_ref, dy_ref, dpost_ref):
        i = pl.program_id(0)
        first = geo.first_of_seg(i)
        lat = i < nlat
        diff = x_ref[...] - t_ref[...]
        part = jnp.where(lat, 0.5 * jnp.sum(jnp.mean(diff * diff, axis=-1, keepdims=True), axis=0, keepdims=True), 0.0)

        @pl.when(i == 0)
        def _():
            loss_ref[...] = part

        @pl.when(i > 0)
        def _():
            loss_ref[...] += part

        dx = jnp.where(lat, diff * (1.0 / D), 0.0)
        dx_ref[...] = dx
        rp = 3 * post[3] + 2
        w = post[4]
        _, vjp2 = jax.vjp(lambda yy, gg, ga: w * ga * (_rms(yy) * gg), y_ref[...], gp_ref[...], tabp_ref[rp:rp + 1, :])
        dy, dgp, dga = vjp2(dx)
        dy_ref[...] = dy
        _acc_rows(dpost_ref, first, [dgp, dga])

    tspec = pl.BlockSpec((geo.RT, D), lambda i: (jnp.minimum(i, nlat - 1), 0))
    return pl.pallas_call(
        body, name=name, grid=(geo.nt,),
        in_specs=[geo.row(D), tspec, geo.row(D), _full((1, D)), geo.seg(9, D)],
        out_specs=[_full((1, 1)), geo.row(D), geo.row(D), geo.seg(2, D)],
        out_shape=[S_((1, 1), F32), S_((geo.T, D), F32), S_((geo.T, D), F32), S_((2, 2, D), F32)],
        compiler_params=_cp())(xf, tgt, post[0], post[1], post[2])


def _fill_ext(geo, ext, prev_ref, cur_ref, next_ref, i):
    RT = geo.RT
    ext[0:HALO, :] = jnp.where(geo.prev_ok(i), prev_ref[RT - HALO:RT, :], 0.0).astype(ext.dtype)
    ext[HALO:HALO + RT, :] = cur_ref[...].astype(ext.dtype)
    ext[HALO + RT:2 * HALO + RT, :] = jnp.where(geo.next_ok(i), next_ref[0:HALO, :], 0.0).astype(ext.dtype)


CONV_W = 4
CONV_LEFT = 2


def _lru_gates(u, za, zx, lam):
    r = jax.nn.sigmoid(za)
    i = jax.nn.sigmoid(zx)
    log_a = -LRU_C * r * jax.nn.softplus(-lam)
    a = jnp.exp(log_a)
    return a, jnp.sqrt(1.0 - jnp.exp(2.0 * log_a)) * (i * u)


def _lru_coef_fwd(geo, P, W, cw, cb, wa, ba, wx, bx, lam, name):
    RT = geo.RT
    nblk, LB = wa.shape[1], wa.shape[2]

    def body(pp, pc, pn, cw_ref, cb_ref, wa_ref, ba_ref, wx_ref, bx_ref, lam_ref, u_ref, a0, b0, a1, b1, ext):
        i = pl.program_id(0)
        _fill_ext(geo, ext, pp, pc, pn, i)
        u = cb_ref[...] + ext[pl.ds(HALO - CONV_LEFT, RT), :] * cw_ref[0:1, :]
        for k in range(1, CONV_W):
            u = u + ext[pl.ds(HALO - CONV_LEFT + k, RT), :] * cw_ref[k:k + 1, :]
        u_ref[...] = u
        for d, (a_ref, b_ref) in enumerate(((a0, b0), (a1, b1))):
            for n in range(nblk):
                sl = slice(n * LB, (n + 1) * LB)
                un = u[:, sl]
                za = _dot(un, wa_ref[d, n], NN) + ba_ref[d:d + 1, sl]
                zx = _dot(un, wx_ref[d, n], NN) + bx_ref[d:d + 1, sl]
                a, b = _lru_gates(un, za, zx, lam_ref[d:d + 1, sl])
                a_ref[:, sl] = a
                b_ref[:, sl] = b

    return pl.pallas_call(
        body, name=name, grid=(geo.nt,),
        in_specs=[geo.prev(W, 1), geo.row(W, 1), geo.next(W, 1), _full(cw.shape), _full(cb.shape), _full(wa.shape),
                  _full(ba.shape), _full(wx.shape), _full(bx.shape), _full(lam.shape)],
        out_specs=[geo.row(W)] * 5, out_shape=[S_((geo.T, W), F32)] * 5,
        scratch_shapes=[pltpu.VMEM((RT + 2 * HALO, W), F32)], compiler_params=_cp())(P, P, P, cw, cb, wa, ba, wx, bx, lam)


def _lru_coef_bwd(geo, u, da, db, W, wa, ba, wx, bx, lam, name):
    nblk, LB = wa.shape[1], wa.shape[2]

    def body(u_ref, da0, db0, da1, db1, wa_ref, ba_ref, wx_ref, bx_ref, lam_ref, du_ref, dwa, dba, dwx, dbx, dlam):
        i = pl.program_id(0)

        @pl.when(i == 0)
        def _():
            for r in (dwa, dba, dwx, dbx, dlam):
                r[...] = jnp.zeros(r.shape, F32)

        u = u_ref[...]
        for n in range(nblk):
            sl = slice(n * LB, (n + 1) * LB)
            un = u[:, sl]
            dun = jnp.zeros_like(un)
            for d, (da_ref, db_ref) in enumerate(((da0, db0), (da1, db1))):
                za = _dot(un, wa_ref[d, n], NN) + ba_ref[d:d + 1, sl]
                zx = _dot(un, wx_ref[d, n], NN) + bx_ref[d:d + 1, sl]
                _, vjp = jax.vjp(_lru_gates, un, za, zx, lam_ref[d:d + 1, sl])
                du_e, dza, dzx, dl = vjp((da_ref[:, sl], db_ref[:, sl]))
                dun = dun + du_e + _dot(dza, wa_ref[d, n], NT) + _dot(dzx, wx_ref[d, n], NT)
                dwa[d, n] += _dot(un, dza, TN)
                dwx[d, n] += _dot(un, dzx, TN)
                dba[d:d + 1, sl] += jnp.sum(dza, axis=0, keepdims=True)
                dbx[d:d + 1, sl] += jnp.sum(dzx, axis=0, keepdims=True)
                dlam[d:d + 1, sl] += dl
            du_ref[:, sl] = dun

    row = geo.row(W)
    return pl.pallas_call(
        body, name=name, grid=(geo.nt,),
        in_specs=[row] * 5 + [_full(wa.shape), _full(ba.shape), _full(wx.shape), _full(bx.shape), _full(lam.shape)],
        out_specs=[row, _full(wa.shape), _full(ba.shape), _full(wx.shape), _full(bx.shape), _full(lam.shape)],
        out_shape=[S_((geo.T, W), F32), S_(wa.shape, F32), S_(ba.shape, F32), S_(wx.shape, F32), S_(bx.shape, F32), S_(lam.shape, F32)],
        compiler_params=_cp())(u, da[0], db[0], da[1], db[1], wa, ba, wx, bx, lam)


def _scan_order(d, k, nlat, nctx):
    if d == 0:
        return jnp.where(k < nctx, nlat + k, k - nctx)
    return jnp.where(k < nctx, nlat + nctx - 1 - k, nlat - 1 - (k - nctx))


def _chunk_scan(a, b, reverse):
    n = a.shape[0]
    row = lax.broadcasted_iota(jnp.int32, a.shape, 0)
    s = 1
    while s < n:
        if reverse:
            ok = row < n - s
            a_s, b_s = pltpu.roll(a, n - s, 0), pltpu.roll(b, n - s, 0)
        else:
            ok = row >= s
            a_s, b_s = pltpu.roll(a, s, 0), pltpu.roll(b, s, 0)
        b = a * jnp.where(ok, b_s, 0.0) + b
        a = a * jnp.where(ok, a_s, 1.0)
        s *= 2
    return a, b


def _shift1(x, reverse, fill):
    n = x.shape[0]
    row = lax.broadcasted_iota(jnp.int32, x.shape, 0)
    if reverse:
        return jnp.where(row == n - 1, fill, pltpu.roll(x, n - 1, 0))
    return jnp.where(row == 0, fill, pltpu.roll(x, 1, 0))


def _lru_scan_fwd(geo, a, b, d, name):
    W = a.shape[1]
    tw = _tile(W, 256, 128)
    RT, nlat, nctx = geo.RT, geo.nlat, geo.nctx
    rev = d == 1

    def body(a_ref, b_ref, h_ref, hp_ref, carry):
        k = pl.program_id(1)

        @pl.when(k == 0)
        def _():
            carry[...] = jnp.zeros(carry.shape, F32)

        ac, bc = _chunk_scan(a_ref[...], b_ref[...], rev)
        h = bc + ac * carry[...]
        h_ref[...] = h
        hp_ref[...] = _shift1(h, rev, carry[...])
        carry[...] = h[0:1, :] if rev else h[RT - 1:RT, :]

    spec = pl.BlockSpec((RT, tw), lambda j, k: (_scan_order(d, k, nlat, nctx), j))
    return pl.pallas_call(body, name=name, grid=(W // tw, geo.nt), in_specs=[spec, spec], out_specs=[spec, spec],
                          out_shape=[S_((geo.T, W), F32)] * 2, scratch_shapes=[pltpu.VMEM((1, tw), F32)],
                          compiler_params=_cp())(a, b)


def _lru_scan_bwd(geo, dh, a, hprev, d, name):
    W = a.shape[1]
    tw = _tile(W, 256, 128)
    RT, nlat, nctx, nt = geo.RT, geo.nlat, geo.nctx, geo.nt
    rev = d == 1

    def body(dh_ref, a_ref, hp_ref, da_ref, db_ref, carry):
        k = pl.program_id(1)

        @pl.when(k == 0)
        def _():
            carry[...] = jnp.zeros(carry.shape, F32)

        av = a_ref[...]
        a_next = _shift1(av, not rev, jnp.ones((1, tw), F32))
        ac, bc = _chunk_scan(a_next, dh_ref[...], not rev)
        lam = bc + ac * carry[...]
        db_ref[...] = lam
        da_ref[...] = lam * hp_ref[...]
        first = (av * lam)[RT - 1:RT, :] if rev else (av * lam)[0:1, :]
        carry[...] = first

    spec = pl.BlockSpec((RT, tw), lambda j, k: (_scan_order(d, nt - 1 - k, nlat, nctx), j))
    return pl.pallas_call(body, name=name, grid=(W // tw, nt), in_specs=[spec] * 3, out_specs=[spec] * 2,
                          out_shape=[S_((geo.T, W), F32)] * 2, scratch_shapes=[pltpu.VMEM((1, tw), F32)],
                          compiler_params=_cp())(dh, a, hprev)


def _conv_bwd_assemble(geo, P, du, pieces, W, cw, name):
    RT = geo.RT

    def body(pp, pc, pn, dup, duc, dun, dg, dq, dk, dv, dq1, dk1, dv1, dog, cw_ref, dP_ref, dcw, dcb, ext_r, ext_d):
        i = pl.program_id(0)
        _fill_ext(geo, ext_r, pp, pc, pn, i)
        _fill_ext(geo, ext_d, dup, duc, dun, i)
        du_c = duc[...]
        rows = []
        dr = None
        for k in range(CONV_W):
            rows.append(jnp.sum(du_c * ext_r[pl.ds(HALO - CONV_LEFT + k, RT), :], axis=0, keepdims=True))
            t = ext_d[pl.ds(HALO + CONV_LEFT - k, RT), :] * cw_ref[k:k + 1, :]
            dr = t if dr is None else dr + t
        first = i == 0
        _acc_rows(dcw, first, rows)
        _acc_rows(dcb, first, [jnp.sum(du_c, axis=0, keepdims=True)])
        for j, v in enumerate((dg[...], dr, dq[...] + dq1[...], dk[...] + dk1[...], dv[...] + dv1[...], dog[...])):
            dP_ref[:, j * W:(j + 1) * W] = v

    row = geo.row(W)
    return pl.pallas_call(
        body, name=name, grid=(geo.nt,),
        in_specs=[geo.prev(W, 1), geo.row(W, 1), geo.next(W, 1), geo.prev(W), row, geo.next(W)] + [row] * 8 + [_full(cw.shape)],
        out_specs=[geo.row(6 * W), _full((CONV_W, W)), _full((1, W))],
        out_shape=[S_((geo.T, 6 * W), F32), S_((CONV_W, W), F32), S_((1, W), F32)],
        scratch_shapes=[pltpu.VMEM((RT + 2 * HALO, W), F32)] * 2, compiler_params=_cp())(P, P, P, du, du, du, *pieces, cw)


def _rot_half(x, cos, sin):
    hw = x.shape[1] // 2
    x1, x2 = x[:, :hw], x[:, hw:]
    return jnp.concatenate([x1 * cos - x2 * sin, x1 * sin + x2 * cos], axis=-1)


def _ret_chunk(q, k, v, s, logit, cos, sin, rev):
    C, dk = q.shape
    lg = -jax.nn.softplus(-logit)
    q = _rot_half(q, cos, sin)
    k = _rot_half(k, cos, sin) * (dk ** -0.5)
    i = lax.broadcasted_iota(jnp.int32, (C, 1), 0).astype(F32)
    j = lax.broadcasted_iota(jnp.int32, (1, C), 1).astype(F32)
    if rev:
        diff, qe, ke = j - i, C - i, i
    else:
        diff, qe, ke = i - j, i + 1.0, C - 1.0 - i
    intra = jnp.where(diff >= 0, jnp.exp(lg * jnp.maximum(diff, 0.0)), 0.0)
    scores = _bdot(q, k, "nt") * intra
    o = _bdot(scores, v, "nn") + _bdot(q * jnp.exp(lg * qe), s, "nn")
    s_new = s * jnp.exp(lg * C) + _bdot(k * jnp.exp(lg * ke), v, "tn")
    return o, s_new


def _ret_specs(P, W, H, d, nlc, ncc, order_of_step):
    C = RET_CHUNK
    dk = W // H

    def cidx(k):
        return _scan_order(d, order_of_step(k), nlc, ncc)

    per_w = W // dk
    q = pl.BlockSpec((C, dk), lambda h, k: (cidx(k), 2 * per_w + h))
    kk = pl.BlockSpec((C, dk), lambda h, k: (cidx(k), 3 * per_w + h))
    v = pl.BlockSpec((C, dk), lambda h, k: (cidx(k), 4 * per_w + h))
    tab = pl.BlockSpec((C, dk // 2), lambda h, k: (cidx(k), 0))
    logit = pl.BlockSpec((None, 1, 1), lambda h, k: (h, 0, 0))
    o = pl.BlockSpec((C, dk), lambda h, k: (cidx(k), h))
    return q, kk, v, tab, logit, o


def _ret_fwd(geo, P, W, H, logits, cos, sin, name):
    C = RET_CHUNK
    dk = W // H
    nlc, ncc = geo.S // C, geo.LC // C
    nch = nlc + ncc

    def body(*refs):
        ins, outs, scrs = refs[:12], refs[12:16], refs[16:]
        k = pl.program_id(1)
        for d in range(2):
            q_ref, k_ref, v_ref, cos_ref, sin_ref, lg_ref = ins[6 * d:6 * d + 6]
            o_ref, st_ref = outs[2 * d:2 * d + 2]
            s_scr = scrs[d]

            @pl.when(k == 0)
            def _(s_scr=s_scr):
                s_scr[...] = jnp.zeros(s_scr.shape, F32)

            st_ref[...] = s_scr[...]
            o, s_new = _ret_chunk(q_ref[...], k_ref[...], v_ref[...], s_scr[...], lg_ref[...], cos_ref[...], sin_ref[...], d == 1)
            o_ref[...] = o
            s_scr[...] = s_new

    in_specs, out_specs = [], []
    st = pl.BlockSpec((None, None, dk, dk), lambda h, k: (h, k, 0, 0))
    for d in range(2):
        q, kk, v, tab, logit, o = _ret_specs(P, W, H, d, nlc, ncc, lambda k: k)
        in_specs += [q, kk, v, tab, tab, logit]
        out_specs += [o, st]
    return pl.pallas_call(body, name=name, grid=(H, nch), in_specs=in_specs, out_specs=out_specs,
                          out_shape=[S_((geo.T, W), F32), S_((H, nch, dk, dk), F32)] * 2,
                          scratch_shapes=[pltpu.VMEM((dk, dk), F32)] * 2, compiler_params=_cp())(
                              P, P, P, cos, sin, logits[0], P, P, P, cos, sin, logits[1])


def _ret_bwd(geo, P, W, H, logits, cos, sin, states, do, name):
    C = RET_CHUNK
    dk = W // H
    nlc, ncc = geo.S // C, geo.LC // C
    nch = nlc + ncc

    def body(*refs):
        ins, outs, scrs = refs[:16], refs[16:24], refs[24:]
        k = pl.program_id(1)
        for d in range(2):
            q_ref, k_ref, v_ref, cos_ref, sin_ref, lg_ref, st_ref, do_ref = ins[8 * d:8 * d + 8]
            dq_ref, dk_ref, dv_ref, dlg_ref = outs[4 * d:4 * d + 4]
            ds_scr = scrs[d]

            @pl.when(k == 0)
            def _(ds_scr=ds_scr, dlg_ref=dlg_ref):
                ds_scr[...] = jnp.zeros(ds_scr.shape, F32)
                dlg_ref[...] = jnp.zeros(dlg_ref.shape, F32)

            cos, sin = cos_ref[...], sin_ref[...]
            _, vjp = jax.vjp(lambda a, b, c, s, lg, cos=cos, sin=sin, d=d: _ret_chunk(a, b, c, s, lg, cos, sin, d == 1),
                             q_ref[...], k_ref[...], v_ref[...], st_ref[...], lg_ref[...])
            dq, dkk, dv, ds, dlg = vjp((do_ref[...], ds_scr[...]))
            dq_ref[...] = dq
            dk_ref[...] = dkk
            dv_ref[...] = dv
            ds_scr[...] = ds
            dlg_ref[...] += dlg

    in_specs, out_specs, ops = [], [], []
    st = pl.BlockSpec((None, None, dk, dk), lambda h, k: (h, nch - 1 - k, 0, 0))
    for d in range(2):
        q, kk, v, tab, logit, o = _ret_specs(P, W, H, d, nlc, ncc, lambda k: nch - 1 - k)
        in_specs += [q, kk, v, tab, tab, logit, st, o]
        out_specs += [o, o, o, logit]
        ops += [P, P, P, cos, sin, logits[d], states[d], do]
    return pl.pallas_call(body, name=name, grid=(H, nch), in_specs=in_specs, out_specs=out_specs,
                          out_shape=([S_((geo.T, W), F32)] * 3 + [S_((H, 1, 1), F32)]) * 2,
                          scratch_shapes=[pltpu.VMEM((dk, dk), F32)] * 2, compiler_params=_cp())(*ops)


def _mix_even_math(g, h0, h1, o0, o1, og, gn, H):
    lru = jax.nn.gelu(g) * (h0 + h1)
    o = o0 + o1
    dv = o.shape[1] // H
    parts = []
    for h in range(H):
        oh = o[:, h * dv:(h + 1) * dv]
        mu = jnp.mean(oh, axis=-1, keepdims=True)
        var = jnp.mean(jnp.square(oh - mu), axis=-1, keepdims=True)
        parts.append((oh - mu) * lax.rsqrt(var + EPS))
    y = jnp.concatenate(parts, axis=-1) * gn
    return lru, y * jax.nn.silu(og)


def _mix_even_fwd(geo, P, W, H, h0, h1, o0, o1, gn, name):
    def body(g_ref, h0r, h1r, o0r, o1r, og_ref, gn_ref, m_ref):
        lru, ret = _mix_even_math(g_ref[...], h0r[...], h1r[...], o0r[...], o1r[...], og_ref[...], gn_ref[...], H)
        m_ref[:, :W] = lru.astype(BF)
        m_ref[:, W:] = ret.astype(BF)

    row = geo.row(W)
    return pl.pallas_call(body, name=name, grid=(geo.nt,), in_specs=[geo.row(W, 0), row, row, row, row, geo.row(W, 5), _full((1, W))],
                          out_specs=geo.row(2 * W), out_shape=S_((geo.T, 2 * W), BF), compiler_params=_cp())(P, h0, h1, o0, o1, P, gn)


def _mix_even_bwd(geo, P, W, H, h0, h1, o0, o1, gn, dmix, name):
    def body(g_ref, h0r, h1r, o0r, o1r, og_ref, gn_ref, dl_ref, dr_ref, dg_ref, dh_ref, do_ref, dog_ref, dgn_ref):
        i = pl.program_id(0)
        _, vjp = jax.vjp(lambda g, hs, os_, og, gn_: _mix_even_math(g, hs, 0.0, os_, 0.0, og, gn_, H),
                         g_ref[...], h0r[...] + h1r[...], o0r[...] + o1r[...], og_ref[...], gn_ref[...])
        dg, dh, do, dog, dgn = vjp((dl_ref[...], dr_ref[...]))
        dg_ref[...] = dg
        dh_ref[...] = dh
        do_ref[...] = do
        dog_ref[...] = dog
        _acc_rows(dgn_ref, i == 0, [dgn])

    row = geo.row(W)
    return pl.pallas_call(
        body, name=name, grid=(geo.nt,),
        in_specs=[geo.row(W, 0), row, row, row, row, geo.row(W, 5), _full((1, W)), geo.row(W, 0), geo.row(W, 1)],
        out_specs=[row] * 4 + [_full((1, W))], out_shape=[S_((geo.T, W), F32)] * 4 + [S_((1, W), F32)],
        compiler_params=_cp())(P, h0, h1, o0, o1, P, gn, dmix, dmix)


def _head_norm_rot(x, g, cosf, sinf):
    xn = _rms(x) * g
    return xn * cosf + _rollv(xn, x.shape[1] // 2, 1) * sinf


def _qk_prep_fwd(geo, P, dims, qg, kg, cosf, sinf, name):
    PW, QW, KW, hd = dims
    qscale = hd ** -0.5 * LOG2E

    def body(p_ref, qg_ref, kg_ref, c_ref, s_ref, q_ref, k_ref, v_ref):
        c, s = c_ref[...], s_ref[...]
        for h in range(QW // hd):
            qh = _head_norm_rot(p_ref[:, PW + h * hd:PW + (h + 1) * hd], qg_ref[...], c, s)
            q_ref[:, h * hd:(h + 1) * hd] = (qh * qscale).astype(BF)
        for h in range(KW // hd):
            o = PW + QW + h * hd
            k_ref[:, h * hd:(h + 1) * hd] = _head_norm_rot(p_ref[:, o:o + hd], kg_ref[...], c, s).astype(BF)
            v_ref[:, 2 * h * hd:(2 * h + 1) * hd] = p_ref[:, o + KW:o + KW + hd].astype(BF)
            v_ref[:, (2 * h + 1) * hd:(2 * h + 2) * hd] = jnp.ones((geo.RT, hd), BF)

    tot = PW + QW + 2 * KW
    return pl.pallas_call(
        body, name=name, grid=(geo.nt,),
        in_specs=[geo.row(tot), _full((1, hd)), _full((1, hd)), geo.row(hd), geo.row(hd)],
        out_specs=[geo.row(QW), geo.row(KW), geo.row(2 * KW)],
        out_shape=[S_((geo.T, QW), BF), S_((geo.T, KW), BF), S_((geo.T, 2 * KW), BF)], compiler_params=_cp())(P, qg, kg, cosf, sinf)


def _qk_prep_bwd(geo, P, dims, qg, kg, cosf, sinf, dpool, dq, dk, dv, name):
    PW, QW, KW, hd = dims

    def body(p_ref, qg_ref, kg_ref, c_ref, s_ref, dpool_ref, dq_ref, dk_ref, dv_ref, dP_ref, dqg_ref, dkg_ref):
        i = pl.program_id(0)
        c, s = c_ref[...], s_ref[...]
        dP_ref[:, :PW] = dpool_ref[...]
        f = lambda x, g: _head_norm_rot(x, g, c, s)
        dqg = jnp.zeros((1, hd), F32)
        for h in range(QW // hd):
            o = PW + h * hd
            _, vjp = jax.vjp(f, p_ref[:, o:o + hd], qg_ref[...])
            dx, dg = vjp(dq_ref[:, h * hd:(h + 1) * hd])
            dP_ref[:, o:o + hd] = dx
            dqg = dqg + dg
        dkg = jnp.zeros((1, hd), F32)
        for h in range(KW // hd):
            o = PW + QW + h * hd
            _, vjp = jax.vjp(f, p_ref[:, o:o + hd], kg_ref[...])
            dx, dg = vjp(dk_ref[:, h * hd:(h + 1) * hd])
            dP_ref[:, o:o + hd] = dx
            dkg = dkg + dg
        dP_ref[:, PW + QW + KW:] = dv_ref[...]
        _acc_rows(dqg_ref, i == 0, [dqg])
        _acc_rows(dkg_ref, i == 0, [dkg])

    tot = PW + QW + 2 * KW
    return pl.pallas_call(
        body, name=name, grid=(geo.nt,),
        in_specs=[geo.row(tot), _full((1, hd)), _full((1, hd)), geo.row(hd), geo.row(hd), geo.row(PW), geo.row(QW), geo.row(KW), geo.row(KW)],
        out_specs=[geo.row(tot), _full((1, hd)), _full((1, hd))],
        out_shape=[S_((geo.T, tot), F32), S_((1, hd), F32), S_((1, hd), F32)], compiler_params=_cp())(
            P, qg, kg, cosf, sinf, dpool, dq, dk, dv)


def _att_tiles(T, q_rows):
    return _tile(T, q_rows, 16), _tile(T, 768, 16)


def _stack_heads(ref, G, hd):
    return jnp.concatenate([ref[:, g * hd:(g + 1) * hd] for g in range(G)], axis=0)


def _att_fwd(q, k, v1, hd, name):
    T, QW = q.shape
    KV = k.shape[1] // hd
    G = QW // hd // KV
    tq, tk = _att_tiles(T, 768)
    nk = T // tk

    def body(q_ref, k_ref, v_ref, o_ref, lse_ref, m_scr, acc):
        ki = pl.program_id(2)

        @pl.when(ki == 0)
        def _():
            m_scr[...] = jnp.full(m_scr.shape, -jnp.inf, F32)
            acc[...] = jnp.zeros(acc.shape, F32)

        s = _dot(_stack_heads(q_ref, G, hd), k_ref[...], NT)
        m_prev = m_scr[...]
        m_new = jnp.maximum(m_prev, jnp.max(s, axis=-1, keepdims=True))
        p = jnp.exp2(s - jnp.tile(m_new, (1, tk // hd)))
        acc[...] = jnp.tile(jnp.exp2(m_prev - m_new), (1, 2)) * acc[...] + _dot(p, v_ref[...], NN)
        m_scr[...] = m_new

        @pl.when(ki == nk - 1)
        def _():
            a = acc[...]
            o = a[:, :hd] / a[:, hd:]
            lse = m_scr[...] + jnp.log2(a[:, hd:])
            for g in range(G):
                o_ref[:, g * hd:(g + 1) * hd] = o[g * tq:(g + 1) * tq].astype(BF)
                lse_ref[g] = lse[g * tq:(g + 1) * tq]

    return pl.pallas_call(
        body, name=name, grid=(KV, T // tq, nk),
        in_specs=[pl.BlockSpec((tq, G * hd), lambda a, i, j: (i, a)), pl.BlockSpec((tk, hd), lambda a, i, j: (j, a)),
                  pl.BlockSpec((tk, 2 * hd), lambda a, i, j: (j, a))],
        out_specs=[pl.BlockSpec((tq, G * hd), lambda a, i, j: (i, a)), pl.BlockSpec((G, tq, hd), lambda a, i, j: (a, i, 0))],
        out_shape=[S_((T, QW), BF), S_((QW // hd, T, hd), F32)],
        scratch_shapes=[pltpu.VMEM((G * tq, hd), F32), pltpu.VMEM((G * tq, 2 * hd), F32)],
        compiler_params=_cp())(q, k, v1)


def _att_delta(geo, o, dmix, PW, hd, name):
    QW = o.shape[1]
    nh = QW // hd

    def body(o_ref, do_ref, d_ref):
        for h in range(nh):
            sl = slice(h * hd, (h + 1) * hd)
            d = jnp.sum(o_ref[:, sl].astype(F32) * do_ref[:, PW + h * hd:PW + (h + 1) * hd], axis=-1, keepdims=True)
            d_ref[h] = jnp.broadcast_to(d, (geo.RT, hd))

    return pl.pallas_call(body, name=name, grid=(geo.nt,), in_specs=[geo.row(QW), geo.row(PW + QW)],
                          out_specs=pl.BlockSpec((nh, geo.RT, hd), lambda i: (0, i, 0)), out_shape=S_((nh, geo.T, hd), F32),
                          compiler_params=_cp())(o, dmix)


def _att_bwd(q, k, v1, dmix, PW, lse, delta, hd, name):
    T, QW = q.shape
    KW = k.shape[1]
    KV = KW // hd
    G = QW // hd // KV
    tq, tk = _att_tiles(T, 256)
    nq, nk = T // tq, T // tk
    scale = hd ** -0.5
    pb = PW // hd

    def body(q_ref, k_ref, v_ref, *rest):
        do_refs, (lse_ref, dl_ref, dq_ref, dk_ref, dv_ref, dq_scr) = rest[:G], rest[G:]
        ki, qi = pl.program_id(1), pl.program_id(2)
        q3 = _stack_heads(q_ref, G, hd)
        do3 = jnp.concatenate([r[...] for r in do_refs], axis=0)
        lse = jnp.concatenate([lse_ref[g] for g in range(G)], axis=0)
        dl = jnp.concatenate([dl_ref[g] for g in range(G)], axis=0)
        kk = k_ref[...]
        p = jnp.exp2(_dot(q3, kk, NT) - jnp.tile(lse, (1, tk // hd)))
        ds = p * (_dot(do3, v_ref[:, :hd], NT) - jnp.tile(dl, (1, tk // hd)))
        pv = _dot(p, do3, TN)
        pk = _dot(ds, q3, TN)
        pq = _dot(ds, kk, NN)

        @pl.when(qi == 0)
        def _():
            dk_ref[...] = pk
            dv_ref[...] = pv

        @pl.when(qi > 0)
        def _():
            dk_ref[...] += pk
            dv_ref[...] += pv

        @pl.when(qi == nq - 1)
        def _():
            dk_ref[...] = dk_ref[...] * LN2

        @pl.when(ki == 0)
        def _():
            dq_scr[qi] = pq

        @pl.when(ki > 0)
        def _():
            dq_scr[qi] += pq

        @pl.when(ki == nk - 1)
        def _():
            full = dq_scr[qi]
            for g in range(G):
                dq_ref[:, g * hd:(g + 1) * hd] = full[g * tq:(g + 1) * tq] * scale

    qs = pl.BlockSpec((tq, G * hd), lambda a, j, i: (i, a))
    ks = pl.BlockSpec((tk, hd), lambda a, j, i: (j, a))
    vs = pl.BlockSpec((tk, 2 * hd), lambda a, j, i: (j, a))
    dos = [pl.BlockSpec((tq, hd), lambda a, j, i, g=g: (i, pb + a * G + g)) for g in range(G)]
    st = pl.BlockSpec((G, tq, hd), lambda a, j, i: (a, i, 0))
    dqs = pl.BlockSpec((tq, G * hd), lambda a, j, i: (jnp.where(j == nk - 1, i, 0), a))
    return pl.pallas_call(body, name=name, grid=(KV, nk, nq), in_specs=[qs, ks, vs] + dos + [st, st], out_specs=[dqs, ks, ks],
                          out_shape=[S_((T, QW), F32), S_((T, KW), F32), S_((T, KW), F32)],
                          scratch_shapes=[pltpu.VMEM((nq, G * tq, hd), F32)], compiler_params=_cp())(
                              q, k, v1, *([dmix] * G), lse, delta)


def _pool_cnt(pos, L, w):
    return (jnp.minimum(pos + w // 2, L) - jnp.maximum(pos - w // 2, 0)).astype(F32)


def _pool_mean(ext, gi, w, G, RT, cnt):
    acc = ext[pl.ds(HALO - w // 2, RT), gi * G:(gi + 1) * G]
    for off in range(-w // 2 + 1, w // 2):
        acc = acc + ext[pl.ds(HALO + off, RT), gi * G:(gi + 1) * G]
    return acc / cnt


def _pool_fwd(geo, P, PW, att, pw, ps, name):
    RT = geo.RT
    G = pw.shape[1]
    QW = att.shape[1]

    def body(pp, pc, pn, att_ref, pw_ref, ps_ref, m_ref, ext):
        i = pl.program_id(0)
        _fill_ext(geo, ext, pp, pc, pn, i)
        pos, L = geo.pos(i)
        for gi, w in enumerate(POOL_WINDOWS):
            sl = slice(gi * G, (gi + 1) * G)
            xm = _pool_mean(ext, gi, w, G, RT, _pool_cnt(pos, L, w)) - pc[:, sl]
            m_ref[:, sl] = (_dot(xm, pw_ref[gi], NN) * ps_ref[:, sl]).astype(BF)
        m_ref[:, PW:] = att_ref[...]

    return pl.pallas_call(
        body, name=name, grid=(geo.nt,),
        in_specs=[geo.prev(PW), geo.row(PW), geo.next(PW), geo.row(QW), _full(pw.shape), _full(ps.shape)],
        out_specs=geo.row(PW + QW), out_shape=S_((geo.T, PW + QW), BF),
        scratch_shapes=[pltpu.VMEM((RT + 2 * HALO, PW), F32)], compiler_params=_cp())(P, P, P, att, pw, ps)


def _pool_bwd(geo, P, PW, dmix, pw, ps, name):
    RT = geo.RT
    G = pw.shape[1]
    RE = RT + 2 * HALO

    def body(pp, pc, pn, dp_, dc, dn, pw_ref, ps_ref, dx_ref, dpw, dps, ext, extd, dmc):
        i = pl.program_id(0)
        _fill_ext(geo, ext, pp, pc, pn, i)
        _fill_ext(geo, extd, dp_, dc, dn, i)
        pos, L = geo.pos(i)
        r = lax.broadcasted_iota(jnp.int32, (RE, 1), 0)
        pos_e = pos[0:1, :] - HALO + r
        rows_s = []
        for gi, w in enumerate(POOL_WINDOWS):
            sl = slice(gi * G, (gi + 1) * G)
            xm = _pool_mean(ext, gi, w, G, RT, _pool_cnt(pos, L, w)) - pc[:, sl]
            pre = _dot(xm, pw_ref[gi], NN)
            dout = dc[:, sl]
            rows_s.append(jnp.sum(dout * pre, axis=0, keepdims=True))
            gw = _dot(xm, dout * ps_ref[:, sl], TN)

            @pl.when(i == 0)
            def _(gi=gi, gw=gw):
                dpw[gi] = gw

            @pl.when(i > 0)
            def _(gi=gi, gw=gw):
                dpw[gi] += gw

            dm_e = _dot(extd[:, sl] * ps_ref[:, sl], pw_ref[gi], NT)
            dmc[...] = dm_e / jnp.maximum(_pool_cnt(pos_e, L, w), 1.0)
            acc = -dm_e[HALO:HALO + RT]
            for off in range(-w // 2 + 1, w // 2 + 1):
                acc = acc + dmc[pl.ds(HALO + off, RT), :]
            dx_ref[:, sl] = acc
        _acc_rows(dps, i == 0, [jnp.concatenate(rows_s, axis=-1)])

    return pl.pallas_call(
        body, name=name, grid=(geo.nt,),
        in_specs=[geo.prev(PW), geo.row(PW), geo.next(PW), geo.prev(PW), geo.row(PW), geo.next(PW), _full(pw.shape), _full(ps.shape)],
        out_specs=[geo.row(PW), _full(pw.shape), _full(ps.shape)],
        out_shape=[S_((geo.T, PW), F32), S_(pw.shape, F32), S_(ps.shape, F32)],
        scratch_shapes=[pltpu.VMEM((RE, PW), F32), pltpu.VMEM((RE, PW), F32), pltpu.VMEM((RE, G), F32)],
        compiler_params=_cp())(P, P, P, dmix, dmix, dmix, pw, ps)


def _mod_fwd(A, mod_w, name):
    L, D, MC = mod_w.shape
    tn = _tile(MC, 768, 128)

    def body(a_ref, w_ref, o_ref):
        o_ref[...] = _dot(jax.nn.silu(a_ref[...]), w_ref[...], NN)

    return pl.pallas_call(body, name=name, grid=(L, MC // tn),
                          in_specs=[_full(A.shape), pl.BlockSpec((None, D, tn), lambda l, j: (l, 0, j))],
                          out_specs=pl.BlockSpec((None, 16, tn), lambda l, j: (l, 0, j)), out_shape=S_((L, 16, MC), F32),
                          compiler_params=_cp())(A, mod_w)


def _mod_bwd(A, DM, mod_w, name):
    L, D, MC = mod_w.shape
    tn = _tile(MC, 768, 128)
    nj = MC // tn

    def body(a_ref, dm_ref, w_ref, gw_ref, da_ref, acc):
        l, j = pl.program_id(0), pl.program_id(1)
        sa, vjp = jax.vjp(jax.nn.silu, a_ref[...])
        gw_ref[...] = _dot(sa, dm_ref[...], TN)
        part = _dot(dm_ref[...], w_ref[...], NT)
        first = jnp.logical_and(l == 0, j == 0)

        @pl.when(first)
        def _():
            acc[...] = part

        @pl.when(jnp.logical_not(first))
        def _():
            acc[...] += part

        @pl.when(jnp.logical_and(l == L - 1, j == nj - 1))
        def _():
            da_ref[...] = vjp(acc[...])[0]

    wspec = pl.BlockSpec((None, D, tn), lambda l, j: (l, 0, j))
    return pl.pallas_call(body, name=name, grid=(L, nj),
                          in_specs=[_full(A.shape), pl.BlockSpec((None, 16, tn), lambda l, j: (l, 0, j)), wspec],
                          out_specs=[wspec, _full(A.shape)], out_shape=[S_((L, D, MC), F32), S_(A.shape, F32)],
                          scratch_shapes=[pltpu.VMEM(A.shape, F32)], compiler_params=_cp())(A, DM, mod_w)


PACK_COLS = 1024


def _pack_rows(shape):
    n = 1
    for d in shape:
        n *= d
    return n, -(-n // (8 * PACK_COLS)) * 8


def _pack(arrs):
    parts = []
    for a in arrs:
        n, rows = _pack_rows(a.shape)
        parts.append(jnp.pad(a.reshape(-1).astype(F32), (0, rows * PACK_COLS - n)).reshape(rows, PACK_COLS))
    return jnp.concatenate(parts)


def _unpack(packed, shapes, lead=()):
    out, off = [], 0
    for s in shapes:
        n, rows = _pack_rows(s)
        blk = packed[..., off:off + rows, :].reshape(lead + (rows * PACK_COLS,))
        out.append(blk[..., :n].reshape(lead + tuple(s)))
        off += rows
    return out


def _unshard_last(g):
    g = jnp.moveaxis(g, 0, -2)
    return g.reshape(g.shape[:-2] + (g.shape[-2] * g.shape[-1],))


def _my_shard(a, me):
    n = a.shape[-1] // NDEV
    return lax.dynamic_slice_in_dim(a, me * n, n, axis=a.ndim - 1)


def _rot_tables(S, LC, dk, hd):
    t = jnp.arange(S, dtype=F32)
    n_r = dk // 2
    ang1 = t[:, None] * (RET_THETA ** (-jnp.arange(n_r, dtype=F32) / n_r))
    cos1 = jnp.concatenate([jnp.cos(ang1), jnp.ones((LC, n_r), F32)])
    sin1 = jnp.concatenate([jnp.sin(ang1), jnp.zeros((LC, n_r), F32)])
    n_ax = hd // 4
    f_ax = ROPE_THETA ** (-jnp.arange(n_ax, dtype=F32) / n_ax)
    row = jnp.floor(t / GRID_W)
    col = t - row * GRID_W
    ang2 = jnp.concatenate([row[:, None] * f_ax, col[:, None] * f_ax], axis=-1)
    c2, s2 = jnp.cos(ang2), jnp.sin(ang2)
    cosf = jnp.concatenate([jnp.concatenate([c2, c2], axis=-1), jnp.ones((LC, hd), F32)])
    sinf = jnp.concatenate([jnp.concatenate([-s2, s2], axis=-1), jnp.zeros((LC, hd), F32)])
    return cos1, sin1, cosf, sinf


SMALL = ("c_ctx", "mod_b", "norm_pre", "norm_post", "lru_conv_w", "lru_conv_b", "lru_wa", "lru_ba", "lru_wx", "lru_bx",
         "lru_lambda", "ret_decay_logit", "ret_gn", "pool_w", "pool_scale", "q_norm", "k_norm")
WEIGHTS = ("c_ctx", "mod_w", "mod_b", "norm_pre", "norm_post", "ffn_gate", "ffn_up", "ffn_down", "ev_w_in", "ev_w_out",
           "lru_conv_w", "lru_conv_b", "lru_wa", "lru_ba", "lru_wx", "lru_bx", "lru_lambda", "ret_decay_logit", "ret_gn",
           "od_w_in", "od_w_out", "pool_w", "pool_scale", "q_norm", "k_norm")
INPUTS = ("x", "c", "ctx") + WEIGHTS + ("loss_target",) + tuple("m_" + w for w in WEIGHTS) + tuple("v_" + w for w in WEIGHTS)


def _step(p):
    x, c, ctx = p["x"], p["c"], p["ctx"]
    _, S, D = x.shape
    LC = ctx.shape[1]
    geo = _Geo(S, LC, D)
    T = geo.T
    xi, yi, ci = _me()
    me = 4 * xi + 2 * yi + ci
    L = p["mod_w"].shape[0]
    assert L == 2
    W = p["lru_conv_b"].shape[-1]
    H = p["ret_decay_logit"].shape[-1]
    hd = p["q_norm"].shape[-1]
    G = p["pool_w"].shape[-1]
    PW = G * len(POOL_WINDOWS)
    od_mix = p["od_w_out"].shape[1] * NDEV
    od_in = p["od_w_in"].shape[2] * NDEV
    QW = od_mix - PW
    KW = (od_in - od_mix) // 2
    assert p["ev_w_in"].shape[2] * NDEV == 6 * W and p["ret_gn"].shape[-1] == W and p["ev_w_out"].shape[1] * NDEV == 2 * W
    odims = (PW, QW, KW, hd)
    cos1, sin1, cosf, sinf = _rot_tables(S, LC, W // H, hd)

    sh0 = [(D,), p["norm_pre"].shape, p["norm_post"].shape, p["lru_conv_w"].shape[1:], p["lru_ba"].shape[1:],
           p["lru_bx"].shape[1:], p["lru_lambda"].shape[1:], p["pool_scale"].shape[1:]]
    pack0 = _pack([c, p["norm_pre"], p["norm_post"], p["lru_conv_w"], p["lru_ba"], p["lru_bx"], p["lru_lambda"], p["pool_scale"]])
    (g0,) = _all_gather([pack0], "gather_small")
    c_all, npre, npost, conv_w, ba, bx, lam, pscale = _unpack(g0, sh0, (NDEV,))
    npre, npost, conv_w, ba, bx, lam, pscale = [_unshard_last(a) for a in (npre, npost, conv_w, ba, bx, lam, pscale)]
    pscale = pscale[None]
    conv_b = p["lru_conv_b"]
    wa, wx = p["lru_wa"][0], p["lru_wx"][0]
    gn = p["ret_gn"]
    logits = p["ret_decay_logit"][0].reshape(2, H, 1, 1)
    pool_w = p["pool_w"][0]
    qg, kg = p["q_norm"], p["k_norm"]

    A = jnp.concatenate([c_all, p["c_ctx"][None], jnp.zeros((7, D), F32)])
    M = _mod_fwd(A, p["mod_w"], "mod_fwd")
    (Mg,) = _all_gather([M], "gather_mod")
    MC = M.shape[2]
    tabs = []
    for l in range(L):
        ml = lax.dynamic_index_in_dim(Mg[:, l], me, axis=1, keepdims=False).reshape(NDEV * MC) + p["mod_b"][l]
        mc = Mg[:, l, 8].reshape(NDEV * MC) + p["mod_b"][l]
        tabs.append(jnp.stack([ml.reshape(9, D), mc.reshape(9, D)]))

    def cast2(a, name):
        return _cast_bf16(a.reshape(-1, a.shape[-1]), name).reshape(a.shape)

    loc = {(l, j): [cast2(p[n][l, j], f"cast_{n}_{l}{j}") for n in ("ffn_gate", "ffn_up", "ffn_down")]
           for l in range(L) for j in range(2)}
    loc["ev"] = [cast2(p["ev_w_in"][0], "cast_ev_in"), cast2(p["ev_w_out"][0], "cast_ev_out")]
    loc["od"] = [cast2(p["od_w_in"][0], "cast_od_in"), cast2(p["od_w_out"][0], "cast_od_out")]

    def gather_start(key, name, after):
        return _xchg_start(True, loc[key], [jnp.broadcast_to(a[None], (NDEV,) + a.shape) for a in loc[key]], name, [after])

    ffn_w = {(0, 0): _all_gather(loc[0, 0], "gather_ffn_00")}

    def gp(a, l, s):
        return a[l, s][None]

    st, tok = gather_start("ev", "gs_ev", ffn_w[0, 0][0])
    x0 = jnp.concatenate([x[0], ctx[0]])
    h0, h0t = _norm_fwd(geo, x0, None, (_tie(gp(npre, 0, 0), tok), tabs[0], 0), "pre_00")
    y0, G0, U0 = _ffn_fwd(h0, *ffn_w[0, 0], name="ffn_fwd_00")
    x1, h1, h1t = _norm_fwd(geo, x0, (y0, gp(npost, 0, 0), tabs[0], 0, FFN_STEP), (gp(npre, 0, 1), tabs[0], 1), "post_00")
    ev_in, ev_out = _xchg_wait(st, h1, "gw_ev")
    ev_out_f = ev_out.reshape(2 * W, D)

    st, tok = gather_start((0, 1), "gs_ffn_01", h1)
    Pe = _mm_cols(h1, ev_in, "ev_in", dep=tok)
    u, a0, b0, a1, b1 = _lru_coef_fwd(geo, Pe, W, conv_w, conv_b, wa, ba, wx, bx, lam, "lru_coef")
    hs0, hp0 = _lru_scan_fwd(geo, a0, b0, 0, "lru_scan_f0")
    hs1, hp1 = _lru_scan_fwd(geo, a1, b1, 1, "lru_scan_f1")
    o0, st0, o1, st1 = _ret_fwd(geo, Pe, W, H, logits, cos1, sin1, "ret_fwd")
    mixe = _mix_even_fwd(geo, Pe, W, H, hs0, hs1, o0, o1, gn, "mix_even")
    y1 = _mm_full(mixe, ev_out_f, NN, "ev_out")
    x2, h2, h2t = _norm_fwd(geo, x1, (y1, gp(npost, 0, 1), tabs[0], 1, 1.0), (gp(npre, 0, 2), tabs[0], 2), "post_01")
    ffn_w[0, 1] = _xchg_wait(st, h2, "gw_ffn_01")

    st, tok = gather_start((1, 0), "gs_ffn_10", h2)
    y2, G2, U2 = _ffn_fwd(h2, *ffn_w[0, 1], name="ffn_fwd_01", dep=tok)
    x3, h3, h3t = _norm_fwd(geo, x2, (y2, gp(npost, 0, 2), tabs[0], 2, FFN_STEP), (gp(npre, 1, 0), tabs[1], 0), "post_02")
    ffn_w[1, 0] = _xchg_wait(st, h3, "gw_ffn_10")

    st, tok = gather_start("od", "gs_od", h3)
    y3, G3, U3 = _ffn_fwd(h3, *ffn_w[1, 0], name="ffn_fwd_10", dep=tok)
    x4, h4, h4t = _norm_fwd(geo, x3, (y3, gp(npost, 1, 0), tabs[1], 0, FFN_STEP), (gp(npre, 1, 1), tabs[1], 1), "post_10")
    od_inw, od_out = _xchg_wait(st, h4, "gw_od")
    od_out_f = od_out.reshape(od_mix, D)

    st, tok = gather_start((1, 1), "gs_ffn_11", h4)
    Po = _mm_cols(h4, od_inw, "od_in", dep=tok)
    qr, kr, vr = _qk_prep_fwd(geo, Po, odims, qg, kg, cosf, sinf, "qk_prep")
    att, lse = _att_fwd(qr, kr, vr, hd, "att_fwd")
    mixo = _pool_fwd(geo, Po, PW, att, pool_w, pscale, "pool_fwd")
    y4 = _mm_full(mixo, od_out_f, NN, "od_out")
    x5, h5, h5t = _norm_fwd(geo, x4, (y4, gp(npost, 1, 1), tabs[1], 1, 1.0), (gp(npre, 1, 2), tabs[1], 2), "post_11")
    ffn_w[1, 1] = _xchg_wait(st, h5, "gw_ffn_11")

    y5, G5, U5 = _ffn_fwd(h5, *ffn_w[1, 1], name="ffn_fwd_11")
    (x6,) = _norm_fwd(geo, x5, (y5, gp(npost, 1, 2), tabs[1], 2, FFN_STEP), None, "post_12")

    big_g = {}
    tokbox = [None]
    deferred = []

    def gpt(l, s):
        return _tie(gp(npre, l, s), tokbox[0])

    def a2a_start(key, srcs, name, dh):
        own = [lax.dynamic_index_in_dim(a, me, 0, keepdims=False) for a in srcs]
        state, token = _xchg_start(False, srcs, [jnp.zeros(a.shape, a.dtype) for a in srcs], name)
        big_g[key] = (state, own)
        tokbox[0] = token
        return dh

    def ffn_bwd(dy, h, Gs, Us, key):
        tag = f"{key[0]}{key[1]}"
        dh, dG, dU, Aact = _ffn_bwd_act(dy, Gs, Us, *ffn_w[key], name=f"ffn_bwd_{tag}")
        if key != (0, 0):
            srcs = [_ffn_wgrad_in(h, dG, f"ffn_wg_{tag}"), _ffn_wgrad_in(h, dU, f"ffn_wu_{tag}"), _ffn_wgrad_out(Aact, dy, f"ffn_wd_{tag}")]
            return a2a_start(key, srcs, f"as_ffn_{tag}", dh)
        def finish(dep):
            parts = [None] * 3
            a2a_start(key, [_ffn_wgrad_out(Aact, dy, f"ffn_wd_{tag}", dep=dep)], f"as_ffn_{tag}_d", dh)
            parts[2] = big_g[key]
            a2a_start(key, [_ffn_wgrad_in(h, dG, f"ffn_wg_{tag}", dep=tokbox[0])], f"as_ffn_{tag}_g", dh)
            parts[0] = big_g[key]
            a2a_start(key, [_ffn_wgrad_in(h, dU, f"ffn_wu_{tag}", dep=tokbox[0])], f"as_ffn_{tag}_u", dh)
            parts[1] = big_g[key]
            big_g[key] = parts

        deferred.append(finish)
        return dh

    loss_p, dx6, dy5, dpost5 = _loss_bwd(geo, x6, p["loss_target"][0], (y5, gp(npost, 1, 2), tabs[1], 2, FFN_STEP), "loss")
    dh5 = ffn_bwd(dy5, h5t, G5, U5, (1, 1))
    dx5, dy4, dpre5, dpost4 = _norm_bwd(geo, dx6, dh5, x5, (gpt(1, 2), tabs[1], 2),
                                        (y4, gp(npost, 1, 1), tabs[1], 1, 1.0), "nb_5")

    dmixo = _mm_full(dy4, od_out_f, NT, "od_out_d")
    g_od_out = _mm_tn_rows(mixo, dy4, NDEV, "od_out_w")
    dpool, g_pool_w, g_pscale = _pool_bwd(geo, Po, PW, dmixo, pool_w, pscale, "pool_bwd")
    delta = _att_delta(geo, att, dmixo, PW, hd, "att_delta")
    dq, dk, dv = _att_bwd(qr, kr, vr, dmixo, PW, lse, delta, hd, "att_bwd")
    dPo, g_qn, g_kn = _qk_prep_bwd(geo, Po, odims, qg, kg, cosf, sinf, dpool, dq, dk, dv, "qk_prep_bwd")
    dh4 = _mm_nt_cols(dPo, od_inw, "od_in_d")
    g_od_in = _mm_tn_cols(h4t, dPo, NDEV, "od_in_w")
    dh4 = a2a_start("od", [g_od_in, g_od_out], "as_od", dh4)
    dx4, dy3, dpre4, dpost3 = _norm_bwd(geo, dx5, dh4, x4, (gpt(1, 1), tabs[1], 1),
                                        (y3, gp(npost, 1, 0), tabs[1], 0, FFN_STEP), "nb_4")

    dh3 = ffn_bwd(dy3, h3t, G3, U3, (1, 0))
    dx3, dy2, dpre3, dpost2 = _norm_bwd(geo, dx4, dh3, x3, (gpt(1, 0), tabs[1], 0),
                                        (y2, gp(npost, 0, 2), tabs[0], 2, FFN_STEP), "nb_3")

    dh2 = ffn_bwd(dy2, h2t, G2, U2, (0, 1))
    dx2, dy1, dpre2, dpost1 = _norm_bwd(geo, dx3, dh2, x2, (gpt(0, 2), tabs[0], 2),
                                        (y1, gp(npost, 0, 1), tabs[0], 1, 1.0), "nb_2")

    dmixe = _mm_full(dy1, ev_out_f, NT, "ev_out_d")
    g_ev_out = _mm_tn_rows(mixe, dy1, NDEV, "ev_out_w")
    dg, dhs, dos, dog, g_gn = _mix_even_bwd(geo, Pe, W, H, hs0, hs1, o0, o1, gn, dmixe, "mix_even_bwd")
    da0, db0 = _lru_scan_bwd(geo, dhs, a0, hp0, 0, "lru_scan_b0")
    da1, db1 = _lru_scan_bwd(geo, dhs, a1, hp1, 1, "lru_scan_b1")
    du, g_wa, g_ba, g_wx, g_bx, g_lam = _lru_coef_bwd(geo, u, (da0, da1), (db0, db1), W, wa, ba, wx, bx, lam, "lru_coef_bwd")
    dq0, dk0, dv0, glg0, dq1, dk1, dv1, glg1 = _ret_bwd(geo, Pe, W, H, logits, cos1, sin1, (st0, st1), dos, "ret_bwd")
    dPe, g_cw, g_cb = _conv_bwd_assemble(geo, Pe, du, (dg, dq0, dk0, dv0, dq1, dk1, dv1, dog), W, conv_w, "conv_bwd")
    dh1 = _mm_nt_cols(dPe, ev_in, "ev_in_d")
    g_ev_in = _mm_tn_cols(h1t, dPe, NDEV, "ev_in_w")
    dh1 = a2a_start("ev", [g_ev_in, g_ev_out], "as_ev", dh1)
    dx1, dy0, dpre1, dpost0 = _norm_bwd(geo, dx2, dh1, x1, (gpt(0, 1), tabs[0], 1),
                                        (y0, gp(npost, 0, 0), tabs[0], 0, FFN_STEP), "nb_1")

    dh0 = ffn_bwd(dy0, h0t, G0, U0, (0, 0))
    dx0, dpre0 = _norm_bwd(geo, dx1, dh0, x0, (gpt(0, 0), tabs[0], 0), None, "nb_0")

    dpre = [[dpre0, dpre1, dpre2], [dpre3, dpre4, dpre5]]
    dpost = [[dpost0, dpost1, dpost2], [dpost3, dpost4, dpost5]]
    dtab = jnp.stack([jnp.stack([jnp.stack([dpre[l][s][:, 1], dpre[l][s][:, 2], dpost[l][s][:, 1]], axis=1) for s in range(3)], axis=1)
                      for l in range(L)])
    dtab_p = _pack([jnp.moveaxis(dtab.reshape(L, 2, 9 * D), 1, 0)])
    (dtab_g,) = _all_gather([dtab_p], "gather_dtab")
    dtab_sum = _sum_n(dtab_g, "sum_dtab")
    (dm_all,) = _unpack(dtab_g, [(2, L, 9 * D)], (NDEV,))
    (dm_sum,) = _unpack(dtab_sum, [(2, L, 9 * D)])
    (g_mod_b,) = _unpack(_sum_n(jnp.stack([_pack([dm_sum[0]]), _pack([dm_sum[1]])]), "sum_mod_b"), [(L, 9 * D)])
    dml = lax.dynamic_slice_in_dim(dm_all[:, 0], me * MC, MC, axis=2)
    dmc = lax.dynamic_slice_in_dim(dm_sum[1], me * MC, MC, axis=1)
    DM = jnp.concatenate([jnp.moveaxis(dml, 0, 1), dmc[:, None], jnp.zeros((L, 7, MC), F32)], axis=1)
    g_mod_w, dA = _mod_bwd(A, DM, p["mod_w"], "mod_bwd")

    g_npre = jnp.stack([jnp.stack([dpre[l][s][0, 0] + dpre[l][s][1, 0] for s in range(3)]) for l in range(L)])
    g_npost = jnp.stack([jnp.stack([dpost[l][s][0, 0] + dpost[l][s][1, 0] for s in range(3)]) for l in range(L)])
    g_logit = jnp.stack([glg0.reshape(H), glg1.reshape(H)])
    small_parts = [dA[8], g_npre, g_npost, g_cw, g_cb, g_wa, g_ba, g_wx, g_bx, g_lam, g_logit, g_gn, g_pool_w, g_pscale, g_qn, g_kn]
    (sg,) = _all_gather([_pack(small_parts)], "gather_small_g")
    ssum = _unpack(_sum_n(sg, "sum_small_g"), [a.shape for a in small_parts])
    (g_cctx, g_npre, g_npost, g_cw, g_cb, g_wa, g_ba, g_wx, g_bx, g_lam, g_logit, g_gn, g_pool_w, g_pscale, g_qn, g_kn) = ssum
    small_g = {
        "c_ctx": g_cctx, "mod_b": g_mod_b, "norm_pre": _my_shard(g_npre, me), "norm_post": _my_shard(g_npost, me),
        "lru_conv_w": _my_shard(g_cw, me)[None], "lru_conv_b": g_cb, "lru_wa": g_wa[None], "lru_ba": _my_shard(g_ba, me)[None],
        "lru_wx": g_wx[None], "lru_bx": _my_shard(g_bx, me)[None], "lru_lambda": _my_shard(g_lam, me)[None],
        "ret_decay_logit": g_logit[None], "ret_gn": g_gn, "pool_w": g_pool_w[None], "pool_scale": _my_shard(g_pscale, me),
        "q_norm": g_qn, "k_norm": g_kn,
    }
    shapes = [p[n].shape for n in SMALL]
    s_out = _reduce_adam(_pack([small_g[n] for n in SMALL])[None], _pack([p[n] for n in SMALL]),
                         _pack([p["m_" + n] for n in SMALL]), _pack([p["v_" + n] for n in SMALL]), "adam_small")
    deferred[0](s_out[0][:8, :128])
    res = {}
    for kind, packed in zip(("g", "d", "m", "v"), s_out):
        for n, a in zip(SMALL, _unpack(packed, shapes)):
            res[kind, n] = a

    def big(name, pieces, own, idx=None):
        w, m, v = p[name], p["m_" + name], p["v_" + name]
        if idx is not None:
            w, m, v = w[idx], m[idx], v[idx]
        shp = w.shape
        tag = name + ("" if idx is None else "_" + "".join(str(i) for i in idx))
        outs = _reduce_adam(pieces.reshape((pieces.shape[0], -1, shp[-1])), w.reshape(-1, shp[-1]), m.reshape(-1, shp[-1]),
                            v.reshape(-1, shp[-1]), "adam_" + tag, None if own is None else own.reshape(-1, shp[-1]))
        return [o.reshape(shp) for o in outs]

    got = {}
    after = [dx0]
    for key in ((1, 1), "od", (1, 0), (0, 1), "ev", (0, 0)):
        tag = key if isinstance(key, str) else f"ffn_{key[0]}{key[1]}"
        if key != (0, 0):
            state, own = big_g[key]
        else:
            for name, (pieces, own_) in (("mod_w", (g_mod_w[None], None)), ("ev_w_in", got["ev"][0]), ("ev_w_out", got["ev"][1]),
                                         ("od_w_in", got["od"][0]), ("od_w_out", got["od"][1])):
                outs = big(name, pieces, own_, None if name == "mod_w" else (0,))
                for kind, o in zip(("g", "d", "m", "v"), outs):
                    res[kind, name] = o if name == "mod_w" else o[None]
            after = [s_out[0]] + [res["g", n] for n in ("mod_w", "ev_w_in", "ev_w_out", "od_w_in", "od_w_out")]
            got[key] = [None] * 3
            for wi in (2, 0, 1):
                st_w, own_w = big_g[key][wi]
                lands = _xchg_wait(st_w, after, f"aw_{tag}_{wi}")
                got[key][wi] = (lands[0], own_w[0])
                after = [lands[0]]
            continue
        lands = _xchg_wait(state, after, "aw_" + tag)
        got[key] = list(zip(lands, own))
        after = [lands[0]]

    keys = [(l, j) for l in range(L) for j in range(2)]
    for wi, name in enumerate(("ffn_gate", "ffn_up", "ffn_down")):
        shp = p[name].shape
        st3 = (len(keys), shp[-2], shp[-1])
        outs = _reduce_adam_stack([got[k][wi][0] for k in keys], [got[k][wi][1] for k in keys], p[name].reshape(st3),
                                  p["m_" + name].reshape(st3), p["v_" + name].reshape(st3), "adam_" + name)
        for kind, o in zip(("g", "d", "m", "v"), outs):
            res[kind, name] = o.reshape(shp)

    loss = lax.psum(loss_p[0, 0], ("x", "y", "c"))
    grad_x = dx0[:S][None]
    return (loss, grad_x) + tuple(res[kind, n] for kind in ("g", "d", "m", "v") for n in WEIGHTS)


def kernel(
        x, c, ctx, c_ctx, mod_w, mod_b, norm_pre, norm_post, ffn_gate, ffn_up, ffn_down, ev_w_in, ev_w_out, lru_conv_w,
        lru_conv_b, lru_wa, lru_ba, lru_wx, lru_bx, lru_lambda, ret_decay_logit, ret_gn, od_w_in, od_w_out, pool_w,
        pool_scale, q_norm, k_norm, loss_target, m_c_ctx, m_mod_w, m_mod_b, m_norm_pre, m_norm_post, m_ffn_gate, m_ffn_up,
        m_ffn_down, m_ev_w_in, m_ev_w_out, m_lru_conv_w, m_lru_conv_b, m_lru_wa, m_lru_ba, m_lru_wx, m_lru_bx, m_lru_lambda,
        m_ret_decay_logit, m_ret_gn, m_od_w_in, m_od_w_out, m_pool_w, m_pool_scale, m_q_norm, m_k_norm, v_c_ctx, v_mod_w,
        v_mod_b, v_norm_pre, v_norm_post, v_ffn_gate, v_ffn_up, v_ffn_down, v_ev_w_in, v_ev_w_out, v_lru_conv_w,
        v_lru_conv_b, v_lru_wa, v_lru_ba, v_lru_wx, v_lru_bx, v_lru_lambda, v_ret_decay_logit, v_ret_gn, v_od_w_in,
        v_od_w_out, v_pool_w, v_pool_scale, v_q_norm, v_k_norm):
    args = locals()
    return _step({n: args[n] for n in INPUTS})
```

```python
import functools

import jax
import jax.numpy as jnp
from jax import lax
from jax.experimental import pallas as pl
from jax.experimental.pallas import tpu as pltpu

F32 = jnp.float32
BF = jnp.bfloat16
S_ = jax.ShapeDtypeStruct
MESH = pl.DeviceIdType.MESH

NDEV = 8
EPS = 1e-6
FFN_STEP = 0.5
LRU_C = 8.0
RET_CHUNK = 128
RET_THETA = 10000.0
ROPE_THETA = 10000.0
GRID_W = 64
POOL_WINDOWS = (2, 4, 8, 16)
ROW_TILE = 256
HALO = 8
VMEM_LIMIT = 58 * 1024 * 1024
FFN_FWD_ROWS = 768
FFN_BWD_ROWS = 528
FFN_FWD_SPLIT = 1
FFN_BWD_SPLIT = 1
WGRAD_ROWS = 1408
MM_ROWS = 1408

ADAM_LR = 0.001
ADAM_B1 = 0.9
ADAM_B2 = 0.999
ADAM_EPS = 1e-08
ADAM_WD = 0.01
ADAM_STEP = 10

LOG2E = 1.4426950408889634
LN2 = 0.6931471805599453

NN = ((1,), (0,))
NT = ((1,), (1,))
TN = ((0,), (0,))


def _dot(a, b, dn):
    return lax.dot_general(a.astype(BF), b.astype(BF), (dn, ((), ())), preferred_element_type=F32)


@functools.partial(jax.custom_vjp, nondiff_argnums=(2,))
def _bdot(a, b, mode):
    return _dot(a, b, {"nn": NN, "nt": NT, "tn": TN}[mode])


def _bdot_fwd(a, b, mode):
    return _bdot(a, b, mode), (a, b)


def _bdot_bwd(mode, res, g):
    a, b = res
    if mode == "nn":
        return _dot(g, b, NT), _dot(a, g, TN)
    if mode == "nt":
        return _dot(g, b, NN), _dot(g, a, TN)
    return _dot(b, g, NT), _dot(a, g, NN)


_bdot.defvjp(_bdot_fwd, _bdot_bwd)


@functools.partial(jax.custom_vjp, nondiff_argnums=(1, 2))
def _rollv(x, shift, axis):
    return pltpu.roll(x, shift, axis)


def _rollv_fwd(x, shift, axis):
    return pltpu.roll(x, shift, axis), None


def _rollv_bwd(shift, axis, _, g):
    n = g.shape[axis]
    return (pltpu.roll(g, (n - shift) % n, axis),)


_rollv.defvjp(_rollv_fwd, _rollv_bwd)


def _cp(vmem=VMEM_LIMIT):
    return pltpu.CompilerParams(vmem_limit_bytes=vmem)


def _tile(n, pref, mult=8):
    if n <= pref:
        return n
    for t in range(pref, 0, -1):
        if n % t == 0 and t % mult == 0:
            return t
    return n


def _full(shape):
    nd = len(shape)
    return pl.BlockSpec(tuple(shape), lambda *_: (0,) * nd)


def _me():
    return lax.axis_index("x"), lax.axis_index("y"), lax.axis_index("c")


def _all_gather(arrs, name):
    n = len(arrs)

    def body(*refs):
        xs, outs = refs[:n], refs[n:2 * n]
        send_sems, recv_sems, local_sems = refs[2 * n:]
        x, y, c = _me()
        me, sibling = (x, y, c), (x, y, 1 - c)
        chips = [(1 - x, y), (x, 1 - y), (1 - x, 1 - y)]

        def blk(out, p):
            return out.at[4 * p[0] + 2 * p[1] + p[2]]

        def copy(a, k, block, to, src=None):
            return pltpu.make_async_remote_copy(
                src_ref=blk(outs[a], block) if src is None else src, dst_ref=blk(outs[a], block),
                send_sem=send_sems.at[a, k], recv_sem=recv_sems.at[a, k], device_id=to, device_id_type=MESH)

        mine = [pltpu.make_async_copy(xs[a], blk(outs[a], me), local_sems.at[a]) for a in range(n)]
        for cp in mine:
            cp.start()
        first = []
        for a in range(n):
            first.append(copy(a, 0, me, sibling, src=xs[a]))
            first += [copy(a, 1 + j, me, (*chip, c), src=xs[a]) for j, chip in enumerate(chips)]
        for cp in first:
            cp.start()
        passed = []
        for j, chip in enumerate(chips):
            for a in range(n):
                copy(a, 1 + j, (*chip, c), me).wait_recv()
                fw = copy(a, 4 + j, (*chip, c), sibling)
                fw.start()
                passed.append(fw)
        for a in range(n):
            copy(a, 0, sibling, me).wait_recv()
            for j, chip in enumerate(chips):
                copy(a, 4 + j, (*chip, 1 - c), me).wait_recv()
        for cp in first + passed:
            cp.wait_send()
        for cp in mine:
            cp.wait()

    anyspec = pl.BlockSpec(memory_space=pl.ANY)
    return pl.pallas_call(
        body, name=name,
        out_shape=[S_((NDEV,) + a.shape, a.dtype) for a in arrs],
        in_specs=[anyspec] * n, out_specs=[anyspec] * n,
        scratch_shapes=[pltpu.SemaphoreType.DMA((n, 7)), pltpu.SemaphoreType.DMA((n, 7)), pltpu.SemaphoreType.DMA((n,))],
    )(*arrs)


def _all_to_all(arrs, name):
    n = len(arrs)

    def body(*refs):
        xs, outs = refs[:n], refs[n:2 * n]
        send_sems, recv_sems, local_sems = refs[2 * n:]
        x, y, c = _me()
        me_idx = 4 * x + 2 * y + c
        mine = [pltpu.make_async_copy(xs[a].at[me_idx], outs[a].at[me_idx], local_sems.at[a]) for a in range(n)]
        for cp in mine:
            cp.start()
        copies = []
        for k in range(1, NDEV):
            kx, ky, kc = (k >> 2) & 1, (k >> 1) & 1, k & 1
            px = 1 - x if kx else x
            py = 1 - y if ky else y
            pc = 1 - c if kc else c
            p_idx = 4 * px + 2 * py + pc
            for a in range(n):
                copies.append(pltpu.make_async_remote_copy(
                    src_ref=xs[a].at[p_idx], dst_ref=outs[a].at[me_idx],
                    send_sem=send_sems.at[a, k - 1], recv_sem=recv_sems.at[a, k - 1],
                    device_id=(px, py, pc), device_id_type=MESH))
        for cp in copies:
            cp.start()
        for cp in copies:
            cp.wait_recv()
        for cp in copies:
            cp.wait_send()
        for cp in mine:
            cp.wait()

    anyspec = pl.BlockSpec(memory_space=pl.ANY)
    return pl.pallas_call(
        body, name=name,
        out_shape=[S_(a.shape, a.dtype) for a in arrs],
        in_specs=[anyspec] * n, out_specs=[anyspec] * n,
        scratch_shapes=[pltpu.SemaphoreType.DMA((n, 7)), pltpu.SemaphoreType.DMA((n, 7)), pltpu.SemaphoreType.DMA((n,))],
    )(*arrs)


HBM_SPEC = pl.BlockSpec(memory_space=pltpu.HBM)
SEM_SPEC = pl.BlockSpec(memory_space=pltpu.SEMAPHORE)
EFFECT = pltpu.SideEffectType.DATAFLOW_SIDE_EFFECTING


def _peers():
    x, y, c = _me()
    out = []
    for k in range(1, NDEV):
        px = 1 - x if (k >> 2) & 1 else x
        py = 1 - y if (k >> 1) & 1 else y
        pc = 1 - c if k & 1 else c
        out.append(((px, py, pc), 4 * px + 2 * py + pc))
    return out, 4 * x + 2 * y + c


def _xchg_copies(gather, xs, lands, send, recv):
    peers, me_idx = _peers()
    out = []
    for k, (dev, p_idx) in enumerate(peers):
        for a in range(len(xs)):
            out.append(pltpu.make_async_remote_copy(
                src_ref=xs[a] if gather else xs[a].at[p_idx], dst_ref=lands[a].at[me_idx],
                send_sem=send[a].at[k], recv_sem=recv[a].at[k], device_id=dev, device_id_type=MESH))
    return out


def _xchg_start(gather, xs, lands, name, after=()):
    n = len(xs)
    na = len(after)

    def body(*refs):
        xr, lr = refs[:n], refs[n:2 * n]
        outs = refs[2 * n + na:]
        for cp in _xchg_copies(gather, xr, lr, outs[:n], outs[n:2 * n]):
            cp.start()
        outs[4 * n][...] = jnp.zeros(outs[4 * n].shape, F32)

    ops = [pltpu.with_memory_space_constraint(a, pltpu.HBM) for a in list(xs) + list(lands)]
    outs = pl.pallas_call(
        body, name=name,
        out_shape=[pltpu.SemaphoreType.DMA((NDEV - 1,))] * (2 * n) + [pltpu.HBM(a.shape, a.dtype) for a in ops]
        + [S_((8, 128), F32)],
        in_specs=[HBM_SPEC] * (2 * n) + [pl.BlockSpec(memory_space=pl.ANY)] * na,
        out_specs=[SEM_SPEC] * (2 * n) + [HBM_SPEC] * (2 * n) + [pl.BlockSpec(memory_space=pltpu.VMEM)],
        input_output_aliases={i: 2 * n + i for i in range(2 * n)},
        compiler_params=pltpu.CompilerParams(has_side_effects=EFFECT),
    )(*ops, *after)
    return (gather, n, outs[:4 * n]), outs[4 * n]


def _xchg_wait(state, after, name):
    gather, n, st = state
    send, recv, xs, lands = st[:n], st[n:2 * n], st[2 * n:3 * n], st[3 * n:4 * n]
    after = list(after) if isinstance(after, (list, tuple)) else [after]

    def body(*refs):
        xr, lr = refs[:n], refs[n:2 * n]
        sr, rr = refs[2 * n:3 * n], refs[3 * n:4 * n]
        for cp in _xchg_copies(gather, xr, lr, sr, rr):
            cp.wait_send()
            cp.wait_recv()

    outs = pl.pallas_call(
        body, name=name,
        out_shape=[pltpu.HBM(a.shape, a.dtype) for a in list(xs) + list(lands)],
        in_specs=[HBM_SPEC] * (2 * n) + [SEM_SPEC] * (2 * n) + [pl.BlockSpec(memory_space=pl.ANY)] * len(after),
        out_specs=[HBM_SPEC] * (2 * n), input_output_aliases={i: i for i in range(2 * n)},
        compiler_params=pltpu.CompilerParams(has_side_effects=EFFECT),
    )(*xs, *lands, *send, *recv, *after)
    return outs[n:]


def _tie(a, token):
    return a + token[0, 0].astype(a.dtype)


def _cast_bf16(a, name):
    R, C = a.shape
    tr = _tile(R, 512, 16)

    def body(a_ref, o_ref):
        o_ref[...] = a_ref[...].astype(BF)

    return pl.pallas_call(body, name=name, grid=(R // tr,), in_specs=[pl.BlockSpec((tr, C), lambda i: (i, 0))],
                          out_specs=pl.BlockSpec((tr, C), lambda i: (i, 0)), out_shape=S_((R, C), BF), compiler_params=_cp())(a)


def _sum_n(a, name):
    n, R, C = a.shape
    tr = _tile(R, 256, 8)

    def body(a_ref, o_ref):
        acc = a_ref[0].astype(F32)
        for i in range(1, n):
            acc = acc + a_ref[i].astype(F32)
        o_ref[...] = acc

    return pl.pallas_call(body, name=name, grid=(R // tr,), in_specs=[pl.BlockSpec((n, tr, C), lambda i: (0, i, 0))],
                          out_specs=pl.BlockSpec((tr, C), lambda i: (i, 0)), out_shape=S_((R, C), F32), compiler_params=_cp())(a)


def _adam_math(w, g, m, v):
    m = ADAM_B1 * m + (1.0 - ADAM_B1) * g
    v = ADAM_B2 * v + (1.0 - ADAM_B2) * jnp.square(g)
    m_hat = m / (1.0 - ADAM_B1 ** ADAM_STEP)
    v_hat = v / (1.0 - ADAM_B2 ** ADAM_STEP)
    delta = -ADAM_LR * (m_hat / (jnp.sqrt(v_hat) + ADAM_EPS) + ADAM_WD * w)
    return delta, m, v


def _reduce_adam(pieces, w, m, v, name, own=None):
    n, R, C = pieces.shape
    tr = _tile(R, 256, 8)

    def body(*refs):
        p_ref, w_ref, m_ref, v_ref = refs[:4]
        g_ref, d_ref, mo_ref, vo_ref = refs[-4:]
        g = p_ref[0].astype(F32)
        for i in range(1, n):
            g = g + p_ref[i].astype(F32)
        if own is not None:
            g = g + refs[4][...].astype(F32)
        d, mn, vn = _adam_math(w_ref[...], g, m_ref[...], v_ref[...])
        g_ref[...] = g
        d_ref[...] = d
        mo_ref[...] = mn
        vo_ref[...] = vn

    row = pl.BlockSpec((tr, C), lambda i: (i, 0))
    ins = [pieces, w, m, v] + ([] if own is None else [own])
    return pl.pallas_call(body, name=name, grid=(R // tr,),
                          in_specs=[pl.BlockSpec((n, tr, C), lambda i: (0, i, 0))] + [row] * (len(ins) - 1),
                          out_specs=[row] * 4, out_shape=[S_((R, C), F32)] * 4, compiler_params=_cp())(*ins)


def _reduce_adam_stack(pieces, owns, w, m, v, name):
    F = len(pieces)
    n, R, C = pieces[0].shape
    tr = _tile(R, max(8, 131072 // C), 8)

    def body(*refs):
        w_ref, m_ref, v_ref = refs[2 * F:2 * F + 3]
        g_ref, d_ref, mo_ref, vo_ref = refs[-4:]
        f = pl.program_id(0)
        for ff in range(F):
            @pl.when(f == ff)
            def _(ff=ff):
                g = refs[ff][0].astype(F32)
                for i in range(1, n):
                    g = g + refs[ff][i].astype(F32)
                g = g + refs[F + ff][...].astype(F32)
                d, mn, vn = _adam_math(w_ref[...], g, m_ref[...], v_ref[...])
                g_ref[...] = g
                d_ref[...] = d
                mo_ref[...] = mn
                vo_ref[...] = vn

    pspecs = [pl.BlockSpec((n, tr, C), lambda f, r, ff=ff: (0, jnp.where(f == ff, r, 0), 0)) for ff in range(F)]
    ospecs = [pl.BlockSpec((tr, C), lambda f, r, ff=ff: (jnp.where(f == ff, r, 0), 0)) for ff in range(F)]
    st = pl.BlockSpec((None, tr, C), lambda f, r: (f, r, 0))
    return pl.pallas_call(body, name=name, grid=(F, R // tr), in_specs=pspecs + ospecs + [st] * 3, out_specs=[st] * 4,
                          out_shape=[S_((F, R, C), F32)] * 4, compiler_params=_cp())(*pieces, *owns, w, m, v)


def _mm_cols(a, wb, name, out_dtype=F32, dep=None):
    M, K = a.shape
    NB, _, nb = wb.shape
    tm = _tile(M, MM_ROWS, 16)

    def body(*refs):
        refs[-1][...] = _dot(refs[0][...], refs[1][...], NN).astype(out_dtype)

    deps = [] if dep is None else [dep]
    return pl.pallas_call(body, name=name, grid=(M // tm, NB),
                          in_specs=[pl.BlockSpec((tm, K), lambda i, j: (i, 0)), pl.BlockSpec((None, K, nb), lambda i, j: (j, 0, 0))]
                          + [_full(d.shape) for d in deps],
                          out_specs=pl.BlockSpec((tm, nb), lambda i, j: (i, j)), out_shape=S_((M, NB * nb), out_dtype),
                          compiler_params=_cp())(a, wb, *deps)


def _mm_nt_cols(g, wb, name):
    M = g.shape[0]
    NB, K, nb = wb.shape
    tm = _tile(M, MM_ROWS, 16)

    def body(g_ref, w_ref, o_ref):
        j = pl.program_id(1)
        part = _dot(g_ref[...], w_ref[...], NT)

        @pl.when(j == 0)
        def _():
            o_ref[...] = part

        @pl.when(j > 0)
        def _():
            o_ref[...] += part

    return pl.pallas_call(body, name=name, grid=(M // tm, NB),
                          in_specs=[pl.BlockSpec((tm, nb), lambda i, j: (i, j)), pl.BlockSpec((None, K, nb), lambda i, j: (j, 0, 0))],
                          out_specs=pl.BlockSpec((tm, K), lambda i, j: (i, 0)), out_shape=S_((M, K), F32),
                          compiler_params=_cp())(g, wb)


def _mm_full(a, w, dn, name, out_dtype=F32):
    M, K = a.shape
    N = w.shape[1] if dn == NN else w.shape[0]
    tm = _tile(M, 768, 16)

    def body(a_ref, w_ref, o_ref):
        o_ref[...] = _dot(a_ref[...], w_ref[...], dn).astype(out_dtype)

    return pl.pallas_call(body, name=name, grid=(M // tm,),
                          in_specs=[pl.BlockSpec((tm, K), lambda i: (i, 0)), _full(w.shape)],
                          out_specs=pl.BlockSpec((tm, N), lambda i: (i, 0)), out_shape=S_((M, N), out_dtype),
                          compiler_params=_cp())(a, w)


def _mm_tn_cols(at, g, NB, name):
    K, M = at.shape
    nb = g.shape[1] // NB
    tk = _tile(M, WGRAD_ROWS, 128)
    nk = M // tk

    def body(a_ref, g_ref, o_ref, acc):
        k = pl.program_id(1)
        part = _dot(a_ref[...], g_ref[...], NN)

        @pl.when(k == 0)
        def _():
            acc[...] = part

        @pl.when(k > 0)
        def _():
            acc[...] += part

        @pl.when(k == nk - 1)
        def _():
            o_ref[...] = acc[...].astype(BF)

    return pl.pallas_call(body, name=name, grid=(NB, nk),
                          in_specs=[pl.BlockSpec((K, tk), lambda b, k: (0, k)), pl.BlockSpec((tk, nb), lambda b, k: (k, b))],
                          out_specs=pl.BlockSpec((None, K, nb), lambda b, k: (b, 0, 0)), out_shape=S_((NB, K, nb), BF),
                          scratch_shapes=[pltpu.VMEM((K, nb), F32)], compiler_params=_cp())(at, g)


def _mm_tn_rows(a, g, NB, name):
    M = a.shape[0]
    kb = a.shape[1] // NB
    N = g.shape[1]
    tk = _tile(M, WGRAD_ROWS, 128)
    nk = M // tk

    def body(a_ref, g_ref, o_ref, acc):
        k = pl.program_id(1)
        part = _dot(a_ref[...], g_ref[...], TN)

        @pl.when(k == 0)
        def _():
            acc[...] = part

        @pl.when(k > 0)
        def _():
            acc[...] += part

        @pl.when(k == nk - 1)
        def _():
            o_ref[...] = acc[...].astype(BF)

    return pl.pallas_call(body, name=name, grid=(NB, nk),
                          in_specs=[pl.BlockSpec((tk, kb), lambda b, k: (k, b)), pl.BlockSpec((tk, N), lambda b, k: (k, 0))],
                          out_specs=pl.BlockSpec((None, kb, N), lambda b, k: (b, 0, 0)), out_shape=S_((NB, kb, N), BF),
                          scratch_shapes=[pltpu.VMEM((kb, N), F32)], compiler_params=_cp())(a, g)


def _ffn_fwd(h, wg, wu, wd, name, dep=None):
    T, D = h.shape
    NB, _, nb = wg.shape
    tm = _tile(T, FFN_FWD_ROWS, 16)

    def body(*refs):
        h_ref, wg_ref, wu_ref, wd_ref = refs[:4]
        y_ref, g_ref, u_ref = refs[-3:]
        b = pl.program_id(1)
        parts = []
        for r in range(FFN_FWD_SPLIT):
            rows = pl.ds(r * (tm // FFN_FWD_SPLIT), tm // FFN_FWD_SPLIT)
            hh = h_ref[rows, :]
            g = _dot(hh, wg_ref[...], NN).astype(BF)
            u = _dot(hh, wu_ref[...], NN).astype(BF)
            g_ref[rows, :] = g
            u_ref[rows, :] = u
            gf = g.astype(F32)
            parts.append((rows, _dot(gf * jax.nn.sigmoid(gf) * u.astype(F32), wd_ref[...], NN)))

        @pl.when(b == 0)
        def _():
            for rows, part in parts:
                y_ref[rows, :] = part

        @pl.when(b > 0)
        def _():
            for rows, part in parts:
                y_ref[rows, :] += part

    deps = [] if dep is None else [dep]
    wcol = pl.BlockSpec((None, D, nb), lambda i, b: (b, 0, 0))
    act = pl.BlockSpec((None, tm, nb), lambda i, b: (b, i, 0))
    return pl.pallas_call(
        body, name=name, grid=(T // tm, NB),
        in_specs=[pl.BlockSpec((tm, D), lambda i, b: (i, 0)), wcol, wcol, pl.BlockSpec((None, nb, D), lambda i, b: (b, 0, 0))]
        + [_full(d.shape) for d in deps],
        out_specs=[pl.BlockSpec((tm, D), lambda i, b: (i, 0)), act, act],
        out_shape=[S_((T, D), F32), S_((NB, T, nb), BF), S_((NB, T, nb), BF)], compiler_params=_cp())(h, wg, wu, wd, *deps)


def _ffn_bwd_act(dy, G, U, wg, wu, wd, name):
    T, D = dy.shape
    NB, _, nb = wg.shape
    tm = _tile(T, FFN_BWD_ROWS, 16)

    def body(dy_ref, g_ref, u_ref, wg_ref, wu_ref, wd_ref, dh_ref, dg_ref, du_ref, a_ref):
        b = pl.program_id(1)
        parts = []
        for r in range(FFN_BWD_SPLIT):
            rows = pl.ds(r * (tm // FFN_BWD_SPLIT), tm // FFN_BWD_SPLIT)
            g, u = g_ref[rows, :].astype(F32), u_ref[rows, :].astype(F32)
            da = _dot(dy_ref[rows, :], wd_ref[...], NT)
            s = jax.nn.sigmoid(g)
            silu = g * s
            du = da * silu
            dg = da * u * (s * (1.0 + g * (1.0 - s)))
            dg_ref[rows, :] = dg.astype(BF)
            du_ref[rows, :] = du.astype(BF)
            a_ref[rows, :] = (silu * u).astype(BF)
            parts.append((rows, _dot(dg, wg_ref[...], NT) + _dot(du, wu_ref[...], NT)))

        @pl.when(b == 0)
        def _():
            for rows, part in parts:
                dh_ref[rows, :] = part

        @pl.when(b > 0)
        def _():
            for rows, part in parts:
                dh_ref[rows, :] += part

    wcol = pl.BlockSpec((None, D, nb), lambda i, b: (b, 0, 0))
    wrow = pl.BlockSpec((None, nb, D), lambda i, b: (b, 0, 0))
    act = pl.BlockSpec((None, tm, nb), lambda i, b: (b, i, 0))
    row = pl.BlockSpec((tm, D), lambda i, b: (i, 0))
    return pl.pallas_call(
        body, name=name, grid=(T // tm, NB),
        in_specs=[row, act, act, wcol, wcol, wrow],
        out_specs=[row, act, act, act],
        out_shape=[S_((T, D), F32)] + [S_((NB, T, nb), BF)] * 3, compiler_params=_cp())(dy, G, U, wg, wu, wd)


def _ffn_wgrad_in(ht, dact, name, dep=None):
    D, T = ht.shape
    NB, _, nb = dact.shape
    tk = _tile(T, WGRAD_ROWS, 128)
    nk = T // tk
    deps = [] if dep is None else [dep]

    def body(*refs):
        h_ref, d_ref = refs[:2]
        o_ref, acc = refs[-2:]
        k = pl.program_id(1)
        part = _dot(h_ref[...], d_ref[...], NN)

        @pl.when(k == 0)
        def _():
            acc[...] = part

        @pl.when(k > 0)
        def _():
            acc[...] += part

        @pl.when(k == nk - 1)
        def _():
            o_ref[...] = acc[...].astype(BF)

    return pl.pallas_call(body, name=name, grid=(NB, nk),
                          in_specs=[pl.BlockSpec((D, tk), lambda b, k: (0, k)), pl.BlockSpec((None, tk, nb), lambda b, k: (b, k, 0))]
                          + [_full(d.shape) for d in deps],
                          out_specs=pl.BlockSpec((None, D, nb), lambda b, k: (b, 0, 0)), out_shape=S_((NB, D, nb), BF),
                          scratch_shapes=[pltpu.VMEM((D, nb), F32)], compiler_params=_cp())(ht, dact, *deps)


def _ffn_wgrad_out(act, dy, name, dep=None):
    NB, T, nb = act.shape
    D = dy.shape[1]
    tk = _tile(T, WGRAD_ROWS, 128)
    nk = T // tk
    deps = [] if dep is None else [dep]

    def body(*refs):
        a_ref, d_ref = refs[:2]
        o_ref, acc = refs[-2:]
        k = pl.program_id(1)
        part = _dot(a_ref[...], d_ref[...], TN)

        @pl.when(k == 0)
        def _():
            acc[...] = part

        @pl.when(k > 0)
        def _():
            acc[...] += part

        @pl.when(k == nk - 1)
        def _():
            o_ref[...] = acc[...].astype(BF)

    return pl.pallas_call(body, name=name, grid=(NB, nk),
                          in_specs=[pl.BlockSpec((None, tk, nb), lambda b, k: (b, k, 0)), pl.BlockSpec((tk, D), lambda b, k: (k, 0))]
                          + [_full(d.shape) for d in deps],
                          out_specs=pl.BlockSpec((None, nb, D), lambda b, k: (b, 0, 0)), out_shape=S_((NB, nb, D), BF),
                          scratch_shapes=[pltpu.VMEM((nb, D), F32)], compiler_params=_cp())(act, dy, *deps)


class _Geo:
    def __init__(self, S, LC, D):
        self.S, self.LC, self.D, self.T = S, LC, D, S + LC
        self.RT = _tile(LC, ROW_TILE, 8)
        assert S % self.RT == 0 and self.RT >= 2 * HALO
        self.nlat, self.nctx = S // self.RT, LC // self.RT
        self.nt = self.nlat + self.nctx

    def row(self, C, cb=0):
        return pl.BlockSpec((self.RT, C), lambda i: (i, cb))

    def prev(self, C, cb=0):
        return pl.BlockSpec((self.RT, C), lambda i: (jnp.maximum(i - 1, 0), cb))

    def next(self, C, cb=0):
        nt = self.nt
        return pl.BlockSpec((self.RT, C), lambda i: (jnp.minimum(i + 1, nt - 1), cb))

    def seg(self, r, C):
        nlat = self.nlat
        return pl.BlockSpec((None, r, C), lambda i: (jnp.minimum(i // nlat, 1), 0, 0))

    def first_of_seg(self, i):
        return jnp.logical_or(i == 0, i == self.nlat)

    def prev_ok(self, i):
        return jnp.logical_and(i != 0, i != self.nlat)

    def next_ok(self, i):
        return jnp.logical_and(i != self.nlat - 1, i != self.nt - 1)

    def pos(self, i):
        r = lax.broadcasted_iota(jnp.int32, (self.RT, 1), 0)
        is_ctx = i >= self.nlat
        base = jnp.where(is_ctx, (i - self.nlat) * self.RT, i * self.RT)
        return base + r, jnp.where(is_ctx, self.LC, self.S)


def _rms(x):
    return x * lax.rsqrt(jnp.mean(x * x, axis=-1, keepdims=True) + EPS)


def _modulate(x, g, shift, scale):
    return (_rms(x) * g) * (1 + scale) + shift


def _post(x, y, g, gate, w):
    return x + w * gate * (_rms(y) * g)


def _norm_fwd(geo, x, post, pre, name):
    D = geo.D
    ins, specs = [x], [geo.row(D)]
    if post is not None:
        ins += [post[0], post[1], post[2]]
        specs += [geo.row(D), _full((1, D)), geo.seg(9, D)]
    if pre is not None:
        ins += [pre[0], pre[1]]
        specs += [_full((1, D)), geo.seg(9, D)]

    def body(*refs):
        it = iter(refs)
        xv = next(it)[...]
        if post is not None:
            y_ref, gp_ref, tab_ref = next(it), next(it), next(it)
        if pre is not None:
            gq_ref, tabn_ref = next(it), next(it)
        if post is not None:
            r = 3 * post[3] + 2
            xv = _post(xv, y_ref[...], gp_ref[...], tab_ref[r:r + 1, :], post[4])
            next(it)[...] = xv
        if pre is not None:
            r = 3 * pre[2]
            h = _modulate(xv, gq_ref[...], tabn_ref[r:r + 1, :], tabn_ref[r + 1:r + 2, :]).astype(BF)
            next(it)[...] = h
            next(it)[...] = h.T

    outs, ospecs = [], []
    if post is not None:
        outs.append(S_((geo.T, D), F32))
        ospecs.append(geo.row(D))
    if pre is not None:
        outs += [S_((geo.T, D), BF), S_((D, geo.T), BF)]
        ospecs += [geo.row(D), pl.BlockSpec((D, geo.RT), lambda i: (0, i))]
    return pl.pallas_call(body, name=name, grid=(geo.nt,), in_specs=specs, out_specs=ospecs, out_shape=outs,
                          compiler_params=_cp())(*ins)


def _acc_rows(ref, first, rows):
    for k, v in enumerate(rows):
        @pl.when(first)
        def _(k=k, v=v):
            ref[k:k + 1, :] = v

        @pl.when(jnp.logical_not(first))
        def _(k=k, v=v):
            ref[k:k + 1, :] += v


def _norm_bwd(geo, dxo, dh, x, pre, post, name):
    D = geo.D
    ins = [dxo, dh, x, pre[0], pre[1]]
    specs = [geo.row(D)] * 3 + [_full((1, D)), geo.seg(9, D)]
    if post is not None:
        ins += [post[0], post[1], post[2]]
        specs += [geo.row(D), _full((1, D)), geo.seg(9, D)]

    def body(*refs):
        i = pl.program_id(0)
        first = geo.first_of_seg(i)
        dxo_ref, dh_ref, x_ref, gq_ref, tab_ref = refs[:5]
        k = 5
        if post is not None:
            y_ref, gp_ref, tabp_ref = refs[5:8]
            k = 8
        outs = refs[k:]
        r = 3 * pre[2]
        _, vjp = jax.vjp(_modulate, x_ref[...], gq_ref[...], tab_ref[r:r + 1, :], tab_ref[r + 1:r + 2, :])
        dx, dg, dsh, dsc = vjp(dh_ref[...].astype(F32))
        dx = dx + dxo_ref[...]
        outs[0][...] = dx
        if post is None:
            _acc_rows(outs[1], first, [dg, dsh, dsc])
            return
        _acc_rows(outs[2], first, [dg, dsh, dsc])
        rp = 3 * post[3] + 2
        w = post[4]
        _, vjp2 = jax.vjp(lambda yy, gg, ga: w * ga * (_rms(yy) * gg), y_ref[...], gp_ref[...], tabp_ref[rp:rp + 1, :])
        dy, dgp, dga = vjp2(dx)
        outs[1][...] = dy
        _acc_rows(outs[3], first, [dgp, dga])

    if post is None:
        outs, ospecs = [S_((geo.T, D), F32), S_((2, 3, D), F32)], [geo.row(D), geo.seg(3, D)]
    else:
        outs = [S_((geo.T, D), F32), S_((geo.T, D), F32), S_((2, 3, D), F32), S_((2, 2, D), F32)]
        ospecs = [geo.row(D), geo.row(D), geo.seg(3, D), geo.seg(2, D)]
    return pl.pallas_call(body, name=name, grid=(geo.nt,), in_specs=specs, out_specs=ospecs, out_shape=outs,
                          compiler_params=_cp())(*ins)


def _loss_bwd(geo, xf, tgt, post, name):
    D = geo.D
    nlat = geo.nlat

    def body(x_ref, t_ref, y_ref, gp_ref, tabp_ref, loss_ref, dx_ref, dy_ref, dpost_ref):
        i = pl.program_id(0)
        first = geo.first_of_seg(i)
        lat = i < nlat
        diff = x_ref[...] - t_ref[...]
        part = jnp.where(lat, 0.5 * jnp.sum(jnp.mean(diff * diff, axis=-1, keepdims=True), axis=0, keepdims=True), 0.0)

        @pl.when(i == 0)
        def _():
            loss_ref[...] = part

        @pl.when(i > 0)
        def _():
            loss_ref[...] += part

        dx = jnp.where(lat, diff * (1.0 / D), 0.0)
        dx_ref[...] = dx
        rp = 3 * post[3] + 2
        w = post[4]
        _, vjp2 = jax.vjp(lambda yy, gg, ga: w * ga * (_rms(yy) * gg), y_ref[...], gp_ref[...], tabp_ref[rp:rp + 1, :])
        dy, dgp, dga = vjp2(dx)
        dy_ref[...] = dy
        _acc_rows(dpost_ref, first, [dgp, dga])

    tspec = pl.BlockSpec((geo.RT, D), lambda i: (jnp.minimum(i, nlat - 1), 0))
    return pl.pallas_call(
        body, name=name, grid=(geo.nt,),
        in_specs=[geo.row(D), tspec, geo.row(D), _full((1, D)), geo.seg(9, D)],
        out_specs=[_full((1, 1)), geo.row(D), geo.row(D), geo.seg(2, D)],
        out_shape=[S_((1, 1), F32), S_((geo.T, D), F32), S_((geo.T, D), F32), S_((2, 2, D), F32)],
        compiler_params=_cp())(xf, tgt, post[0], post[1], post[2])


def _fill_ext(geo, ext, prev_ref, cur_ref, next_ref, i):
    RT = geo.RT
    ext[0:HALO, :] = jnp.where(geo.prev_ok(i), prev_ref[RT - HALO:RT, :], 0.0).astype(ext.dtype)
    ext[HALO:HALO + RT, :] = cur_ref[...].astype(ext.dtype)
    ext[HALO + RT:2 * HALO + RT, :] = jnp.where(geo.next_ok(i), next_ref[0:HALO, :], 0.0).astype(ext.dtype)


CONV_W = 4
CONV_LEFT = 2


def _lru_gates(u, za, zx, lam):
    r = jax.nn.sigmoid(za)
    i = jax.nn.sigmoid(zx)
    log_a = -LRU_C * r * jax.nn.softplus(-lam)
    a = jnp.exp(log_a)
    return a, jnp.sqrt(1.0 - jnp.exp(2.0 * log_a)) * (i * u)


def _lru_coef_fwd(geo, P, W, cw, cb, wa, ba, wx, bx, lam, name):
    RT = geo.RT
    nblk, LB = wa.shape[1], wa.shape[2]

    def body(pp, pc, pn, cw_ref, cb_ref, wa_ref, ba_ref, wx_ref, bx_ref, lam_ref, u_ref, a0, b0, a1, b1, ext):
        i = pl.program_id(0)
        _fill_ext(geo, ext, pp, pc, pn, i)
        u = cb_ref[...] + ext[pl.ds(HALO - CONV_LEFT, RT), :] * cw_ref[0:1, :]
        for k in range(1, CONV_W):
            u = u + ext[pl.ds(HALO - CONV_LEFT + k, RT), :] * cw_ref[k:k + 1, :]
        u_ref[...] = u
        for d, (a_ref, b_ref) in enumerate(((a0, b0), (a1, b1))):
            for n in range(nblk):
                sl = slice(n * LB, (n + 1) * LB)
                un = u[:, sl]
                za = _dot(un, wa_ref[d, n], NN) + ba_ref[d:d + 1, sl]
                zx = _dot(un, wx_ref[d, n], NN) + bx_ref[d:d + 1, sl]
                a, b = _lru_gates(un, za, zx, lam_ref[d:d + 1, sl])
                a_ref[:, sl] = a
                b_ref[:, sl] = b

    return pl.pallas_call(
        body, name=name, grid=(geo.nt,),
        in_specs=[geo.prev(W, 1), geo.row(W, 1), geo.next(W, 1), _full(cw.shape), _full(cb.shape), _full(wa.shape),
                  _full(ba.shape), _full(wx.shape), _full(bx.shape), _full(lam.shape)],
        out_specs=[geo.row(W)] * 5, out_shape=[S_((geo.T, W), F32)] * 5,
        scratch_shapes=[pltpu.VMEM((RT + 2 * HALO, W), F32)], compiler_params=_cp())(P, P, P, cw, cb, wa, ba, wx, bx, lam)


def _lru_coef_bwd(geo, u, da, db, W, wa, ba, wx, bx, lam, name):
    nblk, LB = wa.shape[1], wa.shape[2]

    def body(u_ref, da0, db0, da1, db1, wa_ref, ba_ref, wx_ref, bx_ref, lam_ref, du_ref, dwa, dba, dwx, dbx, dlam):
        i = pl.program_id(0)

        @pl.when(i == 0)
        def _():
            for r in (dwa, dba, dwx, dbx, dlam):
                r[...] = jnp.zeros(r.shape, F32)

        u = u_ref[...]
        for n in range(nblk):
            sl = slice(n * LB, (n + 1) * LB)
            un = u[:, sl]
            dun = jnp.zeros_like(un)
            for d, (da_ref, db_ref) in enumerate(((da0, db0), (da1, db1))):
                za = _dot(un, wa_ref[d, n], NN) + ba_ref[d:d + 1, sl]
                zx = _dot(un, wx_ref[d, n], NN) + bx_ref[d:d + 1, sl]
                _, vjp = jax.vjp(_lru_gates, un, za, zx, lam_ref[d:d + 1, sl])
                du_e, dza, dzx, dl = vjp((da_ref[:, sl], db_ref[:, sl]))
                dun = dun + du_e + _dot(dza, wa_ref[d, n], NT) + _dot(dzx, wx_ref[d, n], NT)
                dwa[d, n] += _dot(un, dza, TN)
                dwx[d, n] += _dot(un, dzx, TN)
                dba[d:d + 1, sl] += jnp.sum(dza, axis=0, keepdims=True)
                dbx[d:d + 1, sl] += jnp.sum(dzx, axis=0, keepdims=True)
                dlam[d:d + 1, sl] += dl
            du_ref[:, sl] = dun

    row = geo.row(W)
    return pl.pallas_call(
        body, name=name, grid=(geo.nt,),
        in_specs=[row] * 5 + [_full(wa.shape), _full(ba.shape), _full(wx.shape), _full(bx.shape), _full(lam.shape)],
        out_specs=[row, _full(wa.shape), _full(ba.shape), _full(wx.shape), _full(bx.shape), _full(lam.shape)],
        out_shape=[S_((geo.T, W), F32), S_(wa.shape, F32), S_(ba.shape, F32), S_(wx.shape, F32), S_(bx.shape, F32), S_(lam.shape, F32)],
        compiler_params=_cp())(u, da[0], db[0], da[1], db[1], wa, ba, wx, bx, lam)


def _scan_order(d, k, nlat, nctx):
    if d == 0:
        return jnp.where(k < nctx, nlat + k, k - nctx)
    return jnp.where(k < nctx, nlat + nctx - 1 - k, nlat - 1 - (k - nctx))


def _chunk_scan(a, b, reverse):
    n = a.shape[0]
    row = lax.broadcasted_iota(jnp.int32, a.shape, 0)
    s = 1
    while s < n:
        if reverse:
            ok = row < n - s
            a_s, b_s = pltpu.roll(a, n - s, 0), pltpu.roll(b, n - s, 0)
        else:
            ok = row >= s
            a_s, b_s = pltpu.roll(a, s, 0), pltpu.roll(b, s, 0)
        b = a * jnp.where(ok, b_s, 0.0) + b
        a = a * jnp.where(ok, a_s, 1.0)
        s *= 2
    return a, b


def _shift1(x, reverse, fill):
    n = x.shape[0]
    row = lax.broadcasted_iota(jnp.int32, x.shape, 0)
    if reverse:
        return jnp.where(row == n - 1, fill, pltpu.roll(x, n - 1, 0))
    return jnp.where(row == 0, fill, pltpu.roll(x, 1, 0))


def _lru_scan_fwd(geo, a, b, d, name):
    W = a.shape[1]
    tw = _tile(W, 256, 128)
    RT, nlat, nctx = geo.RT, geo.nlat, geo.nctx
    rev = d == 1

    def body(a_ref, b_ref, h_ref, hp_ref, carry):
        k = pl.program_id(1)

        @pl.when(k == 0)
        def _():
            carry[...] = jnp.zeros(carry.shape, F32)

        ac, bc = _chunk_scan(a_ref[...], b_ref[...], rev)
        h = bc + ac * carry[...]
        h_ref[...] = h
        hp_ref[...] = _shift1(h, rev, carry[...])
        carry[...] = h[0:1, :] if rev else h[RT - 1:RT, :]

    spec = pl.BlockSpec((RT, tw), lambda j, k: (_scan_order(d, k, nlat, nctx), j))
    return pl.pallas_call(body, name=name, grid=(W // tw, geo.nt), in_specs=[spec, spec], out_specs=[spec, spec],
                          out_shape=[S_((geo.T, W), F32)] * 2, scratch_shapes=[pltpu.VMEM((1, tw), F32)],
                          compiler_params=_cp())(a, b)


def _lru_scan_bwd(geo, dh, a, hprev, d, name):
    W = a.shape[1]
    tw = _tile(W, 256, 128)
    RT, nlat, nctx, nt = geo.RT, geo.nlat, geo.nctx, geo.nt
    rev = d == 1

    def body(dh_ref, a_ref, hp_ref, da_ref, db_ref, carry):
        k = pl.program_id(1)

        @pl.when(k == 0)
        def _():
            carry[...] = jnp.zeros(carry.shape, F32)

        av = a_ref[...]
        a_next = _shift1(av, not rev, jnp.ones((1, tw), F32))
        ac, bc = _chunk_scan(a_next, dh_ref[...], not rev)
        lam = bc + ac * carry[...]
        db_ref[...] = lam
        da_ref[...] = lam * hp_ref[...]
        first = (av * lam)[RT - 1:RT, :] if rev else (av * lam)[0:1, :]
        carry[...] = first

    spec = pl.BlockSpec((RT, tw), lambda j, k: (_scan_order(d, nt - 1 - k, nlat, nctx), j))
    return pl.pallas_call(body, name=name, grid=(W // tw, nt), in_specs=[spec] * 3, out_specs=[spec] * 2,
                          out_shape=[S_((geo.T, W), F32)] * 2, scratch_shapes=[pltpu.VMEM((1, tw), F32)],
                          compiler_params=_cp())(dh, a, hprev)


def _conv_bwd_assemble(geo, P, du, pieces, W, cw, name):
    RT = geo.RT

    def body(pp, pc, pn, dup, duc, dun, dg, dq, dk, dv, dq1, dk1, dv1, dog, cw_ref, dP_ref, dcw, dcb, ext_r, ext_d):
        i = pl.program_id(0)
        _fill_ext(geo, ext_r, pp, pc, pn, i)
        _fill_ext(geo, ext_d, dup, duc, dun, i)
        du_c = duc[...]
        rows = []
        dr = None
        for k in range(CONV_W):
            rows.append(jnp.sum(du_c * ext_r[pl.ds(HALO - CONV_LEFT + k, RT), :], axis=0, keepdims=True))
            t = ext_d[pl.ds(HALO + CONV_LEFT - k, RT), :] * cw_ref[k:k + 1, :]
            dr = t if dr is None else dr + t
        first = i == 0
        _acc_rows(dcw, first, rows)
        _acc_rows(dcb, first, [jnp.sum(du_c, axis=0, keepdims=True)])
        for j, v in enumerate((dg[...], dr, dq[...] + dq1[...], dk[...] + dk1[...], dv[...] + dv1[...], dog[...])):
            dP_ref[:, j * W:(j + 1) * W] = v

    row = geo.row(W)
    return pl.pallas_call(
        body, name=name, grid=(geo.nt,),
        in_specs=[geo.prev(W, 1), geo.row(W, 1), geo.next(W, 1), geo.prev(W), row, geo.next(W)] + [row] * 8 + [_full(cw.shape)],
        out_specs=[geo.row(6 * W), _full((CONV_W, W)), _full((1, W))],
        out_shape=[S_((geo.T, 6 * W), F32), S_((CONV_W, W), F32), S_((1, W), F32)],
        scratch_shapes=[pltpu.VMEM((RT + 2 * HALO, W), F32)] * 2, compiler_params=_cp())(P, P, P, du, du, du, *pieces, cw)


def _rot_half(x, cos, sin):
    hw = x.shape[1] // 2
    x1, x2 = x[:, :hw], x[:, hw:]
    return jnp.concatenate([x1 * cos - x2 * sin, x1 * sin + x2 * cos], axis=-1)


def _ret_chunk(q, k, v, s, logit, cos, sin, rev):
    C, dk = q.shape
    lg = -jax.nn.softplus(-logit)
    q = _rot_half(q, cos, sin)
    k = _rot_half(k, cos, sin) * (dk ** -0.5)
    i = lax.broadcasted_iota(jnp.int32, (C, 1), 0).astype(F32)
    j = lax.broadcasted_iota(jnp.int32, (1, C), 1).astype(F32)
    if rev:
        diff, qe, ke = j - i, C - i, i
    else:
        diff, qe, ke = i - j, i + 1.0, C - 1.0 - i
    intra = jnp.where(diff >= 0, jnp.exp(lg * jnp.maximum(diff, 0.0)), 0.0)
    scores = _bdot(q, k, "nt") * intra
    o = _bdot(scores, v, "nn") + _bdot(q * jnp.exp(lg * qe), s, "nn")
    s_new = s * jnp.exp(lg * C) + _bdot(k * jnp.exp(lg * ke), v, "tn")
    return o, s_new


def _ret_specs(P, W, H, d, nlc, ncc, order_of_step):
    C = RET_CHUNK
    dk = W // H

    def cidx(k):
        return _scan_order(d, order_of_step(k), nlc, ncc)

    per_w = W // dk
    q = pl.BlockSpec((C, dk), lambda h, k: (cidx(k), 2 * per_w + h))
    kk = pl.BlockSpec((C, dk), lambda h, k: (cidx(k), 3 * per_w + h))
    v = pl.BlockSpec((C, dk), lambda h, k: (cidx(k), 4 * per_w + h))
    tab = pl.BlockSpec((C, dk // 2), lambda h, k: (cidx(k), 0))
    logit = pl.BlockSpec((None, 1, 1), lambda h, k: (h, 0, 0))
    o = pl.BlockSpec((C, dk), lambda h, k: (cidx(k), h))
    return q, kk, v, tab, logit, o


def _ret_fwd(geo, P, W, H, logits, cos, sin, name):
    C = RET_CHUNK
    dk = W // H
    nlc, ncc = geo.S // C, geo.LC // C
    nch = nlc + ncc

    def body(*refs):
        ins, outs, scrs = refs[:12], refs[12:16], refs[16:]
        k = pl.program_id(1)
        for d in range(2):
            q_ref, k_ref, v_ref, cos_ref, sin_ref, lg_ref = ins[6 * d:6 * d + 6]
            o_ref, st_ref = outs[2 * d:2 * d + 2]
            s_scr = scrs[d]

            @pl.when(k == 0)
            def _(s_scr=s_scr):
                s_scr[...] = jnp.zeros(s_scr.shape, F32)

            st_ref[...] = s_scr[...]
            o, s_new = _ret_chunk(q_ref[...], k_ref[...], v_ref[...], s_scr[...], lg_ref[...], cos_ref[...], sin_ref[...], d == 1)
            o_ref[...] = o
            s_scr[...] = s_new

    in_specs, out_specs = [], []
    st = pl.BlockSpec((None, None, dk, dk), lambda h, k: (h, k, 0, 0))
    for d in range(2):
        q, kk, v, tab, logit, o = _ret_specs(P, W, H, d, nlc, ncc, lambda k: k)
        in_specs += [q, kk, v, tab, tab, logit]
        out_specs += [o, st]
    return pl.pallas_call(body, name=name, grid=(H, nch), in_specs=in_specs, out_specs=out_specs,
                          out_shape=[S_((geo.T, W), F32), S_((H, nch, dk, dk), F32)] * 2,
                          scratch_shapes=[pltpu.VMEM((dk, dk), F32)] * 2, compiler_params=_cp())(
                              P, P, P, cos, sin, logits[0], P, P, P, cos, sin, logits[1])


def _ret_bwd(geo, P, W, H, logits, cos, sin, states, do, name):
    C = RET_CHUNK
    dk = W // H
    nlc, ncc = geo.S // C, geo.LC // C
    nch = nlc + ncc

    def body(*refs):
        ins, outs, scrs = refs[:16], refs[16:24], refs[24:]
        k = pl.program_id(1)
        for d in range(2):
            q_ref, k_ref, v_ref, cos_ref, sin_ref, lg_ref, st_ref, do_ref = ins[8 * d:8 * d + 8]
            dq_ref, dk_ref, dv_ref, dlg_ref = outs[4 * d:4 * d + 4]
            ds_scr = scrs[d]

            @pl.when(k == 0)
            def _(ds_scr=ds_scr, dlg_ref=dlg_ref):
                ds_scr[...] = jnp.zeros(ds_scr.shape, F32)
                dlg_ref[...] = jnp.zeros(dlg_ref.shape, F32)

            cos, sin = cos_ref[...], sin_ref[...]
            _, vjp = jax.vjp(lambda a, b, c, s, lg, cos=cos, sin=sin, d=d: _ret_chunk(a, b, c, s, lg, cos, sin, d == 1),
                             q_ref[...], k_ref[...], v_ref[...], st_ref[...], lg_ref[...])
            dq, dkk, dv, ds, dlg = vjp((do_ref[...], ds_scr[...]))
            dq_ref[...] = dq
            dk_ref[...] = dkk
            dv_ref[...] = dv
            ds_scr[...] = ds
            dlg_ref[...] += dlg

    in_specs, out_specs, ops = [], [], []
    st = pl.BlockSpec((None, None, dk, dk), lambda h, k: (h, nch - 1 - k, 0, 0))
    for d in range(2):
        q, kk, v, tab, logit, o = _ret_specs(P, W, H, d, nlc, ncc, lambda k: nch - 1 - k)
        in_specs += [q, kk, v, tab, tab, logit, st, o]
        out_specs += [o, o, o, logit]
        ops += [P, P, P, cos, sin, logits[d], states[d], do]
    return pl.pallas_call(body, name=name, grid=(H, nch), in_specs=in_specs, out_specs=out_specs,
                          out_shape=([S_((geo.T, W), F32)] * 3 + [S_((H, 1, 1), F32)]) * 2,
                          scratch_shapes=[pltpu.VMEM((dk, dk), F32)] * 2, compiler_params=_cp())(*ops)


def _mix_even_math(g, h0, h1, o0, o1, og, gn, H):
    lru = jax.nn.gelu(g) * (h0 + h1)
    o = o0 + o1
    dv = o.shape[1] // H
    parts = []
    for h in range(H):
        oh = o[:, h * dv:(h + 1) * dv]
        mu = jnp.mean(oh, axis=-1, keepdims=True)
        var = jnp.mean(jnp.square(oh - mu), axis=-1, keepdims=True)
        parts.append((oh - mu) * lax.rsqrt(var + EPS))
    y = jnp.concatenate(parts, axis=-1) * gn
    return lru, y * jax.nn.silu(og)


def _mix_even_fwd(geo, P, W, H, h0, h1, o0, o1, gn, name):
    def body(g_ref, h0r, h1r, o0r, o1r, og_ref, gn_ref, m_ref):
        lru, ret = _mix_even_math(g_ref[...], h0r[...], h1r[...], o0r[...], o1r[...], og_ref[...], gn_ref[...], H)
        m_ref[:, :W] = lru.astype(BF)
        m_ref[:, W:] = ret.astype(BF)

    row = geo.row(W)
    return pl.pallas_call(body, name=name, grid=(geo.nt,), in_specs=[geo.row(W, 0), row, row, row, row, geo.row(W, 5), _full((1, W))],
                          out_specs=geo.row(2 * W), out_shape=S_((geo.T, 2 * W), BF), compiler_params=_cp())(P, h0, h1, o0, o1, P, gn)


def _mix_even_bwd(geo, P, W, H, h0, h1, o0, o1, gn, dmix, name):
    def body(g_ref, h0r, h1r, o0r, o1r, og_ref, gn_ref, dl_ref, dr_ref, dg_ref, dh_ref, do_ref, dog_ref, dgn_ref):
        i = pl.program_id(0)
        _, vjp = jax.vjp(lambda g, hs, os_, og, gn_: _mix_even_math(g, hs, 0.0, os_, 0.0, og, gn_, H),
                         g_ref[...], h0r[...] + h1r[...], o0r[...] + o1r[...], og_ref[...], gn_ref[...])
        dg, dh, do, dog, dgn = vjp((dl_ref[...], dr_ref[...]))
        dg_ref[...] = dg
        dh_ref[...] = dh
        do_ref[...] = do
        dog_ref[...] = dog
        _acc_rows(dgn_ref, i == 0, [dgn])

    row = geo.row(W)
    return pl.pallas_call(
        body, name=name, grid=(geo.nt,),
        in_specs=[geo.row(W, 0), row, row, row, row, geo.row(W, 5), _full((1, W)), geo.row(W, 0), geo.row(W, 1)],
        out_specs=[row] * 4 + [_full((1, W))], out_shape=[S_((geo.T, W), F32)] * 4 + [S_((1, W), F32)],
        compiler_params=_cp())(P, h0, h1, o0, o1, P, gn, dmix, dmix)


def _head_norm_rot(x, g, cosf, sinf):
    xn = _rms(x) * g
    return xn * cosf + _rollv(xn, x.shape[1] // 2, 1) * sinf


def _qk_prep_fwd(geo, P, dims, qg, kg, cosf, sinf, name):
    PW, QW, KW, hd = dims
    qscale = hd ** -0.5 * LOG2E

    def body(p_ref, qg_ref, kg_ref, c_ref, s_ref, q_ref, k_ref, v_ref):
        c, s = c_ref[...], s_ref[...]
        for h in range(QW // hd):
            qh = _head_norm_rot(p_ref[:, PW + h * hd:PW + (h + 1) * hd], qg_ref[...], c, s)
            q_ref[:, h * hd:(h + 1) * hd] = (qh * qscale).astype(BF)
        for h in range(KW // hd):
            o = PW + QW + h * hd
            k_ref[:, h * hd:(h + 1) * hd] = _head_norm_rot(p_ref[:, o:o + hd], kg_ref[...], c, s).astype(BF)
            v_ref[:, 2 * h * hd:(2 * h + 1) * hd] = p_ref[:, o + KW:o + KW + hd].astype(BF)
            v_ref[:, (2 * h + 1) * hd:(2 * h + 2) * hd] = jnp.ones((geo.RT, hd), BF)

    tot = PW + QW + 2 * KW
    return pl.pallas_call(
        body, name=name, grid=(geo.nt,),
        in_specs=[geo.row(tot), _full((1, hd)), _full((1, hd)), geo.row(hd), geo.row(hd)],
        out_specs=[geo.row(QW), geo.row(KW), geo.row(2 * KW)],
        out_shape=[S_((geo.T, QW), BF), S_((geo.T, KW), BF), S_((geo.T, 2 * KW), BF)], compiler_params=_cp())(P, qg, kg, cosf, sinf)


def _qk_prep_bwd(geo, P, dims, qg, kg, cosf, sinf, dpool, dq, dk, dv, name):
    PW, QW, KW, hd = dims

    def body(p_ref, qg_ref, kg_ref, c_ref, s_ref, dpool_ref, dq_ref, dk_ref, dv_ref, dP_ref, dqg_ref, dkg_ref):
        i = pl.program_id(0)
        c, s = c_ref[...], s_ref[...]
        dP_ref[:, :PW] = dpool_ref[...]
        f = lambda x, g: _head_norm_rot(x, g, c, s)
        dqg = jnp.zeros((1, hd), F32)
        for h in range(QW // hd):
            o = PW + h * hd
            _, vjp = jax.vjp(f, p_ref[:, o:o + hd], qg_ref[...])
            dx, dg = vjp(dq_ref[:, h * hd:(h + 1) * hd])
            dP_ref[:, o:o + hd] = dx
            dqg = dqg + dg
        dkg = jnp.zeros((1, hd), F32)
        for h in range(KW // hd):
            o = PW + QW + h * hd
            _, vjp = jax.vjp(f, p_ref[:, o:o + hd], kg_ref[...])
            dx, dg = vjp(dk_ref[:, h * hd:(h + 1) * hd])
            dP_ref[:, o:o + hd] = dx
            dkg = dkg + dg
        dP_ref[:, PW + QW + KW:] = dv_ref[...]
        _acc_rows(dqg_ref, i == 0, [dqg])
        _acc_rows(dkg_ref, i == 0, [dkg])

    tot = PW + QW + 2 * KW
    return pl.pallas_call(
        body, name=name, grid=(geo.nt,),
        in_specs=[geo.row(tot), _full((1, hd)), _full((1, hd)), geo.row(hd), geo.row(hd), geo.row(PW), geo.row(QW), geo.row(KW), geo.row(KW)],
        out_specs=[geo.row(tot), _full((1, hd)), _full((1, hd))],
        out_shape=[S_((geo.T, tot), F32), S_((1, hd), F32), S_((1, hd), F32)], compiler_params=_cp())(
            P, qg, kg, cosf, sinf, dpool, dq, dk, dv)


def _att_tiles(T, q_rows):
    return _tile(T, q_rows, 16), _tile(T, 768, 16)


def _stack_heads(ref, G, hd):
    return jnp.concatenate([ref[:, g * hd:(g + 1) * hd] for g in range(G)], axis=0)


def _att_fwd(q, k, v1, hd, name):
    T, QW = q.shape
    KV = k.shape[1] // hd
    G = QW // hd // KV
    tq, tk = _att_tiles(T, 768)
    nk = T // tk

    def body(q_ref, k_ref, v_ref, o_ref, lse_ref, m_scr, acc):
        ki = pl.program_id(2)

        @pl.when(ki == 0)
        def _():
            m_scr[...] = jnp.full(m_scr.shape, -jnp.inf, F32)
            acc[...] = jnp.zeros(acc.shape, F32)

        s = _dot(_stack_heads(q_ref, G, hd), k_ref[...], NT)
        m_prev = m_scr[...]
        m_new = jnp.maximum(m_prev, jnp.max(s, axis=-1, keepdims=True))
        p = jnp.exp2(s - jnp.tile(m_new, (1, tk // hd)))
        acc[...] = jnp.tile(jnp.exp2(m_prev - m_new), (1, 2)) * acc[...] + _dot(p, v_ref[...], NN)
        m_scr[...] = m_new

        @pl.when(ki == nk - 1)
        def _():
            a = acc[...]
            o = a[:, :hd] / a[:, hd:]
            lse = m_scr[...] + jnp.log2(a[:, hd:])
            for g in range(G):
                o_ref[:, g * hd:(g + 1) * hd] = o[g * tq:(g + 1) * tq].astype(BF)
                lse_ref[g] = lse[g * tq:(g + 1) * tq]

    return pl.pallas_call(
        body, name=name, grid=(KV, T // tq, nk),
        in_specs=[pl.BlockSpec((tq, G * hd), lambda a, i, j: (i, a)), pl.BlockSpec((tk, hd), lambda a, i, j: (j, a)),
                  pl.BlockSpec((tk, 2 * hd), lambda a, i, j: (j, a))],
        out_specs=[pl.BlockSpec((tq, G * hd), lambda a, i, j: (i, a)), pl.BlockSpec((G, tq, hd), lambda a, i, j: (a, i, 0))],
        out_shape=[S_((T, QW), BF), S_((QW // hd, T, hd), F32)],
        scratch_shapes=[pltpu.VMEM((G * tq, hd), F32), pltpu.VMEM((G * tq, 2 * hd), F32)],
        compiler_params=_cp())(q, k, v1)


def _att_delta(geo, o, dmix, PW, hd, name):
    QW = o.shape[1]
    nh = QW // hd

    def body(o_ref, do_ref, d_ref):
        for h in range(nh):
            sl = slice(h * hd, (h + 1) * hd)
            d = jnp.sum(o_ref[:, sl].astype(F32) * do_ref[:, PW + h * hd:PW + (h + 1) * hd], axis=-1, keepdims=True)
            d_ref[h] = jnp.broadcast_to(d, (geo.RT, hd))

    return pl.pallas_call(body, name=name, grid=(geo.nt,), in_specs=[geo.row(QW), geo.row(PW + QW)],
                          out_specs=pl.BlockSpec((nh, geo.RT, hd), lambda i: (0, i, 0)), out_shape=S_((nh, geo.T, hd), F32),
                          compiler_params=_cp())(o, dmix)


def _att_bwd(q, k, v1, dmix, PW, lse, delta, hd, name):
    T, QW = q.shape
    KW = k.shape[1]
    KV = KW // hd
    G = QW // hd // KV
    tq, tk = _att_tiles(T, 256)
    nq, nk = T // tq, T // tk
    scale = hd ** -0.5
    pb = PW // hd

    def body(q_ref, k_ref, v_ref, *rest):
        do_refs, (lse_ref, dl_ref, dq_ref, dk_ref, dv_ref, dq_scr) = rest[:G], rest[G:]
        ki, qi = pl.program_id(1), pl.program_id(2)
        q3 = _stack_heads(q_ref, G, hd)
        do3 = jnp.concatenate([r[...] for r in do_refs], axis=0)
        lse = jnp.concatenate([lse_ref[g] for g in range(G)], axis=0)
        dl = jnp.concatenate([dl_ref[g] for g in range(G)], axis=0)
        kk = k_ref[...]
        p = jnp.exp2(_dot(q3, kk, NT) - jnp.tile(lse, (1, tk // hd)))
        ds = p * (_dot(do3, v_ref[:, :hd], NT) - jnp.tile(dl, (1, tk // hd)))
        pv = _dot(p, do3, TN)
        pk = _dot(ds, q3, TN)
        pq = _dot(ds, kk, NN)

        @pl.when(qi == 0)
        def _():
            dk_ref[...] = pk
            dv_ref[...] = pv

        @pl.when(qi > 0)
        def _():
            dk_ref[...] += pk
            dv_ref[...] += pv

        @pl.when(qi == nq - 1)
        def _():
            dk_ref[...] = dk_ref[...] * LN2

        @pl.when(ki == 0)
        def _():
            dq_scr[qi] = pq

        @pl.when(ki > 0)
        def _():
            dq_scr[qi] += pq

        @pl.when(ki == nk - 1)
        def _():
            full = dq_scr[qi]
            for g in range(G):
                dq_ref[:, g * hd:(g + 1) * hd] = full[g * tq:(g + 1) * tq] * scale

    qs = pl.BlockSpec((tq, G * hd), lambda a, j, i: (i, a))
    ks = pl.BlockSpec((tk, hd), lambda a, j, i: (j, a))
    vs = pl.BlockSpec((tk, 2 * hd), lambda a, j, i: (j, a))
    dos = [pl.BlockSpec((tq, hd), lambda a, j, i, g=g: (i, pb + a * G + g)) for g in range(G)]
    st = pl.BlockSpec((G, tq, hd), lambda a, j, i: (a, i, 0))
    dqs = pl.BlockSpec((tq, G * hd), lambda a, j, i: (jnp.where(j == nk - 1, i, 0), a))
    return pl.pallas_call(body, name=name, grid=(KV, nk, nq), in_specs=[qs, ks, vs] + dos + [st, st], out_specs=[dqs, ks, ks],
                          out_shape=[S_((T, QW), F32), S_((T, KW), F32), S_((T, KW), F32)],
                          scratch_shapes=[pltpu.VMEM((nq, G * tq, hd), F32)], compiler_params=_cp())(
                              q, k, v1, *([dmix] * G), lse, delta)


def _pool_cnt(pos, L, w):
    return (jnp.minimum(pos + w // 2, L) - jnp.maximum(pos - w // 2, 0)).astype(F32)


def _pool_mean(ext, gi, w, G, RT, cnt):
    acc = ext[pl.ds(HALO - w // 2, RT), gi * G:(gi + 1) * G]
    for off in range(-w // 2 + 1, w // 2):
        acc = acc + ext[pl.ds(HALO + off, RT), gi * G:(gi + 1) * G]
    return acc / cnt


def _pool_fwd(geo, P, PW, att, pw, ps, name):
    RT = geo.RT
    G = pw.shape[1]
    QW = att.shape[1]

    def body(pp, pc, pn, att_ref, pw_ref, ps_ref, m_ref, ext):
        i = pl.program_id(0)
        _fill_ext(geo, ext, pp, pc, pn, i)
        pos, L = geo.pos(i)
        for gi, w in enumerate(POOL_WINDOWS):
            sl = slice(gi * G, (gi + 1) * G)
            xm = _pool_mean(ext, gi, w, G, RT, _pool_cnt(pos, L, w)) - pc[:, sl]
            m_ref[:, sl] = (_dot(xm, pw_ref[gi], NN) * ps_ref[:, sl]).astype(BF)
        m_ref[:, PW:] = att_ref[...]

    return pl.pallas_call(
        body, name=name, grid=(geo.nt,),
        in_specs=[geo.prev(PW), geo.row(PW), geo.next(PW), geo.row(QW), _full(pw.shape), _full(ps.shape)],
        out_specs=geo.row(PW + QW), out_shape=S_((geo.T, PW + QW), BF),
        scratch_shapes=[pltpu.VMEM((RT + 2 * HALO, PW), F32)], compiler_params=_cp())(P, P, P, att, pw, ps)


def _pool_bwd(geo, P, PW, dmix, pw, ps, name):
    RT = geo.RT
    G = pw.shape[1]
    RE = RT + 2 * HALO

    def body(pp, pc, pn, dp_, dc, dn, pw_ref, ps_ref, dx_ref, dpw, dps, ext, extd, dmc):
        i = pl.program_id(0)
        _fill_ext(geo, ext, pp, pc, pn, i)
        _fill_ext(geo, extd, dp_, dc, dn, i)
        pos, L = geo.pos(i)
        r = lax.broadcasted_iota(jnp.int32, (RE, 1), 0)
        pos_e = pos[0:1, :] - HALO + r
        rows_s = []
        for gi, w in enumerate(POOL_WINDOWS):
            sl = slice(gi * G, (gi + 1) * G)
            xm = _pool_mean(ext, gi, w, G, RT, _pool_cnt(pos, L, w)) - pc[:, sl]
            pre = _dot(xm, pw_ref[gi], NN)
            dout = dc[:, sl]
            rows_s.append(jnp.sum(dout * pre, axis=0, keepdims=True))
            gw = _dot(xm, dout * ps_ref[:, sl], TN)

            @pl.when(i == 0)
            def _(gi=gi, gw=gw):
                dpw[gi] = gw

            @pl.when(i > 0)
            def _(gi=gi, gw=gw):
                dpw[gi] += gw

            dm_e = _dot(extd[:, sl] * ps_ref[:, sl], pw_ref[gi], NT)
            dmc[...] = dm_e / jnp.maximum(_pool_cnt(pos_e, L, w), 1.0)
            acc = -dm_e[HALO:HALO + RT]
            for off in range(-w // 2 + 1, w // 2 + 1):
                acc = acc + dmc[pl.ds(HALO + off, RT), :]
            dx_ref[:, sl] = acc
        _acc_rows(dps, i == 0, [jnp.concatenate(rows_s, axis=-1)])

    return pl.pallas_call(
        body, name=name, grid=(geo.nt,),
        in_specs=[geo.prev(PW), geo.row(PW), geo.next(PW), geo.prev(PW), geo.row(PW), geo.next(PW), _full(pw.shape), _full(ps.shape)],
        out_specs=[geo.row(PW), _full(pw.shape), _full(ps.shape)],
        out_shape=[S_((geo.T, PW), F32), S_(pw.shape, F32), S_(ps.shape, F32)],
        scratch_shapes=[pltpu.VMEM((RE, PW), F32), pltpu.VMEM((RE, PW), F32), pltpu.VMEM((RE, G), F32)],
        compiler_params=_cp())(P, P, P, dmix, dmix, dmix, pw, ps)


def _mod_fwd(A, mod_w, name):
    L, D, MC = mod_w.shape
    tn = _tile(MC, 768, 128)

    def body(a_ref, w_ref, o_ref):
        o_ref[...] = _dot(jax.nn.silu(a_ref[...]), w_ref[...], NN)

    return pl.pallas_call(body, name=name, grid=(L, MC // tn),
                          in_specs=[_full(A.shape), pl.BlockSpec((None, D, tn), lambda l, j: (l, 0, j))],
                          out_specs=pl.BlockSpec((None, 16, tn), lambda l, j: (l, 0, j)), out_shape=S_((L, 16, MC), F32),
                          compiler_params=_cp())(A, mod_w)


def _mod_bwd(A, DM, mod_w, name):
    L, D, MC = mod_w.shape
    tn = _tile(MC, 768, 128)
    nj = MC // tn

    def body(a_ref, dm_ref, w_ref, gw_ref, da_ref, acc):
        l, j = pl.program_id(0), pl.program_id(1)
        sa, vjp = jax.vjp(jax.nn.silu, a_ref[...])
        gw_ref[...] = _dot(sa, dm_ref[...], TN)
        part = _dot(dm_ref[...], w_ref[...], NT)
        first = jnp.logical_and(l == 0, j == 0)

        @pl.when(first)
        def _():
            acc[...] = part

        @pl.when(jnp.logical_not(first))
        def _():
            acc[...] += part

        @pl.when(jnp.logical_and(l == L - 1, j == nj - 1))
        def _():
            da_ref[...] = vjp(acc[...])[0]

    wspec = pl.BlockSpec((None, D, tn), lambda l, j: (l, 0, j))
    return pl.pallas_call(body, name=name, grid=(L, nj),
                          in_specs=[_full(A.shape), pl.BlockSpec((None, 16, tn), lambda l, j: (l, 0, j)), wspec],
                          out_specs=[wspec, _full(A.shape)], out_shape=[S_((L, D, MC), F32), S_(A.shape, F32)],
                          scratch_shapes=[pltpu.VMEM(A.shape, F32)], compiler_params=_cp())(A, DM, mod_w)


PACK_COLS = 1024


def _pack_rows(shape):
    n = 1
    for d in shape:
        n *= d
    return n, -(-n // (8 * PACK_COLS)) * 8


def _pack(arrs):
    parts = []
    for a in arrs:
        n, rows = _pack_rows(a.shape)
        parts.append(jnp.pad(a.reshape(-1).astype(F32), (0, rows * PACK_COLS - n)).reshape(rows, PACK_COLS))
    return jnp.concatenate(parts)


def _unpack(packed, shapes, lead=()):
    out, off = [], 0
    for s in shapes:
        n, rows = _pack_rows(s)
        blk = packed[..., off:off + rows, :].reshape(lead + (rows * PACK_COLS,))
        out.append(blk[..., :n].reshape(lead + tuple(s)))
        off += rows
    return out


def _unshard_last(g):
    g = jnp.moveaxis(g, 0, -2)
    return g.reshape(g.shape[:-2] + (g.shape[-2] * g.shape[-1],))


def _my_shard(a, me):
    n = a.shape[-1] // NDEV
    return lax.dynamic_slice_in_dim(a, me * n, n, axis=a.ndim - 1)


def _rot_tables(S, LC, dk, hd):
    t = jnp.arange(S, dtype=F32)
    n_r = dk // 2
    ang1 = t[:, None] * (RET_THETA ** (-jnp.arange(n_r, dtype=F32) / n_r))
    cos1 = jnp.concatenate([jnp.cos(ang1), jnp.ones((LC, n_r), F32)])
    sin1 = jnp.concatenate([jnp.sin(ang1), jnp.zeros((LC, n_r), F32)])
    n_ax = hd // 4
    f_ax = ROPE_THETA ** (-jnp.arange(n_ax, dtype=F32) / n_ax)
    row = jnp.floor(t / GRID_W)
    col = t - row * GRID_W
    ang2 = jnp.concatenate([row[:, None] * f_ax, col[:, None] * f_ax], axis=-1)
    c2, s2 = jnp.cos(ang2), jnp.sin(ang2)
    cosf = jnp.concatenate([jnp.concatenate([c2, c2], axis=-1), jnp.ones((LC, hd), F32)])
    sinf = jnp.concatenate([jnp.concatenate([-s2, s2], axis=-1), jnp.zeros((LC, hd), F32)])
    return cos1, sin1, cosf, sinf


SMALL = ("c_ctx", "mod_b", "norm_pre", "norm_post", "lru_conv_w", "lru_conv_b", "lru_wa", "lru_ba", "lru_wx", "lru_bx",
         "lru_lambda", "ret_decay_logit", "ret_gn", "pool_w", "pool_scale", "q_norm", "k_norm")
WEIGHTS = ("c_ctx", "mod_w", "mod_b", "norm_pre", "norm_post", "ffn_gate", "ffn_up", "ffn_down", "ev_w_in", "ev_w_out",
           "lru_conv_w", "lru_conv_b", "lru_wa", "lru_ba", "lru_wx", "lru_bx", "lru_lambda", "ret_decay_logit", "ret_gn",
           "od_w_in", "od_w_out", "pool_w", "pool_scale", "q_norm", "k_norm")
INPUTS = ("x", "c", "ctx") + WEIGHTS + ("loss_target",) + tuple("m_" + w for w in WEIGHTS) + tuple("v_" + w for w in WEIGHTS)


def _step(p):
    x, c, ctx = p["x"], p["c"], p["ctx"]
    _, S, D = x.shape
    LC = ctx.shape[1]
    geo = _Geo(S, LC, D)
    T = geo.T
    xi, yi, ci = _me()
    me = 4 * xi + 2 * yi + ci
    L = p["mod_w"].shape[0]
    assert L == 2
    W = p["lru_conv_b"].shape[-1]
    H = p["ret_decay_logit"].shape[-1]
    hd = p["q_norm"].shape[-1]
    G = p["pool_w"].shape[-1]
    PW = G * len(POOL_WINDOWS)
    od_mix = p["od_w_out"].shape[1] * NDEV
    od_in = p["od_w_in"].shape[2] * NDEV
    QW = od_mix - PW
    KW = (od_in - od_mix) // 2
    assert p["ev_w_in"].shape[2] * NDEV == 6 * W and p["ret_gn"].shape[-1] == W and p["ev_w_out"].shape[1] * NDEV == 2 * W
    odims = (PW, QW, KW, hd)
    cos1, sin1, cosf, sinf = _rot_tables(S, LC, W // H, hd)

    sh0 = [(D,), p["norm_pre"].shape, p["norm_post"].shape, p["lru_conv_w"].shape[1:], p["lru_ba"].shape[1:],
           p["lru_bx"].shape[1:], p["lru_lambda"].shape[1:], p["pool_scale"].shape[1:]]
    pack0 = _pack([c, p["norm_pre"], p["norm_post"], p["lru_conv_w"], p["lru_ba"], p["lru_bx"], p["lru_lambda"], p["pool_scale"]])
    (g0,) = _all_gather([pack0], "gather_small")
    c_all, npre, npost, conv_w, ba, bx, lam, pscale = _unpack(g0, sh0, (NDEV,))
    npre, npost, conv_w, ba, bx, lam, pscale = [_unshard_last(a) for a in (npre, npost, conv_w, ba, bx, lam, pscale)]
    pscale = pscale[None]
    conv_b = p["lru_conv_b"]
    wa, wx = p["lru_wa"][0], p["lru_wx"][0]
    gn = p["ret_gn"]
    logits = p["ret_decay_logit"][0].reshape(2, H, 1, 1)
    pool_w = p["pool_w"][0]
    qg, kg = p["q_norm"], p["k_norm"]

    A = jnp.concatenate([c_all, p["c_ctx"][None], jnp.zeros((7, D), F32)])
    M = _mod_fwd(A, p["mod_w"], "mod_fwd")
    (Mg,) = _all_gather([M], "gather_mod")
    MC = M.shape[2]
    tabs = []
    for l in range(L):
        ml = lax.dynamic_index_in_dim(Mg[:, l], me, axis=1, keepdims=False).reshape(NDEV * MC) + p["mod_b"][l]
        mc = Mg[:, l, 8].reshape(NDEV * MC) + p["mod_b"][l]
        tabs.append(jnp.stack([ml.reshape(9, D), mc.reshape(9, D)]))

    def cast2(a, name):
        return _cast_bf16(a.reshape(-1, a.shape[-1]), name).reshape(a.shape)

    loc = {(l, j): [cast2(p[n][l, j], f"cast_{n}_{l}{j}") for n in ("ffn_gate", "ffn_up", "ffn_down")]
           for l in range(L) for j in range(2)}
    loc["ev"] = [cast2(p["ev_w_in"][0], "cast_ev_in"), cast2(p["ev_w_out"][0], "cast_ev_out")]
    loc["od"] = [cast2(p["od_w_in"][0], "cast_od_in"), cast2(p["od_w_out"][0], "cast_od_out")]

    def gather_start(key, name, after):
        return _xchg_start(True, loc[key], [jnp.broadcast_to(a[None], (NDEV,) + a.shape) for a in loc[key]], name, [after])

    ffn_w = {(0, 0): _all_gather(loc[0, 0], "gather_ffn_00")}

    def gp(a, l, s):
        return a[l, s][None]

    st, tok = gather_start("ev", "gs_ev", ffn_w[0, 0][0])
    x0 = jnp.concatenate([x[0], ctx[0]])
    h0, h0t = _norm_fwd(geo, x0, None, (_tie(gp(npre, 0, 0), tok), tabs[0], 0), "pre_00")
    y0, G0, U0 = _ffn_fwd(h0, *ffn_w[0, 0], name="ffn_fwd_00")
    x1, h1, h1t = _norm_fwd(geo, x0, (y0, gp(npost, 0, 0), tabs[0], 0, FFN_STEP), (gp(npre, 0, 1), tabs[0], 1), "post_00")
    ev_in, ev_out = _xchg_wait(st, h1, "gw_ev")
    ev_out_f = ev_out.reshape(2 * W, D)

    st, tok = gather_start((0, 1), "gs_ffn_01", h1)
    Pe = _mm_cols(h1, ev_in, "ev_in", dep=tok)
    u, a0, b0, a1, b1 = _lru_coef_fwd(geo, Pe, W, conv_w, conv_b, wa, ba, wx, bx, lam, "lru_coef")
    hs0, hp0 = _lru_scan_fwd(geo, a0, b0, 0, "lru_scan_f0")
    hs1, hp1 = _lru_scan_fwd(geo, a1, b1, 1, "lru_scan_f1")
    o0, st0, o1, st1 = _ret_fwd(geo, Pe, W, H, logits, cos1, sin1, "ret_fwd")
    mixe = _mix_even_fwd(geo, Pe, W, H, hs0, hs1, o0, o1, gn, "mix_even")
    y1 = _mm_full(mixe, ev_out_f, NN, "ev_out")
    x2, h2, h2t = _norm_fwd(geo, x1, (y1, gp(npost, 0, 1), tabs[0], 1, 1.0), (gp(npre, 0, 2), tabs[0], 2), "post_01")
    ffn_w[0, 1] = _xchg_wait(st, h2, "gw_ffn_01")

    st, tok = gather_start((1, 0), "gs_ffn_10", h2)
    y2, G2, U2 = _ffn_fwd(h2, *ffn_w[0, 1], name="ffn_fwd_01", dep=tok)
    x3, h3, h3t = _norm_fwd(geo, x2, (y2, gp(npost, 0, 2), tabs[0], 2, FFN_STEP), (gp(npre, 1, 0), tabs[1], 0), "post_02")
    ffn_w[1, 0] = _xchg_wait(st, h3, "gw_ffn_10")

    st, tok = gather_start("od", "gs_od", h3)
    y3, G3, U3 = _ffn_fwd(h3, *ffn_w[1, 0], name="ffn_fwd_10", dep=tok)
    x4, h4, h4t = _norm_fwd(geo, x3, (y3, gp(npost, 1, 0), tabs[1], 0, FFN_STEP), (gp(npre, 1, 1), tabs[1], 1), "post_10")
    od_inw, od_out = _xchg_wait(st, h4, "gw_od")
    od_out_f = od_out.reshape(od_mix, D)

    st, tok = gather_start((1, 1), "gs_ffn_11", h4)
    Po = _mm_cols(h4, od_inw, "od_in", dep=tok)
    qr, kr, vr = _qk_prep_fwd(geo, Po, odims, qg, kg, cosf, sinf, "qk_prep")
    att, lse = _att_fwd(qr, kr, vr, hd, "att_fwd")
    mixo = _pool_fwd(geo, Po, PW, att, pool_w, pscale, "pool_fwd")
    y4 = _mm_full(mixo, od_out_f, NN, "od_out")
    x5, h5, h5t = _norm_fwd(geo, x4, (y4, gp(npost, 1, 1), tabs[1], 1, 1.0), (gp(npre, 1, 2), tabs[1], 2), "post_11")
    ffn_w[1, 1] = _xchg_wait(st, h5, "gw_ffn_11")

    y5, G5, U5 = _ffn_fwd(h5, *ffn_w[1, 1], name="ffn_fwd_11")
    (x6,) = _norm_fwd(geo, x5, (y5, gp(npost, 1, 2), tabs[1], 2, FFN_STEP), None, "post_12")

    big_g = {}
    tokbox = [None]
    deferred = []

    def gpt(l, s):
        return _tie(gp(npre, l, s), tokbox[0])

    def a2a_start(key, srcs, name, dh):
        own = [lax.dynamic_index_in_dim(a, me, 0, keepdims=False) for a in srcs]
        state, token = _xchg_start(False, srcs, [jnp.zeros(a.shape, a.dtype) for a in srcs], name)
        big_g[key] = (state, own)
        tokbox[0] = token
        return dh

    def ffn_bwd(dy, h, Gs, Us, key):
        tag = f"{key[0]}{key[1]}"
        dh, dG, dU, Aact = _ffn_bwd_act(dy, Gs, Us, *ffn_w[key], name=f"ffn_bwd_{tag}")
        if key != (0, 0):
            srcs = [_ffn_wgrad_in(h, dG, f"ffn_wg_{tag}"), _ffn_wgrad_in(h, dU, f"ffn_wu_{tag}"), _ffn_wgrad_out(Aact, dy, f"ffn_wd_{tag}")]
            return a2a_start(key, srcs, f"as_ffn_{tag}", dh)
        def finish(dep):
            parts = [None] * 3
            a2a_start(key, [_ffn_wgrad_out(Aact, dy, f"ffn_wd_{tag}", dep=dep)], f"as_ffn_{tag}_d", dh)
            parts[2] = big_g[key]
            a2a_start(key, [_ffn_wgrad_in(h, dG, f"ffn_wg_{tag}", dep=tokbox[0])], f"as_ffn_{tag}_g", dh)
            parts[0] = big_g[key]
            a2a_start(key, [_ffn_wgrad_in(h, dU, f"ffn_wu_{tag}", dep=tokbox[0])], f"as_ffn_{tag}_u", dh)
            parts[1] = big_g[key]
            big_g[key] = parts

        deferred.append(finish)
        return dh

    loss_p, dx6, dy5, dpost5 = _loss_bwd(geo, x6, p["loss_target"][0], (y5, gp(npost, 1, 2), tabs[1], 2, FFN_STEP), "loss")
    dh5 = ffn_bwd(dy5, h5t, G5, U5, (1, 1))
    dx5, dy4, dpre5, dpost4 = _norm_bwd(geo, dx6, dh5, x5, (gpt(1, 2), tabs[1], 2),
                                        (y4, gp(npost, 1, 1), tabs[1], 1, 1.0), "nb_5")

    dmixo = _mm_full(dy4, od_out_f, NT, "od_out_d")
    g_od_out = _mm_tn_rows(mixo, dy4, NDEV, "od_out_w")
    dpool, g_pool_w, g_pscale = _pool_bwd(geo, Po, PW, dmixo, pool_w, pscale, "pool_bwd")
    delta = _att_delta(geo, att, dmixo, PW, hd, "att_delta")
    dq, dk, dv = _att_bwd(qr, kr, vr, dmixo, PW, lse, delta, hd, "att_bwd")
    dPo, g_qn, g_kn = _qk_prep_bwd(geo, Po, odims, qg, kg, cosf, sinf, dpool, dq, dk, dv, "qk_prep_bwd")
    dh4 = _mm_nt_cols(dPo, od_inw, "od_in_d")
    g_od_in = _mm_tn_cols(h4t, dPo, NDEV, "od_in_w")
    dh4 = a2a_start("od", [g_od_in, g_od_out], "as_od", dh4)
    dx4, dy3, dpre4, dpost3 = _norm_bwd(geo, dx5, dh4, x4, (gpt(1, 1), tabs[1], 1),
                                        (y3, gp(npost, 1, 0), tabs[1], 0, FFN_STEP), "nb_4")

    dh3 = ffn_bwd(dy3, h3t, G3, U3, (1, 0))
    dx3, dy2, dpre3, dpost2 = _norm_bwd(geo, dx4, dh3, x3, (gpt(1, 0), tabs[1], 0),
                                        (y2, gp(npost, 0, 2), tabs[0], 2, FFN_STEP), "nb_3")

    dh2 = ffn_bwd(dy2, h2t, G2, U2, (0, 1))
    dx2, dy1, dpre2, dpost1 = _norm_bwd(geo, dx3, dh2, x2, (gpt(0, 2), tabs[0], 2),
                                        (y1, gp(npost, 0, 1), tabs[0], 1, 1.0), "nb_2")

    dmixe = _mm_full(dy1, ev_out_f, NT, "ev_out_d")
    g_ev_out = _mm_tn_rows(mixe, dy1, NDEV, "ev_out_w")
    dg, dhs, dos, dog, g_gn = _mix_even_bwd(geo, Pe, W, H, hs0, hs1, o0, o1, gn, dmixe, "mix_even_bwd")
    da0, db0 = _lru_scan_bwd(geo, dhs, a0, hp0, 0, "lru_scan_b0")
    da1, db1 = _lru_scan_bwd(geo, dhs, a1, hp1, 1, "lru_scan_b1")
    du, g_wa, g_ba, g_wx, g_bx, g_lam = _lru_coef_bwd(geo, u, (da0, da1), (db0, db1), W, wa, ba, wx, bx, lam, "lru_coef_bwd")
    dq0, dk0, dv0, glg0, dq1, dk1, dv1, glg1 = _ret_bwd(geo, Pe, W, H, logits, cos1, sin1, (st0, st1), dos, "ret_bwd")
    dPe, g_cw, g_cb = _conv_bwd_assemble(geo, Pe, du, (dg, dq0, dk0, dv0, dq1, dk1, dv1, dog), W, conv_w, "conv_bwd")
    dh1 = _mm_nt_cols(dPe, ev_in, "ev_in_d")
    g_ev_in = _mm_tn_cols(h1t, dPe, NDEV, "ev_in_w")
    dh1 = a2a_start("ev", [g_ev_in, g_ev_out], "as_ev", dh1)
    dx1, dy0, dpre1, dpost0 = _norm_bwd(geo, dx2, dh1, x1, (gpt(0, 1), tabs[0], 1),
                                        (y0, gp(npost, 0, 0), tabs[0], 0, FFN_STEP), "nb_1")

    dh0 = ffn_bwd(dy0, h0t, G0, U0, (0, 0))
    dx0, dpre0 = _norm_bwd(geo, dx1, dh0, x0, (gpt(0, 0), tabs[0], 0), None, "nb_0")

    dpre = [[dpre0, dpre1, dpre2], [dpre3, dpre4, dpre5]]
    dpost = [[dpost0, dpost1, dpost2], [dpost3, dpost4, dpost5]]
    dtab = jnp.stack([jnp.stack([jnp.stack([dpre[l][s][:, 1], dpre[l][s][:, 2], dpost[l][s][:, 1]], axis=1) for s in range(3)], axis=1)
                      for l in range(L)])
    dtab_p = _pack([jnp.moveaxis(dtab.reshape(L, 2, 9 * D), 1, 0)])
    (dtab_g,) = _all_gather([dtab_p], "gather_dtab")
    dtab_sum = _sum_n(dtab_g, "sum_dtab")
    (dm_all,) = _unpack(dtab_g, [(2, L, 9 * D)], (NDEV,))
    (dm_sum,) = _unpack(dtab_sum, [(2, L, 9 * D)])
    (g_mod_b,) = _unpack(_sum_n(jnp.stack([_pack([dm_sum[0]]), _pack([dm_sum[1]])]), "sum_mod_b"), [(L, 9 * D)])
    dml = lax.dynamic_slice_in_dim(dm_all[:, 0], me * MC, MC, axis=2)
    dmc = lax.dynamic_slice_in_dim(dm_sum[1], me * MC, MC, axis=1)
    DM = jnp.concatenate([jnp.moveaxis(dml, 0, 1), dmc[:, None], jnp.zeros((L, 7, MC), F32)], axis=1)
    g_mod_w, dA = _mod_bwd(A, DM, p["mod_w"], "mod_bwd")

    g_npre = jnp.stack([jnp.stack([dpre[l][s][0, 0] + dpre[l][s][1, 0] for s in range(3)]) for l in range(L)])
    g_npost = jnp.stack([jnp.stack([dpost[l][s][0, 0] + dpost[l][s][1, 0] for s in range(3)]) for l in range(L)])
    g_logit = jnp.stack([glg0.reshape(H), glg1.reshape(H)])
    small_parts = [dA[8], g_npre, g_npost, g_cw, g_cb, g_wa, g_ba, g_wx, g_bx, g_lam, g_logit, g_gn, g_pool_w, g_pscale, g_qn, g_kn]
    (sg,) = _all_gather([_pack(small_parts)], "gather_small_g")
    ssum = _unpack(_sum_n(sg, "sum_small_g"), [a.shape for a in small_parts])
    (g_cctx, g_npre, g_npost, g_cw, g_cb, g_wa, g_ba, g_wx, g_bx, g_lam, g_logit, g_gn, g_pool_w, g_pscale, g_qn, g_kn) = ssum
    small_g = {
        "c_ctx": g_cctx, "mod_b": g_mod_b, "norm_pre": _my_shard(g_npre, me), "norm_post": _my_shard(g_npost, me),
        "lru_conv_w": _my_shard(g_cw, me)[None], "lru_conv_b": g_cb, "lru_wa": g_wa[None], "lru_ba": _my_shard(g_ba, me)[None],
        "lru_wx": g_wx[None], "lru_bx": _my_shard(g_bx, me)[None], "lru_lambda": _my_shard(g_lam, me)[None],
        "ret_decay_logit": g_logit[None], "ret_gn": g_gn, "pool_w": g_pool_w[None], "pool_scale": _my_shard(g_pscale, me),
        "q_norm": g_qn, "k_norm": g_kn,
    }
    shapes = [p[n].shape for n in SMALL]
    s_out = _reduce_adam(_pack([small_g[n] for n in SMALL])[None], _pack([p[n] for n in SMALL]),
                         _pack([p["m_" + n] for n in SMALL]), _pack([p["v_" + n] for n in SMALL]), "adam_small")
    deferred[0](s_out[0][:8, :128])
    res = {}
    for kind, packed in zip(("g", "d", "m", "v"), s_out):
        for n, a in zip(SMALL, _unpack(packed, shapes)):
            res[kind, n] = a

    def big(name, pieces, own, idx=None):
        w, m, v = p[name], p["m_" + name], p["v_" + name]
        if idx is not None:
            w, m, v = w[idx], m[idx], v[idx]
        shp = w.shape
        tag = name + ("" if idx is None else "_" + "".join(str(i) for i in idx))
        outs = _reduce_adam(pieces.reshape((pieces.shape[0], -1, shp[-1])), w.reshape(-1, shp[-1]), m.reshape(-1, shp[-1]),
                            v.reshape(-1, shp[-1]), "adam_" + tag, None if own is None else own.reshape(-1, shp[-1]))
        return [o.reshape(shp) for o in outs]

    got = {}
    after = [dx0]
    for key in ((1, 1), "od", (1, 0), (0, 1), "ev", (0, 0)):
        tag = key if isinstance(key, str) else f"ffn_{key[0]}{key[1]}"
        if key != (0, 0):
            state, own = big_g[key]
        else:
            for name, (pieces, own_) in (("mod_w", (g_mod_w[None], None)), ("ev_w_in", got["ev"][0]), ("ev_w_out", got["ev"][1]),
                                         ("od_w_in", got["od"][0]), ("od_w_out", got["od"][1])):
                outs = big(name, pieces, own_, None if name == "mod_w" else (0,))
                for kind, o in zip(("g", "d", "m", "v"), outs):
                    res[kind, name] = o if name == "mod_w" else o[None]
            after = [s_out[0]] + [res["g", n] for n in ("mod_w", "ev_w_in", "ev_w_out", "od_w_in", "od_w_out")]
            got[key] = [None] * 3
            for wi in (2, 0, 1):
                st_w, own_w = big_g[key][wi]
                lands = _xchg_wait(st_w, after, f"aw_{tag}_{wi}")
                got[key][wi] = (lands[0], own_w[0])
                after = [lands[0]]
            continue
        lands = _xchg_wait(state, after, "aw_" + tag)
        got[key] = list(zip(lands, own))
        after = [lands[0]]

    keys = [(l, j) for l in range(L) for j in range(2)]
    for wi, name in enumerate(("ffn_gate", "ffn_up", "ffn_down")):
        shp = p[name].shape
        st3 = (len(keys), shp[-2], shp[-1])
        outs = _reduce_adam_stack([got[k][wi][0] for k in keys], [got[k][wi][1] for k in keys], p[name].reshape(st3),
                                  p["m_" + name].reshape(st3), p["v_" + name].reshape(st3), "adam_" + name)
        for kind, o in zip(("g", "d", "m", "v"), outs):
            res[kind, name] = o.reshape(shp)

    loss = lax.psum(loss_p[0, 0], ("x", "y", "c"))
    grad_x = dx0[:S][None]
    return (loss, grad_x) + tuple(res[kind, n] for kind in ("g", "d", "m", "v") for n in WEIGHTS)


def kernel(
        x, c, ctx, c_ctx, mod_w, mod_b, norm_pre, norm_post, ffn_gate, ffn_up, ffn_down, ev_w_in, ev_w_out, lru_conv_w,
        lru_conv_b, lru_wa, lru_ba, lru_wx, lru_bx, lru_lambda, ret_decay_logit, ret_gn, od_w_in, od_w_out, pool_w,
        pool_scale, q_norm, k_norm, loss_target, m_c_ctx, m_mod_w, m_mod_b, m_norm_pre, m_norm_post, m_ffn_gate, m_ffn_up,
        m_ffn_down, m_ev_w_in, m_ev_w_out, m_lru_conv_w, m_lru_conv_b, m_lru_wa, m_lru_ba, m_lru_wx, m_lru_bx, m_lru_lambda,
        m_ret_decay_logit, m_ret_gn, m_od_w_in, m_od_w_out, m_pool_w, m_pool_scale, m_q_norm, m_k_norm, v_c_ctx, v_mod_w,
        v_mod_b, v_norm_pre, v_norm_post, v_ffn_gate, v_ffn_up, v_ffn_down, v_ev_w_in, v_ev_w_out, v_lru_conv_w,
        v_lru_conv_b, v_lru_wa, v_lru_ba, v_lru_wx, v_lru_bx, v_lru_lambda, v_ret_decay_logit, v_ret_gn, v_od_w_in,
        v_od_w_out, v_pool_w, v_pool_scale, v_q_norm, v_k_norm):
    args = locals()
    return _step({n: args[n] for n in INPUTS})
```

```python
import functools

import jax
import jax.numpy as jnp
from jax import lax
from jax.experimental import pallas as pl
from jax.experimental.pallas import tpu as pltpu

F32 = jnp.float32
BF = jnp.bfloat16
S_ = jax.ShapeDtypeStruct
MESH = pl.DeviceIdType.MESH

NDEV = 8
EPS = 1e-6
FFN_STEP = 0.5
LRU_C = 8.0
RET_CHUNK = 128
RET_THETA = 10000.0
ROPE_THETA = 10000.0
GRID_W = 64
POOL_WINDOWS = (2, 4, 8, 16)
ROW_TILE = 256
HALO = 8
VMEM_LIMIT = 58 * 1024 * 1024
FFN_FWD_ROWS = 768
FFN_BWD_ROWS = 528
FFN_FWD_SPLIT = 1
FFN_BWD_SPLIT = 1
WGRAD_ROWS = 1408
MM_ROWS = 1408

ADAM_LR = 0.001
ADAM_B1 = 0.9
ADAM_B2 = 0.999
ADAM_EPS = 1e-08
ADAM_WD = 0.01
ADAM_STEP = 10

LOG2E = 1.4426950408889634
LN2 = 0.6931471805599453

NN = ((1,), (0,))
NT = ((1,), (1,))
TN = ((0,), (0,))


def _dot(a, b, dn):
    return lax.dot_general(a.astype(BF), b.astype(BF), (dn, ((), ())), preferred_element_type=F32)


@functools.partial(jax.custom_vjp, nondiff_argnums=(2,))
def _bdot(a, b, mode):
    return _dot(a, b, {"nn": NN, "nt": NT, "tn": TN}[mode])


def _bdot_fwd(a, b, mode):
    return _bdot(a, b, mode), (a, b)


def _bdot_bwd(mode, res, g):
    a, b = res
    if mode == "nn":
        return _dot(g, b, NT), _dot(a, g, TN)
    if mode == "nt":
        return _dot(g, b, NN), _dot(g, a, TN)
    return _dot(b, g, NT), _dot(a, g, NN)


_bdot.defvjp(_bdot_fwd, _bdot_bwd)


@functools.partial(jax.custom_vjp, nondiff_argnums=(1, 2))
def _rollv(x, shift, axis):
    return pltpu.roll(x, shift, axis)


def _rollv_fwd(x, shift, axis):
    return pltpu.roll(x, shift, axis), None


def _rollv_bwd(shift, axis, _, g):
    n = g.shape[axis]
    return (pltpu.roll(g, (n - shift) % n, axis),)


_rollv.defvjp(_rollv_fwd, _rollv_bwd)


def _cp(vmem=VMEM_LIMIT):
    return pltpu.CompilerParams(vmem_limit_bytes=vmem)


def _tile(n, pref, mult=8):
    if n <= pref:
        return n
    for t in range(pref, 0, -1):
        if n % t == 0 and t % mult == 0:
            return t
    return n


def _full(shape):
    nd = len(shape)
    return pl.BlockSpec(tuple(shape), lambda *_: (0,) * nd)


def _me():
    return lax.axis_index("x"), lax.axis_index("y"), lax.axis_index("c")


def _all_gather(arrs, name):
    n = len(arrs)

    def body(*refs):
        xs, outs = refs[:n], refs[n:2 * n]
        send_sems, recv_sems, local_sems = refs[2 * n:]
        x, y, c = _me()
        me, sibling = (x, y, c), (x, y, 1 - c)
        chips = [(1 - x, y), (x, 1 - y), (1 - x, 1 - y)]

        def blk(out, p):
            return out.at[4 * p[0] + 2 * p[1] + p[2]]

        def copy(a, k, block, to, src=None):
            return pltpu.make_async_remote_copy(
                src_ref=blk(outs[a], block) if src is None else src, dst_ref=blk(outs[a], block),
                send_sem=send_sems.at[a, k], recv_sem=recv_sems.at[a, k], device_id=to, device_id_type=MESH)

        mine = [pltpu.make_async_copy(xs[a], blk(outs[a], me), local_sems.at[a]) for a in range(n)]
        for cp in mine:
            cp.start()
        first = []
        for a in range(n):
            first.append(copy(a, 0, me, sibling, src=xs[a]))
            first += [copy(a, 1 + j, me, (*chip, c), src=xs[a]) for j, chip in enumerate(chips)]
        for cp in first:
            cp.start()
        passed = []
        for j, chip in enumerate(chips):
            for a in range(n):
                copy(a, 1 + j, (*chip, c), me).wait_recv()
                fw = copy(a, 4 + j, (*chip, c), sibling)
                fw.start()
                passed.append(fw)
        for a in range(n):
            copy(a, 0, sibling, me).wait_recv()
            for j, chip in enumerate(chips):
                copy(a, 4 + j, (*chip, 1 - c), me).wait_recv()
        for cp in first + passed:
            cp.wait_send()
        for cp in mine:
            cp.wait()

    anyspec = pl.BlockSpec(memory_space=pl.ANY)
    return pl.pallas_call(
        body, name=name,
        out_shape=[S_((NDEV,) + a.shape, a.dtype) for a in arrs],
        in_specs=[anyspec] * n, out_specs=[anyspec] * n,
        scratch_shapes=[pltpu.SemaphoreType.DMA((n, 7)), pltpu.SemaphoreType.DMA((n, 7)), pltpu.SemaphoreType.DMA((n,))],
    )(*arrs)


def _all_to_all(arrs, name):
    n = len(arrs)

    def body(*refs):
        xs, outs = refs[:n], refs[n:2 * n]
        send_sems, recv_sems, local_sems = refs[2 * n:]
        x, y, c = _me()
        me_idx = 4 * x + 2 * y + c
        mine = [pltpu.make_async_copy(xs[a].at[me_idx], outs[a].at[me_idx], local_sems.at[a]) for a in range(n)]
        for cp in mine:
            cp.start()
        copies = []
        for k in range(1, NDEV):
            kx, ky, kc = (k >> 2) & 1, (k >> 1) & 1, k & 1
            px = 1 - x if kx else x
            py = 1 - y if ky else y
            pc = 1 - c if kc else c
            p_idx = 4 * px + 2 * py + pc
            for a in range(n):
                copies.append(pltpu.make_async_remote_copy(
                    src_ref=xs[a].at[p_idx], dst_ref=outs[a].at[me_idx],
                    send_sem=send_sems.at[a, k - 1], recv_sem=recv_sems.at[a, k - 1],
                    device_id=(px, py, pc), device_id_type=MESH))
        for cp in copies:
            cp.start()
        for cp in copies:
            cp.wait_recv()
        for cp in copies:
            cp.wait_send()
        for cp in mine:
            cp.wait()

    anyspec = pl.BlockSpec(memory_space=pl.ANY)
    return pl.pallas_call(
        body, name=name,
        out_shape=[S_(a.shape, a.dtype) for a in arrs],
        in_specs=[anyspec] * n, out_specs=[anyspec] * n,
        scratch_shapes=[pltpu.SemaphoreType.DMA((n, 7)), pltpu.SemaphoreType.DMA((n, 7)), pltpu.SemaphoreType.DMA((n,))],
    )(*arrs)


HBM_SPEC = pl.BlockSpec(memory_space=pltpu.HBM)
SEM_SPEC = pl.BlockSpec(memory_space=pltpu.SEMAPHORE)
EFFECT = pltpu.SideEffectType.DATAFLOW_SIDE_EFFECTING


def _peers():
    x, y, c = _me()
    out = []
    for k in range(1, NDEV):
        px = 1 - x if (k >> 2) & 1 else x
        py = 1 - y if (k >> 1) & 1 else y
        pc = 1 - c if k & 1 else c
        out.append(((px, py, pc), 4 * px + 2 * py + pc))
    return out, 4 * x + 2 * y + c


def _xchg_copies(gather, xs, lands, send, recv):
    peers, me_idx = _peers()
    out = []
    for k, (dev, p_idx) in enumerate(peers):
        for a in range(len(xs)):
            out.append(pltpu.make_async_remote_copy(
                src_ref=xs[a] if gather else xs[a].at[p_idx], dst_ref=lands[a].at[me_idx],
                send_sem=send[a].at[k], recv_sem=recv[a].at[k], device_id=dev, device_id_type=MESH))
    return out


def _xchg_start(gather, xs, lands, name, after=()):
    n = len(xs)
    na = len(after)

    def body(*refs):
        xr, lr = refs[:n], refs[n:2 * n]
        outs = refs[2 * n + na:]
        for cp in _xchg_copies(gather, xr, lr, outs[:n], outs[n:2 * n]):
            cp.start()
        outs[4 * n][...] = jnp.zeros(outs[4 * n].shape, F32)

    ops = [pltpu.with_memory_space_constraint(a, pltpu.HBM) for a in list(xs) + list(lands)]
    outs = pl.pallas_call(
        body, name=name,
        out_shape=[pltpu.SemaphoreType.DMA((NDEV - 1,))] * (2 * n) + [pltpu.HBM(a.shape, a.dtype) for a in ops]
        + [S_((8, 128), F32)],
        in_specs=[HBM_SPEC] * (2 * n) + [pl.BlockSpec(memory_space=pl.ANY)] * na,
        out_specs=[SEM_SPEC] * (2 * n) + [HBM_SPEC] * (2 * n) + [pl.BlockSpec(memory_space=pltpu.VMEM)],
        input_output_aliases={i: 2 * n + i for i in range(2 * n)},
        compiler_params=pltpu.CompilerParams(has_side_effects=EFFECT),
    )(*ops, *after)
    return (gather, n, outs[:4 * n]), outs[4 * n]


def _xchg_wait(state, after, name):
    gather, n, st = state
    send, recv, xs, lands = st[:n], st[n:2 * n], st[2 * n:3 * n], st[3 * n:4 * n]
    after = list(after) if isinstance(after, (list, tuple)) else [after]

    def body(*refs):
        xr, lr = refs[:n], refs[n:2 * n]
        sr, rr = refs[2 * n:3 * n], refs[3 * n:4 * n]
        for cp in _xchg_copies(gather, xr, lr, sr, rr):
            cp.wait_send()
            cp.wait_recv()

    outs = pl.pallas_call(
        body, name=name,
        out_shape=[pltpu.HBM(a.shape, a.dtype) for a in list(xs) + list(lands)],
        in_specs=[HBM_SPEC] * (2 * n) + [SEM_SPEC] * (2 * n) + [pl.BlockSpec(memory_space=pl.ANY)] * len(after),
        out_specs=[HBM_SPEC] * (2 * n), input_output_aliases={i: i for i in range(2 * n)},
        compiler_params=pltpu.CompilerParams(has_side_effects=EFFECT),
    )(*xs, *lands, *send, *recv, *after)
    return outs[n:]


def _tie(a, token):
    return a + token[0, 0].astype(a.dtype)


def _cast_bf16(a, name):
    R, C = a.shape
    tr = _tile(R, 512, 16)

    def body(a_ref, o_ref):
        o_ref[...] = a_ref[...].astype(BF)

    return pl.pallas_call(body, name=name, grid=(R // tr,), in_specs=[pl.BlockSpec((tr, C), lambda i: (i, 0))],
                          out_specs=pl.BlockSpec((tr, C), lambda i: (i, 0)), out_shape=S_((R, C), BF), compiler_params=_cp())(a)


def _sum_n(a, name):
    n, R, C = a.shape
    tr = _tile(R, 256, 8)

    def body(a_ref, o_ref):
        acc = a_ref[0].astype(F32)
        for i in range(1, n):
            acc = acc + a_ref[i].astype(F32)
        o_ref[...] = acc

    return pl.pallas_call(body, name=name, grid=(R // tr,), in_specs=[pl.BlockSpec((n, tr, C), lambda i: (0, i, 0))],
                          out_specs=pl.BlockSpec((tr, C), lambda i: (i, 0)), out_shape=S_((R, C), F32), compiler_params=_cp())(a)


def _adam_math(w, g, m, v):
    m = ADAM_B1 * m + (1.0 - ADAM_B1) * g
    v = ADAM_B2 * v + (1.0 - ADAM_B2) * jnp.square(g)
    m_hat = m / (1.0 - ADAM_B1 ** ADAM_STEP)
    v_hat = v / (1.0 - ADAM_B2 ** ADAM_STEP)
    delta = -ADAM_LR * (m_hat / (jnp.sqrt(v_hat) + ADAM_EPS) + ADAM_WD * w)
    return delta, m, v


def _reduce_adam(pieces, w, m, v, name, own=None):
    n, R, C = pieces.shape
    tr = _tile(R, 256, 8)

    def body(*refs):
        p_ref, w_ref, m_ref, v_ref = refs[:4]
        g_ref, d_ref, mo_ref, vo_ref = refs[-4:]
        g = p_ref[0].astype(F32)
        for i in range(1, n):
            g = g + p_ref[i].astype(F32)
        if own is not None:
            g = g + refs[4][...].astype(F32)
        d, mn, vn = _adam_math(w_ref[...], g, m_ref[...], v_ref[...])
        g_ref[...] = g
        d_ref[...] = d
        mo_ref[...] = mn
        vo_ref[...] = vn

    row = pl.BlockSpec((tr, C), lambda i: (i, 0))
    ins = [pieces, w, m, v] + ([] if own is None else [own])
    return pl.pallas_call(body, name=name, grid=(R // tr,),
                          in_specs=[pl.BlockSpec((n, tr, C), lambda i: (0, i, 0))] + [row] * (len(ins) - 1),
                          out_specs=[row] * 4, out_shape=[S_((R, C), F32)] * 4, compiler_params=_cp())(*ins)


def _reduce_adam_stack(pieces, owns, w, m, v, name):
    F = len(pieces)
    n, R, C = pieces[0].shape
    tr = _tile(R, max(8, 131072 // C), 8)

    def body(*refs):
        w_ref, m_ref, v_ref = refs[2 * F:2 * F + 3]
        g_ref, d_ref, mo_ref, vo_ref = refs[-4:]
        f = pl.program_id(0)
        for ff in range(F):
            @pl.when(f == ff)
            def _(ff=ff):
                g = refs[ff][0].astype(F32)
                for i in range(1, n):
                    g = g + refs[ff][i].astype(F32)
                g = g + refs[F + ff][...].astype(F32)
                d, mn, vn = _adam_math(w_ref[...], g, m_ref[...], v_ref[...])
                g_ref[...] = g
                d_ref[...] = d
                mo_ref[...] = mn
                vo_ref[...] = vn

    pspecs = [pl.BlockSpec((n, tr, C), lambda f, r, ff=ff: (0, jnp.where(f == ff, r, 0), 0)) for ff in range(F)]
    ospecs = [pl.BlockSpec((tr, C), lambda f, r, ff=ff: (jnp.where(f == ff, r, 0), 0)) for ff in range(F)]
    st = pl.BlockSpec((None, tr, C), lambda f, r: (f, r, 0))
    return pl.pallas_call(body, name=name, grid=(F, R // tr), in_specs=pspecs + ospecs + [st] * 3, out_specs=[st] * 4,
                          out_shape=[S_((F, R, C), F32)] * 4, compiler_params=_cp())(*pieces, *owns, w, m, v)


def _mm_cols(a, wb, name, out_dtype=F32, dep=None):
    M, K = a.shape
    NB, _, nb = wb.shape
    tm = _tile(M, MM_ROWS, 16)

    def body(*refs):
        refs[-1][...] = _dot(refs[0][...], refs[1][...], NN).astype(out_dtype)

    deps = [] if dep is None else [dep]
    return pl.pallas_call(body, name=name, grid=(M // tm, NB),
                          in_specs=[pl.BlockSpec((tm, K), lambda i, j: (i, 0)), pl.BlockSpec((None, K, nb), lambda i, j: (j, 0, 0))]
                          + [_full(d.shape) for d in deps],
                          out_specs=pl.BlockSpec((tm, nb), lambda i, j: (i, j)), out_shape=S_((M, NB * nb), out_dtype),
                          compiler_params=_cp())(a, wb, *deps)


def _mm_nt_cols(g, wb, name):
    M = g.shape[0]
    NB, K, nb = wb.shape
    tm = _tile(M, MM_ROWS, 16)

    def body(g_ref, w_ref, o_ref):
        j = pl.program_id(1)
        part = _dot(g_ref[...], w_ref[...], NT)

        @pl.when(j == 0)
        def _():
            o_ref[...] = part

        @pl.when(j > 0)
        def _():
            o_ref[...] += part

    return pl.pallas_call(body, name=name, grid=(M // tm, NB),
                          in_specs=[pl.BlockSpec((tm, nb), lambda i, j: (i, j)), pl.BlockSpec((None, K, nb), lambda i, j: (j, 0, 0))],
                          out_specs=pl.BlockSpec((tm, K), lambda i, j: (i, 0)), out_shape=S_((M, K), F32),
                          compiler_params=_cp())(g, wb)


def _mm_full(a, w, dn, name, out_dtype=F32):
    M, K = a.shape
    N = w.shape[1] if dn == NN else w.shape[0]
    tm = _tile(M, 768, 16)

    def body(a_ref, w_ref, o_ref):
        o_ref[...] = _dot(a_ref[...], w_ref[...], dn).astype(out_dtype)

    return pl.pallas_call(body, name=name, grid=(M // tm,),
                          in_specs=[pl.BlockSpec((tm, K), lambda i: (i, 0)), _full(w.shape)],
                          out_specs=pl.BlockSpec((tm, N), lambda i: (i, 0)), out_shape=S_((M, N), out_dtype),
                          compiler_params=_cp())(a, w)


def _mm_tn_cols(at, g, NB, name):
    K, M = at.shape
    nb = g.shape[1] // NB
    tk = _tile(M, WGRAD_ROWS, 128)
    nk = M // tk

    def body(a_ref, g_ref, o_ref, acc):
        k = pl.program_id(1)
        part = _dot(a_ref[...], g_ref[...], NN)

        @pl.when(k == 0)
        def _():
            acc[...] = part

        @pl.when(k > 0)
        def _():
            acc[...] += part

        @pl.when(k == nk - 1)
        def _():
            o_ref[...] = acc[...].astype(BF)

    return pl.pallas_call(body, name=name, grid=(NB, nk),
                          in_specs=[pl.BlockSpec((K, tk), lambda b, k: (0, k)), pl.BlockSpec((tk, nb), lambda b, k: (k, b))],
                          out_specs=pl.BlockSpec((None, K, nb), lambda b, k: (b, 0, 0)), out_shape=S_((NB, K, nb), BF),
                          scratch_shapes=[pltpu.VMEM((K, nb), F32)], compiler_params=_cp())(at, g)


def _mm_tn_rows(a, g, NB, name):
    M = a.shape[0]
    kb = a.shape[1] // NB
    N = g.shape[1]
    tk = _tile(M, WGRAD_ROWS, 128)
    nk = M // tk

    def body(a_ref, g_ref, o_ref, acc):
        k = pl.program_id(1)
        part = _dot(a_ref[...], g_ref[...], TN)

        @pl.when(k == 0)
        def _():
            acc[...] = part

        @pl.when(k > 0)
        def _():
            acc[...] += part

        @pl.when(k == nk - 1)
        def _():
            o_ref[...] = acc[...].astype(BF)

    return pl.pallas_call(body, name=name, grid=(NB, nk),
                          in_specs=[pl.BlockSpec((tk, kb), lambda b, k: (k, b)), pl.BlockSpec((tk, N), lambda b, k: (k, 0))],
                          out_specs=pl.BlockSpec((None, kb, N), lambda b, k: (b, 0, 0)), out_shape=S_((NB, kb, N), BF),
                          scratch_shapes=[pltpu.VMEM((kb, N), F32)], compiler_params=_cp())(a, g)


def _ffn_fwd(h, wg, wu, wd, name, dep=None):
    T, D = h.shape
    NB, _, nb = wg.shape
    tm = _tile(T, FFN_FWD_ROWS, 16)

    def body(*refs):
        h_ref, wg_ref, wu_ref, wd_ref = refs[:4]
        y_ref, g_ref, u_ref = refs[-3:]
        b = pl.program_id(1)
        parts = []
        for r in range(FFN_FWD_SPLIT):
            rows = pl.ds(r * (tm // FFN_FWD_SPLIT), tm // FFN_FWD_SPLIT)
            hh = h_ref[rows, :]
            g = _dot(hh, wg_ref[...], NN).astype(BF)
            u = _dot(hh, wu_ref[...], NN).astype(BF)
            g_ref[rows, :] = g
            u_ref[rows, :] = u
            gf = g.astype(F32)
            parts.append((rows, _dot(gf * jax.nn.sigmoid(gf) * u.astype(F32), wd_ref[...], NN)))

        @pl.when(b == 0)
        def _():
            for rows, part in parts:
                y_ref[rows, :] = part

        @pl.when(b > 0)
        def _():
            for rows, part in parts:
                y_ref[rows, :] += part

    deps = [] if dep is None else [dep]
    wcol = pl.BlockSpec((None, D, nb), lambda i, b: (b, 0, 0))
    act = pl.BlockSpec((None, tm, nb), lambda i, b: (b, i, 0))
    return pl.pallas_call(
        body, name=name, grid=(T // tm, NB),
        in_specs=[pl.BlockSpec((tm, D), lambda i, b: (i, 0)), wcol, wcol, pl.BlockSpec((None, nb, D), lambda i, b: (b, 0, 0))]
        + [_full(d.shape) for d in deps],
        out_specs=[pl.BlockSpec((tm, D), lambda i, b: (i, 0)), act, act],
        out_shape=[S_((T, D), F32), S_((NB, T, nb), BF), S_((NB, T, nb), BF)], compiler_params=_cp())(h, wg, wu, wd, *deps)


def _ffn_bwd_act(dy, G, U, wg, wu, wd, name):
    T, D = dy.shape
    NB, _, nb = wg.shape
    tm = _tile(T, FFN_BWD_ROWS, 16)

    def body(dy_ref, g_ref, u_ref, wg_ref, wu_ref, wd_ref, dh_ref, dg_ref, du_ref, a_ref):
        b = pl.program_id(1)
        parts = []
        for r in range(FFN_BWD_SPLIT):
            rows = pl.ds(r * (tm // FFN_BWD_SPLIT), tm // FFN_BWD_SPLIT)
            g, u = g_ref[rows, :].astype(F32), u_ref[rows, :].astype(F32)
            da = _dot(dy_ref[rows, :], wd_ref[...], NT)
            s = jax.nn.sigmoid(g)
            silu = g * s
            du = da * silu
            dg = da * u * (s * (1.0 + g * (1.0 - s)))
            dg_ref[rows, :] = dg.astype(BF)
            du_ref[rows, :] = du.astype(BF)
            a_ref[rows, :] = (silu * u).astype(BF)
            parts.append((rows, _dot(dg, wg_ref[...], NT) + _dot(du, wu_ref[...], NT)))

        @pl.when(b == 0)
        def _():
            for rows, part in parts:
                dh_ref[rows, :] = part

        @pl.when(b > 0)
        def _():
            for rows, part in parts:
                dh_ref[rows, :] += part

    wcol = pl.BlockSpec((None, D, nb), lambda i, b: (b, 0, 0))
    wrow = pl.BlockSpec((None, nb, D), lambda i, b: (b, 0, 0))
    act = pl.BlockSpec((None, tm, nb), lambda i, b: (b, i, 0))
    row = pl.BlockSpec((tm, D), lambda i, b: (i, 0))
    return pl.pallas_call(
        body, name=name, grid=(T // tm, NB),
        in_specs=[row, act, act, wcol, wcol, wrow],
        out_specs=[row, act, act, act],
        out_shape=[S_((T, D), F32)] + [S_((NB, T, nb), BF)] * 3, compiler_params=_cp())(dy, G, U, wg, wu, wd)


def _ffn_wgrad_in(ht, dact, name, dep=None):
    D, T = ht.shape
    NB, _, nb = dact.shape
    tk = _tile(T, WGRAD_ROWS, 128)
    nk = T // tk
    deps = [] if dep is None else [dep]

    def body(*refs):
        h_ref, d_ref = refs[:2]
        o_ref, acc = refs[-2:]
        k = pl.program_id(1)
        part = _dot(h_ref[...], d_ref[...], NN)

        @pl.when(k == 0)
        def _():
            acc[...] = part

        @pl.when(k > 0)
        def _():
            acc[...] += part

        @pl.when(k == nk - 1)
        def _():
            o_ref[...] = acc[...].astype(BF)

    return pl.pallas_call(body, name=name, grid=(NB, nk),
                          in_specs=[pl.BlockSpec((D, tk), lambda b, k: (0, k)), pl.BlockSpec((None, tk, nb), lambda b, k: (b, k, 0))]
                          + [_full(d.shape) for d in deps],
                          out_specs=pl.BlockSpec((None, D, nb), lambda b, k: (b, 0, 0)), out_shape=S_((NB, D, nb), BF),
                          scratch_shapes=[pltpu.VMEM((D, nb), F32)], compiler_params=_cp())(ht, dact, *deps)


def _ffn_wgrad_out(act, dy, name, dep=None):
    NB, T, nb = act.shape
    D = dy.shape[1]
    tk = _tile(T, WGRAD_ROWS, 128)
    nk = T // tk
    deps = [] if dep is None else [dep]

    def body(*refs):
        a_ref, d_ref = refs[:2]
        o_ref, acc = refs[-2:]
        k = pl.program_id(1)
        part = _dot(a_ref[...], d_ref[...], TN)

        @pl.when(k == 0)
        def _():
            acc[...] = part

        @pl.when(k > 0)
        def _():
            acc[...] += part

        @pl.when(k == nk - 1)
        def _():
            o_ref[...] = acc[...].astype(BF)

    return pl.pallas_call(body, name=name, grid=(NB, nk),
                          in_specs=[pl.BlockSpec((None, tk, nb), lambda b, k: (b, k, 0)), pl.BlockSpec((tk, D), lambda b, k: (k, 0))]
                          + [_full(d.shape) for d in deps],
                          out_specs=pl.BlockSpec((None, nb, D), lambda b, k: (b, 0, 0)), out_shape=S_((NB, nb, D), BF),
                          scratch_shapes=[pltpu.VMEM((nb, D), F32)], compiler_params=_cp())(act, dy, *deps)


class _Geo:
    def __init__(self, S, LC, D):
        self.S, self.LC, self.D, self.T = S, LC, D, S + LC
        self.RT = _tile(LC, ROW_TILE, 8)
        assert S % self.RT == 0 and self.RT >= 2 * HALO
        self.nlat, self.nctx = S // self.RT, LC // self.RT
        self.nt = self.nlat + self.nctx

    def row(self, C, cb=0):
        return pl.BlockSpec((self.RT, C), lambda i: (i, cb))

    def prev(self, C, cb=0):
        return pl.BlockSpec((self.RT, C), lambda i: (jnp.maximum(i - 1, 0), cb))

    def next(self, C, cb=0):
        nt = self.nt
        return pl.BlockSpec((self.RT, C), lambda i: (jnp.minimum(i + 1, nt - 1), cb))

    def seg(self, r, C):
        nlat = self.nlat
        return pl.BlockSpec((None, r, C), lambda i: (jnp.minimum(i // nlat, 1), 0, 0))

    def first_of_seg(self, i):
        return jnp.logical_or(i == 0, i == self.nlat)

    def prev_ok(self, i):
        return jnp.logical_and(i != 0, i != self.nlat)

    def next_ok(self, i):
        return jnp.logical_and(i != self.nlat - 1, i != self.nt - 1)

    def pos(self, i):
        r = lax.broadcasted_iota(jnp.int32, (self.RT, 1), 0)
        is_ctx = i >= self.nlat
        base = jnp.where(is_ctx, (i - self.nlat) * self.RT, i * self.RT)
        return base + r, jnp.where(is_ctx, self.LC, self.S)


def _rms(x):
    return x * lax.rsqrt(jnp.mean(x * x, axis=-1, keepdims=True) + EPS)


def _modulate(x, g, shift, scale):
    return (_rms(x) * g) * (1 + scale) + shift


def _post(x, y, g, gate, w):
    return x + w * gate * (_rms(y) * g)


def _norm_fwd(geo, x, post, pre, name):
    D = geo.D
    ins, specs = [x], [geo.row(D)]
    if post is not None:
        ins += [post[0], post[1], post[2]]
        specs += [geo.row(D), _full((1, D)), geo.seg(9, D)]
    if pre is not None:
        ins += [pre[0], pre[1]]
        specs += [_full((1, D)), geo.seg(9, D)]

    def body(*refs):
        it = iter(refs)
        xv = next(it)[...]
        if post is not None:
            y_ref, gp_ref, tab_ref = next(it), next(it), next(it)
        if pre is not None:
            gq_ref, tabn_ref = next(it), next(it)
        if post is not None:
            r = 3 * post[3] + 2
            xv = _post(xv, y_ref[...], gp_ref[...], tab_ref[r:r + 1, :], post[4])
            next(it)[...] = xv
        if pre is not None:
            r = 3 * pre[2]
            h = _modulate(xv, gq_ref[...], tabn_ref[r:r + 1, :], tabn_ref[r + 1:r + 2, :]).astype(BF)
            next(it)[...] = h
            next(it)[...] = h.T

    outs, ospecs = [], []
    if post is not None:
        outs.append(S_((geo.T, D), F32))
        ospecs.append(geo.row(D))
    if pre is not None:
        outs += [S_((geo.T, D), BF), S_((D, geo.T), BF)]
        ospecs += [geo.row(D), pl.BlockSpec((D, geo.RT), lambda i: (0, i))]
    return pl.pallas_call(body, name=name, grid=(geo.nt,), in_specs=specs, out_specs=ospecs, out_shape=outs,
                          compiler_params=_cp())(*ins)


def _acc_rows(ref, first, rows):
    for k, v in enumerate(rows):
        @pl.when(first)
        def _(k=k, v=v):
            ref[k:k + 1, :] = v

        @pl.when(jnp.logical_not(first))
        def _(k=k, v=v):
            ref[k:k + 1, :] += v


def _norm_bwd(geo, dxo, dh, x, pre, post, name):
    D = geo.D
    ins = [dxo, dh, x, pre[0], pre[1]]
    specs = [geo.row(D)] * 3 + [_full((1, D)), geo.seg(9, D)]
    if post is not None:
        ins += [post[0], post[1], post[2]]
        specs += [geo.row(D), _full((1, D)), geo.seg(9, D)]

    def body(*refs):
        i = pl.program_id(0)
        first = geo.first_of_seg(i)
        dxo_ref, dh_ref, x_ref, gq_ref, tab_ref = refs[:5]
        k = 5
        if post is not None:
            y_ref, gp_ref, tabp_ref = refs[5:8]
            k = 8
        outs = refs[k:]
        r = 3 * pre[2]
        _, vjp = jax.vjp(_modulate, x_ref[...], gq_ref[...], tab_ref[r:r + 1, :], tab_ref[r + 1:r + 2, :])
        dx, dg, dsh, dsc = vjp(dh_ref[...].astype(F32))
        dx = dx + dxo_ref[...]
        outs[0][...] = dx
        if post is None:
            _acc_rows(outs[1], first, [dg, dsh, dsc])
            return
        _acc_rows(outs[2], first, [dg, dsh, dsc])
        rp = 3 * post[3] + 2
        w = post[4]
        _, vjp2 = jax.vjp(lambda yy, gg, ga: w * ga * (_rms(yy) * gg), y_ref[...], gp_ref[...], tabp_ref[rp:rp + 1, :])
        dy, dgp, dga = vjp2(dx)
        outs[1][...] = dy.astype(BF)
        _acc_rows(outs[3], first, [dgp, dga])

    if post is None:
        outs, ospecs = [S_((geo.T, D), F32), S_((2, 3, D), F32)], [geo.row(D), geo.seg(3, D)]
    else:
        outs = [S_((geo.T, D), F32), S_((geo.T, D), BF), S_((2, 3, D), F32), S_((2, 2, D), F32)]
        ospecs = [geo.row(D), geo.row(D), geo.seg(3, D), geo.seg(2, D)]
    return pl.pallas_call(body, name=name, grid=(geo.nt,), in_specs=specs, out_specs=ospecs, out_shape=outs,
                          compiler_params=_cp())(*ins)


def _loss_bwd(geo, xf, tgt, post, name):
    D = geo.D
    nlat = geo.nlat

    def body(x_ref, t_ref, y_ref, gp_ref, tabp_ref, loss_ref, dx_ref, dy_ref, dpost_ref):
        i = pl.program_id(0)
        first = geo.first_of_seg(i)
        lat = i < nlat
        diff = x_ref[...] - t_ref[...]
        part = jnp.where(lat, 0.5 * jnp.sum(jnp.mean(diff * diff, axis=-1, keepdims=True), axis=0, keepdims=True), 0.0)

        @pl.when(i == 0)
        def _():
            loss_ref[...] = part

        @pl.when(i > 0)
        def _():
            loss_ref[...] += part

        dx = jnp.where(lat, diff * (1.0 / D), 0.0)
        dx_ref[...] = dx
        rp = 3 * post[3] + 2
        w = post[4]
        _, vjp2 = jax.vjp(lambda yy, gg, ga: w * ga * (_rms(yy) * gg), y_ref[...], gp_ref[...], tabp_ref[rp:rp + 1, :])
        dy, dgp, dga = vjp2(dx)
        dy_ref[...] = dy.astype(BF)
        _acc_rows(dpost_ref, first, [dgp, dga])

    tspec = pl.BlockSpec((geo.RT, D), lambda i: (jnp.minimum(i, nlat - 1), 0))
    return pl.pallas_call(
        body, name=name, grid=(geo.nt,),
        in_specs=[geo.row(D), tspec, geo.row(D), _full((1, D)), geo.seg(9, D)],
        out_specs=[_full((1, 1)), geo.row(D), geo.row(D), geo.seg(2, D)],
        out_shape=[S_((1, 1), F32), S_((geo.T, D), F32), S_((geo.T, D), BF), S_((2, 2, D), F32)],
        compiler_params=_cp())(xf, tgt, post[0], post[1], post[2])


def _fill_ext(geo, ext, prev_ref, cur_ref, next_ref, i):
    RT = geo.RT
    ext[0:HALO, :] = jnp.where(geo.prev_ok(i), prev_ref[RT - HALO:RT, :], 0.0).astype(ext.dtype)
    ext[HALO:HALO + RT, :] = cur_ref[...].astype(ext.dtype)
    ext[HALO + RT:2 * HALO + RT, :] = jnp.where(geo.next_ok(i), next_ref[0:HALO, :], 0.0).astype(ext.dtype)


CONV_W = 4
CONV_LEFT = 2


def _lru_gates(u, za, zx, lam):
    r = jax.nn.sigmoid(za)
    i = jax.nn.sigmoid(zx)
    log_a = -LRU_C * r * jax.nn.softplus(-lam)
    a = jnp.exp(log_a)
    return a, jnp.sqrt(1.0 - jnp.exp(2.0 * log_a)) * (i * u)


def _lru_coef_fwd(geo, P, W, cw, cb, wa, ba, wx, bx, lam, name):
    RT = geo.RT
    nblk, LB = wa.shape[1], wa.shape[2]

    def body(pp, pc, pn, cw_ref, cb_ref, wa_ref, ba_ref, wx_ref, bx_ref, lam_ref, u_ref, a0, b0, a1, b1, ext):
        i = pl.program_id(0)
        _fill_ext(geo, ext, pp, pc, pn, i)
        u = cb_ref[...] + ext[pl.ds(HALO - CONV_LEFT, RT), :] * cw_ref[0:1, :]
        for k in range(1, CONV_W):
            u = u + ext[pl.ds(HALO - CONV_LEFT + k, RT), :] * cw_ref[k:k + 1, :]
        u_ref[...] = u
        for d, (a_ref, b_ref) in enumerate(((a0, b0), (a1, b1))):
            for n in range(nblk):
                sl = slice(n * LB, (n + 1) * LB)
                un = u[:, sl]
                za = _dot(un, wa_ref[d, n], NN) + ba_ref[d:d + 1, sl]
                zx = _dot(un, wx_ref[d, n], NN) + bx_ref[d:d + 1, sl]
                a, b = _lru_gates(un, za, zx, lam_ref[d:d + 1, sl])
                a_ref[:, sl] = a
                b_ref[:, sl] = b

    return pl.pallas_call(
        body, name=name, grid=(geo.nt,),
        in_specs=[geo.prev(W, 1), geo.row(W, 1), geo.next(W, 1), _full(cw.shape), _full(cb.shape), _full(wa.shape),
                  _full(ba.shape), _full(wx.shape), _full(bx.shape), _full(lam.shape)],
        out_specs=[geo.row(W)] * 5, out_shape=[S_((geo.T, W), F32)] * 5,
        scratch_shapes=[pltpu.VMEM((RT + 2 * HALO, W), F32)], compiler_params=_cp())(P, P, P, cw, cb, wa, ba, wx, bx, lam)


def _lru_coef_bwd(geo, u, da, db, W, wa, ba, wx, bx, lam, name):
    nblk, LB = wa.shape[1], wa.shape[2]

    def body(u_ref, da0, db0, da1, db1, wa_ref, ba_ref, wx_ref, bx_ref, lam_ref, du_ref, dwa, dba, dwx, dbx, dlam):
        i = pl.program_id(0)

        @pl.when(i == 0)
        def _():
            for r in (dwa, dba, dwx, dbx, dlam):
                r[...] = jnp.zeros(r.shape, F32)

        u = u_ref[...]
        for n in range(nblk):
            sl = slice(n * LB, (n + 1) * LB)
            un = u[:, sl]
            dun = jnp.zeros_like(un)
            for d, (da_ref, db_ref) in enumerate(((da0, db0), (da1, db1))):
                za = _dot(un, wa_ref[d, n], NN) + ba_ref[d:d + 1, sl]
                zx = _dot(un, wx_ref[d, n], NN) + bx_ref[d:d + 1, sl]
                _, vjp = jax.vjp(_lru_gates, un, za, zx, lam_ref[d:d + 1, sl])
                du_e, dza, dzx, dl = vjp((da_ref[:, sl], db_ref[:, sl]))
                dun = dun + du_e + _dot(dza, wa_ref[d, n], NT) + _dot(dzx, wx_ref[d, n], NT)
                dwa[d, n] += _dot(un, dza, TN)
                dwx[d, n] += _dot(un, dzx, TN)
                dba[d:d + 1, sl] += jnp.sum(dza, axis=0, keepdims=True)
                dbx[d:d + 1, sl] += jnp.sum(dzx, axis=0, keepdims=True)
                dlam[d:d + 1, sl] += dl
            du_ref[:, sl] = dun

    row = geo.row(W)
    return pl.pallas_call(
        body, name=name, grid=(geo.nt,),
        in_specs=[row] * 5 + [_full(wa.shape), _full(ba.shape), _full(wx.shape), _full(bx.shape), _full(lam.shape)],
        out_specs=[row, _full(wa.shape), _full(ba.shape), _full(wx.shape), _full(bx.shape), _full(lam.shape)],
        out_shape=[S_((geo.T, W), F32), S_(wa.shape, F32), S_(ba.shape, F32), S_(wx.shape, F32), S_(bx.shape, F32), S_(lam.shape, F32)],
        compiler_params=_cp())(u, da[0], db[0], da[1], db[1], wa, ba, wx, bx, lam)


def _scan_order(d, k, nlat, nctx):
    if d == 0:
        return jnp.where(k < nctx, nlat + k, k - nctx)
    return jnp.where(k < nctx, nlat + nctx - 1 - k, nlat - 1 - (k - nctx))


def _chunk_scan(a, b, reverse):
    n = a.shape[0]
    row = lax.broadcasted_iota(jnp.int32, a.shape, 0)
    s = 1
    while s < n:
        if reverse:
            ok = row < n - s
            a_s, b_s = pltpu.roll(a, n - s, 0), pltpu.roll(b, n - s, 0)
        else:
            ok = row >= s
            a_s, b_s = pltpu.roll(a, s, 0), pltpu.roll(b, s, 0)
        b = a * jnp.where(ok, b_s, 0.0) + b
        a = a * jnp.where(ok, a_s, 1.0)
        s *= 2
    return a, b


def _shift1(x, reverse, fill):
    n = x.shape[0]
    row = lax.broadcasted_iota(jnp.int32, x.shape, 0)
    if reverse:
        return jnp.where(row == n - 1, fill, pltpu.roll(x, n - 1, 0))
    return jnp.where(row == 0, fill, pltpu.roll(x, 1, 0))


def _lru_scan_fwd(geo, a, b, d, name):
    W = a.shape[1]
    tw = _tile(W, 256, 128)
    RT, nlat, nctx = geo.RT, geo.nlat, geo.nctx
    rev = d == 1

    def body(a_ref, b_ref, h_ref, hp_ref, carry):
        k = pl.program_id(1)

        @pl.when(k == 0)
        def _():
            carry[...] = jnp.zeros(carry.shape, F32)

        ac, bc = _chunk_scan(a_ref[...], b_ref[...], rev)
        h = bc + ac * carry[...]
        h_ref[...] = h
        hp_ref[...] = _shift1(h, rev, carry[...])
        carry[...] = h[0:1, :] if rev else h[RT - 1:RT, :]

    spec = pl.BlockSpec((RT, tw), lambda j, k: (_scan_order(d, k, nlat, nctx), j))
    return pl.pallas_call(body, name=name, grid=(W // tw, geo.nt), in_specs=[spec, spec], out_specs=[spec, spec],
                          out_shape=[S_((geo.T, W), F32)] * 2, scratch_shapes=[pltpu.VMEM((1, tw), F32)],
                          compiler_params=_cp())(a, b)


def _lru_scan_bwd(geo, dh, a, hprev, d, name):
    W = a.shape[1]
    tw = _tile(W, 256, 128)
    RT, nlat, nctx, nt = geo.RT, geo.nlat, geo.nctx, geo.nt
    rev = d == 1

    def body(dh_ref, a_ref, hp_ref, da_ref, db_ref, carry):
        k = pl.program_id(1)

        @pl.when(k == 0)
        def _():
            carry[...] = jnp.zeros(carry.shape, F32)

        av = a_ref[...]
        a_next = _shift1(av, not rev, jnp.ones((1, tw), F32))
        ac, bc = _chunk_scan(a_next, dh_ref[...], not rev)
        lam = bc + ac * carry[...]
        db_ref[...] = lam
        da_ref[...] = lam * hp_ref[...]
        first = (av * lam)[RT - 1:RT, :] if rev else (av * lam)[0:1, :]
        carry[...] = first

    spec = pl.BlockSpec((RT, tw), lambda j, k: (_scan_order(d, nt - 1 - k, nlat, nctx), j))
    return pl.pallas_call(body, name=name, grid=(W // tw, nt), in_specs=[spec] * 3, out_specs=[spec] * 2,
                          out_shape=[S_((geo.T, W), F32)] * 2, scratch_shapes=[pltpu.VMEM((1, tw), F32)],
                          compiler_params=_cp())(dh, a, hprev)


def _conv_bwd_assemble(geo, P, du, pieces, W, cw, name):
    RT = geo.RT

    def body(pp, pc, pn, dup, duc, dun, dg, dq, dk, dv, dq1, dk1, dv1, dog, cw_ref, dP_ref, dcw, dcb, ext_r, ext_d):
        i = pl.program_id(0)
        _fill_ext(geo, ext_r, pp, pc, pn, i)
        _fill_ext(geo, ext_d, dup, duc, dun, i)
        du_c = duc[...]
        rows = []
        dr = None
        for k in range(CONV_W):
            rows.append(jnp.sum(du_c * ext_r[pl.ds(HALO - CONV_LEFT + k, RT), :], axis=0, keepdims=True))
            t = ext_d[pl.ds(HALO + CONV_LEFT - k, RT), :] * cw_ref[k:k + 1, :]
            dr = t if dr is None else dr + t
        first = i == 0
        _acc_rows(dcw, first, rows)
        _acc_rows(dcb, first, [jnp.sum(du_c, axis=0, keepdims=True)])
        for j, v in enumerate((dg[...], dr, dq[...] + dq1[...], dk[...] + dk1[...], dv[...] + dv1[...], dog[...])):
            dP_ref[:, j * W:(j + 1) * W] = v

    row = geo.row(W)
    return pl.pallas_call(
        body, name=name, grid=(geo.nt,),
        in_specs=[geo.prev(W, 1), geo.row(W, 1), geo.next(W, 1), geo.prev(W), row, geo.next(W)] + [row] * 8 + [_full(cw.shape)],
        out_specs=[geo.row(6 * W), _full((CONV_W, W)), _full((1, W))],
        out_shape=[S_((geo.T, 6 * W), F32), S_((CONV_W, W), F32), S_((1, W), F32)],
        scratch_shapes=[pltpu.VMEM((RT + 2 * HALO, W), F32)] * 2, compiler_params=_cp())(P, P, P, du, du, du, *pieces, cw)


def _rot_half(x, cos, sin):
    hw = x.shape[1] // 2
    x1, x2 = x[:, :hw], x[:, hw:]
    return jnp.concatenate([x1 * cos - x2 * sin, x1 * sin + x2 * cos], axis=-1)


def _ret_chunk(q, k, v, s, logit, cos, sin, rev):
    C, dk = q.shape
    lg = -jax.nn.softplus(-logit)
    q = _rot_half(q, cos, sin)
    k = _rot_half(k, cos, sin) * (dk ** -0.5)
    i = lax.broadcasted_iota(jnp.int32, (C, 1), 0).astype(F32)
    j = lax.broadcasted_iota(jnp.int32, (1, C), 1).astype(F32)
    if rev:
        diff, qe, ke = j - i, C - i, i
    else:
        diff, qe, ke = i - j, i + 1.0, C - 1.0 - i
    intra = jnp.where(diff >= 0, jnp.exp(lg * jnp.maximum(diff, 0.0)), 0.0)
    scores = _bdot(q, k, "nt") * intra
    o = _bdot(scores, v, "nn") + _bdot(q * jnp.exp(lg * qe), s, "nn")
    s_new = s * jnp.exp(lg * C) + _bdot(k * jnp.exp(lg * ke), v, "tn")
    return o, s_new


def _ret_specs(P, W, H, d, nlc, ncc, order_of_step):
    C = RET_CHUNK
    dk = W // H

    def cidx(k):
        return _scan_order(d, order_of_step(k), nlc, ncc)

    per_w = W // dk
    q = pl.BlockSpec((C, dk), lambda h, k: (cidx(k), 2 * per_w + h))
    kk = pl.BlockSpec((C, dk), lambda h, k: (cidx(k), 3 * per_w + h))
    v = pl.BlockSpec((C, dk), lambda h, k: (cidx(k), 4 * per_w + h))
    tab = pl.BlockSpec((C, dk // 2), lambda h, k: (cidx(k), 0))
    logit = pl.BlockSpec((None, 1, 1), lambda h, k: (h, 0, 0))
    o = pl.BlockSpec((C, dk), lambda h, k: (cidx(k), h))
    return q, kk, v, tab, logit, o


def _ret_fwd(geo, P, W, H, logits, cos, sin, name):
    C = RET_CHUNK
    dk = W // H
    nlc, ncc = geo.S // C, geo.LC // C
    nch = nlc + ncc

    def body(*refs):
        ins, outs, scrs = refs[:12], refs[12:16], refs[16:]
        k = pl.program_id(1)
        for d in range(2):
            q_ref, k_ref, v_ref, cos_ref, sin_ref, lg_ref = ins[6 * d:6 * d + 6]
            o_ref, st_ref = outs[2 * d:2 * d + 2]
            s_scr = scrs[d]

            @pl.when(k == 0)
            def _(s_scr=s_scr):
                s_scr[...] = jnp.zeros(s_scr.shape, F32)

            st_ref[...] = s_scr[...]
            o, s_new = _ret_chunk(q_ref[...], k_ref[...], v_ref[...], s_scr[...], lg_ref[...], cos_ref[...], sin_ref[...], d == 1)
            o_ref[...] = o
            s_scr[...] = s_new

    in_specs, out_specs = [], []
    st = pl.BlockSpec((None, None, dk, dk), lambda h, k: (h, k, 0, 0))
    for d in range(2):
        q, kk, v, tab, logit, o = _ret_specs(P, W, H, d, nlc, ncc, lambda k: k)
        in_specs += [q, kk, v, tab, tab, logit]
        out_specs += [o, st]
    return pl.pallas_call(body, name=name, grid=(H, nch), in_specs=in_specs, out_specs=out_specs,
                          out_shape=[S_((geo.T, W), F32), S_((H, nch, dk, dk), F32)] * 2,
                          scratch_shapes=[pltpu.VMEM((dk, dk), F32)] * 2, compiler_params=_cp())(
                              P, P, P, cos, sin, logits[0], P, P, P, cos, sin, logits[1])


def _ret_bwd(geo, P, W, H, logits, cos, sin, states, do, name):
    C = RET_CHUNK
    dk = W // H
    nlc, ncc = geo.S // C, geo.LC // C
    nch = nlc + ncc

    def body(*refs):
        ins, outs, scrs = refs[:16], refs[16:24], refs[24:]
        k = pl.program_id(1)
        for d in range(2):
            q_ref, k_ref, v_ref, cos_ref, sin_ref, lg_ref, st_ref, do_ref = ins[8 * d:8 * d + 8]
            dq_ref, dk_ref, dv_ref, dlg_ref = outs[4 * d:4 * d + 4]
            ds_scr = scrs[d]

            @pl.when(k == 0)
            def _(ds_scr=ds_scr, dlg_ref=dlg_ref):
                ds_scr[...] = jnp.zeros(ds_scr.shape, F32)
                dlg_ref[...] = jnp.zeros(dlg_ref.shape, F32)

            cos, sin = cos_ref[...], sin_ref[...]
            _, vjp = jax.vjp(lambda a, b, c, s, lg, cos=cos, sin=sin, d=d: _ret_chunk(a, b, c, s, lg, cos, sin, d == 1),
                             q_ref[...], k_ref[...], v_ref[...], st_ref[...], lg_ref[...])
            dq, dkk, dv, ds, dlg = vjp((do_ref[...], ds_scr[...]))
            dq_ref[...] = dq
            dk_ref[...] = dkk
            dv_ref[...] = dv
            ds_scr[...] = ds
            dlg_ref[...] += dlg

    in_specs, out_specs, ops = [], [], []
    st = pl.BlockSpec((None, None, dk, dk), lambda h, k: (h, nch - 1 - k, 0, 0))
    for d in range(2):
        q, kk, v, tab, logit, o = _ret_specs(P, W, H, d, nlc, ncc, lambda k: nch - 1 - k)
        in_specs += [q, kk, v, tab, tab, logit, st, o]
        out_specs += [o, o, o, logit]
        ops += [P, P, P, cos, sin, logits[d], states[d], do]
    return pl.pallas_call(body, name=name, grid=(H, nch), in_specs=in_specs, out_specs=out_specs,
                          out_shape=([S_((geo.T, W), F32)] * 3 + [S_((H, 1, 1), F32)]) * 2,
                          scratch_shapes=[pltpu.VMEM((dk, dk), F32)] * 2, compiler_params=_cp())(*ops)


def _mix_even_math(g, h0, h1, o0, o1, og, gn, H):
    lru = jax.nn.gelu(g) * (h0 + h1)
    o = o0 + o1
    dv = o.shape[1] // H
    parts = []
    for h in range(H):
        oh = o[:, h * dv:(h + 1) * dv]
        mu = jnp.mean(oh, axis=-1, keepdims=True)
        var = jnp.mean(jnp.square(oh - mu), axis=-1, keepdims=True)
        parts.append((oh - mu) * lax.rsqrt(var + EPS))
    y = jnp.concatenate(parts, axis=-1) * gn
    return lru, y * jax.nn.silu(og)


def _mix_even_fwd(geo, P, W, H, h0, h1, o0, o1, gn, name):
    def body(g_ref, h0r, h1r, o0r, o1r, og_ref, gn_ref, m_ref):
        lru, ret = _mix_even_math(g_ref[...], h0r[...], h1r[...], o0r[...], o1r[...], og_ref[...], gn_ref[...], H)
        m_ref[:, :W] = lru.astype(BF)
        m_ref[:, W:] = ret.astype(BF)

    row = geo.row(W)
    return pl.pallas_call(body, name=name, grid=(geo.nt,), in_specs=[geo.row(W, 0), row, row, row, row, geo.row(W, 5), _full((1, W))],
                          out_specs=geo.row(2 * W), out_shape=S_((geo.T, 2 * W), BF), compiler_params=_cp())(P, h0, h1, o0, o1, P, gn)


def _mix_even_bwd(geo, P, W, H, h0, h1, o0, o1, gn, dmix, name):
    def body(g_ref, h0r, h1r, o0r, o1r, og_ref, gn_ref, dl_ref, dr_ref, dg_ref, dh_ref, do_ref, dog_ref, dgn_ref):
        i = pl.program_id(0)
        _, vjp = jax.vjp(lambda g, hs, os_, og, gn_: _mix_even_math(g, hs, 0.0, os_, 0.0, og, gn_, H),
                         g_ref[...], h0r[...] + h1r[...], o0r[...] + o1r[...], og_ref[...], gn_ref[...])
        dg, dh, do, dog, dgn = vjp((dl_ref[...], dr_ref[...]))
        dg_ref[...] = dg
        dh_ref[...] = dh
        do_ref[...] = do
        dog_ref[...] = dog
        _acc_rows(dgn_ref, i == 0, [dgn])

    row = geo.row(W)
    return pl.pallas_call(
        body, name=name, grid=(geo.nt,),
        in_specs=[geo.row(W, 0), row, row, row, row, geo.row(W, 5), _full((1, W)), geo.row(W, 0), geo.row(W, 1)],
        out_specs=[row] * 4 + [_full((1, W))], out_shape=[S_((geo.T, W), F32)] * 4 + [S_((1, W), F32)],
        compiler_params=_cp())(P, h0, h1, o0, o1, P, gn, dmix, dmix)


def _head_norm_rot(x, g, cosf, sinf):
    xn = _rms(x) * g
    return xn * cosf + _rollv(xn, x.shape[1] // 2, 1) * sinf


def _qk_prep_fwd(geo, P, dims, qg, kg, cosf, sinf, name):
    PW, QW, KW, hd = dims
    qscale = hd ** -0.5 * LOG2E

    def body(p_ref, qg_ref, kg_ref, c_ref, s_ref, q_ref, k_ref, v_ref):
        c, s = c_ref[...], s_ref[...]
        for h in range(QW // hd):
            qh = _head_norm_rot(p_ref[:, PW + h * hd:PW + (h + 1) * hd], qg_ref[...], c, s)
            q_ref[:, h * hd:(h + 1) * hd] = (qh * qscale).astype(BF)
        for h in range(KW // hd):
            o = PW + QW + h * hd
            k_ref[:, h * hd:(h + 1) * hd] = _head_norm_rot(p_ref[:, o:o + hd], kg_ref[...], c, s).astype(BF)
            v_ref[:, 2 * h * hd:(2 * h + 1) * hd] = p_ref[:, o + KW:o + KW + hd].astype(BF)
            v_ref[:, (2 * h + 1) * hd:(2 * h + 2) * hd] = jnp.ones((geo.RT, hd), BF)

    tot = PW + QW + 2 * KW
    return pl.pallas_call(
        body, name=name, grid=(geo.nt,),
        in_specs=[geo.row(tot), _full((1, hd)), _full((1, hd)), geo.row(hd), geo.row(hd)],
        out_specs=[geo.row(QW), geo.row(KW), geo.row(2 * KW)],
        out_shape=[S_((geo.T, QW), BF), S_((geo.T, KW), BF), S_((geo.T, 2 * KW), BF)], compiler_params=_cp())(P, qg, kg, cosf, sinf)


def _qk_prep_bwd(geo, P, dims, qg, kg, cosf, sinf, dpool, dq, dk, dv, name):
    PW, QW, KW, hd = dims

    def body(p_ref, qg_ref, kg_ref, c_ref, s_ref, dpool_ref, dq_ref, dk_ref, dv_ref, dP_ref, dqg_ref, dkg_ref):
        i = pl.program_id(0)
        c, s = c_ref[...], s_ref[...]
        dP_ref[:, :PW] = dpool_ref[...]
        f = lambda x, g: _head_norm_rot(x, g, c, s)
        dqg = jnp.zeros((1, hd), F32)
        for h in range(QW // hd):
            o = PW + h * hd
            _, vjp = jax.vjp(f, p_ref[:, o:o + hd], qg_ref[...])
            dx, dg = vjp(dq_ref[:, h * hd:(h + 1) * hd])
            dP_ref[:, o:o + hd] = dx
            dqg = dqg + dg
        dkg = jnp.zeros((1, hd), F32)
        for h in range(KW // hd):
            o = PW + QW + h * hd
            _, vjp = jax.vjp(f, p_ref[:, o:o + hd], kg_ref[...])
            dx, dg = vjp(dk_ref[:, h * hd:(h + 1) * hd])
            dP_ref[:, o:o + hd] = dx
            dkg = dkg + dg
        dP_ref[:, PW + QW + KW:] = dv_ref[...]
        _acc_rows(dqg_ref, i == 0, [dqg])
        _acc_rows(dkg_ref, i == 0, [dkg])

    tot = PW + QW + 2 * KW
    return pl.pallas_call(
        body, name=name, grid=(geo.nt,),
        in_specs=[geo.row(tot), _full((1, hd)), _full((1, hd)), geo.row(hd), geo.row(hd), geo.row(PW), geo.row(QW), geo.row(KW), geo.row(KW)],
        out_specs=[geo.row(tot), _full((1, hd)), _full((1, hd))],
        out_shape=[S_((geo.T, tot), F32), S_((1, hd), F32), S_((1, hd), F32)], compiler_params=_cp())(
            P, qg, kg, cosf, sinf, dpool, dq, dk, dv)


def _att_tiles(T, q_rows):
    return _tile(T, q_rows, 16), _tile(T, 768, 16)


def _stack_heads(ref, G, hd):
    return jnp.concatenate([ref[:, g * hd:(g + 1) * hd] for g in range(G)], axis=0)


def _att_fwd(q, k, v1, hd, name):
    T, QW = q.shape
    KV = k.shape[1] // hd
    G = QW // hd // KV
    tq, tk = _att_tiles(T, 768)
    nk = T // tk

    def body(q_ref, k_ref, v_ref, o_ref, lse_ref, m_scr, acc):
        ki = pl.program_id(2)

        @pl.when(ki == 0)
        def _():
            m_scr[...] = jnp.full(m_scr.shape, -jnp.inf, F32)
            acc[...] = jnp.zeros(acc.shape, F32)

        s = _dot(_stack_heads(q_ref, G, hd), k_ref[...], NT)
        m_prev = m_scr[...]
        m_new = jnp.maximum(m_prev, jnp.max(s, axis=-1, keepdims=True))
        p = jnp.exp2(s - jnp.tile(m_new, (1, tk // hd)))
        acc[...] = jnp.tile(jnp.exp2(m_prev - m_new), (1, 2)) * acc[...] + _dot(p, v_ref[...], NN)
        m_scr[...] = m_new

        @pl.when(ki == nk - 1)
        def _():
            a = acc[...]
            o = a[:, :hd] / a[:, hd:]
            lse = m_scr[...] + jnp.log2(a[:, hd:])
            for g in range(G):
                o_ref[:, g * hd:(g + 1) * hd] = o[g * tq:(g + 1) * tq].astype(BF)
                lse_ref[g] = lse[g * tq:(g + 1) * tq]

    return pl.pallas_call(
        body, name=name, grid=(KV, T // tq, nk),
        in_specs=[pl.BlockSpec((tq, G * hd), lambda a, i, j: (i, a)), pl.BlockSpec((tk, hd), lambda a, i, j: (j, a)),
                  pl.BlockSpec((tk, 2 * hd), lambda a, i, j: (j, a))],
        out_specs=[pl.BlockSpec((tq, G * hd), lambda a, i, j: (i, a)), pl.BlockSpec((G, tq, hd), lambda a, i, j: (a, i, 0))],
        out_shape=[S_((T, QW), BF), S_((QW // hd, T, hd), F32)],
        scratch_shapes=[pltpu.VMEM((G * tq, hd), F32), pltpu.VMEM((G * tq, 2 * hd), F32)],
        compiler_params=_cp())(q, k, v1)


def _att_delta(geo, o, dmix, PW, hd, name):
    QW = o.shape[1]
    nh = QW // hd

    def body(o_ref, do_ref, d_ref):
        for h in range(nh):
            sl = slice(h * hd, (h + 1) * hd)
            d = jnp.sum(o_ref[:, sl].astype(F32) * do_ref[:, PW + h * hd:PW + (h + 1) * hd], axis=-1, keepdims=True)
            d_ref[h] = jnp.broadcast_to(d, (geo.RT, hd))

    return pl.pallas_call(body, name=name, grid=(geo.nt,), in_specs=[geo.row(QW), geo.row(PW + QW)],
                          out_specs=pl.BlockSpec((nh, geo.RT, hd), lambda i: (0, i, 0)), out_shape=S_((nh, geo.T, hd), F32),
                          compiler_params=_cp())(o, dmix)


def _att_bwd(q, k, v1, dmix, PW, lse, delta, hd, name):
    T, QW = q.shape
    KW = k.shape[1]
    KV = KW // hd
    G = QW // hd // KV
    tq, tk = _att_tiles(T, 256)
    nq, nk = T // tq, T // tk
    scale = hd ** -0.5
    pb = PW // hd

    def body(q_ref, k_ref, v_ref, *rest):
        do_refs, (lse_ref, dl_ref, dq_ref, dk_ref, dv_ref, dq_scr) = rest[:G], rest[G:]
        ki, qi = pl.program_id(1), pl.program_id(2)
        q3 = _stack_heads(q_ref, G, hd)
        do3 = jnp.concatenate([r[...] for r in do_refs], axis=0)
        lse = jnp.concatenate([lse_ref[g] for g in range(G)], axis=0)
        dl = jnp.concatenate([dl_ref[g] for g in range(G)], axis=0)
        kk = k_ref[...]
        p = jnp.exp2(_dot(q3, kk, NT) - jnp.tile(lse, (1, tk // hd)))
        ds = p * (_dot(do3, v_ref[:, :hd], NT) - jnp.tile(dl, (1, tk // hd)))
        pv = _dot(p, do3, TN)
        pk = _dot(ds, q3, TN)
        pq = _dot(ds, kk, NN)

        @pl.when(qi == 0)
        def _():
            dk_ref[...] = pk
            dv_ref[...] = pv

        @pl.when(qi > 0)
        def _():
            dk_ref[...] += pk
            dv_ref[...] += pv

        @pl.when(qi == nq - 1)
        def _():
            dk_ref[...] = dk_ref[...] * LN2

        @pl.when(ki == 0)
        def _():
            dq_scr[qi] = pq

        @pl.when(ki > 0)
        def _():
            dq_scr[qi] += pq

        @pl.when(ki == nk - 1)
        def _():
            full = dq_scr[qi]
            for g in range(G):
                dq_ref[:, g * hd:(g + 1) * hd] = full[g * tq:(g + 1) * tq] * scale

    qs = pl.BlockSpec((tq, G * hd), lambda a, j, i: (i, a))
    ks = pl.BlockSpec((tk, hd), lambda a, j, i: (j, a))
    vs = pl.BlockSpec((tk, 2 * hd), lambda a, j, i: (j, a))
    dos = [pl.BlockSpec((tq, hd), lambda a, j, i, g=g: (i, pb + a * G + g)) for g in range(G)]
    st = pl.BlockSpec((G, tq, hd), lambda a, j, i: (a, i, 0))
    dqs = pl.BlockSpec((tq, G * hd), lambda a, j, i: (jnp.where(j == nk - 1, i, 0), a))
    return pl.pallas_call(body, name=name, grid=(KV, nk, nq), in_specs=[qs, ks, vs] + dos + [st, st], out_specs=[dqs, ks, ks],
                          out_shape=[S_((T, QW), F32), S_((T, KW), F32), S_((T, KW), F32)],
                          scratch_shapes=[pltpu.VMEM((nq, G * tq, hd), F32)], compiler_params=_cp())(
                              q, k, v1, *([dmix] * G), lse, delta)


def _pool_cnt(pos, L, w):
    return (jnp.minimum(pos + w // 2, L) - jnp.maximum(pos - w // 2, 0)).astype(F32)


def _pool_mean(ext, gi, w, G, RT, cnt):
    acc = ext[pl.ds(HALO - w // 2, RT), gi * G:(gi + 1) * G]
    for off in range(-w // 2 + 1, w // 2):
        acc = acc + ext[pl.ds(HALO + off, RT), gi * G:(gi + 1) * G]
    return acc / cnt


def _pool_fwd(geo, P, PW, att, pw, ps, name):
    RT = geo.RT
    G = pw.shape[1]
    QW = att.shape[1]

    def body(pp, pc, pn, att_ref, pw_ref, ps_ref, m_ref, ext):
        i = pl.program_id(0)
        _fill_ext(geo, ext, pp, pc, pn, i)
        pos, L = geo.pos(i)
        for gi, w in enumerate(POOL_WINDOWS):
            sl = slice(gi * G, (gi + 1) * G)
            xm = _pool_mean(ext, gi, w, G, RT, _pool_cnt(pos, L, w)) - pc[:, sl]
            m_ref[:, sl] = (_dot(xm, pw_ref[gi], NN) * ps_ref[:, sl]).astype(BF)
        m_ref[:, PW:] = att_ref[...]

    return pl.pallas_call(
        body, name=name, grid=(geo.nt,),
        in_specs=[geo.prev(PW), geo.row(PW), geo.next(PW), geo.row(QW), _full(pw.shape), _full(ps.shape)],
        out_specs=geo.row(PW + QW), out_shape=S_((geo.T, PW + QW), BF),
        scratch_shapes=[pltpu.VMEM((RT + 2 * HALO, PW), F32)], compiler_params=_cp())(P, P, P, att, pw, ps)


def _pool_bwd(geo, P, PW, dmix, pw, ps, name):
    RT = geo.RT
    G = pw.shape[1]
    RE = RT + 2 * HALO

    def body(pp, pc, pn, dp_, dc, dn, pw_ref, ps_ref, dx_ref, dpw, dps, ext, extd, dmc):
        i = pl.program_id(0)
        _fill_ext(geo, ext, pp, pc, pn, i)
        _fill_ext(geo, extd, dp_, dc, dn, i)
        pos, L = geo.pos(i)
        r = lax.broadcasted_iota(jnp.int32, (RE, 1), 0)
        pos_e = pos[0:1, :] - HALO + r
        rows_s = []
        for gi, w in enumerate(POOL_WINDOWS):
            sl = slice(gi * G, (gi + 1) * G)
            xm = _pool_mean(ext, gi, w, G, RT, _pool_cnt(pos, L, w)) - pc[:, sl]
            pre = _dot(xm, pw_ref[gi], NN)
            dout = dc[:, sl]
            rows_s.append(jnp.sum(dout * pre, axis=0, keepdims=True))
            gw = _dot(xm, dout * ps_ref[:, sl], TN)

            @pl.when(i == 0)
            def _(gi=gi, gw=gw):
                dpw[gi] = gw

            @pl.when(i > 0)
            def _(gi=gi, gw=gw):
                dpw[gi] += gw

            dm_e = _dot(extd[:, sl] * ps_ref[:, sl], pw_ref[gi], NT)
            dmc[...] = dm_e / jnp.maximum(_pool_cnt(pos_e, L, w), 1.0)
            acc = -dm_e[HALO:HALO + RT]
            for off in range(-w // 2 + 1, w // 2 + 1):
                acc = acc + dmc[pl.ds(HALO + off, RT), :]
            dx_ref[:, sl] = acc
        _acc_rows(dps, i == 0, [jnp.concatenate(rows_s, axis=-1)])

    return pl.pallas_call(
        body, name=name, grid=(geo.nt,),
        in_specs=[geo.prev(PW), geo.row(PW), geo.next(PW), geo.prev(PW), geo.row(PW), geo.next(PW), _full(pw.shape), _full(ps.shape)],
        out_specs=[geo.row(PW), _full(pw.shape), _full(ps.shape)],
        out_shape=[S_((geo.T, PW), F32), S_(pw.shape, F32), S_(ps.shape, F32)],
        scratch_shapes=[pltpu.VMEM((RE, PW), F32), pltpu.VMEM((RE, PW), F32), pltpu.VMEM((RE, G), F32)],
        compiler_params=_cp())(P, P, P, dmix, dmix, dmix, pw, ps)


def _mod_fwd(A, mod_w, name):
    L, D, MC = mod_w.shape
    tn = _tile(MC, 768, 128)

    def body(a_ref, w_ref, o_ref):
        o_ref[...] = _dot(jax.nn.silu(a_ref[...]), w_ref[...], NN)

    return pl.pallas_call(body, name=name, grid=(L, MC // tn),
                          in_specs=[_full(A.shape), pl.BlockSpec((None, D, tn), lambda l, j: (l, 0, j))],
                          out_specs=pl.BlockSpec((None, 16, tn), lambda l, j: (l, 0, j)), out_shape=S_((L, 16, MC), F32),
                          compiler_params=_cp())(A, mod_w)


def _mod_bwd(A, DM, mod_w, name):
    L, D, MC = mod_w.shape
    tn = _tile(MC, 768, 128)
    nj = MC // tn

    def body(a_ref, dm_ref, w_ref, gw_ref, da_ref, acc):
        l, j = pl.program_id(0), pl.program_id(1)
        sa, vjp = jax.vjp(jax.nn.silu, a_ref[...])
        gw_ref[...] = _dot(sa, dm_ref[...], TN)
        part = _dot(dm_ref[...], w_ref[...], NT)
        first = jnp.logical_and(l == 0, j == 0)

        @pl.when(first)
        def _():
            acc[...] = part

        @pl.when(jnp.logical_not(first))
        def _():
            acc[...] += part

        @pl.when(jnp.logical_and(l == L - 1, j == nj - 1))
        def _():
            da_ref[...] = vjp(acc[...])[0]

    wspec = pl.BlockSpec((None, D, tn), lambda l, j: (l, 0, j))
    return pl.pallas_call(body, name=name, grid=(L, nj),
                          in_specs=[_full(A.shape), pl.BlockSpec((None, 16, tn), lambda l, j: (l, 0, j)), wspec],
                          out_specs=[wspec, _full(A.shape)], out_shape=[S_((L, D, MC), F32), S_(A.shape, F32)],
                          scratch_shapes=[pltpu.VMEM(A.shape, F32)], compiler_params=_cp())(A, DM, mod_w)


PACK_COLS = 1024


def _pack_rows(shape):
    n = 1
    for d in shape:
        n *= d
    return n, -(-n // (8 * PACK_COLS)) * 8


def _pack(arrs):
    parts = []
    for a in arrs:
        n, rows = _pack_rows(a.shape)
        parts.append(jnp.pad(a.reshape(-1).astype(F32), (0, rows * PACK_COLS - n)).reshape(rows, PACK_COLS))
    return jnp.concatenate(parts)


def _unpack(packed, shapes, lead=()):
    out, off = [], 0
    for s in shapes:
        n, rows = _pack_rows(s)
        blk = packed[..., off:off + rows, :].reshape(lead + (rows * PACK_COLS,))
        out.append(blk[..., :n].reshape(lead + tuple(s)))
        off += rows
    return out


def _unshard_last(g):
    g = jnp.moveaxis(g, 0, -2)
    return g.reshape(g.shape[:-2] + (g.shape[-2] * g.shape[-1],))


def _my_shard(a, me):
    n = a.shape[-1] // NDEV
    return lax.dynamic_slice_in_dim(a, me * n, n, axis=a.ndim - 1)


def _rot_tables(S, LC, dk, hd):
    t = jnp.arange(S, dtype=F32)
    n_r = dk // 2
    ang1 = t[:, None] * (RET_THETA ** (-jnp.arange(n_r, dtype=F32) / n_r))
    cos1 = jnp.concatenate([jnp.cos(ang1), jnp.ones((LC, n_r), F32)])
    sin1 = jnp.concatenate([jnp.sin(ang1), jnp.zeros((LC, n_r), F32)])
    n_ax = hd // 4
    f_ax = ROPE_THETA ** (-jnp.arange(n_ax, dtype=F32) / n_ax)
    row = jnp.floor(t / GRID_W)
    col = t - row * GRID_W
    ang2 = jnp.concatenate([row[:, None] * f_ax, col[:, None] * f_ax], axis=-1)
    c2, s2 = jnp.cos(ang2), jnp.sin(ang2)
    cosf = jnp.concatenate([jnp.concatenate([c2, c2], axis=-1), jnp.ones((LC, hd), F32)])
    sinf = jnp.concatenate([jnp.concatenate([-s2, s2], axis=-1), jnp.zeros((LC, hd), F32)])
    return cos1, sin1, cosf, sinf


SMALL = ("c_ctx", "mod_b", "norm_pre", "norm_post", "lru_conv_w", "lru_conv_b", "lru_wa", "lru_ba", "lru_wx", "lru_bx",
         "lru_lambda", "ret_decay_logit", "ret_gn", "pool_w", "pool_scale", "q_norm", "k_norm")
WEIGHTS = ("c_ctx", "mod_w", "mod_b", "norm_pre", "norm_post", "ffn_gate", "ffn_up", "ffn_down", "ev_w_in", "ev_w_out",
           "lru_conv_w", "lru_conv_b", "lru_wa", "lru_ba", "lru_wx", "lru_bx", "lru_lambda", "ret_decay_logit", "ret_gn",
           "od_w_in", "od_w_out", "pool_w", "pool_scale", "q_norm", "k_norm")
INPUTS = ("x", "c", "ctx") + WEIGHTS + ("loss_target",) + tuple("m_" + w for w in WEIGHTS) + tuple("v_" + w for w in WEIGHTS)


def _step(p):
    x, c, ctx = p["x"], p["c"], p["ctx"]
    _, S, D = x.shape
    LC = ctx.shape[1]
    geo = _Geo(S, LC, D)
    T = geo.T
    xi, yi, ci = _me()
    me = 4 * xi + 2 * yi + ci
    L = p["mod_w"].shape[0]
    assert L == 2
    W = p["lru_conv_b"].shape[-1]
    H = p["ret_decay_logit"].shape[-1]
    hd = p["q_norm"].shape[-1]
    G = p["pool_w"].shape[-1]
    PW = G * len(POOL_WINDOWS)
    od_mix = p["od_w_out"].shape[1] * NDEV
    od_in = p["od_w_in"].shape[2] * NDEV
    QW = od_mix - PW
    KW = (od_in - od_mix) // 2
    assert p["ev_w_in"].shape[2] * NDEV == 6 * W and p["ret_gn"].shape[-1] == W and p["ev_w_out"].shape[1] * NDEV == 2 * W
    odims = (PW, QW, KW, hd)
    cos1, sin1, cosf, sinf = _rot_tables(S, LC, W // H, hd)

    sh0 = [(D,), p["norm_pre"].shape, p["norm_post"].shape, p["lru_conv_w"].shape[1:], p["lru_ba"].shape[1:],
           p["lru_bx"].shape[1:], p["lru_lambda"].shape[1:], p["pool_scale"].shape[1:]]
    pack0 = _pack([c, p["norm_pre"], p["norm_post"], p["lru_conv_w"], p["lru_ba"], p["lru_bx"], p["lru_lambda"], p["pool_scale"]])
    (g0,) = _all_gather([pack0], "gather_small")
    c_all, npre, npost, conv_w, ba, bx, lam, pscale = _unpack(g0, sh0, (NDEV,))
    npre, npost, conv_w, ba, bx, lam, pscale = [_unshard_last(a) for a in (npre, npost, conv_w, ba, bx, lam, pscale)]
    pscale = pscale[None]
    conv_b = p["lru_conv_b"]
    wa, wx = p["lru_wa"][0], p["lru_wx"][0]
    gn = p["ret_gn"]
    logits = p["ret_decay_logit"][0].reshape(2, H, 1, 1)
    pool_w = p["pool_w"][0]
    qg, kg = p["q_norm"], p["k_norm"]

    A = jnp.concatenate([c_all, p["c_ctx"][None], jnp.zeros((7, D), F32)])
    M = _mod_fwd(A, p["mod_w"], "mod_fwd")
    (Mg,) = _all_gather([M], "gather_mod")
    MC = M.shape[2]
    tabs = []
    for l in range(L):
        ml = lax.dynamic_index_in_dim(Mg[:, l], me, axis=1, keepdims=False).reshape(NDEV * MC) + p["mod_b"][l]
        mc = Mg[:, l, 8].reshape(NDEV * MC) + p["mod_b"][l]
        tabs.append(jnp.stack([ml.reshape(9, D), mc.reshape(9, D)]))

    def cast2(a, name):
        return _cast_bf16(a.reshape(-1, a.shape[-1]), name).reshape(a.shape)

    loc = {(l, j): [cast2(p[n][l, j], f"cast_{n}_{l}{j}") for n in ("ffn_gate", "ffn_up", "ffn_down")]
           for l in range(L) for j in range(2)}
    loc["ev"] = [cast2(p["ev_w_in"][0], "cast_ev_in"), cast2(p["ev_w_out"][0], "cast_ev_out")]
    loc["od"] = [cast2(p["od_w_in"][0], "cast_od_in"), cast2(p["od_w_out"][0], "cast_od_out")]

    def gather_start(key, name, after):
        return _xchg_start(True, loc[key], [jnp.broadcast_to(a[None], (NDEV,) + a.shape) for a in loc[key]], name, [after])

    ffn_w = {(0, 0): _all_gather(loc[0, 0], "gather_ffn_00")}

    def gp(a, l, s):
        return a[l, s][None]

    st, tok = gather_start("ev", "gs_ev", ffn_w[0, 0][0])
    x0 = jnp.concatenate([x[0], ctx[0]])
    h0, h0t = _norm_fwd(geo, x0, None, (_tie(gp(npre, 0, 0), tok), tabs[0], 0), "pre_00")
    y0, G0, U0 = _ffn_fwd(h0, *ffn_w[0, 0], name="ffn_fwd_00")
    x1, h1, h1t = _norm_fwd(geo, x0, (y0, gp(npost, 0, 0), tabs[0], 0, FFN_STEP), (gp(npre, 0, 1), tabs[0], 1), "post_00")
    ev_in, ev_out = _xchg_wait(st, h1, "gw_ev")
    ev_out_f = ev_out.reshape(2 * W, D)

    st, tok = gather_start((0, 1), "gs_ffn_01", h1)
    Pe = _mm_cols(h1, ev_in, "ev_in", dep=tok)
    u, a0, b0, a1, b1 = _lru_coef_fwd(geo, Pe, W, conv_w, conv_b, wa, ba, wx, bx, lam, "lru_coef")
    hs0, hp0 = _lru_scan_fwd(geo, a0, b0, 0, "lru_scan_f0")
    hs1, hp1 = _lru_scan_fwd(geo, a1, b1, 1, "lru_scan_f1")
    o0, st0, o1, st1 = _ret_fwd(geo, Pe, W, H, logits, cos1, sin1, "ret_fwd")
    mixe = _mix_even_fwd(geo, Pe, W, H, hs0, hs1, o0, o1, gn, "mix_even")
    y1 = _mm_full(mixe, ev_out_f, NN, "ev_out")
    x2, h2, h2t = _norm_fwd(geo, x1, (y1, gp(npost, 0, 1), tabs[0], 1, 1.0), (gp(npre, 0, 2), tabs[0], 2), "post_01")
    ffn_w[0, 1] = _xchg_wait(st, h2, "gw_ffn_01")

    st, tok = gather_start((1, 0), "gs_ffn_10", h2)
    y2, G2, U2 = _ffn_fwd(h2, *ffn_w[0, 1], name="ffn_fwd_01", dep=tok)
    x3, h3, h3t = _norm_fwd(geo, x2, (y2, gp(npost, 0, 2), tabs[0], 2, FFN_STEP), (gp(npre, 1, 0), tabs[1], 0), "post_02")
    ffn_w[1, 0] = _xchg_wait(st, h3, "gw_ffn_10")

    st, tok = gather_start("od", "gs_od", h3)
    y3, G3, U3 = _ffn_fwd(h3, *ffn_w[1, 0], name="ffn_fwd_10", dep=tok)
    x4, h4, h4t = _norm_fwd(geo, x3, (y3, gp(npost, 1, 0), tabs[1], 0, FFN_STEP), (gp(npre, 1, 1), tabs[1], 1), "post_10")
    od_inw, od_out = _xchg_wait(st, h4, "gw_od")
    od_out_f = od_out.reshape(od_mix, D)

    st, tok = gather_start((1, 1), "gs_ffn_11", h4)
    Po = _mm_cols(h4, od_inw, "od_in", dep=tok)
    qr, kr, vr = _qk_prep_fwd(geo, Po, odims, qg, kg, cosf, sinf, "qk_prep")
    att, lse = _att_fwd(qr, kr, vr, hd, "att_fwd")
    mixo = _pool_fwd(geo, Po, PW, att, pool_w, pscale, "pool_fwd")
    y4 = _mm_full(mixo, od_out_f, NN, "od_out")
    x5, h5, h5t = _norm_fwd(geo, x4, (y4, gp(npost, 1, 1), tabs[1], 1, 1.0), (gp(npre, 1, 2), tabs[1], 2), "post_11")
    ffn_w[1, 1] = _xchg_wait(st, h5, "gw_ffn_11")

    y5, G5, U5 = _ffn_fwd(h5, *ffn_w[1, 1], name="ffn_fwd_11")
    (x6,) = _norm_fwd(geo, x5, (y5, gp(npost, 1, 2), tabs[1], 2, FFN_STEP), None, "post_12")

    big_g = {}
    tokbox = [None]
    deferred = []

    def gpt(l, s):
        return _tie(gp(npre, l, s), tokbox[0])

    def a2a_start(key, srcs, name, dh):
        own = [lax.dynamic_index_in_dim(a, me, 0, keepdims=False) for a in srcs]
        state, token = _xchg_start(False, srcs, [jnp.zeros(a.shape, a.dtype) for a in srcs], name)
        big_g[key] = (state, own)
        tokbox[0] = token
        return dh

    def ffn_bwd(dy, h, Gs, Us, key):
        tag = f"{key[0]}{key[1]}"
        dh, dG, dU, Aact = _ffn_bwd_act(dy, Gs, Us, *ffn_w[key], name=f"ffn_bwd_{tag}")
        if key != (0, 0):
            srcs = [_ffn_wgrad_in(h, dG, f"ffn_wg_{tag}"), _ffn_wgrad_in(h, dU, f"ffn_wu_{tag}"), _ffn_wgrad_out(Aact, dy, f"ffn_wd_{tag}")]
            return a2a_start(key, srcs, f"as_ffn_{tag}", dh)
        def finish(dep):
            parts = [None] * 3
            a2a_start(key, [_ffn_wgrad_out(Aact, dy, f"ffn_wd_{tag}", dep=dep)], f"as_ffn_{tag}_d", dh)
            parts[2] = big_g[key]
            a2a_start(key, [_ffn_wgrad_in(h, dG, f"ffn_wg_{tag}", dep=tokbox[0])], f"as_ffn_{tag}_g", dh)
            parts[0] = big_g[key]
            a2a_start(key, [_ffn_wgrad_in(h, dU, f"ffn_wu_{tag}", dep=tokbox[0])], f"as_ffn_{tag}_u", dh)
            parts[1] = big_g[key]
            big_g[key] = parts

        deferred.append(finish)
        return dh

    loss_p, dx6, dy5, dpost5 = _loss_bwd(geo, x6, p["loss_target"][0], (y5, gp(npost, 1, 2), tabs[1], 2, FFN_STEP), "loss")
    dh5 = ffn_bwd(dy5, h5t, G5, U5, (1, 1))
    dx5, dy4, dpre5, dpost4 = _norm_bwd(geo, dx6, dh5, x5, (gpt(1, 2), tabs[1], 2),
                                        (y4, gp(npost, 1, 1), tabs[1], 1, 1.0), "nb_5")

    dmixo = _mm_full(dy4, od_out_f, NT, "od_out_d")
    g_od_out = _mm_tn_rows(mixo, dy4, NDEV, "od_out_w")
    dpool, g_pool_w, g_pscale = _pool_bwd(geo, Po, PW, dmixo, pool_w, pscale, "pool_bwd")
    delta = _att_delta(geo, att, dmixo, PW, hd, "att_delta")
    dq, dk, dv = _att_bwd(qr, kr, vr, dmixo, PW, lse, delta, hd, "att_bwd")
    dPo, g_qn, g_kn = _qk_prep_bwd(geo, Po, odims, qg, kg, cosf, sinf, dpool, dq, dk, dv, "qk_prep_bwd")
    dh4 = _mm_nt_cols(dPo, od_inw, "od_in_d")
    g_od_in = _mm_tn_cols(h4t, dPo, NDEV, "od_in_w")
    dh4 = a2a_start("od", [g_od_in, g_od_out], "as_od", dh4)
    dx4, dy3, dpre4, dpost3 = _norm_bwd(geo, dx5, dh4, x4, (gpt(1, 1), tabs[1], 1),
                                        (y3, gp(npost, 1, 0), tabs[1], 0, FFN_STEP), "nb_4")

    dh3 = ffn_bwd(dy3, h3t, G3, U3, (1, 0))
    dx3, dy2, dpre3, dpost2 = _norm_bwd(geo, dx4, dh3, x3, (gpt(1, 0), tabs[1], 0),
                                        (y2, gp(npost, 0, 2), tabs[0], 2, FFN_STEP), "nb_3")

    dh2 = ffn_bwd(dy2, h2t, G2, U2, (0, 1))
    dx2, dy1, dpre2, dpost1 = _norm_bwd(geo, dx3, dh2, x2, (gpt(0, 2), tabs[0], 2),
                                        (y1, gp(npost, 0, 1), tabs[0], 1, 1.0), "nb_2")

    dmixe = _mm_full(dy1, ev_out_f, NT, "ev_out_d")
    g_ev_out = _mm_tn_rows(mixe, dy1, NDEV, "ev_out_w")
    dg, dhs, dos, dog, g_gn = _mix_even_bwd(geo, Pe, W, H, hs0, hs1, o0, o1, gn, dmixe, "mix_even_bwd")
    da0, db0 = _lru_scan_bwd(geo, dhs, a0, hp0, 0, "lru_scan_b0")
    da1, db1 = _lru_scan_bwd(geo, dhs, a1, hp1, 1, "lru_scan_b1")
    du, g_wa, g_ba, g_wx, g_bx, g_lam = _lru_coef_bwd(geo, u, (da0, da1), (db0, db1), W, wa, ba, wx, bx, lam, "lru_coef_bwd")
    dq0, dk0, dv0, glg0, dq1, dk1, dv1, glg1 = _ret_bwd(geo, Pe, W, H, logits, cos1, sin1, (st0, st1), dos, "ret_bwd")
    dPe, g_cw, g_cb = _conv_bwd_assemble(geo, Pe, du, (dg, dq0, dk0, dv0, dq1, dk1, dv1, dog), W, conv_w, "conv_bwd")
    dh1 = _mm_nt_cols(dPe, ev_in, "ev_in_d")
    g_ev_in = _mm_tn_cols(h1t, dPe, NDEV, "ev_in_w")
    dh1 = a2a_start("ev", [g_ev_in, g_ev_out], "as_ev", dh1)
    dx1, dy0, dpre1, dpost0 = _norm_bwd(geo, dx2, dh1, x1, (gpt(0, 1), tabs[0], 1),
                                        (y0, gp(npost, 0, 0), tabs[0], 0, FFN_STEP), "nb_1")

    dh0 = ffn_bwd(dy0, h0t, G0, U0, (0, 0))
    dx0, dpre0 = _norm_bwd(geo, dx1, dh0, x0, (gpt(0, 0), tabs[0], 0), None, "nb_0")

    dpre = [[dpre0, dpre1, dpre2], [dpre3, dpre4, dpre5]]
    dpost = [[dpost0, dpost1, dpost2], [dpost3, dpost4, dpost5]]
    dtab = jnp.stack([jnp.stack([jnp.stack([dpre[l][s][:, 1], dpre[l][s][:, 2], dpost[l][s][:, 1]], axis=1) for s in range(3)], axis=1)
                      for l in range(L)])
    dtab_p = _pack([jnp.moveaxis(dtab.reshape(L, 2, 9 * D), 1, 0)])
    (dtab_g,) = _all_gather([dtab_p], "gather_dtab")
    dtab_sum = _sum_n(dtab_g, "sum_dtab")
    (dm_all,) = _unpack(dtab_g, [(2, L, 9 * D)], (NDEV,))
    (dm_sum,) = _unpack(dtab_sum, [(2, L, 9 * D)])
    (g_mod_b,) = _unpack(_sum_n(jnp.stack([_pack([dm_sum[0]]), _pack([dm_sum[1]])]), "sum_mod_b"), [(L, 9 * D)])
    dml = lax.dynamic_slice_in_dim(dm_all[:, 0], me * MC, MC, axis=2)
    dmc = lax.dynamic_slice_in_dim(dm_sum[1], me * MC, MC, axis=1)
    DM = jnp.concatenate([jnp.moveaxis(dml, 0, 1), dmc[:, None], jnp.zeros((L, 7, MC), F32)], axis=1)
    g_mod_w, dA = _mod_bwd(A, DM, p["mod_w"], "mod_bwd")

    g_npre = jnp.stack([jnp.stack([dpre[l][s][0, 0] + dpre[l][s][1, 0] for s in range(3)]) for l in range(L)])
    g_npost = jnp.stack([jnp.stack([dpost[l][s][0, 0] + dpost[l][s][1, 0] for s in range(3)]) for l in range(L)])
    g_logit = jnp.stack([glg0.reshape(H), glg1.reshape(H)])
    small_parts = [dA[8], g_npre, g_npost, g_cw, g_cb, g_wa, g_ba, g_wx, g_bx, g_lam, g_logit, g_gn, g_pool_w, g_pscale, g_qn, g_kn]
    (sg,) = _all_gather([_pack(small_parts)], "gather_small_g")
    ssum = _unpack(_sum_n(sg, "sum_small_g"), [a.shape for a in small_parts])
    (g_cctx, g_npre, g_npost, g_cw, g_cb, g_wa, g_ba, g_wx, g_bx, g_lam, g_logit, g_gn, g_pool_w, g_pscale, g_qn, g_kn) = ssum
    small_g = {
        "c_ctx": g_cctx, "mod_b": g_mod_b, "norm_pre": _my_shard(g_npre, me), "norm_post": _my_shard(g_npost, me),
        "lru_conv_w": _my_shard(g_cw, me)[None], "lru_conv_b": g_cb, "lru_wa": g_wa[None], "lru_ba": _my_shard(g_ba, me)[None],
        "lru_wx": g_wx[None], "lru_bx": _my_shard(g_bx, me)[None], "lru_lambda": _my_shard(g_lam, me)[None],
        "ret_decay_logit": g_logit[None], "ret_gn": g_gn, "pool_w": g_pool_w[None], "pool_scale": _my_shard(g_pscale, me),
        "q_norm": g_qn, "k_norm": g_kn,
    }
    shapes = [p[n].shape for n in SMALL]
    s_out = _reduce_adam(_pack([small_g[n] for n in SMALL])[None], _pack([p[n] for n in SMALL]),
                         _pack([p["m_" + n] for n in SMALL]), _pack([p["v_" + n] for n in SMALL]), "adam_small")
    deferred[0](s_out[0][:8, :128])
    res = {}
    for kind, packed in zip(("g", "d", "m", "v"), s_out):
        for n, a in zip(SMALL, _unpack(packed, shapes)):
            res[kind, n] = a

    def big(name, pieces, own, idx=None):
        w, m, v = p[name], p["m_" + name], p["v_" + name]
        if idx is not None:
            w, m, v = w[idx], m[idx], v[idx]
        shp = w.shape
        tag = name + ("" if idx is None else "_" + "".join(str(i) for i in idx))
        outs = _reduce_adam(pieces.reshape((pieces.shape[0], -1, shp[-1])), w.reshape(-1, shp[-1]), m.reshape(-1, shp[-1]),
                            v.reshape(-1, shp[-1]), "adam_" + tag, None if own is None else own.reshape(-1, shp[-1]))
        return [o.reshape(shp) for o in outs]

    got = {}
    after = [dx0]
    for key in ((1, 1), "od", (1, 0), (0, 1), "ev", (0, 0)):
        tag = key if isinstance(key, str) else f"ffn_{key[0]}{key[1]}"
        if key != (0, 0):
            state, own = big_g[key]
        else:
            for name, (pieces, own_) in (("mod_w", (g_mod_w[None], None)), ("ev_w_in", got["ev"][0]), ("ev_w_out", got["ev"][1]),
                                         ("od_w_in", got["od"][0]), ("od_w_out", got["od"][1])):
                outs = big(name, pieces, own_, None if name == "mod_w" else (0,))
                for kind, o in zip(("g", "d", "m", "v"), outs):
                    res[kind, name] = o if name == "mod_w" else o[None]
            after = [s_out[0]] + [res["g", n] for n in ("mod_w", "ev_w_in", "ev_w_out", "od_w_in", "od_w_out")]
            got[key] = [None] * 3
            for wi in (2, 0, 1):
                st_w, own_w = big_g[key][wi]
                lands = _xchg_wait(st_w, after, f"aw_{tag}_{wi}")
                got[key][wi] = (lands[0], own_w[0])
                after = [lands[0]]
            continue
        lands = _xchg_wait(state, after, "aw_" + tag)
        got[key] = list(zip(lands, own))
        after = [lands[0]]

    keys = [(l, j) for l in range(L) for j in range(2)]
    for wi, name in enumerate(("ffn_gate", "ffn_up", "ffn_down")):
        shp = p[name].shape
        st3 = (len(keys), shp[-2], shp[-1])
        outs = _reduce_adam_stack([got[k][wi][0] for k in keys], [got[k][wi][1] for k in keys], p[name].reshape(st3),
                                  p["m_" + name].reshape(st3), p["v_" + name].reshape(st3), "adam_" + name)
        for kind, o in zip(("g", "d", "m", "v"), outs):
            res[kind, name] = o.reshape(shp)

    loss = lax.psum(loss_p[0, 0], ("x", "y", "c"))
    grad_x = dx0[:S][None]
    return (loss, grad_x) + tuple(res[kind, n] for kind in ("g", "d", "m", "v") for n in WEIGHTS)


def kernel(
        x, c, ctx, c_ctx, mod_w, mod_b, norm_pre, norm_post, ffn_gate, ffn_up, ffn_down, ev_w_in, ev_w_out, lru_conv_w,
        lru_conv_b, lru_wa, lru_ba, lru_wx, lru_bx, lru_lambda, ret_decay_logit, ret_gn, od_w_in, od_w_out, pool_w,
        pool_scale, q_norm, k_norm, loss_target, m_c_ctx, m_mod_w, m_mod_b, m_norm_pre, m_norm_post, m_ffn_gate, m_ffn_up,
        m_ffn_down, m_ev_w_in, m_ev_w_out, m_lru_conv_w, m_lru_conv_b, m_lru_wa, m_lru_ba, m_lru_wx, m_lru_bx, m_lru_lambda,
        m_ret_decay_logit, m_ret_gn, m_od_w_in, m_od_w_out, m_pool_w, m_pool_scale, m_q_norm, m_k_norm, v_c_ctx, v_mod_w,
        v_mod_b, v_norm_pre, v_norm_post, v_ffn_gate, v_ffn_up, v_ffn_down, v_ev_w_in, v_ev_w_out, v_lru_conv_w,
        v_lru_conv_b, v_lru_wa, v_lru_ba, v_lru_wx, v_lru_bx, v_lru_lambda, v_ret_decay_logit, v_ret_gn, v_od_w_in,
        v_od_w_out, v_pool_w, v_pool_scale, v_q_norm, v_k_norm):
    args = locals()
    return _step({n: args[n] for n in INPUTS})
```
